```python
import jax, jax.numpy as jnp
from jax import lax
import numpy as np

D_MODEL = 2048
BATCH = 4
SEQ = 2048
DEPTH = 1
DEC_BATCH = 32
DEC_SEQ = 4
PAST_LEN = 8192
PAGE_SIZE = 128

HEAD_DIM = 128
HEADS_PER_GROUP = 4
DIL_GROUPS = ((128, 1), (512, 4), (2048, 16))
N_GROUPS = len(DIL_GROUPS)
N_HEADS = N_GROUPS * HEADS_PER_GROUP
ATT_WIDTH = N_HEADS * HEAD_DIM
ATT_OUT = HEADS_PER_GROUP * HEAD_DIM
SPAN = 128
ATT_SCALE = HEAD_DIM ** -0.5
ROT_DIM = HEAD_DIM // 4
ROPE_THETA = 500000.0
D_CONV = D_MODEL // 2
CONV_WIDTH = 31
FFN_HIDDEN = ((8 * D_MODEL + 3 * 256 - 1) // (3 * 256)) * 256
NORM_EPS = 1e-6
LN_EPS = 1e-5
IN_COLS = 2 * D_CONV + 3 * ATT_WIDTH + 2 * D_MODEL

kernel_name = "gated_conformer_dilated_swa_decoder_step"


def rms_norm(x, g):
    xf = x.astype(jnp.float32)
    y = xf * lax.rsqrt(jnp.mean(xf * xf, axis=-1, keepdims=True) + NORM_EPS)
    return (y * g.astype(jnp.float32)).astype(x.dtype)


def layer_norm(x, g, b):
    xf = x.astype(jnp.float32)
    mu = jnp.mean(xf, axis=-1, keepdims=True)
    xc = xf - mu
    var = jnp.mean(xc * xc, axis=-1, keepdims=True)
    y = xc * lax.rsqrt(var + LN_EPS) * g.astype(jnp.float32) + b.astype(jnp.float32)
    return y.astype(x.dtype)


def rope(x, pos):
    half = ROT_DIM // 2
    inv = jnp.power(ROPE_THETA, -jnp.arange(0, ROT_DIM, 2, dtype=jnp.float32) / ROT_DIM)
    ang = pos.astype(jnp.float32)[:, None] * inv[None, :]
    cos = jnp.cos(ang)[None, :, None, :]
    sin = jnp.sin(ang)[None, :, None, :]
    xf = x.astype(jnp.float32)
    x1 = xf[..., :half]
    x2 = xf[..., half:ROT_DIM]
    out = jnp.concatenate([x1 * cos - x2 * sin, x2 * cos + x1 * sin, xf[..., ROT_DIM:]], axis=-1)
    return out.astype(x.dtype)


def band_attention(q, k, v):
    N, L, H, hd = q.shape
    nb = -(-L // SPAN)
    Lp = nb * SPAN
    pad = ((0, 0), (0, Lp - L), (0, 0), (0, 0))
    qb = jnp.pad(q, pad).reshape(N, nb, SPAN, H, hd)
    kb = jnp.pad(k, pad).reshape(N, nb, SPAN, H, hd)
    vb = jnp.pad(v, pad).reshape(N, nb, SPAN, H, hd)

    def with_prev(t):
        prev = jnp.pad(t, ((0, 0), (1, 0), (0, 0), (0, 0), (0, 0)))[:, :-1]
        return jnp.concatenate([prev, t], axis=2)

    kk = with_prev(kb)
    vv = with_prev(vb)
    s = jnp.einsum('nbqhd,nbkhd->nbhqk', qb, kk,
                   preferred_element_type=jnp.float32) * ATT_SCALE
    qi = jnp.arange(SPAN)[:, None]
    ki = jnp.arange(2 * SPAN)[None, :]
    dist = SPAN + qi - ki
    blk = jnp.arange(nb)[:, None, None]
    valid = ((dist >= 0) & (dist <= SPAN))[None] & ((blk * SPAN - SPAN + ki[None]) >= 0)
    s = jnp.where(valid[None, :, None], s, -jnp.inf)
    m = jnp.max(s, axis=-1, keepdims=True)
    p = jnp.exp(s - m)
    den = jnp.sum(p, axis=-1)
    o = jnp.einsum('nbhqk,nbkhd->nbqhd', p, vv.astype(jnp.float32))
    o = o / jnp.transpose(den, (0, 1, 3, 2))[..., None]
    lse = jnp.transpose(m[..., 0] + jnp.log(den), (0, 1, 3, 2))
    o = o.reshape(N, Lp, H, hd)[:, :L].astype(q.dtype)
    lse = lse.reshape(N, Lp, H)[:, :L]
    return o, lse


def dilated_prompt(q, k, v, dil):
    B, S, H, hd = q.shape

    def to_streams(t):
        return t.reshape(B, S // dil, dil, H, hd).transpose(0, 2, 1, 3, 4).reshape(B * dil, S // dil, H, hd)

    o, lse = band_attention(to_streams(q), to_streams(k), to_streams(v))
    o = o.reshape(B, dil, S // dil, H, hd).transpose(0, 2, 1, 3, 4).reshape(B, S, H, hd)
    lse = lse.reshape(B, dil, S // dil, H).transpose(0, 2, 1, 3).reshape(B, S, H)
    return o, lse


def dilated_sample(q, k_new, v_new, k_buf, v_buf, dil):
    Wb = k_buf.shape[1]
    T = q.shape[1]
    kc = jnp.concatenate([k_buf, k_new], axis=1)
    vc = jnp.concatenate([v_buf, v_new], axis=1)
    i = jnp.arange(T)[:, None]
    j = jnp.arange(SPAN + 1)[None, :]
    idx = Wb + i - j * dil
    valid = idx >= 0
    idx = jnp.maximum(idx, 0)
    kg = kc[:, idx]
    vg = vc[:, idx]
    s = jnp.einsum('bthd,btjhd->bthj', q, kg,
                   preferred_element_type=jnp.float32) * ATT_SCALE
    s = jnp.where(valid[None, :, None, :], s, -jnp.inf)
    m = jnp.max(s, axis=-1, keepdims=True)
    p = jnp.exp(s - m)
    den = jnp.sum(p, axis=-1)
    o = jnp.einsum('bthj,btjhd->bthd', p, vg.astype(jnp.float32)) / den[..., None]
    lse = m[..., 0] + jnp.log(den)
    return o.astype(q.dtype), lse, kc[:, T:], vc[:, T:]


def combine_groups(outs, lses):
    w = jax.nn.softmax(jnp.stack(lses, axis=0), axis=0)
    o = jnp.einsum('gnlh,gnlhd->nlhd', w, jnp.stack(outs, axis=0).astype(jnp.float32))
    return o.astype(outs[0].dtype)


def head_group(t, g):
    return t[:, :, g * HEADS_PER_GROUP:(g + 1) * HEADS_PER_GROUP]


def prompt_attention(q, k, v):
    S = q.shape[1]
    outs, lses, new_k, new_v = [], [], [], []
    for g, (win, dil) in enumerate(DIL_GROUPS):
        kg, vg = head_group(k, g), head_group(v, g)
        o, lse = dilated_prompt(head_group(q, g), kg, vg, dil)
        outs.append(o)
        lses.append(lse)
        keep = min(win, S)
        new_k.append(kg[:, S - keep:])
        new_v.append(vg[:, S - keep:])
    return combine_groups(outs, lses), new_k, new_v


def sample_attention(q, k, v, k_bufs, v_bufs):
    outs, lses, new_k, new_v = [], [], [], []
    for g, (win, dil) in enumerate(DIL_GROUPS):
        o, lse, nk, nv = dilated_sample(head_group(q, g), head_group(k, g), head_group(v, g),
                                        k_bufs[g], v_bufs[g], dil)
        outs.append(o)
        lses.append(lse)
        new_k.append(nk)
        new_v.append(nv)
    return combine_groups(outs, lses), new_k, new_v


def conv_branch(u, u_past, w_dw, b_dw, ln_g, ln_b, w_pw, b_pw):
    uc = jnp.concatenate([u_past, u], axis=1)
    y = lax.conv_general_dilated(uc, w_dw[:, None, :], window_strides=(1,), padding='VALID',
                                 dimension_numbers=('NWC', 'WIO', 'NWC'),
                                 feature_group_count=D_CONV) + b_dw
    y = jax.nn.silu(layer_norm(y, ln_g, ln_b))
    return y @ w_pw + b_pw, uc[:, -(CONV_WIDTH - 1):]


def decoder_layer(x, pos, u_past, attend, g_mix, w_in, b_glu, w_dw, b_dw, ln_g, ln_b,
                  w_pw, b_pw, w_o_att, w_out, g_ffn, w_gate, w_up, w_down):
    N, L, _ = x.shape
    h = rms_norm(x, g_mix)
    z = h @ w_in
    o1 = 2 * D_CONV
    o2 = o1 + ATT_WIDTH
    o3 = o2 + ATT_WIDTH
    o4 = o3 + ATT_WIDTH
    o5 = o4 + D_MODEL
    glu = z[..., :o1] + b_glu
    u = glu[..., :D_CONV] * jax.nn.sigmoid(glu[..., D_CONV:])
    q = rope(z[..., o1:o2].reshape(N, L, N_HEADS, HEAD_DIM), pos)
    k = rope(z[..., o2:o3].reshape(N, L, N_HEADS, HEAD_DIM), pos)
    v = z[..., o3:o4].reshape(N, L, N_HEADS, HEAD_DIM)
    gate_conv = jax.nn.sigmoid(z[..., o4:o5])
    gate_attn = jax.nn.sigmoid(z[..., o5:])

    conv_out, new_conv = conv_branch(u, u_past, w_dw, b_dw, ln_g, ln_b, w_pw, b_pw)
    o_att, new_k, new_v = attend(q, k, v)
    attn_out = o_att.reshape(N, L, ATT_OUT) @ w_o_att

    x = x + (gate_conv * conv_out + gate_attn * attn_out) @ w_out
    hf = rms_norm(x, g_ffn)
    x = x + (jax.nn.silu(hf @ w_gate) * (hf @ w_up)) @ w_down
    return x, new_conv, new_k, new_v


def setup_inputs(seed: int = 0) -> dict:
    key = jax.random.key(seed)
    ks = iter(jax.random.split(key, 40))
    f32 = jnp.float32

    def nrm(shape, scale):
        return jax.random.normal(next(ks), shape, f32) * scale

    inp = {}
    inp['x_prompt'] = nrm((BATCH, SEQ, D_MODEL), 1.0)
    inp['x_sample'] = nrm((DEC_BATCH, DEC_SEQ, D_MODEL), 1.0)
    inp['state_conv'] = nrm((DEPTH, DEC_BATCH, CONV_WIDTH - 1, D_CONV), 1.0)
    for win, _ in DIL_GROUPS:
        wb = min(win, PAST_LEN)
        inp['cache_k_w%d' % win] = nrm((DEPTH, DEC_BATCH, wb, HEADS_PER_GROUP, HEAD_DIM), 1.0)
        inp['cache_v_w%d' % win] = nrm((DEPTH, DEC_BATCH, wb, HEADS_PER_GROUP, HEAD_DIM), 1.0)
    inp['g_mix'] = 1.0 + nrm((DEPTH, D_MODEL), 0.02)
    inp['w_in'] = nrm((DEPTH, D_MODEL, IN_COLS), D_MODEL ** -0.5)
    inp['b_glu'] = nrm((DEPTH, 2 * D_CONV), 0.02)
    inp['w_dw'] = nrm((DEPTH, CONV_WIDTH, D_CONV), CONV_WIDTH ** -0.5)
    inp['b_dw'] = nrm((DEPTH, D_CONV), 0.02)
    inp['ln_g'] = 1.0 + nrm((DEPTH, D_CONV), 0.02)
    inp['ln_b'] = nrm((DEPTH, D_CONV), 0.02)
    inp['w_pw'] = nrm((DEPTH, D_CONV, D_MODEL), D_CONV ** -0.5)
    inp['b_pw'] = nrm((DEPTH, D_MODEL), 0.02)
    inp['w_o_att'] = nrm((DEPTH, ATT_OUT, D_MODEL), ATT_OUT ** -0.5)
    inp['w_out'] = nrm((DEPTH, D_MODEL, D_MODEL), D_MODEL ** -0.5)
    inp['g_ffn'] = 1.0 + nrm((DEPTH, D_MODEL), 0.02)
    inp['w_gate'] = nrm((DEPTH, D_MODEL, FFN_HIDDEN), D_MODEL ** -0.5)
    inp['w_up'] = nrm((DEPTH, D_MODEL, FFN_HIDDEN), D_MODEL ** -0.5)
    inp['w_down'] = nrm((DEPTH, FFN_HIDDEN, D_MODEL), FFN_HIDDEN ** -0.5)
    inp['g_final'] = 1.0 + nrm((D_MODEL,), 0.02)
    return inp


def reference(x_prompt, x_sample, state_conv, cache_k_w128, cache_v_w128, cache_k_w512,
              cache_v_w512, cache_k_w2048, cache_v_w2048, g_mix, w_in, b_glu, w_dw, b_dw,
              ln_g, ln_b, w_pw, b_pw, w_o_att, w_out, g_ffn, w_gate, w_up, w_down, g_final):
    pos_p = jnp.arange(x_prompt.shape[1], dtype=jnp.int32)
    pos_s = PAST_LEN + jnp.arange(x_sample.shape[1], dtype=jnp.int32)
    xp, xs = x_prompt, x_sample
    conv_p, conv_s = [], []
    kp = [[] for _ in DIL_GROUPS]
    vp = [[] for _ in DIL_GROUPS]
    ks_ = [[] for _ in DIL_GROUPS]
    vs_ = [[] for _ in DIL_GROUPS]
    for l in range(DEPTH):
        lw = (g_mix[l], w_in[l], b_glu[l], w_dw[l], b_dw[l], ln_g[l], ln_b[l], w_pw[l], b_pw[l],
              w_o_att[l], w_out[l], g_ffn[l], w_gate[l], w_up[l], w_down[l])
        zero_past = jnp.zeros((xp.shape[0], CONV_WIDTH - 1, D_CONV), xp.dtype)
        xp, cp, nkp, nvp = decoder_layer(xp, pos_p, zero_past, prompt_attention, *lw)
        k_bufs = (cache_k_w128[l], cache_k_w512[l], cache_k_w2048[l])
        v_bufs = (cache_v_w128[l], cache_v_w512[l], cache_v_w2048[l])
        attend_s = lambda q, k, v, kb=k_bufs, vb=v_bufs: sample_attention(q, k, v, kb, vb)
        xs, cs, nks, nvs = decoder_layer(xs, pos_s, state_conv[l], attend_s, *lw)
        conv_p.append(cp)
        conv_s.append(cs)
        for g in range(N_GROUPS):
            kp[g].append(nkp[g])
            vp[g].append(nvp[g])
            ks_[g].append(nks[g])
            vs_[g].append(nvs[g])
    y_prompt = rms_norm(xp, g_final)
    y_sample = rms_norm(xs, g_final)
    st = lambda lst: jnp.stack(lst, axis=0)
    return (y_prompt, y_sample, st(conv_p), st(conv_s),
            st(kp[0]), st(ks_[0]), st(vp[0]), st(vs_[0]),
            st(kp[1]), st(ks_[1]), st(vp[1]), st(vs_[1]),
            st(kp[2]), st(ks_[2]), st(vp[2]), st(vs_[2]))
```

```python
import functools

import jax
import jax.numpy as jnp
from jax import lax
from jax.experimental import pallas as pl
from jax.experimental.pallas import tpu as pltpu

F32 = jnp.float32
BF16 = jnp.bfloat16

D_MODEL = 2048
BATCH = 4
SEQ = 2048
DEC_BATCH = 32
DEC_SEQ = 4
PAST_LEN = 8192
HEAD_DIM = 128
HEADS_PER_GROUP = 4
DILATIONS = (1, 4, 16)
WINDOWS = (128, 512, 2048)
N_GROUPS = 3
GROUP_COLS = HEADS_PER_GROUP * HEAD_DIM
ATT_WIDTH = N_GROUPS * GROUP_COLS
SPAN = 128
ATT_SCALE = HEAD_DIM ** -0.5
ROT_DIM = HEAD_DIM // 4
ROPE_THETA = 500000.0
D_CONV = D_MODEL // 2
CONV_WIDTH = 31
FFN_HIDDEN = 5632
NORM_EPS = 1e-6
LN_EPS = 1e-5
IN_COLS = 2 * D_CONV + 3 * ATT_WIDTH + 2 * D_MODEL

ROWS_P = BATCH * SEQ
ROWS_S = DEC_BATCH * DEC_SEQ
NEG = -1e30

VMEM_LIMIT = 56 * 1024 * 1024

TM = 1024
NP = ROWS_P // TM
TN = 512
TM2 = 512
NP2 = ROWS_P // TM2
TH = 512
KH = FFN_HIDDEN // TH
TC = 256
HALO = 32


def _sigmoid(x):
    return 1.0 / (1.0 + jnp.exp(-x))


def _rms(x, g):
    return x * lax.rsqrt(jnp.mean(x * x, axis=-1, keepdims=True) + NORM_EPS) * g


def _params(sem):
    return pltpu.CompilerParams(dimension_semantics=sem, vmem_limit_bytes=VMEM_LIMIT)


def _pidx(i, n):
    return jnp.minimum(i, n - 1)


def _rms_kernel(xp_ref, xs_ref, g_ref, hp_ref, hs_ref):
    i = pl.program_id(0)

    @pl.when(i < NP)
    def _():
        hp_ref[...] = _rms(xp_ref[...], g_ref[...]).astype(BF16)

    @pl.when(i == NP)
    def _():
        hs_ref[...] = _rms(xs_ref[...], g_ref[...]).astype(BF16)


def _rms_call(xp, xs, g):
    return pl.pallas_call(
        _rms_kernel,
        grid=(NP + 1,),
        in_specs=[
            pl.BlockSpec((TM, D_MODEL), lambda i: (_pidx(i, NP), 0)),
            pl.BlockSpec((ROWS_S, D_MODEL), lambda i: (0, 0)),
            pl.BlockSpec((1, D_MODEL), lambda i: (0, 0)),
        ],
        out_specs=[
            pl.BlockSpec((TM, D_MODEL), lambda i: (_pidx(i, NP), 0)),
            pl.BlockSpec((ROWS_S, D_MODEL), lambda i: (0, 0)),
        ],
        out_shape=[
            jax.ShapeDtypeStruct((ROWS_P, D_MODEL), BF16),
            jax.ShapeDtypeStruct((ROWS_S, D_MODEL), BF16),
        ],
        compiler_params=_params(("arbitrary",)),
        name="rms_in",
    )(xp, xs, g)


def _glu_kernel(hp_ref, hs_ref, wa_ref, wb_ref, ba_ref, bb_ref, up_ref, us_ref, wa_s, wb_s):
    i = pl.program_id(1)

    @pl.when(i == 0)
    def _():
        wa_s[...] = wa_ref[...].astype(BF16)
        wb_s[...] = wb_ref[...].astype(BF16)

    def glu(h):
        za = jnp.dot(h, wa_s[...], preferred_element_type=F32) + ba_ref[...]
        zb = jnp.dot(h, wb_s[...], preferred_element_type=F32) + bb_ref[...]
        return za * _sigmoid(zb)

    @pl.when(i < NP)
    def _():
        up_ref[...] = glu(hp_ref[...])

    @pl.when(i == NP)
    def _():
        us_ref[...] = glu(hs_ref[...])


def _glu_call(hp, hs, w_in, b_glu):
    nj = D_CONV // TN
    return pl.pallas_call(
        _glu_kernel,
        grid=(nj, NP + 1),
        in_specs=[
            pl.BlockSpec((TM, D_MODEL), lambda j, i: (_pidx(i, NP), 0)),
            pl.BlockSpec((ROWS_S, D_MODEL), lambda j, i: (0, 0)),
            pl.BlockSpec((D_MODEL, TN), lambda j, i: (0, j)),
            pl.BlockSpec((D_MODEL, TN), lambda j, i: (0, j + nj)),
            pl.BlockSpec((1, TN), lambda j, i: (0, j)),
            pl.BlockSpec((1, TN), lambda j, i: (0, j + nj)),
        ],
        out_specs=[
            pl.BlockSpec((TM, TN), lambda j, i: (_pidx(i, NP), j)),
            pl.BlockSpec((ROWS_S, TN), lambda j, i: (0, j)),
        ],
        out_shape=[
            jax.ShapeDtypeStruct((ROWS_P, D_CONV), F32),
            jax.ShapeDtypeStruct((ROWS_S, D_CONV), F32),
        ],
        scratch_shapes=[pltpu.VMEM((D_MODEL, TN), BF16), pltpu.VMEM((D_MODEL, TN), BF16)],
        compiler_params=_params(("arbitrary", "arbitrary")),
        name="in_glu",
    )(hp, hs, w_in, w_in, b_glu, b_glu)


def _qkv_kernel(hp_ref, hs_ref, w_ref, tp_ref, ts_ref, op_ref, os_ref, w_s):
    i = pl.program_id(1)

    @pl.when(i == 0)
    def _():
        w_s[...] = w_ref[...].astype(BF16)

    def project(h, t_ref, o_ref, rows):
        z = jnp.dot(h, w_s[...], preferred_element_type=F32)
        cos, sin_lo, sin_hi = t_ref[0, 0], t_ref[0, 1], t_ref[0, 2]
        for hh in range(HEADS_PER_GROUP):
            t = z[:, hh * HEAD_DIM:(hh + 1) * HEAD_DIM]
            t = (t * cos + pltpu.roll(t, HEAD_DIM - ROT_DIM // 2, 1) * sin_lo
                 + pltpu.roll(t, ROT_DIM // 2, 1) * sin_hi)
            o_ref[0, pl.ds(hh, rows, stride=HEADS_PER_GROUP), :] = t

    @pl.when(i < NP)
    def _():
        project(hp_ref[...], tp_ref, op_ref, TM)

    @pl.when(i == NP)
    def _():
        project(hs_ref[...], ts_ref, os_ref, ROWS_S)


def _qkv_call(hp, hs, w_in, tab_p, tab_s):
    nj = 3 * N_GROUPS
    off = 2 * D_CONV // TN
    kind = lambda j: jnp.where(j < 2 * N_GROUPS, 0, 1)
    return pl.pallas_call(
        _qkv_kernel,
        grid=(nj, NP + 1),
        in_specs=[
            pl.BlockSpec((TM, D_MODEL), lambda j, i: (_pidx(i, NP), 0)),
            pl.BlockSpec((ROWS_S, D_MODEL), lambda j, i: (0, 0)),
            pl.BlockSpec((D_MODEL, TN), lambda j, i: (0, j + off)),
            pl.BlockSpec((1, 3, TM, HEAD_DIM), lambda j, i: (kind(j), 0, _pidx(i, NP) % (SEQ // TM), 0)),
            pl.BlockSpec((1, 3, ROWS_S, HEAD_DIM), lambda j, i: (kind(j), 0, 0, 0)),
        ],
        out_specs=[
            pl.BlockSpec((1, 4 * TM, HEAD_DIM), lambda j, i: (j, _pidx(i, NP), 0)),
            pl.BlockSpec((1, 4 * ROWS_S, HEAD_DIM), lambda j, i: (j, 0, 0)),
        ],
        out_shape=[
            jax.ShapeDtypeStruct((nj, 4 * ROWS_P, HEAD_DIM), F32),
            jax.ShapeDtypeStruct((nj, 4 * ROWS_S, HEAD_DIM), F32),
        ],
        scratch_shapes=[pltpu.VMEM((D_MODEL, TN), BF16)],
        compiler_params=_params(("arbitrary", "arbitrary")),
        name="in_qkv",
    )(hp, hs, w_in, tab_p, tab_s)


def _gate_kernel(hp_ref, hs_ref, w_ref, op_ref, os_ref, w_s):
    i = pl.program_id(1)

    @pl.when(i == 0)
    def _():
        w_s[...] = w_ref[...].astype(BF16)

    @pl.when(i < NP)
    def _():
        op_ref[...] = _sigmoid(jnp.dot(hp_ref[...], w_s[...], preferred_element_type=F32)).astype(BF16)

    @pl.when(i == NP)
    def _():
        os_ref[...] = _sigmoid(jnp.dot(hs_ref[...], w_s[...], preferred_element_type=F32)).astype(BF16)


def _gate_call(hp, hs, w_in):
    nj = 2 * D_MODEL // TN
    off = (2 * D_CONV + 3 * ATT_WIDTH) // TN
    return pl.pallas_call(
        _gate_kernel,
        grid=(nj, NP + 1),
        in_specs=[
            pl.BlockSpec((TM, D_MODEL), lambda j, i: (_pidx(i, NP), 0)),
            pl.BlockSpec((ROWS_S, D_MODEL), lambda j, i: (0, 0)),
            pl.BlockSpec((D_MODEL, TN), lambda j, i: (0, j + off)),
        ],
        out_specs=[
            pl.BlockSpec((TM, TN), lambda j, i: (_pidx(i, NP), j)),
            pl.BlockSpec((ROWS_S, TN), lambda j, i: (0, j)),
        ],
        out_shape=[
            jax.ShapeDtypeStruct((ROWS_P, 2 * D_MODEL), BF16),
            jax.ShapeDtypeStruct((ROWS_S, 2 * D_MODEL), BF16),
        ],
        scratch_shapes=[pltpu.VMEM((D_MODEL, TN), BF16)],
        compiler_params=_params(("arbitrary", "arbitrary")),
        name="in_gate",
    )(hp, hs, w_in)


def _ln_silu(y, g, b):
    mu = jnp.mean(y, axis=-1, keepdims=True)
    yc = y - mu
    var = jnp.mean(yc * yc, axis=-1, keepdims=True)
    z = yc * lax.rsqrt(var + LN_EPS) * g + b
    return z * _sigmoid(z)


def _conv_p_kernel(cur_ref, prev_ref, w_ref, bdw_ref, g_ref, b_ref, y_ref, ext, ypre):
    c = pl.program_id(1)
    ext[pl.ds(HALO, TC), :] = cur_ref[...]

    @pl.when(c > 0)
    def _():
        ext[0:HALO, :] = prev_ref[TC - HALO:TC, :]

    @pl.when(c == 0)
    def _():
        ext[0:HALO, :] = jnp.zeros((HALO, D_CONV), F32)

    base = HALO - (CONV_WIDTH - 1)
    for lc in range(D_CONV // 128):
        sl = slice(lc * 128, (lc + 1) * 128)
        acc = jnp.zeros((TC, 128), F32)
        for j in range(CONV_WIDTH):
            acc = acc + ext[pl.ds(base + j, TC), sl] * w_ref[j:j + 1, sl]
        ypre[:, sl] = acc + bdw_ref[:, sl]
    y_ref[...] = _ln_silu(ypre[...], g_ref[...], b_ref[...]).astype(BF16)


def _conv_p_call(u_p, w_dw, b_dw, ln_g, ln_b):
    nc = SEQ // TC
    vec = pl.BlockSpec((1, D_CONV), lambda b, c: (0, 0))
    return pl.pallas_call(
        _conv_p_kernel,
        grid=(BATCH, nc),
        in_specs=[
            pl.BlockSpec((TC, D_CONV), lambda b, c: (b * nc + c, 0)),
            pl.BlockSpec((TC, D_CONV), lambda b, c: (b * nc + jnp.maximum(c - 1, 0), 0)),
            pl.BlockSpec((CONV_WIDTH, D_CONV), lambda b, c: (0, 0)),
            vec, vec, vec,
        ],
        out_specs=pl.BlockSpec((TC, D_CONV), lambda b, c: (b * nc + c, 0)),
        out_shape=jax.ShapeDtypeStruct((ROWS_P, D_CONV), BF16),
        scratch_shapes=[pltpu.VMEM((HALO + TC, D_CONV), F32), pltpu.VMEM((TC, D_CONV), F32)],
        compiler_params=_params(("arbitrary", "arbitrary")),
        name="conv_prompt",
    )(u_p, u_p, w_dw, b_dw, ln_g, ln_b)


CONV_SB = 8


def _conv_s_kernel(ext_ref, w_ref, bdw_ref, g_ref, b_ref, y_ref):
    acc = jnp.zeros((CONV_SB, DEC_SEQ, D_CONV), F32)
    for j in range(CONV_WIDTH):
        acc = acc + ext_ref[:, pl.ds(j, DEC_SEQ), :] * w_ref[j:j + 1, :]
    y_ref[...] = _ln_silu(acc + bdw_ref[...], g_ref[...], b_ref[...])


def _conv_s_call(ext_s, w_dw, b_dw, ln_g, ln_b):
    vec = pl.BlockSpec((1, D_CONV), lambda b: (0, 0))
    rows = CONV_WIDTH - 1 + DEC_SEQ
    return pl.pallas_call(
        _conv_s_kernel,
        grid=(DEC_BATCH // CONV_SB,),
        in_specs=[
            pl.BlockSpec((CONV_SB, rows, D_CONV), lambda b: (b, 0, 0)),
            pl.BlockSpec((CONV_WIDTH, D_CONV), lambda b: (0, 0)),
            vec, vec, vec,
        ],
        out_specs=pl.BlockSpec((CONV_SB, DEC_SEQ, D_CONV), lambda b: (b, 0, 0)),
        out_shape=jax.ShapeDtypeStruct((DEC_BATCH, DEC_SEQ, D_CONV), F32),
        compiler_params=_params(("arbitrary",)),
        name="conv_sample",
    )(ext_s, w_dw, b_dw, ln_g, ln_b)


def _softmax_block(q, k, v, mask):
    s = lax.dot_general(q.astype(BF16), k.astype(BF16), (((1,), (1,)), ((), ())),
                        preferred_element_type=F32) * ATT_SCALE
    s = jnp.where(mask, s, NEG)
    m = jnp.max(s, axis=-1, keepdims=True)
    p = jnp.exp(s - m)
    den = jnp.sum(p, axis=-1, keepdims=True)
    o = jnp.dot(p.astype(BF16), v.astype(BF16), preferred_element_type=F32) / den
    lse = m + jnp.log(den)
    return o, jnp.broadcast_to(lse, (SPAN, HEAD_DIM))


def _make_attn_p_kernel(dil):
    stride = HEADS_PER_GROUP * dil
    nblk = SEQ // dil // SPAN
    blk_rows = SPAN * stride

    def kern(q_ref, k_ref, v_ref, o_ref, l_ref):
        qi = lax.broadcasted_iota(jnp.int32, (SPAN, SPAN), 0)
        ki = lax.broadcasted_iota(jnp.int32, (SPAN, SPAN), 1)
        mask_first = ki <= qi
        qi2 = lax.broadcasted_iota(jnp.int32, (SPAN, 2 * SPAN), 0)
        ki2 = lax.broadcasted_iota(jnp.int32, (SPAN, 2 * SPAN), 1)
        mask_band = (ki2 >= qi2) & (ki2 <= qi2 + SPAN)

        for r in range(dil):
            for h in range(HEADS_PER_GROUP):
                off = HEADS_PER_GROUP * r + h
                own = pl.ds(off, SPAN, stride=stride)
                o, lse = _softmax_block(q_ref[0, own, :], k_ref[0, own, :], v_ref[0, own, :], mask_first)
                o_ref[own, :] = o
                l_ref[own, :] = lse

                if nblk > 1:
                    def body(jb, carry, off=off):
                        start = pl.multiple_of(jb * blk_rows, blk_rows)
                        qs = pl.ds(start + off, SPAN, stride=stride)
                        ks = pl.ds(start - blk_rows + off, 2 * SPAN, stride=stride)
                        o, lse = _softmax_block(q_ref[0, qs, :], k_ref[0, ks, :], v_ref[0, ks, :], mask_band)
                        o_ref[qs, :] = o
                        l_ref[qs, :] = lse
                        return carry

                    lax.fori_loop(1, nblk, body, 0)

    return kern


def _attn_p_call(qkv_p, g):
    rows = 4 * SEQ
    blk = lambda t: pl.BlockSpec((1, rows, HEAD_DIM), lambda b, t=t: (t * N_GROUPS + g, b, 0))
    out = pl.BlockSpec((rows, HEAD_DIM), lambda b: (b, 0))
    return pl.pallas_call(
        _make_attn_p_kernel(DILATIONS[g]),
        grid=(BATCH,),
        in_specs=[blk(0), blk(1), blk(2)],
        out_specs=[out, out],
        out_shape=[jax.ShapeDtypeStruct((4 * ROWS_P, HEAD_DIM), F32)] * 2,
        compiler_params=_params(("arbitrary",)),
        name="attn_prompt_g%d" % g,
    )(qkv_p, qkv_p, qkv_p)


NEW_ROWS = DEC_SEQ * HEADS_PER_GROUP


def _finish_sample(parts, o_ref, l_ref, j):
    mx = None
    for s, _ in parts:
        m = jnp.max(s, axis=0) if s.ndim == 3 else s
        mx = m if mx is None else jnp.maximum(mx, m)
    den = jnp.zeros((8, 1), F32)
    acc = jnp.zeros((8, HEAD_DIM), F32)
    for s, v in parts:
        p = jnp.exp(s - mx)
        if s.ndim == 3:
            den = den + jnp.sum(p, axis=0)
            acc = acc + jnp.sum(p * v, axis=0)
        else:
            den = den + p
            acc = acc + p * v
    o_ref[8 * j:8 * j + 8, :] = acc / den
    l_ref[8 * j:8 * j + 8, :] = jnp.broadcast_to(mx + jnp.log(den), (8, HEAD_DIM))


def _score(q, k):
    return jnp.sum(q * k, axis=-1, keepdims=True) * ATT_SCALE


def _attn_s_kernel_d1(q_ref, kn_ref, vn_ref, k_ref, v_ref, ok_ref, ov_ref, o_ref, l_ref, kc, vc):
    nb = WINDOWS[0] * HEADS_PER_GROUP
    kc[0:nb, :] = k_ref[0]
    kc[nb:nb + NEW_ROWS, :] = kn_ref[0]
    vc[0:nb, :] = v_ref[0]
    vc[nb:nb + NEW_ROWS, :] = vn_ref[0]
    ok_ref[0] = kc[NEW_ROWS:nb + NEW_ROWS, :]
    ov_ref[0] = vc[NEW_ROWS:nb + NEW_ROWS, :]
    half = HEADS_PER_GROUP
    n_even, n_odd = SPAN // 2 + 1, SPAN // 2
    for j in range(2):
        q = q_ref[0, 8 * j:8 * j + 8, :]
        parts = []
        for start, n in ((8 * j, n_even), (8 * j + half, n_odd)):
            k3 = kc[pl.ds(start, 8 * n), :].reshape(n, 8, HEAD_DIM)
            v3 = vc[pl.ds(start, 8 * n), :].reshape(n, 8, HEAD_DIM)
            parts.append((_score(q[None], k3), v3))
        _finish_sample(parts, o_ref, l_ref, j)


def _make_attn_s_kernel(dil):
    width = HEADS_PER_GROUP * dil

    def kern(q_ref, kn_ref, vn_ref, k_ref, v_ref, ok_ref, ov_ref, o_ref, l_ref):
        for src, new, dst in ((k_ref, kn_ref, ok_ref), (v_ref, vn_ref, ov_ref)):
            if width == NEW_ROWS:
                dst[0, 0:SPAN - 1] = src[0, 1:SPAN]
                dst[0, SPAN - 1] = new[0]
            else:
                dst[0, :, 0:width - NEW_ROWS, :] = src[0, :, NEW_ROWS:width, :]
                dst[0, 0:SPAN - 1, width - NEW_ROWS:width, :] = src[0, 1:SPAN, 0:NEW_ROWS, :]
                dst[0, SPAN - 1, width - NEW_ROWS:width, :] = new[0]
        for j in range(2):
            q = q_ref[0, 8 * j:8 * j + 8, :]
            k3 = k_ref[0, :, 8 * j:8 * j + 8, :]
            v3 = v_ref[0, :, 8 * j:8 * j + 8, :]
            kn = kn_ref[0, 8 * j:8 * j + 8, :]
            vn = vn_ref[0, 8 * j:8 * j + 8, :]
            parts = [(_score(q[None], k3), v3), (_score(q, kn), vn)]
            _finish_sample(parts, o_ref, l_ref, j)

    return kern


def _attn_s_call(qkv_s, ck, cv, g):
    dil = DILATIONS[g]
    new = lambda t: pl.BlockSpec((1, NEW_ROWS, HEAD_DIM), lambda b, t=t: (t * N_GROUPS + g, b, 0))
    small = pl.BlockSpec((NEW_ROWS, HEAD_DIM), lambda b: (b, 0))
    if dil == 1:
        rows = WINDOWS[g] * HEADS_PER_GROUP
        shape = (DEC_BATCH, rows, HEAD_DIM)
        cache = pl.BlockSpec((1, rows, HEAD_DIM), lambda b: (b, 0, 0))
        kern = _attn_s_kernel_d1
        scratch = [pltpu.VMEM((rows + NEW_ROWS, HEAD_DIM), F32)] * 2
    else:
        width = HEADS_PER_GROUP * dil
        shape = (DEC_BATCH, SPAN, width, HEAD_DIM)
        cache = pl.BlockSpec((1, SPAN, width, HEAD_DIM), lambda b: (b, 0, 0, 0))
        kern = _make_attn_s_kernel(dil)
        scratch = []
    small_shape = jax.ShapeDtypeStruct((4 * ROWS_S, HEAD_DIM), F32)
    return pl.pallas_call(
        kern,
        grid=(DEC_BATCH,),
        in_specs=[new(0), new(1), new(2), cache, cache],
        out_specs=[cache, cache, small, small],
        out_shape=[jax.ShapeDtypeStruct(shape, F32)] * 2 + [small_shape] * 2,
        scratch_shapes=scratch,
        compiler_params=_params(("arbitrary",)),
        name="attn_sample_g%d" % g,
    )(qkv_s, qkv_s, qkv_s, ck.reshape(shape), cv.reshape(shape))


def _mix_kernel(*refs):
    (yp_ref, ys_ref, gp_ref, gs_ref) = refs[0:4]
    op_refs, lp_refs = refs[4:7], refs[7:10]
    os_refs, ls_refs = refs[10:13], refs[13:16]
    wpw_ref, bpw_ref, wo_ref, mp_ref, ms_ref = refs[16:21]
    i = pl.program_id(0)

    def mix(y_ref, g_ref, o_refs, l_refs, m_ref, rows):
        conv = jnp.dot(y_ref[...], wpw_ref[...], preferred_element_type=F32) + bpw_ref[...]
        att = jnp.zeros((rows, D_MODEL), F32)
        for h in range(HEADS_PER_GROUP):
            sl = pl.ds(h, rows, stride=HEADS_PER_GROUP)
            ls = [l[sl, :] for l in l_refs]
            mx = jnp.maximum(jnp.maximum(ls[0], ls[1]), ls[2])
            es = [jnp.exp(l - mx) for l in ls]
            num = es[0] * o_refs[0][sl, :] + es[1] * o_refs[1][sl, :] + es[2] * o_refs[2][sl, :]
            oh = num / (es[0] + es[1] + es[2])
            att = att + jnp.dot(oh.astype(BF16), wo_ref[h * HEAD_DIM:(h + 1) * HEAD_DIM, :],
                                preferred_element_type=F32)
        gates = g_ref[...].astype(F32)
        m_ref[...] = (gates[:, :D_MODEL] * conv + gates[:, D_MODEL:] * att).astype(BF16)

    @pl.when(i < NP2)
    def _():
        mix(yp_ref, gp_ref, op_refs, lp_refs, mp_ref, TM2)

    @pl.when(i == NP2)
    def _():
        mix(ys_ref, gs_ref, os_refs, ls_refs, ms_ref, ROWS_S)


def _mix_call(y_p, y_s, gate_p, gate_s, o_p, l_p, o_s, l_s, w_pw, b_pw, w_o):
    pi = lambda i: (_pidx(i, NP2), 0)
    zero = lambda i: (0, 0)
    hp = pl.BlockSpec((4 * TM2, HEAD_DIM), pi)
    hs = pl.BlockSpec((4 * ROWS_S, HEAD_DIM), zero)
    return pl.pallas_call(
        _mix_kernel,
        grid=(NP2 + 1,),
        in_specs=[
            pl.BlockSpec((TM2, D_CONV), pi), pl.BlockSpec((ROWS_S, D_CONV), zero),
            pl.BlockSpec((TM2, 2 * D_MODEL), pi), pl.BlockSpec((ROWS_S, 2 * D_MODEL), zero),
            hp, hp, hp, hp, hp, hp, hs, hs, hs, hs, hs, hs,
            pl.BlockSpec((D_CONV, D_MODEL), zero), pl.BlockSpec((1, D_MODEL), zero),
            pl.BlockSpec((GROUP_COLS, D_MODEL), zero),
        ],
        out_specs=[pl.BlockSpec((TM2, D_MODEL), pi), pl.BlockSpec((ROWS_S, D_MODEL), zero)],
        out_shape=[jax.ShapeDtypeStruct((ROWS_P, D_MODEL), BF16),
                   jax.ShapeDtypeStruct((ROWS_S, D_MODEL), BF16)],
        compiler_params=_params(("arbitrary",)),
        name="mix",
    )(y_p, y_s, gate_p, gate_s, *o_p, *l_p, *o_s, *l_s, w_pw, b_pw, w_o)


def _outproj_kernel(mp_ref, ms_ref, xp_ref, xs_ref, w_ref, op_ref, os_ref):
    i = pl.program_id(0)

    @pl.when(i < NP2)
    def _():
        op_ref[...] = xp_ref[...] + jnp.dot(mp_ref[...], w_ref[...], preferred_element_type=F32)

    @pl.when(i == NP2)
    def _():
        os_ref[...] = xs_ref[...] + jnp.dot(ms_ref[...], w_ref[...], preferred_element_type=F32)


def _outproj_call(m_p, m_s, xp, xs, w_out):
    pi = lambda i: (_pidx(i, NP2), 0)
    zero = lambda i: (0, 0)
    return pl.pallas_call(
        _outproj_kernel,
        grid=(NP2 + 1,),
        in_specs=[
            pl.BlockSpec((TM2, D_MODEL), pi), pl.BlockSpec((ROWS_S, D_MODEL), zero),
            pl.BlockSpec((TM2, D_MODEL), pi), pl.BlockSpec((ROWS_S, D_MODEL), zero),
            pl.BlockSpec((D_MODEL, D_MODEL), zero),
        ],
        out_specs=[pl.BlockSpec((TM2, D_MODEL), pi), pl.BlockSpec((ROWS_S, D_MODEL), zero)],
        out_shape=[jax.ShapeDtypeStruct((ROWS_P, D_MODEL), F32),
                   jax.ShapeDtypeStruct((ROWS_S, D_MODEL), F32)],
        compiler_params=_params(("arbitrary",)),
        name="out_proj",
    )(m_p, m_s, xp, xs, w_out)


def _ffn_kernel(xp_ref, xs_ref, gf_ref, wg_ref, wu_ref, wd_ref, gl_ref, yp_ref, ys_ref,
                hp_s, hs_s, accp_s, accs_s):
    i = pl.program_id(0)
    k = pl.program_id(1)

    def ffn(x_ref, h_s, acc_s, y_ref):
        @pl.when(k == 0)
        def _():
            h_s[...] = _rms(x_ref[...], gf_ref[...]).astype(BF16)
            acc_s[...] = jnp.zeros(acc_s.shape, F32)

        h = h_s[...]
        a = jnp.dot(h, wg_ref[...], preferred_element_type=F32)
        u = jnp.dot(h, wu_ref[...], preferred_element_type=F32)
        act = (a * _sigmoid(a) * u).astype(BF16)
        acc_s[...] += jnp.dot(act, wd_ref[...], preferred_element_type=F32)

        @pl.when(k == KH - 1)
        def _():
            y_ref[...] = _rms(x_ref[...] + acc_s[...], gl_ref[...])

    @pl.when(i < NP2)
    def _():
        ffn(xp_ref, hp_s, accp_s, yp_ref)

    @pl.when(i == NP2)
    def _():
        ffn(xs_ref, hs_s, accs_s, ys_ref)


def _ffn_call(x1_p, x1_s, g_ffn, w_gate, w_up, w_down, g_final):
    pi = lambda i, k: (_pidx(i, NP2), 0)
    zero = lambda i, k: (0, 0)
    return pl.pallas_call(
        _ffn_kernel,
        grid=(NP2 + 1, KH),
        in_specs=[
            pl.BlockSpec((TM2, D_MODEL), pi), pl.BlockSpec((ROWS_S, D_MODEL), zero),
            pl.BlockSpec((1, D_MODEL), zero),
            pl.BlockSpec((D_MODEL, TH), lambda i, k: (0, k)),
            pl.BlockSpec((D_MODEL, TH), lambda i, k: (0, k)),
            pl.BlockSpec((TH, D_MODEL), lambda i, k: (k, 0)),
            pl.BlockSpec((1, D_MODEL), zero),
        ],
        out_specs=[pl.BlockSpec((TM2, D_MODEL), pi), pl.BlockSpec((ROWS_S, D_MODEL), zero)],
        out_shape=[jax.ShapeDtypeStruct((ROWS_P, D_MODEL), F32),
                   jax.ShapeDtypeStruct((ROWS_S, D_MODEL), F32)],
        scratch_shapes=[pltpu.VMEM((TM2, D_MODEL), BF16), pltpu.VMEM((ROWS_S, D_MODEL), BF16),
                        pltpu.VMEM((TM2, D_MODEL), F32), pltpu.VMEM((ROWS_S, D_MODEL), F32)],
        compiler_params=_params(("arbitrary", "arbitrary")),
        name="ffn",
    )(x1_p, x1_s, g_ffn, w_gate, w_up, w_down, g_final)


def _rope_tables(pos):
    half = ROT_DIM // 2
    inv = jnp.power(ROPE_THETA, -jnp.arange(0, ROT_DIM, 2, dtype=F32) / ROT_DIM)
    ang = pos.astype(F32)[:, None] * inv[None, :]
    cos, sin = jnp.cos(ang), jnp.sin(ang)
    n = pos.shape[0]
    ones = jnp.ones((n, HEAD_DIM - ROT_DIM), F32)
    zeros = jnp.zeros((n, HEAD_DIM - ROT_DIM), F32)
    zh = jnp.zeros((n, half), F32)
    c = jnp.concatenate([cos, cos, ones], axis=1)
    s_lo = jnp.concatenate([-sin, zh, zeros], axis=1)
    s_hi = jnp.concatenate([zh, sin, zeros], axis=1)
    rot = jnp.stack([c, s_lo, s_hi])
    ident = jnp.stack([jnp.ones_like(c), jnp.zeros_like(c), jnp.zeros_like(c)])
    return jnp.stack([rot, ident])


def kernel(x_prompt, x_sample, state_conv, cache_k_w128, cache_v_w128, cache_k_w512, cache_v_w512,
           cache_k_w2048, cache_v_w2048, g_mix, w_in, b_glu, w_dw, b_dw, ln_g, ln_b, w_pw, b_pw,
           w_o_att, w_out, g_ffn, w_gate, w_up, w_down, g_final):
    xp = x_prompt.reshape(ROWS_P, D_MODEL)
    xs = x_sample.reshape(ROWS_S, D_MODEL)
    w_in2 = w_in.reshape(D_MODEL, IN_COLS)

    hp, hs = _rms_call(xp, xs, g_mix)
    u_p, u_s = _glu_call(hp, hs, w_in2, b_glu)
    tab_p = _rope_tables(jnp.arange(SEQ, dtype=jnp.int32))
    tab_s = _rope_tables(PAST_LEN + jnp.arange(ROWS_S, dtype=jnp.int32) % DEC_SEQ)
    qkv_p, qkv_s = _qkv_call(hp, hs, w_in2, tab_p, tab_s)
    gate_p, gate_s = _gate_call(hp, hs, w_in2)

    w_dw2 = w_dw.reshape(CONV_WIDTH, D_CONV)
    y_p = _conv_p_call(u_p, w_dw2, b_dw, ln_g, ln_b)
    ext_s = jnp.concatenate([state_conv[0], u_s.reshape(DEC_BATCH, DEC_SEQ, D_CONV)], axis=1)
    y_s = _conv_s_call(ext_s, w_dw2, b_dw, ln_g, ln_b).reshape(ROWS_S, D_CONV).astype(BF16)
    conv_p = u_p.reshape(BATCH, SEQ, D_CONV)[:, SEQ - (CONV_WIDTH - 1):][None]
    conv_s = ext_s[:, DEC_SEQ:][None]

    caches = ((cache_k_w128, cache_v_w128), (cache_k_w512, cache_v_w512), (cache_k_w2048, cache_v_w2048))
    o_p, l_p, o_s, l_s, cache_out = [], [], [], [], []
    for g in range(N_GROUPS):
        o, l = _attn_p_call(qkv_p, g)
        o_p.append(o)
        l_p.append(l)
        ck, cv, o, l = _attn_s_call(qkv_s, caches[g][0], caches[g][1], g)
        o_s.append(o)
        l_s.append(l)
        keep = min(WINDOWS[g], SEQ)
        new_shape = (1, DEC_BATCH, WINDOWS[g], HEADS_PER_GROUP, HEAD_DIM)
        for t, c in ((1, ck), (2, cv)):
            full = qkv_p[t * N_GROUPS + g].reshape(BATCH, SEQ, HEADS_PER_GROUP, HEAD_DIM)
            cache_out.append(full[:, SEQ - keep:][None])
            cache_out.append(c.reshape(new_shape))

    m_p, m_s = _mix_call(y_p, y_s, gate_p, gate_s, o_p, l_p, o_s, l_s,
                         w_pw.reshape(D_CONV, D_MODEL).astype(BF16), b_pw,
                         w_o_att.reshape(GROUP_COLS, D_MODEL).astype(BF16))
    x1_p, x1_s = _outproj_call(m_p, m_s, xp, xs, w_out.reshape(D_MODEL, D_MODEL).astype(BF16))
    y_p2, y_s2 = _ffn_call(x1_p, x1_s, g_ffn,
                           w_gate.reshape(D_MODEL, FFN_HIDDEN).astype(BF16),
                           w_up.reshape(D_MODEL, FFN_HIDDEN).astype(BF16),
                           w_down.reshape(FFN_HIDDEN, D_MODEL).astype(BF16),
                           g_final.reshape(1, D_MODEL))
    return (y_p2.reshape(BATCH, SEQ, D_MODEL), y_s2.reshape(DEC_BATCH, DEC_SEQ, D_MODEL),
            conv_p, conv_s, *cache_out)
```

```python
import functools

import jax
import jax.numpy as jnp
from jax import lax
from jax.experimental import pallas as pl
from jax.experimental.pallas import tpu as pltpu

F32 = jnp.float32
BF16 = jnp.bfloat16

D_MODEL = 2048
BATCH = 4
SEQ = 2048
DEC_BATCH = 32
DEC_SEQ = 4
PAST_LEN = 8192
HEAD_DIM = 128
HEADS_PER_GROUP = 4
DILATIONS = (1, 4, 16)
WINDOWS = (128, 512, 2048)
N_GROUPS = 3
GROUP_COLS = HEADS_PER_GROUP * HEAD_DIM
ATT_WIDTH = N_GROUPS * GROUP_COLS
SPAN = 128
ATT_SCALE = HEAD_DIM ** -0.5
ROT_DIM = HEAD_DIM // 4
ROPE_THETA = 500000.0
D_CONV = D_MODEL // 2
CONV_WIDTH = 31
FFN_HIDDEN = 5632
NORM_EPS = 1e-6
LN_EPS = 1e-5
IN_COLS = 2 * D_CONV + 3 * ATT_WIDTH + 2 * D_MODEL

ROWS_P = BATCH * SEQ
ROWS_S = DEC_BATCH * DEC_SEQ
NEG = -1e30

VMEM_LIMIT = 56 * 1024 * 1024
FFN_VMEM_LIMIT = 60 * 1024 * 1024

TM = 1024
NP = ROWS_P // TM
TN = 512
NC = 256
TM2 = 512
NP2 = ROWS_P // TM2
TH = 512
KH = FFN_HIDDEN // TH
TC = 256
HALO = 32


def _sigmoid(x):
    return 1.0 / (1.0 + jnp.exp(-x))


def _rms(x, g):
    return x * lax.rsqrt(jnp.mean(x * x, axis=-1, keepdims=True) + NORM_EPS) * g


def _params(sem, vmem_limit=VMEM_LIMIT):
    return pltpu.CompilerParams(dimension_semantics=sem, vmem_limit_bytes=vmem_limit)


def _pidx(i, n):
    return jnp.minimum(i, n - 1)


def _rms_kernel(xp_ref, xs_ref, g_ref, hp_ref, hs_ref):
    i = pl.program_id(0)

    @pl.when(i < NP)
    def _():
        hp_ref[...] = _rms(xp_ref[...], g_ref[...]).astype(BF16)

    @pl.when(i == NP)
    def _():
        hs_ref[...] = _rms(xs_ref[...], g_ref[...]).astype(BF16)


def _rms_call(xp, xs, g):
    return pl.pallas_call(
        _rms_kernel,
        grid=(NP + 1,),
        in_specs=[
            pl.BlockSpec((TM, D_MODEL), lambda i: (_pidx(i, NP), 0)),
            pl.BlockSpec((ROWS_S, D_MODEL), lambda i: (0, 0)),
            pl.BlockSpec((1, D_MODEL), lambda i: (0, 0)),
        ],
        out_specs=[
            pl.BlockSpec((TM, D_MODEL), lambda i: (_pidx(i, NP), 0)),
            pl.BlockSpec((ROWS_S, D_MODEL), lambda i: (0, 0)),
        ],
        out_shape=[
            jax.ShapeDtypeStruct((ROWS_P, D_MODEL), BF16),
            jax.ShapeDtypeStruct((ROWS_S, D_MODEL), BF16),
        ],
        compiler_params=_params(("arbitrary",)),
        name="rms_in",
    )(xp, xs, g)


def _glu_kernel(hp_ref, hs_ref, wa_ref, wb_ref, ba_ref, bb_ref, up_ref, us_ref, wa_s, wb_s):
    i = pl.program_id(1)

    @pl.when(i == 0)
    def _():
        wa_s[...] = wa_ref[...].astype(BF16)
        wb_s[...] = wb_ref[...].astype(BF16)

    def glu(h, u_ref):
        for c in range(TN // NC):
            sl = slice(c * NC, (c + 1) * NC)
            za = jnp.dot(h, wa_s[:, sl], preferred_element_type=F32) + ba_ref[:, sl]
            zb = jnp.dot(h, wb_s[:, sl], preferred_element_type=F32) + bb_ref[:, sl]
            u_ref[:, sl] = za * _sigmoid(zb)

    @pl.when(i < NP)
    def _():
        glu(hp_ref[...], up_ref)

    @pl.when(i == NP)
    def _():
        glu(hs_ref[...], us_ref)


def _glu_call(hp, hs, w_in, b_glu):
    nj = D_CONV // TN
    return pl.pallas_call(
        _glu_kernel,
        grid=(nj, NP + 1),
        in_specs=[
            pl.BlockSpec((TM, D_MODEL), lambda j, i: (_pidx(i, NP), 0)),
            pl.BlockSpec((ROWS_S, D_MODEL), lambda j, i: (0, 0)),
            pl.BlockSpec((D_MODEL, TN), lambda j, i: (0, j)),
            pl.BlockSpec((D_MODEL, TN), lambda j, i: (0, j + nj)),
            pl.BlockSpec((1, TN), lambda j, i: (0, j)),
            pl.BlockSpec((1, TN), lambda j, i: (0, j + nj)),
        ],
        out_specs=[
            pl.BlockSpec((TM, TN), lambda j, i: (_pidx(i, NP), j)),
            pl.BlockSpec((ROWS_S, TN), lambda j, i: (0, j)),
        ],
        out_shape=[
            jax.ShapeDtypeStruct((ROWS_P, D_CONV), F32),
            jax.ShapeDtypeStruct((ROWS_S, D_CONV), F32),
        ],
        scratch_shapes=[pltpu.VMEM((D_MODEL, TN), BF16), pltpu.VMEM((D_MODEL, TN), BF16)],
        compiler_params=_params(("arbitrary", "arbitrary")),
        name="in_glu",
    )(hp, hs, w_in, w_in, b_glu, b_glu)


def _qkv_kernel(hp_ref, hs_ref, w_ref, tp_ref, ts_ref, op_ref, os_ref, w_s):
    i = pl.program_id(1)

    @pl.when(i == 0)
    def _():
        w_s[...] = w_ref[...].astype(BF16)

    def project(h, t_ref, o_ref, rows):
        cos, sin_lo, sin_hi = t_ref[0, 0], t_ref[0, 1], t_ref[0, 2]
        for c in range(TN // NC):
            z = jnp.dot(h, w_s[:, c * NC:(c + 1) * NC], preferred_element_type=F32)
            for hc in range(NC // HEAD_DIM):
                t = z[:, hc * HEAD_DIM:(hc + 1) * HEAD_DIM]
                t = (t * cos + pltpu.roll(t, HEAD_DIM - ROT_DIM // 2, 1) * sin_lo
                     + pltpu.roll(t, ROT_DIM // 2, 1) * sin_hi)
                head = c * (NC // HEAD_DIM) + hc
                o_ref[0, pl.ds(head, rows, stride=HEADS_PER_GROUP), :] = t

    @pl.when(i < NP)
    def _():
        project(hp_ref[...], tp_ref, op_ref, TM)

    @pl.when(i == NP)
    def _():
        project(hs_ref[...], ts_ref, os_ref, ROWS_S)


def _qkv_call(hp, hs, w_in, tab_p, tab_s, first_plane, nj, kind):
    off = 2 * D_CONV // TN + first_plane
    return pl.pallas_call(
        _qkv_kernel,
        grid=(nj, NP + 1),
        in_specs=[
            pl.BlockSpec((TM, D_MODEL), lambda j, i: (_pidx(i, NP), 0)),
            pl.BlockSpec((ROWS_S, D_MODEL), lambda j, i: (0, 0)),
            pl.BlockSpec((D_MODEL, TN), lambda j, i: (0, j + off)),
            pl.BlockSpec((1, 3, TM, HEAD_DIM), lambda j, i: (kind, 0, _pidx(i, NP) % (SEQ // TM), 0)),
            pl.BlockSpec((1, 3, ROWS_S, HEAD_DIM), lambda j, i: (kind, 0, 0, 0)),
        ],
        out_specs=[
            pl.BlockSpec((1, 4 * TM, HEAD_DIM), lambda j, i: (j, _pidx(i, NP), 0)),
            pl.BlockSpec((1, 4 * ROWS_S, HEAD_DIM), lambda j, i: (j, 0, 0)),
        ],
        out_shape=[
            jax.ShapeDtypeStruct((nj, 4 * ROWS_P, HEAD_DIM), F32),
            jax.ShapeDtypeStruct((nj, 4 * ROWS_S, HEAD_DIM), F32),
        ],
        scratch_shapes=[pltpu.VMEM((D_MODEL, TN), BF16)],
        compiler_params=_params(("arbitrary", "arbitrary")),
        name="in_qkv%d" % first_plane,
    )(hp, hs, w_in, tab_p, tab_s)


def _gate_kernel(hp_ref, hs_ref, wa_ref, wb_ref, op_ref, os_ref, w_s):
    i = pl.program_id(1)

    @pl.when(i == 0)
    def _():
        w_s[:, 0:TN] = wa_ref[...].astype(BF16)
        w_s[:, TN:2 * TN] = wb_ref[...].astype(BF16)

    def gate(h, o_ref):
        for c in range(2 * TN // NC):
            sl = slice(c * NC, (c + 1) * NC)
            o_ref[:, sl] = _sigmoid(jnp.dot(h, w_s[:, sl], preferred_element_type=F32)).astype(BF16)

    @pl.when(i < NP)
    def _():
        gate(hp_ref[...], op_ref)

    @pl.when(i == NP)
    def _():
        gate(hs_ref[...], os_ref)


def _gate_call(hp, hs, w_in):
    nj = 2 * D_MODEL // (2 * TN)
    off = (2 * D_CONV + 3 * ATT_WIDTH) // TN
    return pl.pallas_call(
        _gate_kernel,
        grid=(nj, NP + 1),
        in_specs=[
            pl.BlockSpec((TM, D_MODEL), lambda j, i: (_pidx(i, NP), 0)),
            pl.BlockSpec((ROWS_S, D_MODEL), lambda j, i: (0, 0)),
            pl.BlockSpec((D_MODEL, TN), lambda j, i: (0, 2 * j + off)),
            pl.BlockSpec((D_MODEL, TN), lambda j, i: (0, 2 * j + 1 + off)),
        ],
        out_specs=[
            pl.BlockSpec((TM, 2 * TN), lambda j, i: (_pidx(i, NP), j)),
            pl.BlockSpec((ROWS_S, 2 * TN), lambda j, i: (0, j)),
        ],
        out_shape=[
            jax.ShapeDtypeStruct((ROWS_P, 2 * D_MODEL), BF16),
            jax.ShapeDtypeStruct((ROWS_S, 2 * D_MODEL), BF16),
        ],
        scratch_shapes=[pltpu.VMEM((D_MODEL, 2 * TN), BF16)],
        compiler_params=_params(("arbitrary", "arbitrary")),
        name="in_gate",
    )(hp, hs, w_in, w_in)


def _ln_silu(y, g, b):
    mu = jnp.mean(y, axis=-1, keepdims=True)
    yc = y - mu
    var = jnp.mean(yc * yc, axis=-1, keepdims=True)
    z = yc * lax.rsqrt(var + LN_EPS) * g + b
    return z * _sigmoid(z)


def _conv_p_kernel(cur_ref, prev_ref, w_ref, bdw_ref, g_ref, b_ref, y_ref, ext, ypre):
    c = pl.program_id(1)
    n_lc = D_CONV // 128
    for lc in range(n_lc):
        sl = slice(lc * 128, (lc + 1) * 128)
        ext[lc, pl.ds(HALO, TC), :] = cur_ref[:, sl]

    @pl.when(c > 0)
    def _():
        for lc in range(n_lc):
            ext[lc, 0:HALO, :] = prev_ref[TC - HALO:TC, lc * 128:(lc + 1) * 128]

    @pl.when(c == 0)
    def _():
        for lc in range(n_lc):
            ext[lc, 0:HALO, :] = jnp.zeros((HALO, 128), F32)

    base = HALO - (CONV_WIDTH - 1)
    for lc in range(n_lc):
        sl = slice(lc * 128, (lc + 1) * 128)
        for t0 in range(8):
            acc = jnp.zeros((TC // 8, 128), F32)
            for j in range(CONV_WIDTH):
                acc = acc + ext[lc, pl.ds(base + t0 + j, TC // 8, stride=8), :] * w_ref[j:j + 1, sl]
            ypre[lc, pl.ds(t0, TC // 8, stride=8), :] = acc + bdw_ref[:, sl]

    total = jnp.zeros((TC, 1), F32)
    for lc in range(n_lc):
        total = total + jnp.sum(ypre[lc], axis=-1, keepdims=True)
    mu = total * (1.0 / D_CONV)
    sq = jnp.zeros((TC, 1), F32)
    for lc in range(n_lc):
        yc = ypre[lc] - mu
        sq = sq + jnp.sum(yc * yc, axis=-1, keepdims=True)
    inv = lax.rsqrt(sq * (1.0 / D_CONV) + LN_EPS)
    for lc in range(n_lc):
        sl = slice(lc * 128, (lc + 1) * 128)
        z = (ypre[lc] - mu) * inv * g_ref[:, sl] + b_ref[:, sl]
        y_ref[:, sl] = (z * _sigmoid(z)).astype(BF16)


def _conv_p_call(u_p, w_dw, b_dw, ln_g, ln_b):
    nc = SEQ // TC
    vec = pl.BlockSpec((1, D_CONV), lambda b, c: (0, 0))
    return pl.pallas_call(
        _conv_p_kernel,
        grid=(BATCH, nc),
        in_specs=[
            pl.BlockSpec((TC, D_CONV), lambda b, c: (b * nc + c, 0)),
            pl.BlockSpec((TC, D_CONV), lambda b, c: (b * nc + jnp.maximum(c - 1, 0), 0)),
            pl.BlockSpec((CONV_WIDTH, D_CONV), lambda b, c: (0, 0)),
            vec, vec, vec,
        ],
        out_specs=pl.BlockSpec((TC, D_CONV), lambda b, c: (b * nc + c, 0)),
        out_shape=jax.ShapeDtypeStruct((ROWS_P, D_CONV), BF16),
        scratch_shapes=[pltpu.VMEM((D_CONV // 128, HALO + TC, 128), F32),
                        pltpu.VMEM((D_CONV // 128, TC, 128), F32)],
        compiler_params=_params(("arbitrary", "arbitrary")),
        name="conv_prompt",
    )(u_p, u_p, w_dw, b_dw, ln_g, ln_b)


CONV_SB = 8


def _conv_s_kernel(ext_ref, w_ref, bdw_ref, g_ref, b_ref, y_ref):
    acc = jnp.zeros((CONV_SB, DEC_SEQ, D_CONV), F32)
    for j in range(CONV_WIDTH):
        acc = acc + ext_ref[:, pl.ds(j, DEC_SEQ), :] * w_ref[j:j + 1, :]
    y_ref[...] = _ln_silu(acc + bdw_ref[...], g_ref[...], b_ref[...])


def _conv_s_call(ext_s, w_dw, b_dw, ln_g, ln_b):
    vec = pl.BlockSpec((1, D_CONV), lambda b: (0, 0))
    rows = CONV_WIDTH - 1 + DEC_SEQ
    return pl.pallas_call(
        _conv_s_kernel,
        grid=(DEC_BATCH // CONV_SB,),
        in_specs=[
            pl.BlockSpec((CONV_SB, rows, D_CONV), lambda b: (b, 0, 0)),
            pl.BlockSpec((CONV_WIDTH, D_CONV), lambda b: (0, 0)),
            vec, vec, vec,
        ],
        out_specs=pl.BlockSpec((CONV_SB, DEC_SEQ, D_CONV), lambda b: (b, 0, 0)),
        out_shape=jax.ShapeDtypeStruct((DEC_BATCH, DEC_SEQ, D_CONV), F32),
        compiler_params=_params(("arbitrary",)),
        name="conv_sample",
    )(ext_s, w_dw, b_dw, ln_g, ln_b)


def _softmax_block(q, k, v, mask):
    s = lax.dot_general(q.astype(BF16), k.astype(BF16), (((2,), (2,)), ((0,), (0,))),
                        preferred_element_type=F32) * ATT_SCALE
    s = jnp.where(mask[None], s, NEG)
    m = jnp.max(s, axis=-1, keepdims=True)
    p = jnp.exp(s - m)
    den = jnp.sum(p, axis=-1, keepdims=True)
    o = lax.dot_general(p.astype(BF16), v.astype(BF16), (((2,), (1,)), ((0,), (0,))),
                        preferred_element_type=F32) / den
    lse = m + jnp.log(den)
    return o, jnp.broadcast_to(lse, o.shape)


def _make_attn_p_kernel(dil):
    stride = HEADS_PER_GROUP * dil
    nblk = SEQ // dil // SPAN
    blk_rows = SPAN * stride
    pair = min(dil, 2)
    group_rows = pair * HEADS_PER_GROUP
    n_r2 = dil // pair

    def kern(q_ref, k_ref, v_ref, o_ref, l_ref):
        qi = lax.broadcasted_iota(jnp.int32, (SPAN, SPAN), 0)
        ki = lax.broadcasted_iota(jnp.int32, (SPAN, SPAN), 1)
        mask_first = ki <= qi
        qi2 = lax.broadcasted_iota(jnp.int32, (SPAN, 2 * SPAN), 0)
        ki2 = lax.broadcasted_iota(jnp.int32, (SPAN, 2 * SPAN), 1)
        mask_band = (ki2 >= qi2) & (ki2 <= qi2 + SPAN)

        def block(base, key_base, n_keys, mask):
            qs = [pl.ds(base + off, SPAN, stride=stride) for off in range(group_rows)]
            ks = [pl.ds(key_base + off, n_keys, stride=stride) for off in range(group_rows)]
            o, lse = _softmax_block(jnp.stack([q_ref[0, s, :] for s in qs]),
                                    jnp.stack([k_ref[0, s, :] for s in ks]),
                                    jnp.stack([v_ref[0, s, :] for s in ks]), mask)
            for u, s in enumerate(qs):
                o_ref[s, :] = o[u]
                l_ref[s, :] = lse[u]

        def first(r2, carry):
            base = pl.multiple_of(r2 * group_rows, max(group_rows, 8))
            block(base, base, SPAN, mask_first)
            return carry

        def band(t, carry):
            jb = 1 + t // n_r2
            base = pl.multiple_of(jb * blk_rows + (t % n_r2) * group_rows, max(group_rows, 8))
            block(base, base - blk_rows, 2 * SPAN, mask_band)
            return carry

        lax.fori_loop(0, n_r2, first, 0)
        if nblk > 1:
            lax.fori_loop(0, n_r2 * (nblk - 1), band, 0)

    return kern


def _attn_p_wide_kernel(q_ref, k_ref, v_ref, o_ref, l_ref):
    qi = lax.broadcasted_iota(jnp.int32, (SPAN, SPAN), 0)
    ki = lax.broadcasted_iota(jnp.int32, (SPAN, SPAN), 1)
    mask_first = ki <= qi

    def tile(c, carry):
        q8 = jnp.swapaxes(q_ref[0, 0, :, c], 0, 1)
        k8 = jnp.swapaxes(k_ref[0, 0, :, c], 0, 1)
        v8 = jnp.swapaxes(v_ref[0, 0, :, c], 0, 1)
        o, lse = _softmax_block(q8, k8, v8, mask_first)
        o_ref[0, :, c] = jnp.swapaxes(o, 0, 1)
        l_ref[0, :, c] = jnp.swapaxes(lse, 0, 1)
        return carry

    lax.fori_loop(0, q_ref.shape[3], tile, 0)


def _attn_p_call(q_p, k_p, v_p, g):
    dil = DILATIONS[g]
    rows = 4 * SEQ
    if SEQ // dil == SPAN:
        tiles = HEADS_PER_GROUP * dil // 8
        view = lambda a: a.reshape(a.shape[0], BATCH, SPAN, tiles, 8, HEAD_DIM)
        blk = lambda plane: pl.BlockSpec((1, 1, SPAN, tiles, 8, HEAD_DIM), lambda b: (plane, b, 0, 0, 0, 0))
        out = pl.BlockSpec((1, SPAN, tiles, 8, HEAD_DIM), lambda b: (b, 0, 0, 0, 0))
        o, l = pl.pallas_call(
            _attn_p_wide_kernel,
            grid=(BATCH,),
            in_specs=[blk(g), blk(0), blk(0)],
            out_specs=[out, out],
            out_shape=[jax.ShapeDtypeStruct((BATCH, SPAN, tiles, 8, HEAD_DIM), F32)] * 2,
            compiler_params=_params(("arbitrary",)),
            name="attn_prompt_g%d" % g,
        )(view(q_p), view(k_p), view(v_p))
        return o.reshape(4 * ROWS_P, HEAD_DIM), l.reshape(4 * ROWS_P, HEAD_DIM)
    blk = lambda plane: pl.BlockSpec((1, rows, HEAD_DIM), lambda b: (plane, b, 0))
    out = pl.BlockSpec((rows, HEAD_DIM), lambda b: (b, 0))
    return pl.pallas_call(
        _make_attn_p_kernel(dil),
        grid=(BATCH,),
        in_specs=[blk(g), blk(0), blk(0)],
        out_specs=[out, out],
        out_shape=[jax.ShapeDtypeStruct((4 * ROWS_P, HEAD_DIM), F32)] * 2,
        compiler_params=_params(("arbitrary",)),
        name="attn_prompt_g%d" % g,
    )(q_p, k_p, v_p)


NEW_ROWS = DEC_SEQ * HEADS_PER_GROUP


def _finish_sample(parts, o_ref, l_ref, j):
    mx = None
    for s, _ in parts:
        m = jnp.max(s, axis=0) if s.ndim == 3 else s
        mx = m if mx is None else jnp.maximum(mx, m)
    den = jnp.zeros((8, 1), F32)
    acc = jnp.zeros((8, HEAD_DIM), F32)
    for s, v in parts:
        p = jnp.exp(s - mx)
        if s.ndim == 3:
            den = den + jnp.sum(p, axis=0)
            acc = acc + jnp.sum(p * v, axis=0)
        else:
            den = den + p
            acc = acc + p * v
    o_ref[8 * j:8 * j + 8, :] = acc / den
    l_ref[8 * j:8 * j + 8, :] = jnp.broadcast_to(mx + jnp.log(den), (8, HEAD_DIM))


def _score(q, k):
    return jnp.sum(q * k, axis=-1, keepdims=True) * ATT_SCALE


def _attn_s_kernel_d1(q_ref, kn_ref, vn_ref, k_ref, v_ref, ok_ref, ov_ref, o_ref, l_ref, kc, vc):
    nb = WINDOWS[0] * HEADS_PER_GROUP
    kc[0:nb, :] = k_ref[0]
    kc[nb:nb + NEW_ROWS, :] = kn_ref[0]
    vc[0:nb, :] = v_ref[0]
    vc[nb:nb + NEW_ROWS, :] = vn_ref[0]
    ok_ref[0] = kc[NEW_ROWS:nb + NEW_ROWS, :]
    ov_ref[0] = vc[NEW_ROWS:nb + NEW_ROWS, :]
    half = HEADS_PER_GROUP
    n_even, n_odd = SPAN // 2 + 1, SPAN // 2
    for j in range(2):
        q = q_ref[0, 8 * j:8 * j + 8, :]
        parts = []
        for start, n in ((8 * j, n_even), (8 * j + half, n_odd)):
            k3 = kc[pl.ds(start, 8 * n), :].reshape(n, 8, HEAD_DIM)
            v3 = vc[pl.ds(start, 8 * n), :].reshape(n, 8, HEAD_DIM)
            parts.append((_score(q[None], k3), v3))
        _finish_sample(parts, o_ref, l_ref, j)


def _make_attn_s_kernel(dil):
    width = HEADS_PER_GROUP * dil

    def kern(q_ref, kn_ref, vn_ref, k_ref, v_ref, ok_ref, ov_ref, o_ref, l_ref):
        for src, new, dst in ((k_ref, kn_ref, ok_ref), (v_ref, vn_ref, ov_ref)):
            if width == NEW_ROWS:
                dst[0, 0:SPAN - 1] = src[0, 1:SPAN]
                dst[0, SPAN - 1] = new[0]
            else:
                dst[0, :, 0:width - NEW_ROWS, :] = src[0, :, NEW_ROWS:width, :]
                dst[0, 0:SPAN - 1, width - NEW_ROWS:width, :] = src[0, 1:SPAN, 0:NEW_ROWS, :]
                dst[0, SPAN - 1, width - NEW_ROWS:width, :] = new[0]
        for j in range(2):
            q = q_ref[0, 8 * j:8 * j + 8, :]
            k3 = k_ref[0, :, 8 * j:8 * j + 8, :]
            v3 = v_ref[0, :, 8 * j:8 * j + 8, :]
            kn = kn_ref[0, 8 * j:8 * j + 8, :]
            vn = vn_ref[0, 8 * j:8 * j + 8, :]
            parts = [(_score(q[None], k3), v3), (_score(q, kn), vn)]
            _finish_sample(parts, o_ref, l_ref, j)

    return kern


def _attn_s_call(q_s, k_s, v_s, ck, cv, g):
    dil = DILATIONS[g]
    new = lambda plane: pl.BlockSpec((1, NEW_ROWS, HEAD_DIM), lambda b: (plane, b, 0))
    small = pl.BlockSpec((NEW_ROWS, HEAD_DIM), lambda b: (b, 0))
    if dil == 1:
        rows = WINDOWS[g] * HEADS_PER_GROUP
        shape = (DEC_BATCH, rows, HEAD_DIM)
        cache = pl.BlockSpec((1, rows, HEAD_DIM), lambda b: (b, 0, 0))
        kern = _attn_s_kernel_d1
        scratch = [pltpu.VMEM((rows + NEW_ROWS, HEAD_DIM), F32)] * 2
    else:
        width = HEADS_PER_GROUP * dil
        shape = (DEC_BATCH, SPAN, width, HEAD_DIM)
        cache = pl.BlockSpec((1, SPAN, width, HEAD_DIM), lambda b: (b, 0, 0, 0))
        kern = _make_attn_s_kernel(dil)
        scratch = []
    small_shape = jax.ShapeDtypeStruct((4 * ROWS_S, HEAD_DIM), F32)
    return pl.pallas_call(
        kern,
        grid=(DEC_BATCH,),
        in_specs=[new(g), new(0), new(0), cache, cache],
        out_specs=[cache, cache, small, small],
        out_shape=[jax.ShapeDtypeStruct(shape, F32)] * 2 + [small_shape] * 2,
        scratch_shapes=scratch,
        compiler_params=_params(("arbitrary",)),
        name="attn_sample_g%d" % g,
    )(q_s, k_s, v_s, ck.reshape(shape), cv.reshape(shape))


def _mix_kernel(*refs):
    (yp_ref, ys_ref, gp_ref, gs_ref) = refs[0:4]
    op_refs, lp_refs = refs[4:7], refs[7:10]
    os_refs, ls_refs = refs[10:13], refs[13:16]
    wpw_ref, bpw_ref, wo_ref, mp_ref, ms_ref = refs[16:21]
    i = pl.program_id(0)

    def mix(y_ref, g_ref, o_refs, l_refs, m_ref, rows):
        conv = jnp.dot(y_ref[...], wpw_ref[...], preferred_element_type=F32) + bpw_ref[...]
        att = jnp.zeros((rows, D_MODEL), F32)
        for h in range(HEADS_PER_GROUP):
            sl = pl.ds(h, rows, stride=HEADS_PER_GROUP)
            ls = [l[sl, :] for l in l_refs]
            mx = jnp.maximum(jnp.maximum(ls[0], ls[1]), ls[2])
            es = [jnp.exp(l - mx) for l in ls]
            num = es[0] * o_refs[0][sl, :] + es[1] * o_refs[1][sl, :] + es[2] * o_refs[2][sl, :]
            oh = num / (es[0] + es[1] + es[2])
            att = att + jnp.dot(oh.astype(BF16), wo_ref[h * HEAD_DIM:(h + 1) * HEAD_DIM, :],
                                preferred_element_type=F32)
        gates = g_ref[...].astype(F32)
        m_ref[...] = (gates[:, :D_MODEL] * conv + gates[:, D_MODEL:] * att).astype(BF16)

    @pl.when(i < NP2)
    def _():
        mix(yp_ref, gp_ref, op_refs, lp_refs, mp_ref, TM2)

    @pl.when(i == NP2)
    def _():
        mix(ys_ref, gs_ref, os_refs, ls_refs, ms_ref, ROWS_S)


def _mix_call(y_p, y_s, gate_p, gate_s, o_p, l_p, o_s, l_s, w_pw, b_pw, w_o):
    pi = lambda i: (_pidx(i, NP2), 0)
    zero = lambda i: (0, 0)
    hp = pl.BlockSpec((4 * TM2, HEAD_DIM), pi)
    hs = pl.BlockSpec((4 * ROWS_S, HEAD_DIM), zero)
    return pl.pallas_call(
        _mix_kernel,
        grid=(NP2 + 1,),
        in_specs=[
            pl.BlockSpec((TM2, D_CONV), pi), pl.BlockSpec((ROWS_S, D_CONV), zero),
            pl.BlockSpec((TM2, 2 * D_MODEL), pi), pl.BlockSpec((ROWS_S, 2 * D_MODEL), zero),
            hp, hp, hp, hp, hp, hp, hs, hs, hs, hs, hs, hs,
            pl.BlockSpec((D_CONV, D_MODEL), zero), pl.BlockSpec((1, D_MODEL), zero),
            pl.BlockSpec((GROUP_COLS, D_MODEL), zero),
        ],
        out_specs=[pl.BlockSpec((TM2, D_MODEL), pi), pl.BlockSpec((ROWS_S, D_MODEL), zero)],
        out_shape=[jax.ShapeDtypeStruct((ROWS_P, D_MODEL), BF16),
                   jax.ShapeDtypeStruct((ROWS_S, D_MODEL), BF16)],
        compiler_params=_params(("arbitrary",)),
        name="mix",
    )(y_p, y_s, gate_p, gate_s, *o_p, *l_p, *o_s, *l_s, w_pw, b_pw, w_o)


def _outproj_kernel(mp_ref, ms_ref, xp_ref, xs_ref, w_ref, op_ref, os_ref):
    i = pl.program_id(0)

    @pl.when(i < NP2)
    def _():
        op_ref[...] = xp_ref[...] + jnp.dot(mp_ref[...], w_ref[...], preferred_element_type=F32)

    @pl.when(i == NP2)
    def _():
        os_ref[...] = xs_ref[...] + jnp.dot(ms_ref[...], w_ref[...], preferred_element_type=F32)


def _outproj_call(m_p, m_s, xp, xs, w_out):
    pi = lambda i: (_pidx(i, NP2), 0)
    zero = lambda i: (0, 0)
    return pl.pallas_call(
        _outproj_kernel,
        grid=(NP2 + 1,),
        in_specs=[
            pl.BlockSpec((TM2, D_MODEL), pi), pl.BlockSpec((ROWS_S, D_MODEL), zero),
            pl.BlockSpec((TM2, D_MODEL), pi), pl.BlockSpec((ROWS_S, D_MODEL), zero),
            pl.BlockSpec((D_MODEL, D_MODEL), zero),
        ],
        out_specs=[pl.BlockSpec((TM2, D_MODEL), pi), pl.BlockSpec((ROWS_S, D_MODEL), zero)],
        out_shape=[jax.ShapeDtypeStruct((ROWS_P, D_MODEL), F32),
                   jax.ShapeDtypeStruct((ROWS_S, D_MODEL), F32)],
        compiler_params=_params(("arbitrary",)),
        name="out_proj",
    )(m_p, m_s, xp, xs, w_out)


def _ffn_kernel(xp_ref, xs_ref, gf_ref, wg_ref, wu_ref, wd_ref, gl_ref, yp_ref, ys_ref, hp_s, hs_s):
    i = pl.program_id(0)
    k = pl.program_id(1)

    def ffn(x_ref, h_s, y_ref):
        @pl.when(k == 0)
        def _():
            h_s[...] = _rms(x_ref[...], gf_ref[...]).astype(BF16)
            y_ref[...] = jnp.zeros(y_ref.shape, F32)

        h = h_s[...]
        a = jnp.dot(h, wg_ref[...], preferred_element_type=F32)
        u = jnp.dot(h, wu_ref[...], preferred_element_type=F32)
        act = (a * _sigmoid(a) * u).astype(BF16)
        y_ref[...] += jnp.dot(act, wd_ref[...], preferred_element_type=F32)

        @pl.when(k == KH - 1)
        def _():
            y_ref[...] = _rms(x_ref[...] + y_ref[...], gl_ref[...])

    @pl.when(i < NP)
    def _():
        ffn(xp_ref, hp_s, yp_ref)

    @pl.when(i == NP)
    def _():
        ffn(xs_ref, hs_s, ys_ref)


def _ffn_call(x1_p, x1_s, g_ffn, w_gate, w_up, w_down, g_final):
    pi = lambda i, k: (_pidx(i, NP), 0)
    zero = lambda i, k: (0, 0)
    return pl.pallas_call(
        _ffn_kernel,
        grid=(NP + 1, KH),
        in_specs=[
            pl.BlockSpec((TM, D_MODEL), pi, pipeline_mode=pl.Buffered(1)),
            pl.BlockSpec((ROWS_S, D_MODEL), zero, pipeline_mode=pl.Buffered(1)),
            pl.BlockSpec((1, D_MODEL), zero),
            pl.BlockSpec((D_MODEL, TH), lambda i, k: (0, k)),
            pl.BlockSpec((D_MODEL, TH), lambda i, k: (0, k)),
            pl.BlockSpec((TH, D_MODEL), lambda i, k: (k, 0)),
            pl.BlockSpec((1, D_MODEL), zero),
        ],
        out_specs=[pl.BlockSpec((TM, D_MODEL), pi), pl.BlockSpec((ROWS_S, D_MODEL), zero)],
        out_shape=[jax.ShapeDtypeStruct((ROWS_P, D_MODEL), F32),
                   jax.ShapeDtypeStruct((ROWS_S, D_MODEL), F32)],
        scratch_shapes=[pltpu.VMEM((TM, D_MODEL), BF16), pltpu.VMEM((ROWS_S, D_MODEL), BF16)],
        compiler_params=_params(("arbitrary", "arbitrary"), FFN_VMEM_LIMIT),
        name="ffn",
    )(x1_p, x1_s, g_ffn, w_gate, w_up, w_down, g_final)


def _rope_tables(pos):
    half = ROT_DIM // 2
    inv = jnp.power(ROPE_THETA, -jnp.arange(0, ROT_DIM, 2, dtype=F32) / ROT_DIM)
    ang = pos.astype(F32)[:, None] * inv[None, :]
    cos, sin = jnp.cos(ang), jnp.sin(ang)
    n = pos.shape[0]
    ones = jnp.ones((n, HEAD_DIM - ROT_DIM), F32)
    zeros = jnp.zeros((n, HEAD_DIM - ROT_DIM), F32)
    zh = jnp.zeros((n, half), F32)
    c = jnp.concatenate([cos, cos, ones], axis=1)
    s_lo = jnp.concatenate([-sin, zh, zeros], axis=1)
    s_hi = jnp.concatenate([zh, sin, zeros], axis=1)
    rot = jnp.stack([c, s_lo, s_hi])
    ident = jnp.stack([jnp.ones_like(c), jnp.zeros_like(c), jnp.zeros_like(c)])
    return jnp.stack([rot, ident])


def kernel(x_prompt, x_sample, state_conv, cache_k_w128, cache_v_w128, cache_k_w512, cache_v_w512,
           cache_k_w2048, cache_v_w2048, g_mix, w_in, b_glu, w_dw, b_dw, ln_g, ln_b, w_pw, b_pw,
           w_o_att, w_out, g_ffn, w_gate, w_up, w_down, g_final):
    xp = x_prompt.reshape(ROWS_P, D_MODEL)
    xs = x_sample.reshape(ROWS_S, D_MODEL)
    w_in2 = w_in.reshape(D_MODEL, IN_COLS)

    hp, hs = _rms_call(xp, xs, g_mix)
    u_p, u_s = _glu_call(hp, hs, w_in2, b_glu)
    tab_p = _rope_tables(jnp.arange(SEQ, dtype=jnp.int32))
    tab_s = _rope_tables(PAST_LEN + jnp.arange(ROWS_S, dtype=jnp.int32) % DEC_SEQ)
    q_p, q_s = _qkv_call(hp, hs, w_in2, tab_p, tab_s, 0, N_GROUPS, 0)
    kv = [[_qkv_call(hp, hs, w_in2, tab_p, tab_s, (1 + t) * N_GROUPS + g, 1, t) for g in range(N_GROUPS)]
          for t in range(2)]
    gate_p, gate_s = _gate_call(hp, hs, w_in2)

    w_dw2 = w_dw.reshape(CONV_WIDTH, D_CONV)
    y_p = _conv_p_call(u_p, w_dw2, b_dw, ln_g, ln_b)
    ext_s = jnp.concatenate([state_conv[0], u_s.reshape(DEC_BATCH, DEC_SEQ, D_CONV)], axis=1)
    y_s = _conv_s_call(ext_s, w_dw2, b_dw, ln_g, ln_b).reshape(ROWS_S, D_CONV).astype(BF16)
    conv_p = u_p.reshape(BATCH, SEQ, D_CONV)[:, SEQ - (CONV_WIDTH - 1):][None]
    conv_s = ext_s[:, DEC_SEQ:][None]

    caches = ((cache_k_w128, cache_v_w128), (cache_k_w512, cache_v_w512), (cache_k_w2048, cache_v_w2048))
    o_p, l_p, o_s, l_s, cache_out = [], [], [], [], []
    for g in range(N_GROUPS):
        (k_p, k_s), (v_p, v_s) = kv[0][g], kv[1][g]
        o, l = _attn_p_call(q_p, k_p, v_p, g)
        o_p.append(o)
        l_p.append(l)
        ck, cv, o, l = _attn_s_call(q_s, k_s, v_s, caches[g][0], caches[g][1], g)
        o_s.append(o)
        l_s.append(l)
        keep = min(WINDOWS[g], SEQ)
        new_shape = (1, DEC_BATCH, WINDOWS[g], HEADS_PER_GROUP, HEAD_DIM)
        for plane, c in ((k_p, ck), (v_p, cv)):
            full = plane.reshape(BATCH, SEQ, HEADS_PER_GROUP, HEAD_DIM)
            cache_out.append(full[:, SEQ - keep:][None])
            cache_out.append(c.reshape(new_shape))

    m_p, m_s = _mix_call(y_p, y_s, gate_p, gate_s, o_p, l_p, o_s, l_s,
                         w_pw.reshape(D_CONV, D_MODEL).astype(BF16), b_pw,
                         w_o_att.reshape(GROUP_COLS, D_MODEL).astype(BF16))
    x1_p, x1_s = _outproj_call(m_p, m_s, xp, xs, w_out.reshape(D_MODEL, D_MODEL).astype(BF16))
    y_p2, y_s2 = _ffn_call(x1_p, x1_s, g_ffn,
                           w_gate.reshape(D_MODEL, FFN_HIDDEN).astype(BF16),
                           w_up.reshape(D_MODEL, FFN_HIDDEN).astype(BF16),
                           w_down.reshape(FFN_HIDDEN, D_MODEL).astype(BF16),
                           g_final.reshape(1, D_MODEL))
    return (y_p2.reshape(BATCH, SEQ, D_MODEL), y_s2.reshape(DEC_BATCH, DEC_SEQ, D_MODEL),
            conv_p, conv_s, *cache_out)
```

```python
import functools

import jax
import jax.numpy as jnp
from jax import lax
from jax.experimental import pallas as pl
from jax.experimental.pallas import tpu as pltpu

F32 = jnp.float32
BF16 = jnp.bfloat16

D_MODEL = 2048
BATCH = 4
SEQ = 2048
DEC_BATCH = 32
DEC_SEQ = 4
PAST_LEN = 8192
HEAD_DIM = 128
HEADS_PER_GROUP = 4
DILATIONS = (1, 4, 16)
WINDOWS = (128, 512, 2048)
N_GROUPS = 3
GROUP_COLS = HEADS_PER_GROUP * HEAD_DIM
ATT_WIDTH = N_GROUPS * GROUP_COLS
SPAN = 128
ATT_SCALE = HEAD_DIM ** -0.5
ROT_DIM = HEAD_DIM // 4
ROPE_THETA = 500000.0
D_CONV = D_MODEL // 2
CONV_WIDTH = 31
FFN_HIDDEN = 5632
NORM_EPS = 1e-6
LN_EPS = 1e-5
IN_COLS = 2 * D_CONV + 3 * ATT_WIDTH + 2 * D_MODEL

ROWS_P = BATCH * SEQ
ROWS_S = DEC_BATCH * DEC_SEQ
NEG = -1e30

VMEM_LIMIT = 56 * 1024 * 1024
FFN_VMEM_LIMIT = 60 * 1024 * 1024

TM = 1024
NP = ROWS_P // TM
TN = 512
NC = 256
TM2 = 512
NP2 = ROWS_P // TM2
TH = 512
KH = FFN_HIDDEN // TH
TC = 256
HALO = 32
CS = 4


def _sigmoid(x):
    return 1.0 / (1.0 + jnp.exp(-x))


def _rms(x, g):
    return x * lax.rsqrt(jnp.mean(x * x, axis=-1, keepdims=True) + NORM_EPS) * g


def _params(sem, vmem_limit=VMEM_LIMIT):
    return pltpu.CompilerParams(dimension_semantics=sem, vmem_limit_bytes=vmem_limit)


def _pidx(i, n):
    return jnp.minimum(i, n - 1)


def _rms_kernel(xp_ref, xs_ref, g_ref, hp_ref, hs_ref):
    i = pl.program_id(0)

    @pl.when(i < NP)
    def _():
        hp_ref[...] = _rms(xp_ref[...], g_ref[...]).astype(BF16)

    @pl.when(i == NP)
    def _():
        hs_ref[...] = _rms(xs_ref[...], g_ref[...]).astype(BF16)


def _rms_call(xp, xs, g):
    return pl.pallas_call(
        _rms_kernel,
        grid=(NP + 1,),
        in_specs=[
            pl.BlockSpec((TM, D_MODEL), lambda i: (_pidx(i, NP), 0)),
            pl.BlockSpec((ROWS_S, D_MODEL), lambda i: (0, 0)),
            pl.BlockSpec((1, D_MODEL), lambda i: (0, 0)),
        ],
        out_specs=[
            pl.BlockSpec((TM, D_MODEL), lambda i: (_pidx(i, NP), 0)),
            pl.BlockSpec((ROWS_S, D_MODEL), lambda i: (0, 0)),
        ],
        out_shape=[
            jax.ShapeDtypeStruct((ROWS_P, D_MODEL), BF16),
            jax.ShapeDtypeStruct((ROWS_S, D_MODEL), BF16),
        ],
        compiler_params=_params(("arbitrary",)),
        name="rms_in",
    )(xp, xs, g)


def _glu_kernel(hp_ref, hs_ref, wa_ref, wb_ref, ba_ref, bb_ref, up_ref, us_ref, wa_s, wb_s):
    i = pl.program_id(1)

    @pl.when(i == 0)
    def _():
        wa_s[...] = wa_ref[...].astype(BF16)
        wb_s[...] = wb_ref[...].astype(BF16)

    def glu(h, u_ref):
        for c in range(TN // NC):
            sl = slice(c * NC, (c + 1) * NC)
            za = jnp.dot(h, wa_s[:, sl], preferred_element_type=F32) + ba_ref[:, sl]
            zb = jnp.dot(h, wb_s[:, sl], preferred_element_type=F32) + bb_ref[:, sl]
            u_ref[:, sl] = za * _sigmoid(zb)

    @pl.when(i < NP)
    def _():
        glu(hp_ref[...], up_ref)

    @pl.when(i == NP)
    def _():
        glu(hs_ref[...], us_ref)


def _glu_call(hp, hs, w_in, b_glu):
    nj = D_CONV // TN
    return pl.pallas_call(
        _glu_kernel,
        grid=(nj, NP + 1),
        in_specs=[
            pl.BlockSpec((TM, D_MODEL), lambda j, i: (_pidx(i, NP), 0)),
            pl.BlockSpec((ROWS_S, D_MODEL), lambda j, i: (0, 0)),
            pl.BlockSpec((D_MODEL, TN), lambda j, i: (0, j)),
            pl.BlockSpec((D_MODEL, TN), lambda j, i: (0, j + nj)),
            pl.BlockSpec((1, TN), lambda j, i: (0, j)),
            pl.BlockSpec((1, TN), lambda j, i: (0, j + nj)),
        ],
        out_specs=[
            pl.BlockSpec((TM, TN), lambda j, i: (_pidx(i, NP), j)),
            pl.BlockSpec((ROWS_S, TN), lambda j, i: (0, j)),
        ],
        out_shape=[
            jax.ShapeDtypeStruct((ROWS_P, D_CONV), F32),
            jax.ShapeDtypeStruct((ROWS_S, D_CONV), F32),
        ],
        scratch_shapes=[pltpu.VMEM((D_MODEL, TN), BF16), pltpu.VMEM((D_MODEL, TN), BF16)],
        compiler_params=_params(("arbitrary", "arbitrary")),
        name="in_glu",
    )(hp, hs, w_in, w_in, b_glu, b_glu)


def _project_heads(h, w_s, t_ref, o_ref, rows):
    if t_ref is not None:
        cos, sin_lo, sin_hi = t_ref[0], t_ref[1], t_ref[2]
    for c in range(TN // NC):
        z = jnp.dot(h, w_s[:, c * NC:(c + 1) * NC], preferred_element_type=F32)
        for hc in range(NC // HEAD_DIM):
            t = z[:, hc * HEAD_DIM:(hc + 1) * HEAD_DIM]
            if t_ref is not None:
                t = (t * cos + pltpu.roll(t, HEAD_DIM - ROT_DIM // 2, 1) * sin_lo
                     + pltpu.roll(t, ROT_DIM // 2, 1) * sin_hi)
            head = c * (NC // HEAD_DIM) + hc
            o_ref[0, pl.ds(head, rows, stride=HEADS_PER_GROUP), :] = t


def _q_kernel(hp_ref, hs_ref, w_ref, tp_ref, ts_ref, op_ref, os_ref, w_s):
    i = pl.program_id(1)

    @pl.when(i == 0)
    def _():
        w_s[...] = w_ref[...].astype(BF16)

    @pl.when(i < NP)
    def _():
        _project_heads(hp_ref[...], w_s, tp_ref, op_ref, TM)

    @pl.when(i == NP)
    def _():
        _project_heads(hs_ref[...], w_s, ts_ref, os_ref, ROWS_S)


def _kv_kernel(hp_ref, hs_ref, wk_ref, wv_ref, tp_ref, ts_ref, kp_ref, ks_ref, vp_ref, vs_ref, wk_s, wv_s):
    i = pl.program_id(0)

    @pl.when(i == 0)
    def _():
        wk_s[...] = wk_ref[...].astype(BF16)
        wv_s[...] = wv_ref[...].astype(BF16)

    @pl.when(i < NP)
    def _():
        h = hp_ref[...]
        _project_heads(h, wk_s, tp_ref, kp_ref, TM)
        _project_heads(h, wv_s, None, vp_ref, TM)

    @pl.when(i == NP)
    def _():
        h = hs_ref[...]
        _project_heads(h, wk_s, ts_ref, ks_ref, ROWS_S)
        _project_heads(h, wv_s, None, vs_ref, ROWS_S)


Q_PLANE0 = 2 * D_CONV // TN


def _q_call(hp, hs, w_in, tab_p, tab_s):
    return pl.pallas_call(
        _q_kernel,
        grid=(N_GROUPS, NP + 1),
        in_specs=[
            pl.BlockSpec((TM, D_MODEL), lambda j, i: (_pidx(i, NP), 0)),
            pl.BlockSpec((ROWS_S, D_MODEL), lambda j, i: (0, 0)),
            pl.BlockSpec((D_MODEL, TN), lambda j, i: (0, j + Q_PLANE0)),
            pl.BlockSpec((3, TM, HEAD_DIM), lambda j, i: (0, _pidx(i, NP) % (SEQ // TM), 0)),
            pl.BlockSpec((3, ROWS_S, HEAD_DIM), lambda j, i: (0, 0, 0)),
        ],
        out_specs=[
            pl.BlockSpec((1, 4 * TM, HEAD_DIM), lambda j, i: (j, _pidx(i, NP), 0)),
            pl.BlockSpec((1, 4 * ROWS_S, HEAD_DIM), lambda j, i: (j, 0, 0)),
        ],
        out_shape=[
            jax.ShapeDtypeStruct((N_GROUPS, 4 * ROWS_P, HEAD_DIM), F32),
            jax.ShapeDtypeStruct((N_GROUPS, 4 * ROWS_S, HEAD_DIM), F32),
        ],
        scratch_shapes=[pltpu.VMEM((D_MODEL, TN), BF16)],
        compiler_params=_params(("arbitrary", "arbitrary")),
        name="in_q",
    )(hp, hs, w_in, tab_p, tab_s)


def _kv_call(hp, hs, w_in, tab_p, tab_s, g):
    wspec = lambda plane: pl.BlockSpec((D_MODEL, TN), lambda i: (0, Q_PLANE0 + plane))
    outp = pl.BlockSpec((1, 4 * TM, HEAD_DIM), lambda i: (0, _pidx(i, NP), 0))
    outs = pl.BlockSpec((1, 4 * ROWS_S, HEAD_DIM), lambda i: (0, 0, 0))
    shp = jax.ShapeDtypeStruct((1, 4 * ROWS_P, HEAD_DIM), F32)
    shs = jax.ShapeDtypeStruct((1, 4 * ROWS_S, HEAD_DIM), F32)
    return pl.pallas_call(
        _kv_kernel,
        grid=(NP + 1,),
        in_specs=[
            pl.BlockSpec((TM, D_MODEL), lambda i: (_pidx(i, NP), 0)),
            pl.BlockSpec((ROWS_S, D_MODEL), lambda i: (0, 0)),
            wspec(N_GROUPS + g), wspec(2 * N_GROUPS + g),
            pl.BlockSpec((3, TM, HEAD_DIM), lambda i: (0, _pidx(i, NP) % (SEQ // TM), 0)),
            pl.BlockSpec((3, ROWS_S, HEAD_DIM), lambda i: (0, 0, 0)),
        ],
        out_specs=[outp, outs, outp, outs],
        out_shape=[shp, shs, shp, shs],
        scratch_shapes=[pltpu.VMEM((D_MODEL, TN), BF16), pltpu.VMEM((D_MODEL, TN), BF16)],
        compiler_params=_params(("arbitrary",)),
        name="in_kv%d" % g,
    )(hp, hs, w_in, w_in, tab_p, tab_s)


def _gate_kernel(hp_ref, hs_ref, wa_ref, wb_ref, op_ref, os_ref, w_s):
    i = pl.program_id(1)

    @pl.when(i == 0)
    def _():
        w_s[:, 0:TN] = wa_ref[...].astype(BF16)
        w_s[:, TN:2 * TN] = wb_ref[...].astype(BF16)

    def gate(h, o_ref):
        for c in range(2 * TN // NC):
            sl = slice(c * NC, (c + 1) * NC)
            o_ref[:, sl] = _sigmoid(jnp.dot(h, w_s[:, sl], preferred_element_type=F32)).astype(BF16)

    @pl.when(i < NP)
    def _():
        gate(hp_ref[...], op_ref)

    @pl.when(i == NP)
    def _():
        gate(hs_ref[...], os_ref)


def _gate_call(hp, hs, w_in):
    nj = 2 * D_MODEL // (2 * TN)
    off = (2 * D_CONV + 3 * ATT_WIDTH) // TN
    return pl.pallas_call(
        _gate_kernel,
        grid=(nj, NP + 1),
        in_specs=[
            pl.BlockSpec((TM, D_MODEL), lambda j, i: (_pidx(i, NP), 0)),
            pl.BlockSpec((ROWS_S, D_MODEL), lambda j, i: (0, 0)),
            pl.BlockSpec((D_MODEL, TN), lambda j, i: (0, 2 * j + off)),
            pl.BlockSpec((D_MODEL, TN), lambda j, i: (0, 2 * j + 1 + off)),
        ],
        out_specs=[
            pl.BlockSpec((TM, 2 * TN), lambda j, i: (_pidx(i, NP), j)),
            pl.BlockSpec((ROWS_S, 2 * TN), lambda j, i: (0, j)),
        ],
        out_shape=[
            jax.ShapeDtypeStruct((ROWS_P, 2 * D_MODEL), BF16),
            jax.ShapeDtypeStruct((ROWS_S, 2 * D_MODEL), BF16),
        ],
        scratch_shapes=[pltpu.VMEM((D_MODEL, 2 * TN), BF16)],
        compiler_params=_params(("arbitrary", "arbitrary")),
        name="in_gate",
    )(hp, hs, w_in, w_in)


def _ln_silu(y, g, b):
    mu = jnp.mean(y, axis=-1, keepdims=True)
    yc = y - mu
    var = jnp.mean(yc * yc, axis=-1, keepdims=True)
    z = yc * lax.rsqrt(var + LN_EPS) * g + b
    return z * _sigmoid(z)


def _conv_p_kernel(cur_ref, prev_ref, w_ref, bdw_ref, g_ref, b_ref, y_ref, ext, ypre):
    c = pl.program_id(1)
    n_lc = D_CONV // 128
    for lc in range(n_lc):
        sl = slice(lc * 128, (lc + 1) * 128)
        ext[lc, pl.ds(HALO, TC), :] = cur_ref[:, sl]

    @pl.when(c > 0)
    def _():
        for lc in range(n_lc):
            ext[lc, 0:HALO, :] = prev_ref[TC - HALO:TC, lc * 128:(lc + 1) * 128]

    @pl.when(c == 0)
    def _():
        for lc in range(n_lc):
            ext[lc, 0:HALO, :] = jnp.zeros((HALO, 128), F32)

    base = HALO - (CONV_WIDTH - 1)
    for lc in range(n_lc):
        sl = slice(lc * 128, (lc + 1) * 128)
        for t0 in range(CS):
            acc = jnp.zeros((TC // CS, 128), F32)
            for j in range(CONV_WIDTH):
                acc = acc + ext[lc, pl.ds(base + t0 + j, TC // CS, stride=CS), :] * w_ref[j:j + 1, sl]
            ypre[lc, pl.ds(t0, TC // CS, stride=CS), :] = acc + bdw_ref[:, sl]

    total = jnp.zeros((TC, 1), F32)
    for lc in range(n_lc):
        total = total + jnp.sum(ypre[lc], axis=-1, keepdims=True)
    mu = total * (1.0 / D_CONV)
    sq = jnp.zeros((TC, 1), F32)
    for lc in range(n_lc):
        yc = ypre[lc] - mu
        sq = sq + jnp.sum(yc * yc, axis=-1, keepdims=True)
    inv = lax.rsqrt(sq * (1.0 / D_CONV) + LN_EPS)
    for lc in range(n_lc):
        sl = slice(lc * 128, (lc + 1) * 128)
        z = (ypre[lc] - mu) * inv * g_ref[:, sl] + b_ref[:, sl]
        y_ref[:, sl] = (z * _sigmoid(z)).astype(BF16)


def _conv_p_call(u_p, w_dw, b_dw, ln_g, ln_b):
    nc = SEQ // TC
    vec = pl.BlockSpec((1, D_CONV), lambda b, c: (0, 0))
    return pl.pallas_call(
        _conv_p_kernel,
        grid=(BATCH, nc),
        in_specs=[
            pl.BlockSpec((TC, D_CONV), lambda b, c: (b * nc + c, 0)),
            pl.BlockSpec((TC, D_CONV), lambda b, c: (b * nc + jnp.maximum(c - 1, 0), 0)),
            pl.BlockSpec((CONV_WIDTH, D_CONV), lambda b, c: (0, 0)),
            vec, vec, vec,
        ],
        out_specs=pl.BlockSpec((TC, D_CONV), lambda b, c: (b * nc + c, 0)),
        out_shape=jax.ShapeDtypeStruct((ROWS_P, D_CONV), BF16),
        scratch_shapes=[pltpu.VMEM((D_CONV // 128, HALO + TC, 128), F32),
                        pltpu.VMEM((D_CONV // 128, TC, 128), F32)],
        compiler_params=_params(("arbitrary", "arbitrary")),
        name="conv_prompt",
    )(u_p, u_p, w_dw, b_dw, ln_g, ln_b)


CONV_SB = 8


def _conv_s_kernel(ext_ref, w_ref, bdw_ref, g_ref, b_ref, y_ref):
    acc = jnp.zeros((CONV_SB, DEC_SEQ, D_CONV), F32)
    for j in range(CONV_WIDTH):
        acc = acc + ext_ref[:, pl.ds(j, DEC_SEQ), :] * w_ref[j:j + 1, :]
    y_ref[...] = _ln_silu(acc + bdw_ref[...], g_ref[...], b_ref[...])


def _conv_s_call(ext_s, w_dw, b_dw, ln_g, ln_b):
    vec = pl.BlockSpec((1, D_CONV), lambda b: (0, 0))
    rows = CONV_WIDTH - 1 + DEC_SEQ
    return pl.pallas_call(
        _conv_s_kernel,
        grid=(DEC_BATCH // CONV_SB,),
        in_specs=[
            pl.BlockSpec((CONV_SB, rows, D_CONV), lambda b: (b, 0, 0)),
            pl.BlockSpec((CONV_WIDTH, D_CONV), lambda b: (0, 0)),
            vec, vec, vec,
        ],
        out_specs=pl.BlockSpec((CONV_SB, DEC_SEQ, D_CONV), lambda b: (b, 0, 0)),
        out_shape=jax.ShapeDtypeStruct((DEC_BATCH, DEC_SEQ, D_CONV), F32),
        compiler_params=_params(("arbitrary",)),
        name="conv_sample",
    )(ext_s, w_dw, b_dw, ln_g, ln_b)


def _softmax_block(q, k, v, mask):
    s = lax.dot_general(q.astype(BF16), k.astype(BF16), (((2,), (2,)), ((0,), (0,))),
                        preferred_element_type=F32) * ATT_SCALE
    s = jnp.where(mask[None], s, NEG)
    m = jnp.max(s, axis=-1, keepdims=True)
    p = jnp.exp(s - m)
    den = jnp.sum(p, axis=-1, keepdims=True)
    o = lax.dot_general(p.astype(BF16), v.astype(BF16), (((2,), (1,)), ((0,), (0,))),
                        preferred_element_type=F32) / den
    lse = m + jnp.log(den)
    return o, jnp.broadcast_to(lse, o.shape)


def _make_attn_p_kernel(dil):
    stride = HEADS_PER_GROUP * dil
    nblk = SEQ // dil // SPAN
    blk_rows = SPAN * stride
    pair = min(dil, 2)
    group_rows = pair * HEADS_PER_GROUP
    n_r2 = dil // pair

    def kern(q_ref, k_ref, v_ref, o_ref, l_ref):
        qi = lax.broadcasted_iota(jnp.int32, (SPAN, SPAN), 0)
        ki = lax.broadcasted_iota(jnp.int32, (SPAN, SPAN), 1)
        mask_first = ki <= qi
        qi2 = lax.broadcasted_iota(jnp.int32, (SPAN, 2 * SPAN), 0)
        ki2 = lax.broadcasted_iota(jnp.int32, (SPAN, 2 * SPAN), 1)
        mask_band = (ki2 >= qi2) & (ki2 <= qi2 + SPAN)

        def block(base, key_base, n_keys, mask):
            qs = [pl.ds(base + off, SPAN, stride=stride) for off in range(group_rows)]
            ks = [pl.ds(key_base + off, n_keys, stride=stride) for off in range(group_rows)]
            o, lse = _softmax_block(jnp.stack([q_ref[0, s, :] for s in qs]),
                                    jnp.stack([k_ref[0, s, :] for s in ks]),
                                    jnp.stack([v_ref[0, s, :] for s in ks]), mask)
            for u, s in enumerate(qs):
                o_ref[s, :] = o[u]
                l_ref[s, :] = lse[u]

        def first(r2, carry):
            base = pl.multiple_of(r2 * group_rows, max(group_rows, 8))
            block(base, base, SPAN, mask_first)
            return carry

        def band(t, carry):
            jb = 1 + t // n_r2
            base = pl.multiple_of(jb * blk_rows + (t % n_r2) * group_rows, max(group_rows, 8))
            block(base, base - blk_rows, 2 * SPAN, mask_band)
            return carry

        lax.fori_loop(0, n_r2, first, 0)
        if nblk > 1:
            lax.fori_loop(0, n_r2 * (nblk - 1), band, 0)

    return kern


def _attn_p_wide_kernel(q_ref, k_ref, v_ref, o_ref, l_ref):
    qi = lax.broadcasted_iota(jnp.int32, (SPAN, SPAN), 0)
    ki = lax.broadcasted_iota(jnp.int32, (SPAN, SPAN), 1)
    mask_first = ki <= qi

    def tile(c, carry):
        q8 = jnp.swapaxes(q_ref[0, 0, :, c], 0, 1)
        k8 = jnp.swapaxes(k_ref[0, 0, :, c], 0, 1)
        v8 = jnp.swapaxes(v_ref[0, 0, :, c], 0, 1)
        o, lse = _softmax_block(q8, k8, v8, mask_first)
        o_ref[0, :, c] = jnp.swapaxes(o, 0, 1)
        l_ref[0, :, c] = jnp.swapaxes(lse, 0, 1)
        return carry

    lax.fori_loop(0, q_ref.shape[3], tile, 0)


def _attn_p_call(q_p, k_p, v_p, g):
    dil = DILATIONS[g]
    rows = 4 * SEQ
    if SEQ // dil == SPAN:
        tiles = HEADS_PER_GROUP * dil // 8
        view = lambda a: a.reshape(a.shape[0], BATCH, SPAN, tiles, 8, HEAD_DIM)
        blk = lambda plane: pl.BlockSpec((1, 1, SPAN, tiles, 8, HEAD_DIM), lambda b: (plane, b, 0, 0, 0, 0))
        out = pl.BlockSpec((1, SPAN, tiles, 8, HEAD_DIM), lambda b: (b, 0, 0, 0, 0))
        o, l = pl.pallas_call(
            _attn_p_wide_kernel,
            grid=(BATCH,),
            in_specs=[blk(g), blk(0), blk(0)],
            out_specs=[out, out],
            out_shape=[jax.ShapeDtypeStruct((BATCH, SPAN, tiles, 8, HEAD_DIM), F32)] * 2,
            compiler_params=_params(("arbitrary",)),
            name="attn_prompt_g%d" % g,
        )(view(q_p), view(k_p), view(v_p))
        return o.reshape(4 * ROWS_P, HEAD_DIM), l.reshape(4 * ROWS_P, HEAD_DIM)
    blk = lambda plane: pl.BlockSpec((1, rows, HEAD_DIM), lambda b: (plane, b, 0))
    out = pl.BlockSpec((rows, HEAD_DIM), lambda b: (b, 0))
    return pl.pallas_call(
        _make_attn_p_kernel(dil),
        grid=(BATCH,),
        in_specs=[blk(g), blk(0), blk(0)],
        out_specs=[out, out],
        out_shape=[jax.ShapeDtypeStruct((4 * ROWS_P, HEAD_DIM), F32)] * 2,
        compiler_params=_params(("arbitrary",)),
        name="attn_prompt_g%d" % g,
    )(q_p, k_p, v_p)


NEW_ROWS = DEC_SEQ * HEADS_PER_GROUP


def _finish_sample(parts, o_ref, l_ref, j):
    mx = None
    for s, _ in parts:
        m = jnp.max(s, axis=0) if s.ndim == 3 else s
        mx = m if mx is None else jnp.maximum(mx, m)
    den = jnp.zeros((8, 1), F32)
    acc = jnp.zeros((8, HEAD_DIM), F32)
    for s, v in parts:
        p = jnp.exp(s - mx)
        if s.ndim == 3:
            den = den + jnp.sum(p, axis=0)
            acc = acc + jnp.sum(p * v, axis=0)
        else:
            den = den + p
            acc = acc + p * v
    o_ref[8 * j:8 * j + 8, :] = acc / den
    l_ref[8 * j:8 * j + 8, :] = jnp.broadcast_to(mx + jnp.log(den), (8, HEAD_DIM))


def _score(q, k):
    return jnp.sum(q * k, axis=-1, keepdims=True) * ATT_SCALE


def _attn_s_kernel_d1(q_ref, kn_ref, vn_ref, k_ref, v_ref, ok_ref, ov_ref, o_ref, l_ref, kc, vc):
    nb = WINDOWS[0] * HEADS_PER_GROUP
    kc[0:nb, :] = k_ref[0]
    kc[nb:nb + NEW_ROWS, :] = kn_ref[0]
    vc[0:nb, :] = v_ref[0]
    vc[nb:nb + NEW_ROWS, :] = vn_ref[0]
    ok_ref[0] = kc[NEW_ROWS:nb + NEW_ROWS, :]
    ov_ref[0] = vc[NEW_ROWS:nb + NEW_ROWS, :]
    half = HEADS_PER_GROUP
    n_even, n_odd = SPAN // 2 + 1, SPAN // 2
    for j in range(2):
        q = q_ref[0, 8 * j:8 * j + 8, :]
        parts = []
        for start, n in ((8 * j, n_even), (8 * j + half, n_odd)):
            k3 = kc[pl.ds(start, 8 * n), :].reshape(n, 8, HEAD_DIM)
            v3 = vc[pl.ds(start, 8 * n), :].reshape(n, 8, HEAD_DIM)
            parts.append((_score(q[None], k3), v3))
        _finish_sample(parts, o_ref, l_ref, j)


def _make_attn_s_kernel(dil):
    width = HEADS_PER_GROUP * dil

    def kern(q_ref, kn_ref, vn_ref, k_ref, v_ref, ok_ref, ov_ref, o_ref, l_ref):
        for src, new, dst in ((k_ref, kn_ref, ok_ref), (v_ref, vn_ref, ov_ref)):
            if width == NEW_ROWS:
                dst[0, 0:SPAN - 1] = src[0, 1:SPAN]
                dst[0, SPAN - 1] = new[0]
            else:
                dst[0, :, 0:width - NEW_ROWS, :] = src[0, :, NEW_ROWS:width, :]
                dst[0, 0:SPAN - 1, width - NEW_ROWS:width, :] = src[0, 1:SPAN, 0:NEW_ROWS, :]
                dst[0, SPAN - 1, width - NEW_ROWS:width, :] = new[0]
        for j in range(2):
            q = q_ref[0, 8 * j:8 * j + 8, :]
            k3 = k_ref[0, :, 8 * j:8 * j + 8, :]
            v3 = v_ref[0, :, 8 * j:8 * j + 8, :]
            kn = kn_ref[0, 8 * j:8 * j + 8, :]
            vn = vn_ref[0, 8 * j:8 * j + 8, :]
            parts = [(_score(q[None], k3), v3), (_score(q, kn), vn)]
            _finish_sample(parts, o_ref, l_ref, j)

    return kern


def _attn_s_call(q_s, k_s, v_s, ck, cv, g):
    dil = DILATIONS[g]
    new = lambda plane: pl.BlockSpec((1, NEW_ROWS, HEAD_DIM), lambda b: (plane, b, 0))
    small = pl.BlockSpec((NEW_ROWS, HEAD_DIM), lambda b: (b, 0))
    if dil == 1:
        rows = WINDOWS[g] * HEADS_PER_GROUP
        shape = (DEC_BATCH, rows, HEAD_DIM)
        cache = pl.BlockSpec((1, rows, HEAD_DIM), lambda b: (b, 0, 0))
        kern = _attn_s_kernel_d1
        scratch = [pltpu.VMEM((rows + NEW_ROWS, HEAD_DIM), F32)] * 2
    else:
        width = HEADS_PER_GROUP * dil
        shape = (DEC_BATCH, SPAN, width, HEAD_DIM)
        cache = pl.BlockSpec((1, SPAN, width, HEAD_DIM), lambda b: (b, 0, 0, 0))
        kern = _make_attn_s_kernel(dil)
        scratch = []
    small_shape = jax.ShapeDtypeStruct((4 * ROWS_S, HEAD_DIM), F32)
    return pl.pallas_call(
        kern,
        grid=(DEC_BATCH,),
        in_specs=[new(g), new(0), new(0), cache, cache],
        out_specs=[cache, cache, small, small],
        out_shape=[jax.ShapeDtypeStruct(shape, F32)] * 2 + [small_shape] * 2,
        scratch_shapes=scratch,
        compiler_params=_params(("arbitrary",)),
        name="attn_sample_g%d" % g,
    )(q_s, k_s, v_s, ck.reshape(shape), cv.reshape(shape))


def _mix_kernel(*refs):
    (yp_ref, ys_ref, gp_ref, gs_ref) = refs[0:4]
    op_refs, lp_refs = refs[4:7], refs[7:10]
    os_refs, ls_refs = refs[10:13], refs[13:16]
    wpw_ref, bpw_ref, wo_ref, mp_ref, ms_ref = refs[16:21]
    i = pl.program_id(0)

    def mix(y_ref, g_ref, o_refs, l_refs, m_ref, rows):
        conv = jnp.dot(y_ref[...], wpw_ref[...], preferred_element_type=F32) + bpw_ref[...]
        att = jnp.zeros((rows, D_MODEL), F32)
        for h in range(HEADS_PER_GROUP):
            sl = pl.ds(h, rows, stride=HEADS_PER_GROUP)
            ls = [l[sl, :] for l in l_refs]
            mx = jnp.maximum(jnp.maximum(ls[0], ls[1]), ls[2])
            es = [jnp.exp(l - mx) for l in ls]
            num = es[0] * o_refs[0][sl, :] + es[1] * o_refs[1][sl, :] + es[2] * o_refs[2][sl, :]
            oh = num / (es[0] + es[1] + es[2])
            att = att + jnp.dot(oh.astype(BF16), wo_ref[h * HEAD_DIM:(h + 1) * HEAD_DIM, :],
                                preferred_element_type=F32)
        gates = g_ref[...].astype(F32)
        m_ref[...] = (gates[:, :D_MODEL] * conv + gates[:, D_MODEL:] * att).astype(BF16)

    @pl.when(i < NP2)
    def _():
        mix(yp_ref, gp_ref, op_refs, lp_refs, mp_ref, TM2)

    @pl.when(i == NP2)
    def _():
        mix(ys_ref, gs_ref, os_refs, ls_refs, ms_ref, ROWS_S)


def _mix_call(y_p, y_s, gate_p, gate_s, o_p, l_p, o_s, l_s, w_pw, b_pw, w_o):
    pi = lambda i: (_pidx(i, NP2), 0)
    zero = lambda i: (0, 0)
    hp = pl.BlockSpec((4 * TM2, HEAD_DIM), pi)
    hs = pl.BlockSpec((4 * ROWS_S, HEAD_DIM), zero)
    return pl.pallas_call(
        _mix_kernel,
        grid=(NP2 + 1,),
        in_specs=[
            pl.BlockSpec((TM2, D_CONV), pi), pl.BlockSpec((ROWS_S, D_CONV), zero),
            pl.BlockSpec((TM2, 2 * D_MODEL), pi), pl.BlockSpec((ROWS_S, 2 * D_MODEL), zero),
            hp, hp, hp, hp, hp, hp, hs, hs, hs, hs, hs, hs,
            pl.BlockSpec((D_CONV, D_MODEL), zero), pl.BlockSpec((1, D_MODEL), zero),
            pl.BlockSpec((GROUP_COLS, D_MODEL), zero),
        ],
        out_specs=[pl.BlockSpec((TM2, D_MODEL), pi), pl.BlockSpec((ROWS_S, D_MODEL), zero)],
        out_shape=[jax.ShapeDtypeStruct((ROWS_P, D_MODEL), BF16),
                   jax.ShapeDtypeStruct((ROWS_S, D_MODEL), BF16)],
        compiler_params=_params(("arbitrary",)),
        name="mix",
    )(y_p, y_s, gate_p, gate_s, *o_p, *l_p, *o_s, *l_s, w_pw, b_pw, w_o)


def _outproj_kernel(mp_ref, ms_ref, xp_ref, xs_ref, w_ref, op_ref, os_ref):
    i = pl.program_id(0)

    @pl.when(i < NP2)
    def _():
        op_ref[...] = xp_ref[...] + jnp.dot(mp_ref[...], w_ref[...], preferred_element_type=F32)

    @pl.when(i == NP2)
    def _():
        os_ref[...] = xs_ref[...] + jnp.dot(ms_ref[...], w_ref[...], preferred_element_type=F32)


def _outproj_call(m_p, m_s, xp, xs, w_out):
    pi = lambda i: (_pidx(i, NP2), 0)
    zero = lambda i: (0, 0)
    return pl.pallas_call(
        _outproj_kernel,
        grid=(NP2 + 1,),
        in_specs=[
            pl.BlockSpec((TM2, D_MODEL), pi), pl.BlockSpec((ROWS_S, D_MODEL), zero),
            pl.BlockSpec((TM2, D_MODEL), pi), pl.BlockSpec((ROWS_S, D_MODEL), zero),
            pl.BlockSpec((D_MODEL, D_MODEL), zero),
        ],
        out_specs=[pl.BlockSpec((TM2, D_MODEL), pi), pl.BlockSpec((ROWS_S, D_MODEL), zero)],
        out_shape=[jax.ShapeDtypeStruct((ROWS_P, D_MODEL), F32),
                   jax.ShapeDtypeStruct((ROWS_S, D_MODEL), F32)],
        compiler_params=_params(("arbitrary",)),
        name="out_proj",
    )(m_p, m_s, xp, xs, w_out)


def _ffn_kernel(xp_ref, xs_ref, gf_ref, wg_ref, wu_ref, wd_ref, gl_ref, yp_ref, ys_ref, hp_s, hs_s):
    i = pl.program_id(0)
    k = pl.program_id(1)

    def ffn(x_ref, h_s, y_ref):
        @pl.when(k == 0)
        def _():
            h_s[...] = _rms(x_ref[...], gf_ref[...]).astype(BF16)
            y_ref[...] = jnp.zeros(y_ref.shape, F32)

        h = h_s[...]
        a = jnp.dot(h, wg_ref[...], preferred_element_type=F32)
        u = jnp.dot(h, wu_ref[...], preferred_element_type=F32)
        act = (a * _sigmoid(a) * u).astype(BF16)
        y_ref[...] += jnp.dot(act, wd_ref[...], preferred_element_type=F32)

        @pl.when(k == KH - 1)
        def _():
            y_ref[...] = _rms(x_ref[...] + y_ref[...], gl_ref[...])

    ffn(xp_ref, hp_s, yp_ref)

    @pl.when(i == 0)
    def _():
        ffn(xs_ref, hs_s, ys_ref)


def _ffn_call(x1_p, x1_s, g_ffn, w_gate, w_up, w_down, g_final):
    pi = lambda i, k: (i, 0)
    zero = lambda i, k: (0, 0)
    return pl.pallas_call(
        _ffn_kernel,
        grid=(NP, KH),
        in_specs=[
            pl.BlockSpec((TM, D_MODEL), pi, pipeline_mode=pl.Buffered(1)),
            pl.BlockSpec((ROWS_S, D_MODEL), zero, pipeline_mode=pl.Buffered(1)),
            pl.BlockSpec((1, D_MODEL), zero),
            pl.BlockSpec((D_MODEL, TH), lambda i, k: (0, k)),
            pl.BlockSpec((D_MODEL, TH), lambda i, k: (0, k)),
            pl.BlockSpec((TH, D_MODEL), lambda i, k: (k, 0)),
            pl.BlockSpec((1, D_MODEL), zero),
        ],
        out_specs=[pl.BlockSpec((TM, D_MODEL), pi), pl.BlockSpec((ROWS_S, D_MODEL), zero)],
        out_shape=[jax.ShapeDtypeStruct((ROWS_P, D_MODEL), F32),
                   jax.ShapeDtypeStruct((ROWS_S, D_MODEL), F32)],
        scratch_shapes=[pltpu.VMEM((TM, D_MODEL), BF16), pltpu.VMEM((ROWS_S, D_MODEL), BF16)],
        compiler_params=_params(("arbitrary", "arbitrary"), FFN_VMEM_LIMIT),
        name="ffn",
    )(x1_p, x1_s, g_ffn, w_gate, w_up, w_down, g_final)


def _rope_tables(pos):
    half = ROT_DIM // 2
    inv = jnp.power(ROPE_THETA, -jnp.arange(0, ROT_DIM, 2, dtype=F32) / ROT_DIM)
    ang = pos.astype(F32)[:, None] * inv[None, :]
    cos, sin = jnp.cos(ang), jnp.sin(ang)
    n = pos.shape[0]
    ones = jnp.ones((n, HEAD_DIM - ROT_DIM), F32)
    zeros = jnp.zeros((n, HEAD_DIM - ROT_DIM), F32)
    zh = jnp.zeros((n, half), F32)
    c = jnp.concatenate([cos, cos, ones], axis=1)
    s_lo = jnp.concatenate([-sin, zh, zeros], axis=1)
    s_hi = jnp.concatenate([zh, sin, zeros], axis=1)
    return jnp.stack([c, s_lo, s_hi])


def kernel(x_prompt, x_sample, state_conv, cache_k_w128, cache_v_w128, cache_k_w512, cache_v_w512,
           cache_k_w2048, cache_v_w2048, g_mix, w_in, b_glu, w_dw, b_dw, ln_g, ln_b, w_pw, b_pw,
           w_o_att, w_out, g_ffn, w_gate, w_up, w_down, g_final):
    xp = x_prompt.reshape(ROWS_P, D_MODEL)
    xs = x_sample.reshape(ROWS_S, D_MODEL)
    w_in2 = w_in.reshape(D_MODEL, IN_COLS)

    hp, hs = _rms_call(xp, xs, g_mix)
    u_p, u_s = _glu_call(hp, hs, w_in2, b_glu)
    tab_p = _rope_tables(jnp.arange(SEQ, dtype=jnp.int32))
    tab_s = _rope_tables(PAST_LEN + jnp.arange(ROWS_S, dtype=jnp.int32) % DEC_SEQ)
    q_p, q_s = _q_call(hp, hs, w_in2, tab_p, tab_s)
    kv = [_kv_call(hp, hs, w_in2, tab_p, tab_s, g) for g in range(N_GROUPS)]
    gate_p, gate_s = _gate_call(hp, hs, w_in2)

    w_dw2 = w_dw.reshape(CONV_WIDTH, D_CONV)
    y_p = _conv_p_call(u_p, w_dw2, b_dw, ln_g, ln_b)
    ext_s = jnp.concatenate([state_conv[0], u_s.reshape(DEC_BATCH, DEC_SEQ, D_CONV)], axis=1)
    y_s = _conv_s_call(ext_s, w_dw2, b_dw, ln_g, ln_b).reshape(ROWS_S, D_CONV).astype(BF16)
    conv_p = u_p.reshape(BATCH, SEQ, D_CONV)[:, SEQ - (CONV_WIDTH - 1):][None]
    conv_s = ext_s[:, DEC_SEQ:][None]

    caches = ((cache_k_w128, cache_v_w128), (cache_k_w512, cache_v_w512), (cache_k_w2048, cache_v_w2048))
    o_p, l_p, o_s, l_s, cache_out = [], [], [], [], []
    for g in range(N_GROUPS):
        k_p, k_s, v_p, v_s = kv[g]
        o, l = _attn_p_call(q_p, k_p, v_p, g)
        o_p.append(o)
        l_p.append(l)
        ck, cv, o, l = _attn_s_call(q_s, k_s, v_s, caches[g][0], caches[g][1], g)
        o_s.append(o)
        l_s.append(l)
        keep = min(WINDOWS[g], SEQ)
        new_shape = (1, DEC_BATCH, WINDOWS[g], HEADS_PER_GROUP, HEAD_DIM)
        for plane, c in ((k_p, ck), (v_p, cv)):
            full = plane.reshape(BATCH, SEQ, HEADS_PER_GROUP, HEAD_DIM)
            cache_out.append(full[:, SEQ - keep:][None])
            cache_out.append(c.reshape(new_shape))

    m_p, m_s = _mix_call(y_p, y_s, gate_p, gate_s, o_p, l_p, o_s, l_s,
                         w_pw.reshape(D_CONV, D_MODEL).astype(BF16), b_pw,
                         w_o_att.reshape(GROUP_COLS, D_MODEL).astype(BF16))
    x1_p, x1_s = _outproj_call(m_p, m_s, xp, xs, w_out.reshape(D_MODEL, D_MODEL).astype(BF16))
    y_p2, y_s2 = _ffn_call(x1_p, x1_s, g_ffn,
                           w_gate.reshape(D_MODEL, FFN_HIDDEN).astype(BF16),
                           w_up.reshape(D_MODEL, FFN_HIDDEN).astype(BF16),
                           w_down.reshape(FFN_HIDDEN, D_MODEL).astype(BF16),
                           g_final.reshape(1, D_MODEL))
    return (y_p2.reshape(BATCH, SEQ, D_MODEL), y_s2.reshape(DEC_BATCH, DEC_SEQ, D_MODEL),
            conv_p, conv_s, *cache_out)
```

```python
import functools

import jax
import jax.numpy as jnp
from jax import lax
from jax.experimental import pallas as pl
from jax.experimental.pallas import tpu as pltpu

F32 = jnp.float32
BF16 = jnp.bfloat16

D_MODEL = 2048
BATCH = 4
SEQ = 2048
DEC_BATCH = 32
DEC_SEQ = 4
PAST_LEN = 8192
HEAD_DIM = 128
HEADS_PER_GROUP = 4
DILATIONS = (1, 4, 16)
WINDOWS = (128, 512, 2048)
N_GROUPS = 3
GROUP_COLS = HEADS_PER_GROUP * HEAD_DIM
ATT_WIDTH = N_GROUPS * GROUP_COLS
SPAN = 128
ATT_SCALE = HEAD_DIM ** -0.5
ROT_DIM = HEAD_DIM // 4
ROPE_THETA = 500000.0
D_CONV = D_MODEL // 2
CONV_WIDTH = 31
FFN_HIDDEN = 5632
NORM_EPS = 1e-6
LN_EPS = 1e-5
IN_COLS = 2 * D_CONV + 3 * ATT_WIDTH + 2 * D_MODEL

ROWS_P = BATCH * SEQ
ROWS_S = DEC_BATCH * DEC_SEQ
NEG = -1e30

VMEM_LIMIT = 56 * 1024 * 1024
FFN_VMEM_LIMIT = 60 * 1024 * 1024

TM = 1024
NP = ROWS_P // TM
TN = 512
NC = 256
TM2 = 512
NP2 = ROWS_P // TM2
TH = 512
KH = FFN_HIDDEN // TH
TC = 256
HALO = 32
CS = 4


def _sigmoid(x):
    return 1.0 / (1.0 + jnp.exp(-x))


def _rms(x, g):
    return x * lax.rsqrt(jnp.mean(x * x, axis=-1, keepdims=True) + NORM_EPS) * g


def _params(sem, vmem_limit=VMEM_LIMIT):
    return pltpu.CompilerParams(dimension_semantics=sem, vmem_limit_bytes=vmem_limit)


def _pidx(i, n):
    return jnp.minimum(i, n - 1)


def _rms_kernel(xp_ref, xs_ref, g_ref, hp_ref, hs_ref):
    i = pl.program_id(0)

    @pl.when(i < NP)
    def _():
        hp_ref[...] = _rms(xp_ref[...], g_ref[...]).astype(BF16)

    @pl.when(i == NP)
    def _():
        hs_ref[...] = _rms(xs_ref[...], g_ref[...]).astype(BF16)


def _rms_call(xp, xs, g):
    return pl.pallas_call(
        _rms_kernel,
        grid=(NP + 1,),
        in_specs=[
            pl.BlockSpec((TM, D_MODEL), lambda i: (_pidx(i, NP), 0)),
            pl.BlockSpec((ROWS_S, D_MODEL), lambda i: (0, 0)),
            pl.BlockSpec((1, D_MODEL), lambda i: (0, 0)),
        ],
        out_specs=[
            pl.BlockSpec((TM, D_MODEL), lambda i: (_pidx(i, NP), 0)),
            pl.BlockSpec((ROWS_S, D_MODEL), lambda i: (0, 0)),
        ],
        out_shape=[
            jax.ShapeDtypeStruct((ROWS_P, D_MODEL), BF16),
            jax.ShapeDtypeStruct((ROWS_S, D_MODEL), BF16),
        ],
        compiler_params=_params(("arbitrary",)),
        name="rms_in",
    )(xp, xs, g)


def _glu_kernel(hp_ref, hs_ref, wa_ref, wb_ref, ba_ref, bb_ref, up_ref, us_ref, wa_s, wb_s):
    i = pl.program_id(1)

    @pl.when(i == 0)
    def _():
        wa_s[...] = wa_ref[...].astype(BF16)
        wb_s[...] = wb_ref[...].astype(BF16)

    def glu(h, u_ref):
        for c in range(TN // NC):
            sl = slice(c * NC, (c + 1) * NC)
            za = jnp.dot(h, wa_s[:, sl], preferred_element_type=F32) + ba_ref[:, sl]
            zb = jnp.dot(h, wb_s[:, sl], preferred_element_type=F32) + bb_ref[:, sl]
            u_ref[:, sl] = za * _sigmoid(zb)

    @pl.when(i < NP)
    def _():
        glu(hp_ref[...], up_ref)

    @pl.when(i == NP)
    def _():
        glu(hs_ref[...], us_ref)


def _glu_call(hp, hs, w_in, b_glu):
    nj = D_CONV // TN
    return pl.pallas_call(
        _glu_kernel,
        grid=(nj, NP + 1),
        in_specs=[
            pl.BlockSpec((TM, D_MODEL), lambda j, i: (_pidx(i, NP), 0)),
            pl.BlockSpec((ROWS_S, D_MODEL), lambda j, i: (0, 0)),
            pl.BlockSpec((D_MODEL, TN), lambda j, i: (0, j)),
            pl.BlockSpec((D_MODEL, TN), lambda j, i: (0, j + nj)),
            pl.BlockSpec((1, TN), lambda j, i: (0, j)),
            pl.BlockSpec((1, TN), lambda j, i: (0, j + nj)),
        ],
        out_specs=[
            pl.BlockSpec((TM, TN), lambda j, i: (_pidx(i, NP), j)),
            pl.BlockSpec((ROWS_S, TN), lambda j, i: (0, j)),
        ],
        out_shape=[
            jax.ShapeDtypeStruct((ROWS_P, D_CONV), F32),
            jax.ShapeDtypeStruct((ROWS_S, D_CONV), F32),
        ],
        scratch_shapes=[pltpu.VMEM((D_MODEL, TN), BF16), pltpu.VMEM((D_MODEL, TN), BF16)],
        compiler_params=_params(("arbitrary", "arbitrary")),
        name="in_glu",
    )(hp, hs, w_in, w_in, b_glu, b_glu)


def _project_heads(h, w_s, t_ref, o_ref, rows):
    if t_ref is not None:
        cos, sin_lo, sin_hi = t_ref[0], t_ref[1], t_ref[2]
    for c in range(TN // NC):
        z = jnp.dot(h, w_s[:, c * NC:(c + 1) * NC], preferred_element_type=F32)
        for hc in range(NC // HEAD_DIM):
            t = z[:, hc * HEAD_DIM:(hc + 1) * HEAD_DIM]
            if t_ref is not None:
                t = (t * cos + pltpu.roll(t, HEAD_DIM - ROT_DIM // 2, 1) * sin_lo
                     + pltpu.roll(t, ROT_DIM // 2, 1) * sin_hi)
            head = c * (NC // HEAD_DIM) + hc
            o_ref[0, pl.ds(head, rows, stride=HEADS_PER_GROUP), :] = t


def _q_kernel(hp_ref, hs_ref, w_ref, tp_ref, ts_ref, op_ref, os_ref, w_s):
    i = pl.program_id(1)

    @pl.when(i == 0)
    def _():
        w_s[...] = w_ref[...].astype(BF16)

    @pl.when(i < NP)
    def _():
        _project_heads(hp_ref[...], w_s, tp_ref, op_ref, TM)

    @pl.when(i == NP)
    def _():
        _project_heads(hs_ref[...], w_s, ts_ref, os_ref, ROWS_S)


def _kv_kernel(hp_ref, hs_ref, wk_ref, wv_ref, tp_ref, ts_ref, kp_ref, ks_ref, vp_ref, vs_ref, wk_s, wv_s):
    i = pl.program_id(0)

    @pl.when(i == 0)
    def _():
        wk_s[...] = wk_ref[...].astype(BF16)
        wv_s[...] = wv_ref[...].astype(BF16)

    @pl.when(i < NP)
    def _():
        h = hp_ref[...]
        _project_heads(h, wk_s, tp_ref, kp_ref, TM)
        _project_heads(h, wv_s, None, vp_ref, TM)

    @pl.when(i == NP)
    def _():
        h = hs_ref[...]
        _project_heads(h, wk_s, ts_ref, ks_ref, ROWS_S)
        _project_heads(h, wv_s, None, vs_ref, ROWS_S)


Q_PLANE0 = 2 * D_CONV // TN


def _q_call(hp, hs, w_in, tab_p, tab_s):
    return pl.pallas_call(
        _q_kernel,
        grid=(N_GROUPS, NP + 1),
        in_specs=[
            pl.BlockSpec((TM, D_MODEL), lambda j, i: (_pidx(i, NP), 0)),
            pl.BlockSpec((ROWS_S, D_MODEL), lambda j, i: (0, 0)),
            pl.BlockSpec((D_MODEL, TN), lambda j, i: (0, j + Q_PLANE0)),
            pl.BlockSpec((3, TM, HEAD_DIM), lambda j, i: (0, _pidx(i, NP) % (SEQ // TM), 0)),
            pl.BlockSpec((3, ROWS_S, HEAD_DIM), lambda j, i: (0, 0, 0)),
        ],
        out_specs=[
            pl.BlockSpec((1, 4 * TM, HEAD_DIM), lambda j, i: (j, _pidx(i, NP), 0)),
            pl.BlockSpec((1, 4 * ROWS_S, HEAD_DIM), lambda j, i: (j, 0, 0)),
        ],
        out_shape=[
            jax.ShapeDtypeStruct((N_GROUPS, 4 * ROWS_P, HEAD_DIM), F32),
            jax.ShapeDtypeStruct((N_GROUPS, 4 * ROWS_S, HEAD_DIM), F32),
        ],
        scratch_shapes=[pltpu.VMEM((D_MODEL, TN), BF16)],
        compiler_params=_params(("arbitrary", "arbitrary")),
        name="in_q",
    )(hp, hs, w_in, tab_p, tab_s)


def _kv_call(hp, hs, w_in, tab_p, tab_s, g):
    wspec = lambda plane: pl.BlockSpec((D_MODEL, TN), lambda i: (0, Q_PLANE0 + plane))
    outp = pl.BlockSpec((1, 4 * TM, HEAD_DIM), lambda i: (0, _pidx(i, NP), 0))
    outs = pl.BlockSpec((1, 4 * ROWS_S, HEAD_DIM), lambda i: (0, 0, 0))
    shp = jax.ShapeDtypeStruct((1, 4 * ROWS_P, HEAD_DIM), F32)
    shs = jax.ShapeDtypeStruct((1, 4 * ROWS_S, HEAD_DIM), F32)
    return pl.pallas_call(
        _kv_kernel,
        grid=(NP + 1,),
        in_specs=[
            pl.BlockSpec((TM, D_MODEL), lambda i: (_pidx(i, NP), 0)),
            pl.BlockSpec((ROWS_S, D_MODEL), lambda i: (0, 0)),
            wspec(N_GROUPS + g), wspec(2 * N_GROUPS + g),
            pl.BlockSpec((3, TM, HEAD_DIM), lambda i: (0, _pidx(i, NP) % (SEQ // TM), 0)),
            pl.BlockSpec((3, ROWS_S, HEAD_DIM), lambda i: (0, 0, 0)),
        ],
        out_specs=[outp, outs, outp, outs],
        out_shape=[shp, shs, shp, shs],
        scratch_shapes=[pltpu.VMEM((D_MODEL, TN), BF16), pltpu.VMEM((D_MODEL, TN), BF16)],
        compiler_params=_params(("arbitrary",)),
        name="in_kv%d" % g,
    )(hp, hs, w_in, w_in, tab_p, tab_s)


def _gate_kernel(hp_ref, hs_ref, wa_ref, wb_ref, op_ref, os_ref, w_s):
    i = pl.program_id(1)

    @pl.when(i == 0)
    def _():
        w_s[:, 0:TN] = wa_ref[...].astype(BF16)
        w_s[:, TN:2 * TN] = wb_ref[...].astype(BF16)

    def gate(h, o_ref):
        for c in range(2 * TN // NC):
            sl = slice(c * NC, (c + 1) * NC)
            o_ref[:, sl] = _sigmoid(jnp.dot(h, w_s[:, sl], preferred_element_type=F32)).astype(BF16)

    @pl.when(i < NP)
    def _():
        gate(hp_ref[...], op_ref)

    @pl.when(i == NP)
    def _():
        gate(hs_ref[...], os_ref)


def _gate_call(hp, hs, w_in):
    nj = 2 * D_MODEL // (2 * TN)
    off = (2 * D_CONV + 3 * ATT_WIDTH) // TN
    return pl.pallas_call(
        _gate_kernel,
        grid=(nj, NP + 1),
        in_specs=[
            pl.BlockSpec((TM, D_MODEL), lambda j, i: (_pidx(i, NP), 0)),
            pl.BlockSpec((ROWS_S, D_MODEL), lambda j, i: (0, 0)),
            pl.BlockSpec((D_MODEL, TN), lambda j, i: (0, 2 * j + off)),
            pl.BlockSpec((D_MODEL, TN), lambda j, i: (0, 2 * j + 1 + off)),
        ],
        out_specs=[
            pl.BlockSpec((TM, 2 * TN), lambda j, i: (_pidx(i, NP), j)),
            pl.BlockSpec((ROWS_S, 2 * TN), lambda j, i: (0, j)),
        ],
        out_shape=[
            jax.ShapeDtypeStruct((ROWS_P, 2 * D_MODEL), BF16),
            jax.ShapeDtypeStruct((ROWS_S, 2 * D_MODEL), BF16),
        ],
        scratch_shapes=[pltpu.VMEM((D_MODEL, 2 * TN), BF16)],
        compiler_params=_params(("arbitrary", "arbitrary")),
        name="in_gate",
    )(hp, hs, w_in, w_in)


def _ln_silu(y, g, b):
    mu = jnp.mean(y, axis=-1, keepdims=True)
    yc = y - mu
    var = jnp.mean(yc * yc, axis=-1, keepdims=True)
    z = yc * lax.rsqrt(var + LN_EPS) * g + b
    return z * _sigmoid(z)


def _conv_p_kernel(cur_ref, prev_ref, w_ref, bdw_ref, g_ref, b_ref, y_ref, ext, ypre):
    c = pl.program_id(1)
    n_lc = D_CONV // 128
    for lc in range(n_lc):
        sl = slice(lc * 128, (lc + 1) * 128)
        ext[lc, pl.ds(HALO, TC), :] = cur_ref[:, sl]

    @pl.when(c > 0)
    def _():
        for lc in range(n_lc):
            ext[lc, 0:HALO, :] = prev_ref[TC - HALO:TC, lc * 128:(lc + 1) * 128]

    @pl.when(c == 0)
    def _():
        for lc in range(n_lc):
            ext[lc, 0:HALO, :] = jnp.zeros((HALO, 128), F32)

    base = HALO - (CONV_WIDTH - 1)
    for lc in range(n_lc):
        sl = slice(lc * 128, (lc + 1) * 128)
        for t0 in range(CS):
            acc = jnp.zeros((TC // CS, 128), F32)
            for j in range(CONV_WIDTH):
                acc = acc + ext[lc, pl.ds(base + t0 + j, TC // CS, stride=CS), :] * w_ref[j:j + 1, sl]
            ypre[lc, pl.ds(t0, TC // CS, stride=CS), :] = acc + bdw_ref[:, sl]

    total = jnp.zeros((TC, 1), F32)
    for lc in range(n_lc):
        total = total + jnp.sum(ypre[lc], axis=-1, keepdims=True)
    mu = total * (1.0 / D_CONV)
    sq = jnp.zeros((TC, 1), F32)
    for lc in range(n_lc):
        yc = ypre[lc] - mu
        sq = sq + jnp.sum(yc * yc, axis=-1, keepdims=True)
    inv = lax.rsqrt(sq * (1.0 / D_CONV) + LN_EPS)
    for lc in range(n_lc):
        sl = slice(lc * 128, (lc + 1) * 128)
        z = (ypre[lc] - mu) * inv * g_ref[:, sl] + b_ref[:, sl]
        y_ref[:, sl] = (z * _sigmoid(z)).astype(BF16)


def _conv_p_call(u_p, w_dw, b_dw, ln_g, ln_b):
    nc = SEQ // TC
    vec = pl.BlockSpec((1, D_CONV), lambda b, c: (0, 0))
    return pl.pallas_call(
        _conv_p_kernel,
        grid=(BATCH, nc),
        in_specs=[
            pl.BlockSpec((TC, D_CONV), lambda b, c: (b * nc + c, 0)),
            pl.BlockSpec((TC, D_CONV), lambda b, c: (b * nc + jnp.maximum(c - 1, 0), 0)),
            pl.BlockSpec((CONV_WIDTH, D_CONV), lambda b, c: (0, 0)),
            vec, vec, vec,
        ],
        out_specs=pl.BlockSpec((TC, D_CONV), lambda b, c: (b * nc + c, 0)),
        out_shape=jax.ShapeDtypeStruct((ROWS_P, D_CONV), BF16),
        scratch_shapes=[pltpu.VMEM((D_CONV // 128, HALO + TC, 128), F32),
                        pltpu.VMEM((D_CONV // 128, TC, 128), F32)],
        compiler_params=_params(("arbitrary", "arbitrary")),
        name="conv_prompt",
    )(u_p, u_p, w_dw, b_dw, ln_g, ln_b)


CONV_SB = 8


def _conv_s_kernel(ext_ref, w_ref, bdw_ref, g_ref, b_ref, y_ref):
    acc = jnp.zeros((CONV_SB, DEC_SEQ, D_CONV), F32)
    for j in range(CONV_WIDTH):
        acc = acc + ext_ref[:, pl.ds(j, DEC_SEQ), :] * w_ref[j:j + 1, :]
    y_ref[...] = _ln_silu(acc + bdw_ref[...], g_ref[...], b_ref[...])


def _conv_s_call(ext_s, w_dw, b_dw, ln_g, ln_b):
    vec = pl.BlockSpec((1, D_CONV), lambda b: (0, 0))
    rows = CONV_WIDTH - 1 + DEC_SEQ
    return pl.pallas_call(
        _conv_s_kernel,
        grid=(DEC_BATCH // CONV_SB,),
        in_specs=[
            pl.BlockSpec((CONV_SB, rows, D_CONV), lambda b: (b, 0, 0)),
            pl.BlockSpec((CONV_WIDTH, D_CONV), lambda b: (0, 0)),
            vec, vec, vec,
        ],
        out_specs=pl.BlockSpec((CONV_SB, DEC_SEQ, D_CONV), lambda b: (b, 0, 0)),
        out_shape=jax.ShapeDtypeStruct((DEC_BATCH, DEC_SEQ, D_CONV), F32),
        compiler_params=_params(("arbitrary",)),
        name="conv_sample",
    )(ext_s, w_dw, b_dw, ln_g, ln_b)


def _softmax_block(q, k, v, mask):
    s = lax.dot_general(q.astype(BF16), k.astype(BF16), (((2,), (2,)), ((0,), (0,))),
                        preferred_element_type=F32) * ATT_SCALE
    s = jnp.where(mask[None], s, NEG)
    m = jnp.max(s, axis=-1, keepdims=True)
    p = jnp.exp(s - m)
    den = jnp.sum(p, axis=-1, keepdims=True)
    o = lax.dot_general(p.astype(BF16), v.astype(BF16), (((2,), (1,)), ((0,), (0,))),
                        preferred_element_type=F32) / den
    lse = m + jnp.log(den)
    return o, jnp.broadcast_to(lse, o.shape)


def _make_attn_p_kernel(dil):
    stride = HEADS_PER_GROUP * dil
    nblk = SEQ // dil // SPAN
    blk_rows = SPAN * stride
    pair = min(dil, 2)
    group_rows = pair * HEADS_PER_GROUP
    n_r2 = dil // pair

    def kern(q_ref, k_ref, v_ref, o_ref, l_ref):
        qi = lax.broadcasted_iota(jnp.int32, (SPAN, SPAN), 0)
        ki = lax.broadcasted_iota(jnp.int32, (SPAN, SPAN), 1)
        mask_first = ki <= qi
        qi2 = lax.broadcasted_iota(jnp.int32, (SPAN, 2 * SPAN), 0)
        ki2 = lax.broadcasted_iota(jnp.int32, (SPAN, 2 * SPAN), 1)
        mask_band = (ki2 >= qi2) & (ki2 <= qi2 + SPAN)

        def block(base, key_base, n_keys, mask):
            qs = [pl.ds(base + off, SPAN, stride=stride) for off in range(group_rows)]
            ks = [pl.ds(key_base + off, n_keys, stride=stride) for off in range(group_rows)]
            o, lse = _softmax_block(jnp.stack([q_ref[0, s, :] for s in qs]),
                                    jnp.stack([k_ref[0, s, :] for s in ks]),
                                    jnp.stack([v_ref[0, s, :] for s in ks]), mask)
            for u, s in enumerate(qs):
                o_ref[s, :] = o[u]
                l_ref[s, :] = lse[u]

        def first(r2, carry):
            base = pl.multiple_of(r2 * group_rows, max(group_rows, 8))
            block(base, base, SPAN, mask_first)
            return carry

        def band(t, carry):
            jb = 1 + t // n_r2
            base = pl.multiple_of(jb * blk_rows + (t % n_r2) * group_rows, max(group_rows, 8))
            block(base, base - blk_rows, 2 * SPAN, mask_band)
            return carry

        lax.fori_loop(0, n_r2, first, 0)
        if nblk > 1:
            lax.fori_loop(0, n_r2 * (nblk - 1), band, 0)

    return kern


def _attn_p_wide_kernel(q_ref, k_ref, v_ref, o_ref, l_ref):
    qi = lax.broadcasted_iota(jnp.int32, (SPAN, SPAN), 0)
    ki = lax.broadcasted_iota(jnp.int32, (SPAN, SPAN), 1)
    mask_first = ki <= qi

    def tile(c, carry):
        q8 = jnp.swapaxes(q_ref[0, 0, :, c], 0, 1)
        k8 = jnp.swapaxes(k_ref[0, 0, :, c], 0, 1)
        v8 = jnp.swapaxes(v_ref[0, 0, :, c], 0, 1)
        o, lse = _softmax_block(q8, k8, v8, mask_first)
        o_ref[0, :, c] = jnp.swapaxes(o, 0, 1)
        l_ref[0, :, c] = jnp.swapaxes(lse, 0, 1)
        return carry

    lax.fori_loop(0, q_ref.shape[3], tile, 0)


def _attn_p_call(q_p, k_p, v_p, g):
    dil = DILATIONS[g]
    rows = 4 * SEQ
    if SEQ // dil == SPAN:
        tiles = HEADS_PER_GROUP * dil // 8
        view = lambda a: a.reshape(a.shape[0], BATCH, SPAN, tiles, 8, HEAD_DIM)
        blk = lambda plane: pl.BlockSpec((1, 1, SPAN, tiles, 8, HEAD_DIM), lambda b: (plane, b, 0, 0, 0, 0))
        out = pl.BlockSpec((1, SPAN, tiles, 8, HEAD_DIM), lambda b: (b, 0, 0, 0, 0))
        o, l = pl.pallas_call(
            _attn_p_wide_kernel,
            grid=(BATCH,),
            in_specs=[blk(g), blk(0), blk(0)],
            out_specs=[out, out],
            out_shape=[jax.ShapeDtypeStruct((BATCH, SPAN, tiles, 8, HEAD_DIM), F32)] * 2,
            compiler_params=_params(("arbitrary",)),
            name="attn_prompt_g%d" % g,
        )(view(q_p), view(k_p), view(v_p))
        return o.reshape(4 * ROWS_P, HEAD_DIM), l.reshape(4 * ROWS_P, HEAD_DIM)
    blk = lambda plane: pl.BlockSpec((1, rows, HEAD_DIM), lambda b: (plane, b, 0))
    out = pl.BlockSpec((rows, HEAD_DIM), lambda b: (b, 0))
    return pl.pallas_call(
        _make_attn_p_kernel(dil),
        grid=(BATCH,),
        in_specs=[blk(g), blk(0), blk(0)],
        out_specs=[out, out],
        out_shape=[jax.ShapeDtypeStruct((4 * ROWS_P, HEAD_DIM), F32)] * 2,
        compiler_params=_params(("arbitrary",)),
        name="attn_prompt_g%d" % g,
    )(q_p, k_p, v_p)


NEW_ROWS = DEC_SEQ * HEADS_PER_GROUP


def _joint_softmax(parts):
    mx = None
    for s, _ in parts:
        m = jnp.max(s, axis=0) if s.ndim == 3 else s
        mx = m if mx is None else jnp.maximum(mx, m)
    den = jnp.zeros((8, 1), F32)
    acc = jnp.zeros((8, HEAD_DIM), F32)
    for s, v in parts:
        p = jnp.exp(s - mx)
        if s.ndim == 3:
            den = den + jnp.sum(p, axis=0)
            acc = acc + jnp.sum(p * v, axis=0)
        else:
            den = den + p
            acc = acc + p * v
    return acc / den


def _score(q, k):
    return jnp.sum(q * k, axis=-1, keepdims=True) * ATT_SCALE


def _attn_s_kernel(q_ref, kn0, vn0, kn1, vn1, kn2, vn2, k0, v0, k1, v1, k2, v2, o_ref, kc, vc):
    nb = WINDOWS[0] * HEADS_PER_GROUP
    kc[0:nb, :] = k0[0]
    kc[nb:nb + NEW_ROWS, :] = kn0[0]
    vc[0:nb, :] = v0[0]
    vc[nb:nb + NEW_ROWS, :] = vn0[0]
    half = HEADS_PER_GROUP
    n_even, n_odd = SPAN // 2 + 1, SPAN // 2
    for j in range(2):
        rows = slice(8 * j, 8 * j + 8)
        parts = []
        q = q_ref[0, rows, :]
        for start, n in ((8 * j, n_even), (8 * j + half, n_odd)):
            k3 = kc[pl.ds(start, 8 * n), :].reshape(n, 8, HEAD_DIM)
            v3 = vc[pl.ds(start, 8 * n), :].reshape(n, 8, HEAD_DIM)
            parts.append((_score(q[None], k3), v3))
        for g, (k_ref, v_ref, kn_ref, vn_ref) in ((1, (k1, v1, kn1, vn1)), (2, (k2, v2, kn2, vn2))):
            q = q_ref[g, rows, :]
            parts.append((_score(q[None], k_ref[0, :, rows, :]), v_ref[0, :, rows, :]))
            parts.append((_score(q, kn_ref[0, rows, :]), vn_ref[0, rows, :]))
        o_ref[rows, :] = _joint_softmax(parts)


def _attn_s_call(q_s, kv, caches):
    new = pl.BlockSpec((1, NEW_ROWS, HEAD_DIM), lambda b: (0, b, 0))
    in_specs = [pl.BlockSpec((N_GROUPS, NEW_ROWS, HEAD_DIM), lambda b: (0, b, 0))] + [new] * (2 * N_GROUPS)
    args = [q_s] + [kv[g][t] for g in range(N_GROUPS) for t in (1, 3)]
    for g in range(N_GROUPS):
        width = HEADS_PER_GROUP * DILATIONS[g]
        if DILATIONS[g] == 1:
            shape = (DEC_BATCH, WINDOWS[g] * HEADS_PER_GROUP, HEAD_DIM)
            spec = pl.BlockSpec((1,) + shape[1:], lambda b: (b, 0, 0))
        else:
            shape = (DEC_BATCH, SPAN, width, HEAD_DIM)
            spec = pl.BlockSpec((1, SPAN, NEW_ROWS, HEAD_DIM), lambda b: (b, 0, 0, 0))
        in_specs += [spec, spec]
        args += [caches[g][0].reshape(shape), caches[g][1].reshape(shape)]
    rows0 = WINDOWS[0] * HEADS_PER_GROUP + NEW_ROWS
    return pl.pallas_call(
        _attn_s_kernel,
        grid=(DEC_BATCH,),
        in_specs=in_specs,
        out_specs=pl.BlockSpec((NEW_ROWS, HEAD_DIM), lambda b: (b, 0)),
        out_shape=jax.ShapeDtypeStruct((4 * ROWS_S, HEAD_DIM), F32),
        scratch_shapes=[pltpu.VMEM((rows0, HEAD_DIM), F32)] * 2,
        compiler_params=_params(("arbitrary",)),
        name="attn_sample",
    )(*args)


def _mix_kernel(*refs):
    (yp_ref, ys_ref, gp_ref, gs_ref) = refs[0:4]
    op_refs, lp_refs = refs[4:7], refs[7:10]
    os_ref, wpw_ref, bpw_ref, wo_ref, mp_ref, ms_ref = refs[10:16]
    i = pl.program_id(0)

    def combined_prompt_head(h):
        sl = pl.ds(h, TM2, stride=HEADS_PER_GROUP)
        ls = [l[sl, :] for l in lp_refs]
        mx = jnp.maximum(jnp.maximum(ls[0], ls[1]), ls[2])
        es = [jnp.exp(l - mx) for l in ls]
        num = es[0] * op_refs[0][sl, :] + es[1] * op_refs[1][sl, :] + es[2] * op_refs[2][sl, :]
        return num / (es[0] + es[1] + es[2])

    def sample_head(h):
        return os_ref[pl.ds(h, ROWS_S, stride=HEADS_PER_GROUP), :]

    def mix(y_ref, g_ref, head, m_ref, rows):
        conv = jnp.dot(y_ref[...], wpw_ref[...], preferred_element_type=F32) + bpw_ref[...]
        att = jnp.zeros((rows, D_MODEL), F32)
        for h in range(HEADS_PER_GROUP):
            att = att + jnp.dot(head(h).astype(BF16), wo_ref[h * HEAD_DIM:(h + 1) * HEAD_DIM, :],
                                preferred_element_type=F32)
        gates = g_ref[...].astype(F32)
        m_ref[...] = (gates[:, :D_MODEL] * conv + gates[:, D_MODEL:] * att).astype(BF16)

    @pl.when(i < NP2)
    def _():
        mix(yp_ref, gp_ref, combined_prompt_head, mp_ref, TM2)

    @pl.when(i == NP2)
    def _():
        mix(ys_ref, gs_ref, sample_head, ms_ref, ROWS_S)


def _mix_call(y_p, y_s, gate_p, gate_s, o_p, l_p, o_s, w_pw, b_pw, w_o):
    pi = lambda i: (_pidx(i, NP2), 0)
    zero = lambda i: (0, 0)
    hp = pl.BlockSpec((4 * TM2, HEAD_DIM), pi)
    hs = pl.BlockSpec((4 * ROWS_S, HEAD_DIM), zero)
    return pl.pallas_call(
        _mix_kernel,
        grid=(NP2 + 1,),
        in_specs=[
            pl.BlockSpec((TM2, D_CONV), pi), pl.BlockSpec((ROWS_S, D_CONV), zero),
            pl.BlockSpec((TM2, 2 * D_MODEL), pi), pl.BlockSpec((ROWS_S, 2 * D_MODEL), zero),
            hp, hp, hp, hp, hp, hp, hs,
            pl.BlockSpec((D_CONV, D_MODEL), zero), pl.BlockSpec((1, D_MODEL), zero),
            pl.BlockSpec((GROUP_COLS, D_MODEL), zero),
        ],
        out_specs=[pl.BlockSpec((TM2, D_MODEL), pi), pl.BlockSpec((ROWS_S, D_MODEL), zero)],
        out_shape=[jax.ShapeDtypeStruct((ROWS_P, D_MODEL), BF16),
                   jax.ShapeDtypeStruct((ROWS_S, D_MODEL), BF16)],
        compiler_params=_params(("arbitrary",)),
        name="mix",
    )(y_p, y_s, gate_p, gate_s, *o_p, *l_p, o_s, w_pw, b_pw, w_o)


def _outproj_kernel(mp_ref, ms_ref, xp_ref, xs_ref, w_ref, op_ref, os_ref):
    i = pl.program_id(0)

    @pl.when(i < NP2)
    def _():
        op_ref[...] = xp_ref[...] + jnp.dot(mp_ref[...], w_ref[...], preferred_element_type=F32)

    @pl.when(i == NP2)
    def _():
        os_ref[...] = xs_ref[...] + jnp.dot(ms_ref[...], w_ref[...], preferred_element_type=F32)


def _outproj_call(m_p, m_s, xp, xs, w_out):
    pi = lambda i: (_pidx(i, NP2), 0)
    zero = lambda i: (0, 0)
    return pl.pallas_call(
        _outproj_kernel,
        grid=(NP2 + 1,),
        in_specs=[
            pl.BlockSpec((TM2, D_MODEL), pi), pl.BlockSpec((ROWS_S, D_MODEL), zero),
            pl.BlockSpec((TM2, D_MODEL), pi), pl.BlockSpec((ROWS_S, D_MODEL), zero),
            pl.BlockSpec((D_MODEL, D_MODEL), zero),
        ],
        out_specs=[pl.BlockSpec((TM2, D_MODEL), pi), pl.BlockSpec((ROWS_S, D_MODEL), zero)],
        out_shape=[jax.ShapeDtypeStruct((ROWS_P, D_MODEL), F32),
                   jax.ShapeDtypeStruct((ROWS_S, D_MODEL), F32)],
        compiler_params=_params(("arbitrary",)),
        name="out_proj",
    )(m_p, m_s, xp, xs, w_out)


N_CACHE = 2 * N_GROUPS
COPY_STEPS = 2 * DEC_BATCH


def _cache_shift_copies(step, which, cache_refs, new_refs, out_refs, sems):
    b = step // 2
    copies = []
    for g in range(N_GROUPS):
        n = 2 * g + which
        rows = WINDOWS[g] * HEADS_PER_GROUP
        keep = rows - NEW_ROWS
        copies.append(pltpu.make_async_copy(cache_refs[n].at[b, pl.ds(NEW_ROWS, keep)],
                                            out_refs[n].at[b, pl.ds(0, keep)], sems.at[2 * g]))
        copies.append(pltpu.make_async_copy(new_refs[n].at[b], out_refs[n].at[b, pl.ds(keep, NEW_ROWS)],
                                            sems.at[2 * g + 1]))
    return copies


def _ffn_kernel(*refs):
    xp_ref, xs_ref, gf_ref, wg_ref, wu_ref, wd_ref, gl_ref = refs[0:7]
    cache_refs = refs[7:7 + N_CACHE]
    new_refs = refs[7 + N_CACHE:7 + 2 * N_CACHE]
    yp_ref, ys_ref = refs[7 + 2 * N_CACHE:9 + 2 * N_CACHE]
    out_refs = refs[9 + 2 * N_CACHE:9 + 3 * N_CACHE]
    hp_s, hs_s, sems = refs[9 + 3 * N_CACHE:]
    i = pl.program_id(0)
    k = pl.program_id(1)

    step = i * KH + k

    def cache_copies(action):
        for which in range(2):
            @pl.when((step < COPY_STEPS) & (step % 2 == which))
            def _():
                for c in _cache_shift_copies(step, which, cache_refs, new_refs, out_refs, sems):
                    action(c)

    cache_copies(lambda c: c.start())

    def ffn(x_ref, h_s, y_ref):
        @pl.when(k == 0)
        def _():
            h_s[...] = _rms(x_ref[...], gf_ref[...]).astype(BF16)
            y_ref[...] = jnp.zeros(y_ref.shape, F32)

        h = h_s[...]
        a = jnp.dot(h, wg_ref[...], preferred_element_type=F32)
        u = jnp.dot(h, wu_ref[...], preferred_element_type=F32)
        act = (a * _sigmoid(a) * u).astype(BF16)
        y_ref[...] += jnp.dot(act, wd_ref[...], preferred_element_type=F32)

        @pl.when(k == KH - 1)
        def _():
            y_ref[...] = _rms(x_ref[...] + y_ref[...], gl_ref[...])

    ffn(xp_ref, hp_s, yp_ref)

    @pl.when(i == 0)
    def _():
        ffn(xs_ref, hs_s, ys_ref)

    cache_copies(lambda c: c.wait())


def _ffn_call(x1_p, x1_s, g_ffn, w_gate, w_up, w_down, g_final, caches, new_rows):
    assert NP * KH >= COPY_STEPS
    pi = lambda i, k: (i, 0)
    zero = lambda i, k: (0, 0)
    hbm = pl.BlockSpec(memory_space=pl.ANY)
    return pl.pallas_call(
        _ffn_kernel,
        grid=(NP, KH),
        in_specs=[
            pl.BlockSpec((TM, D_MODEL), pi, pipeline_mode=pl.Buffered(1)),
            pl.BlockSpec((ROWS_S, D_MODEL), zero, pipeline_mode=pl.Buffered(1)),
            pl.BlockSpec((1, D_MODEL), zero),
            pl.BlockSpec((D_MODEL, TH), lambda i, k: (0, k)),
            pl.BlockSpec((D_MODEL, TH), lambda i, k: (0, k)),
            pl.BlockSpec((TH, D_MODEL), lambda i, k: (k, 0)),
            pl.BlockSpec((1, D_MODEL), zero),
        ] + [hbm] * (2 * N_CACHE),
        out_specs=[pl.BlockSpec((TM, D_MODEL), pi), pl.BlockSpec((ROWS_S, D_MODEL), zero)] + [hbm] * N_CACHE,
        out_shape=[jax.ShapeDtypeStruct((ROWS_P, D_MODEL), F32),
                   jax.ShapeDtypeStruct((ROWS_S, D_MODEL), F32)]
        + [jax.ShapeDtypeStruct(c.shape, F32) for c in caches],
        scratch_shapes=[pltpu.VMEM((TM, D_MODEL), BF16), pltpu.VMEM((ROWS_S, D_MODEL), BF16),
                        pltpu.SemaphoreType.DMA((N_CACHE,))],
        compiler_params=_params(("arbitrary", "arbitrary"), FFN_VMEM_LIMIT),
        name="ffn",
    )(x1_p, x1_s, g_ffn, w_gate, w_up, w_down, g_final, *caches, *new_rows)


def _rope_tables(pos):
    half = ROT_DIM // 2
    inv = jnp.power(ROPE_THETA, -jnp.arange(0, ROT_DIM, 2, dtype=F32) / ROT_DIM)
    ang = pos.astype(F32)[:, None] * inv[None, :]
    cos, sin = jnp.cos(ang), jnp.sin(ang)
    n = pos.shape[0]
    ones = jnp.ones((n, HEAD_DIM - ROT_DIM), F32)
    zeros = jnp.zeros((n, HEAD_DIM - ROT_DIM), F32)
    zh = jnp.zeros((n, half), F32)
    c = jnp.concatenate([cos, cos, ones], axis=1)
    s_lo = jnp.concatenate([-sin, zh, zeros], axis=1)
    s_hi = jnp.concatenate([zh, sin, zeros], axis=1)
    return jnp.stack([c, s_lo, s_hi])


def kernel(x_prompt, x_sample, state_conv, cache_k_w128, cache_v_w128, cache_k_w512, cache_v_w512,
           cache_k_w2048, cache_v_w2048, g_mix, w_in, b_glu, w_dw, b_dw, ln_g, ln_b, w_pw, b_pw,
           w_o_att, w_out, g_ffn, w_gate, w_up, w_down, g_final):
    xp = x_prompt.reshape(ROWS_P, D_MODEL)
    xs = x_sample.reshape(ROWS_S, D_MODEL)
    w_in2 = w_in.reshape(D_MODEL, IN_COLS)

    hp, hs = _rms_call(xp, xs, g_mix)
    u_p, u_s = _glu_call(hp, hs, w_in2, b_glu)
    tab_p = _rope_tables(jnp.arange(SEQ, dtype=jnp.int32))
    tab_s = _rope_tables(PAST_LEN + jnp.arange(ROWS_S, dtype=jnp.int32) % DEC_SEQ)
    q_p, q_s = _q_call(hp, hs, w_in2, tab_p, tab_s)
    kv = [_kv_call(hp, hs, w_in2, tab_p, tab_s, g) for g in range(N_GROUPS)]
    gate_p, gate_s = _gate_call(hp, hs, w_in2)

    w_dw2 = w_dw.reshape(CONV_WIDTH, D_CONV)
    y_p = _conv_p_call(u_p, w_dw2, b_dw, ln_g, ln_b)
    ext_s = jnp.concatenate([state_conv[0], u_s.reshape(DEC_BATCH, DEC_SEQ, D_CONV)], axis=1)
    y_s = _conv_s_call(ext_s, w_dw2, b_dw, ln_g, ln_b).reshape(ROWS_S, D_CONV).astype(BF16)
    conv_p = u_p.reshape(BATCH, SEQ, D_CONV)[:, SEQ - (CONV_WIDTH - 1):][None]
    conv_s = ext_s[:, DEC_SEQ:][None]

    caches = ((cache_k_w128, cache_v_w128), (cache_k_w512, cache_v_w512), (cache_k_w2048, cache_v_w2048))
    o_p, l_p = [], []
    for g in range(N_GROUPS):
        k_p, _, v_p, _ = kv[g]
        o, l = _attn_p_call(q_p, k_p, v_p, g)
        o_p.append(o)
        l_p.append(l)
    o_s = _attn_s_call(q_s, kv, caches)

    m_p, m_s = _mix_call(y_p, y_s, gate_p, gate_s, o_p, l_p, o_s,
                         w_pw.reshape(D_CONV, D_MODEL).astype(BF16), b_pw,
                         w_o_att.reshape(GROUP_COLS, D_MODEL).astype(BF16))
    x1_p, x1_s = _outproj_call(m_p, m_s, xp, xs, w_out.reshape(D_MODEL, D_MODEL).astype(BF16))

    flat = lambda g, a: a.reshape(DEC_BATCH, WINDOWS[g] * HEADS_PER_GROUP, HEAD_DIM)
    cache_flat = [flat(g, caches[g][t]) for g in range(N_GROUPS) for t in range(2)]
    new_rows = [kv[g][t].reshape(DEC_BATCH, NEW_ROWS, HEAD_DIM) for g in range(N_GROUPS) for t in (1, 3)]
    y_p2, y_s2, *shifted = _ffn_call(x1_p, x1_s, g_ffn,
                                     w_gate.reshape(D_MODEL, FFN_HIDDEN).astype(BF16),
                                     w_up.reshape(D_MODEL, FFN_HIDDEN).astype(BF16),
                                     w_down.reshape(FFN_HIDDEN, D_MODEL).astype(BF16),
                                     g_final.reshape(1, D_MODEL), cache_flat, new_rows)

    cache_out = []
    for g in range(N_GROUPS):
        keep = min(WINDOWS[g], SEQ)
        for t in range(2):
            full = kv[g][2 * t].reshape(BATCH, SEQ, HEADS_PER_GROUP, HEAD_DIM)
            cache_out.append(full[:, SEQ - keep:][None])
            cache_out.append(shifted[2 * g + t].reshape(1, DEC_BATCH, WINDOWS[g], HEADS_PER_GROUP, HEAD_DIM))
    return (y_p2.reshape(BATCH, SEQ, D_MODEL), y_s2.reshape(DEC_BATCH, DEC_SEQ, D_MODEL),
            conv_p, conv_s, *cache_out)
```

```python
import functools

import jax
import jax.numpy as jnp
from jax import lax
from jax.experimental import pallas as pl
from jax.experimental.pallas import tpu as pltpu

F32 = jnp.float32
BF16 = jnp.bfloat16

D_MODEL = 2048
BATCH = 4
SEQ = 2048
DEC_BATCH = 32
DEC_SEQ = 4
PAST_LEN = 8192
HEAD_DIM = 128
HEADS_PER_GROUP = 4
DILATIONS = (1, 4, 16)
WINDOWS = (128, 512, 2048)
N_GROUPS = 3
GROUP_COLS = HEADS_PER_GROUP * HEAD_DIM
ATT_WIDTH = N_GROUPS * GROUP_COLS
SPAN = 128
ATT_SCALE = HEAD_DIM ** -0.5
ROT_DIM = HEAD_DIM // 4
ROPE_THETA = 500000.0
D_CONV = D_MODEL // 2
CONV_WIDTH = 31
FFN_HIDDEN = 5632
NORM_EPS = 1e-6
LN_EPS = 1e-5
IN_COLS = 2 * D_CONV + 3 * ATT_WIDTH + 2 * D_MODEL

ROWS_P = BATCH * SEQ
ROWS_S = DEC_BATCH * DEC_SEQ
NEG = -1e30

VMEM_LIMIT = 56 * 1024 * 1024
FFN_VMEM_LIMIT = 60 * 1024 * 1024

TM = 1024
NP = ROWS_P // TM
TN = 512
NC = 256
TM2 = 512
NP2 = ROWS_P // TM2
TH = 512
KH = FFN_HIDDEN // TH
TC = 256
HALO = 32
CS = 4


def _sigmoid(x):
    return 1.0 / (1.0 + jnp.exp(-x))


def _rms(x, g):
    return x * lax.rsqrt(jnp.mean(x * x, axis=-1, keepdims=True) + NORM_EPS) * g


def _params(sem, vmem_limit=VMEM_LIMIT):
    return pltpu.CompilerParams(dimension_semantics=sem, vmem_limit_bytes=vmem_limit)


def _pidx(i, n):
    return jnp.minimum(i, n - 1)


def _rms_kernel(xp_ref, xs_ref, g_ref, hp_ref, hs_ref):
    i = pl.program_id(0)

    @pl.when(i < NP)
    def _():
        hp_ref[...] = _rms(xp_ref[...], g_ref[...]).astype(BF16)

    @pl.when(i == NP)
    def _():
        hs_ref[...] = _rms(xs_ref[...], g_ref[...]).astype(BF16)


def _rms_call(xp, xs, g):
    return pl.pallas_call(
        _rms_kernel,
        grid=(NP + 1,),
        in_specs=[
            pl.BlockSpec((TM, D_MODEL), lambda i: (_pidx(i, NP), 0)),
            pl.BlockSpec((ROWS_S, D_MODEL), lambda i: (0, 0)),
            pl.BlockSpec((1, D_MODEL), lambda i: (0, 0)),
        ],
        out_specs=[
            pl.BlockSpec((TM, D_MODEL), lambda i: (_pidx(i, NP), 0)),
            pl.BlockSpec((ROWS_S, D_MODEL), lambda i: (0, 0)),
        ],
        out_shape=[
            jax.ShapeDtypeStruct((ROWS_P, D_MODEL), BF16),
            jax.ShapeDtypeStruct((ROWS_S, D_MODEL), BF16),
        ],
        compiler_params=_params(("arbitrary",)),
        name="rms_in",
    )(xp, xs, g)


def _glu_kernel(hp_ref, hs_ref, wa_ref, wb_ref, ba_ref, bb_ref, up_ref, us_ref, wa_s, wb_s):
    i = pl.program_id(1)

    @pl.when(i == 0)
    def _():
        wa_s[...] = wa_ref[...].astype(BF16)
        wb_s[...] = wb_ref[...].astype(BF16)

    def glu(h, u_ref):
        for c in range(TN // NC):
            sl = slice(c * NC, (c + 1) * NC)
            za = jnp.dot(h, wa_s[:, sl], preferred_element_type=F32) + ba_ref[:, sl]
            zb = jnp.dot(h, wb_s[:, sl], preferred_element_type=F32) + bb_ref[:, sl]
            u_ref[:, sl] = za * _sigmoid(zb)

    @pl.when(i < NP)
    def _():
        glu(hp_ref[...], up_ref)

    @pl.when(i == NP)
    def _():
        glu(hs_ref[...], us_ref)


def _glu_call(hp, hs, w_in, b_glu):
    nj = D_CONV // TN
    return pl.pallas_call(
        _glu_kernel,
        grid=(nj, NP + 1),
        in_specs=[
            pl.BlockSpec((TM, D_MODEL), lambda j, i: (_pidx(i, NP), 0)),
            pl.BlockSpec((ROWS_S, D_MODEL), lambda j, i: (0, 0)),
            pl.BlockSpec((D_MODEL, TN), lambda j, i: (0, j)),
            pl.BlockSpec((D_MODEL, TN), lambda j, i: (0, j + nj)),
            pl.BlockSpec((1, TN), lambda j, i: (0, j)),
            pl.BlockSpec((1, TN), lambda j, i: (0, j + nj)),
        ],
        out_specs=[
            pl.BlockSpec((TM, TN), lambda j, i: (_pidx(i, NP), j)),
            pl.BlockSpec((ROWS_S, TN), lambda j, i: (0, j)),
        ],
        out_shape=[
            jax.ShapeDtypeStruct((ROWS_P, D_CONV), F32),
            jax.ShapeDtypeStruct((ROWS_S, D_CONV), F32),
        ],
        scratch_shapes=[pltpu.VMEM((D_MODEL, TN), BF16), pltpu.VMEM((D_MODEL, TN), BF16)],
        compiler_params=_params(("arbitrary", "arbitrary")),
        name="in_glu",
    )(hp, hs, w_in, w_in, b_glu, b_glu)


def _project_heads(h, w_s, t_ref, o_ref, rows):
    if t_ref is not None:
        cos, sin_lo, sin_hi = t_ref[0], t_ref[1], t_ref[2]
    for c in range(TN // NC):
        z = jnp.dot(h, w_s[:, c * NC:(c + 1) * NC], preferred_element_type=F32)
        for hc in range(NC // HEAD_DIM):
            t = z[:, hc * HEAD_DIM:(hc + 1) * HEAD_DIM]
            if t_ref is not None:
                t = (t * cos + pltpu.roll(t, HEAD_DIM - ROT_DIM // 2, 1) * sin_lo
                     + pltpu.roll(t, ROT_DIM // 2, 1) * sin_hi)
            head = c * (NC // HEAD_DIM) + hc
            o_ref[0, pl.ds(head, rows, stride=HEADS_PER_GROUP), :] = t


def _q_kernel(hp_ref, hs_ref, w_ref, tp_ref, ts_ref, op_ref, os_ref, w_s):
    i = pl.program_id(1)

    @pl.when(i == 0)
    def _():
        w_s[...] = w_ref[...].astype(BF16)

    @pl.when(i < NP)
    def _():
        _project_heads(hp_ref[...], w_s, tp_ref, op_ref, TM)

    @pl.when(i == NP)
    def _():
        _project_heads(hs_ref[...], w_s, ts_ref, os_ref, ROWS_S)


def _kv_kernel(hp_ref, hs_ref, wk_ref, wv_ref, tp_ref, ts_ref, kp_ref, ks_ref, vp_ref, vs_ref, wk_s, wv_s):
    i = pl.program_id(0)

    @pl.when(i == 0)
    def _():
        wk_s[...] = wk_ref[...].astype(BF16)
        wv_s[...] = wv_ref[...].astype(BF16)

    @pl.when(i < NP)
    def _():
        h = hp_ref[...]
        _project_heads(h, wk_s, tp_ref, kp_ref, TM)
        _project_heads(h, wv_s, None, vp_ref, TM)

    @pl.when(i == NP)
    def _():
        h = hs_ref[...]
        _project_heads(h, wk_s, ts_ref, ks_ref, ROWS_S)
        _project_heads(h, wv_s, None, vs_ref, ROWS_S)


Q_PLANE0 = 2 * D_CONV // TN


def _q_call(hp, hs, w_in, tab_p, tab_s):
    return pl.pallas_call(
        _q_kernel,
        grid=(N_GROUPS, NP + 1),
        in_specs=[
            pl.BlockSpec((TM, D_MODEL), lambda j, i: (_pidx(i, NP), 0)),
            pl.BlockSpec((ROWS_S, D_MODEL), lambda j, i: (0, 0)),
            pl.BlockSpec((D_MODEL, TN), lambda j, i: (0, j + Q_PLANE0)),
            pl.BlockSpec((3, TM, HEAD_DIM), lambda j, i: (0, _pidx(i, NP) % (SEQ // TM), 0)),
            pl.BlockSpec((3, ROWS_S, HEAD_DIM), lambda j, i: (0, 0, 0)),
        ],
        out_specs=[
            pl.BlockSpec((1, 4 * TM, HEAD_DIM), lambda j, i: (j, _pidx(i, NP), 0)),
            pl.BlockSpec((1, 4 * ROWS_S, HEAD_DIM), lambda j, i: (j, 0, 0)),
        ],
        out_shape=[
            jax.ShapeDtypeStruct((N_GROUPS, 4 * ROWS_P, HEAD_DIM), F32),
            jax.ShapeDtypeStruct((N_GROUPS, 4 * ROWS_S, HEAD_DIM), F32),
        ],
        scratch_shapes=[pltpu.VMEM((D_MODEL, TN), BF16)],
        compiler_params=_params(("arbitrary", "arbitrary")),
        name="in_q",
    )(hp, hs, w_in, tab_p, tab_s)


def _kv_call(hp, hs, w_in, tab_p, tab_s, g):
    wspec = lambda plane: pl.BlockSpec((D_MODEL, TN), lambda i: (0, Q_PLANE0 + plane))
    outp = pl.BlockSpec((1, 4 * TM, HEAD_DIM), lambda i: (0, _pidx(i, NP), 0))
    outs = pl.BlockSpec((1, 4 * ROWS_S, HEAD_DIM), lambda i: (0, 0, 0))
    shp = jax.ShapeDtypeStruct((1, 4 * ROWS_P, HEAD_DIM), F32)
    shs = jax.ShapeDtypeStruct((1, 4 * ROWS_S, HEAD_DIM), F32)
    return pl.pallas_call(
        _kv_kernel,
        grid=(NP + 1,),
        in_specs=[
            pl.BlockSpec((TM, D_MODEL), lambda i: (_pidx(i, NP), 0)),
            pl.BlockSpec((ROWS_S, D_MODEL), lambda i: (0, 0)),
            wspec(N_GROUPS + g), wspec(2 * N_GROUPS + g),
            pl.BlockSpec((3, TM, HEAD_DIM), lambda i: (0, _pidx(i, NP) % (SEQ // TM), 0)),
            pl.BlockSpec((3, ROWS_S, HEAD_DIM), lambda i: (0, 0, 0)),
        ],
        out_specs=[outp, outs, outp, outs],
        out_shape=[shp, shs, shp, shs],
        scratch_shapes=[pltpu.VMEM((D_MODEL, TN), BF16), pltpu.VMEM((D_MODEL, TN), BF16)],
        compiler_params=_params(("arbitrary",)),
        name="in_kv%d" % g,
    )(hp, hs, w_in, w_in, tab_p, tab_s)


def _gate_kernel(hp_ref, hs_ref, wa_ref, wb_ref, op_ref, os_ref, w_s):
    i = pl.program_id(1)

    @pl.when(i == 0)
    def _():
        w_s[:, 0:TN] = wa_ref[...].astype(BF16)
        w_s[:, TN:2 * TN] = wb_ref[...].astype(BF16)

    def gate(h, o_ref):
        for c in range(2 * TN // NC):
            sl = slice(c * NC, (c + 1) * NC)
            o_ref[:, sl] = _sigmoid(jnp.dot(h, w_s[:, sl], preferred_element_type=F32)).astype(BF16)

    @pl.when(i < NP)
    def _():
        gate(hp_ref[...], op_ref)

    @pl.when(i == NP)
    def _():
        gate(hs_ref[...], os_ref)


def _gate_call(hp, hs, w_in):
    nj = 2 * D_MODEL // (2 * TN)
    off = (2 * D_CONV + 3 * ATT_WIDTH) // TN
    return pl.pallas_call(
        _gate_kernel,
        grid=(nj, NP + 1),
        in_specs=[
            pl.BlockSpec((TM, D_MODEL), lambda j, i: (_pidx(i, NP), 0)),
            pl.BlockSpec((ROWS_S, D_MODEL), lambda j, i: (0, 0)),
            pl.BlockSpec((D_MODEL, TN), lambda j, i: (0, 2 * j + off)),
            pl.BlockSpec((D_MODEL, TN), lambda j, i: (0, 2 * j + 1 + off)),
        ],
        out_specs=[
            pl.BlockSpec((TM, 2 * TN), lambda j, i: (_pidx(i, NP), j)),
            pl.BlockSpec((ROWS_S, 2 * TN), lambda j, i: (0, j)),
        ],
        out_shape=[
            jax.ShapeDtypeStruct((ROWS_P, 2 * D_MODEL), BF16),
            jax.ShapeDtypeStruct((ROWS_S, 2 * D_MODEL), BF16),
        ],
        scratch_shapes=[pltpu.VMEM((D_MODEL, 2 * TN), BF16)],
        compiler_params=_params(("arbitrary", "arbitrary")),
        name="in_gate",
    )(hp, hs, w_in, w_in)


def _ln_silu(y, g, b):
    mu = jnp.mean(y, axis=-1, keepdims=True)
    yc = y - mu
    var = jnp.mean(yc * yc, axis=-1, keepdims=True)
    z = yc * lax.rsqrt(var + LN_EPS) * g + b
    return z * _sigmoid(z)


def _conv_p_kernel(cur_ref, prev_ref, w_ref, bdw_ref, g_ref, b_ref, y_ref, ext, ypre):
    c = pl.program_id(1)
    n_lc = D_CONV // 128
    for lc in range(n_lc):
        sl = slice(lc * 128, (lc + 1) * 128)
        ext[lc, pl.ds(HALO, TC), :] = cur_ref[:, sl]

    @pl.when(c > 0)
    def _():
        for lc in range(n_lc):
            ext[lc, 0:HALO, :] = prev_ref[TC - HALO:TC, lc * 128:(lc + 1) * 128]

    @pl.when(c == 0)
    def _():
        for lc in range(n_lc):
            ext[lc, 0:HALO, :] = jnp.zeros((HALO, 128), F32)

    base = HALO - (CONV_WIDTH - 1)
    for lc in range(n_lc):
        sl = slice(lc * 128, (lc + 1) * 128)
        for t0 in range(CS):
            acc = jnp.zeros((TC // CS, 128), F32)
            for j in range(CONV_WIDTH):
                acc = acc + ext[lc, pl.ds(base + t0 + j, TC // CS, stride=CS), :] * w_ref[j:j + 1, sl]
            ypre[lc, pl.ds(t0, TC // CS, stride=CS), :] = acc + bdw_ref[:, sl]

    total = jnp.zeros((TC, 1), F32)
    for lc in range(n_lc):
        total = total + jnp.sum(ypre[lc], axis=-1, keepdims=True)
    mu = total * (1.0 / D_CONV)
    sq = jnp.zeros((TC, 1), F32)
    for lc in range(n_lc):
        yc = ypre[lc] - mu
        sq = sq + jnp.sum(yc * yc, axis=-1, keepdims=True)
    inv = lax.rsqrt(sq * (1.0 / D_CONV) + LN_EPS)
    for lc in range(n_lc):
        sl = slice(lc * 128, (lc + 1) * 128)
        z = (ypre[lc] - mu) * inv * g_ref[:, sl] + b_ref[:, sl]
        y_ref[:, sl] = (z * _sigmoid(z)).astype(BF16)


def _conv_p_call(u_p, w_dw, b_dw, ln_g, ln_b):
    nc = SEQ // TC
    vec = pl.BlockSpec((1, D_CONV), lambda b, c: (0, 0))
    return pl.pallas_call(
        _conv_p_kernel,
        grid=(BATCH, nc),
        in_specs=[
            pl.BlockSpec((TC, D_CONV), lambda b, c: (b * nc + c, 0)),
            pl.BlockSpec((TC, D_CONV), lambda b, c: (b * nc + jnp.maximum(c - 1, 0), 0)),
            pl.BlockSpec((CONV_WIDTH, D_CONV), lambda b, c: (0, 0)),
            vec, vec, vec,
        ],
        out_specs=pl.BlockSpec((TC, D_CONV), lambda b, c: (b * nc + c, 0)),
        out_shape=jax.ShapeDtypeStruct((ROWS_P, D_CONV), BF16),
        scratch_shapes=[pltpu.VMEM((D_CONV // 128, HALO + TC, 128), F32),
                        pltpu.VMEM((D_CONV // 128, TC, 128), F32)],
        compiler_params=_params(("arbitrary", "arbitrary")),
        name="conv_prompt",
    )(u_p, u_p, w_dw, b_dw, ln_g, ln_b)


CONV_SB = 8


def _conv_s_kernel(ext_ref, w_ref, bdw_ref, g_ref, b_ref, y_ref):
    acc = jnp.zeros((CONV_SB, DEC_SEQ, D_CONV), F32)
    for j in range(CONV_WIDTH):
        acc = acc + ext_ref[:, pl.ds(j, DEC_SEQ), :] * w_ref[j:j + 1, :]
    y_ref[...] = _ln_silu(acc + bdw_ref[...], g_ref[...], b_ref[...])


def _conv_s_call(ext_s, w_dw, b_dw, ln_g, ln_b):
    vec = pl.BlockSpec((1, D_CONV), lambda b: (0, 0))
    rows = CONV_WIDTH - 1 + DEC_SEQ
    return pl.pallas_call(
        _conv_s_kernel,
        grid=(DEC_BATCH // CONV_SB,),
        in_specs=[
            pl.BlockSpec((CONV_SB, rows, D_CONV), lambda b: (b, 0, 0)),
            pl.BlockSpec((CONV_WIDTH, D_CONV), lambda b: (0, 0)),
            vec, vec, vec,
        ],
        out_specs=pl.BlockSpec((CONV_SB, DEC_SEQ, D_CONV), lambda b: (b, 0, 0)),
        out_shape=jax.ShapeDtypeStruct((DEC_BATCH, DEC_SEQ, D_CONV), F32),
        compiler_params=_params(("arbitrary",)),
        name="conv_sample",
    )(ext_s, w_dw, b_dw, ln_g, ln_b)


def _softmax_block(q, k, v, mask):
    s = lax.dot_general(q.astype(BF16), k.astype(BF16), (((2,), (2,)), ((0,), (0,))),
                        preferred_element_type=F32) * ATT_SCALE
    s = jnp.where(mask[None], s, NEG)
    m = jnp.max(s, axis=-1, keepdims=True)
    p = jnp.exp(s - m)
    den = jnp.sum(p, axis=-1, keepdims=True)
    o = lax.dot_general(p.astype(BF16), v.astype(BF16), (((2,), (1,)), ((0,), (0,))),
                        preferred_element_type=F32) / den
    lse = m + jnp.log(den)
    return o, jnp.broadcast_to(lse, o.shape)


def _make_attn_p_kernel(dil):
    stride = HEADS_PER_GROUP * dil
    nblk = SEQ // dil // SPAN
    blk_rows = SPAN * stride
    pair = min(dil, 2)
    group_rows = pair * HEADS_PER_GROUP
    n_r2 = dil // pair

    def kern(q_ref, k_ref, v_ref, o_ref, l_ref):
        qi = lax.broadcasted_iota(jnp.int32, (SPAN, SPAN), 0)
        ki = lax.broadcasted_iota(jnp.int32, (SPAN, SPAN), 1)
        mask_first = ki <= qi
        qi2 = lax.broadcasted_iota(jnp.int32, (SPAN, 2 * SPAN), 0)
        ki2 = lax.broadcasted_iota(jnp.int32, (SPAN, 2 * SPAN), 1)
        mask_band = (ki2 >= qi2) & (ki2 <= qi2 + SPAN)

        def block(base, key_base, n_keys, mask):
            qs = [pl.ds(base + off, SPAN, stride=stride) for off in range(group_rows)]
            ks = [pl.ds(key_base + off, n_keys, stride=stride) for off in range(group_rows)]
            o, lse = _softmax_block(jnp.stack([q_ref[0, s, :] for s in qs]),
                                    jnp.stack([k_ref[0, s, :] for s in ks]),
                                    jnp.stack([v_ref[0, s, :] for s in ks]), mask)
            for u, s in enumerate(qs):
                o_ref[s, :] = o[u]
                l_ref[s, :] = lse[u]

        def first(r2, carry):
            base = pl.multiple_of(r2 * group_rows, max(group_rows, 8))
            block(base, base, SPAN, mask_first)
            return carry

        def band(t, carry):
            jb = 1 + t // n_r2
            base = pl.multiple_of(jb * blk_rows + (t % n_r2) * group_rows, max(group_rows, 8))
            block(base, base - blk_rows, 2 * SPAN, mask_band)
            return carry

        lax.fori_loop(0, n_r2, first, 0)
        if nblk > 1:
            lax.fori_loop(0, n_r2 * (nblk - 1), band, 0)

    return kern


def _attn_p_wide_kernel(q_ref, k_ref, v_ref, o_ref, l_ref):
    qi = lax.broadcasted_iota(jnp.int32, (SPAN, SPAN), 0)
    ki = lax.broadcasted_iota(jnp.int32, (SPAN, SPAN), 1)
    mask_first = ki <= qi

    def tile(c, carry):
        q8 = jnp.swapaxes(q_ref[0, 0, :, c], 0, 1)
        k8 = jnp.swapaxes(k_ref[0, 0, :, c], 0, 1)
        v8 = jnp.swapaxes(v_ref[0, 0, :, c], 0, 1)
        o, lse = _softmax_block(q8, k8, v8, mask_first)
        o_ref[0, :, c] = jnp.swapaxes(o, 0, 1)
        l_ref[0, :, c] = jnp.swapaxes(lse, 0, 1)
        return carry

    lax.fori_loop(0, q_ref.shape[3], tile, 0)


def _attn_p_call(q_p, k_p, v_p, g):
    dil = DILATIONS[g]
    rows = 4 * SEQ
    if SEQ // dil == SPAN:
        tiles = HEADS_PER_GROUP * dil // 8
        view = lambda a: a.reshape(a.shape[0], BATCH, SPAN, tiles, 8, HEAD_DIM)
        blk = lambda plane: pl.BlockSpec((1, 1, SPAN, tiles, 8, HEAD_DIM), lambda b: (plane, b, 0, 0, 0, 0))
        out = pl.BlockSpec((1, SPAN, tiles, 8, HEAD_DIM), lambda b: (b, 0, 0, 0, 0))
        o, l = pl.pallas_call(
            _attn_p_wide_kernel,
            grid=(BATCH,),
            in_specs=[blk(g), blk(0), blk(0)],
            out_specs=[out, out],
            out_shape=[jax.ShapeDtypeStruct((BATCH, SPAN, tiles, 8, HEAD_DIM), F32)] * 2,
            compiler_params=_params(("arbitrary",)),
            name="attn_prompt_g%d" % g,
        )(view(q_p), view(k_p), view(v_p))
        return o.reshape(4 * ROWS_P, HEAD_DIM), l.reshape(4 * ROWS_P, HEAD_DIM)
    blk = lambda plane: pl.BlockSpec((1, rows, HEAD_DIM), lambda b: (plane, b, 0))
    out = pl.BlockSpec((rows, HEAD_DIM), lambda b: (b, 0))
    return pl.pallas_call(
        _make_attn_p_kernel(dil),
        grid=(BATCH,),
        in_specs=[blk(g), blk(0), blk(0)],
        out_specs=[out, out],
        out_shape=[jax.ShapeDtypeStruct((4 * ROWS_P, HEAD_DIM), F32)] * 2,
        compiler_params=_params(("arbitrary",)),
        name="attn_prompt_g%d" % g,
    )(q_p, k_p, v_p)


NEW_ROWS = DEC_SEQ * HEADS_PER_GROUP


def _joint_softmax(parts):
    mx = None
    for s, _ in parts:
        m = jnp.max(s, axis=0) if s.ndim == 3 else s
        mx = m if mx is None else jnp.maximum(mx, m)
    den = jnp.zeros((8, 1), F32)
    acc = jnp.zeros((8, HEAD_DIM), F32)
    for s, v in parts:
        p = jnp.exp(s - mx)
        if s.ndim == 3:
            den = den + jnp.sum(p, axis=0)
            acc = acc + jnp.sum(p * v, axis=0)
        else:
            den = den + p
            acc = acc + p * v
    return acc / den


def _score(q, k):
    return jnp.sum(q * k, axis=-1, keepdims=True) * ATT_SCALE


def _attn_s_kernel(q_ref, kn0, vn0, kn1, vn1, kn2, vn2, k0, v0, k1, v1, k2, v2, o_ref, kc, vc):
    nb = WINDOWS[0] * HEADS_PER_GROUP
    kc[0:nb, :] = k0[0]
    kc[nb:nb + NEW_ROWS, :] = kn0[0]
    vc[0:nb, :] = v0[0]
    vc[nb:nb + NEW_ROWS, :] = vn0[0]
    half = HEADS_PER_GROUP
    n_even, n_odd = SPAN // 2 + 1, SPAN // 2
    for j in range(2):
        rows = slice(8 * j, 8 * j + 8)
        parts = []
        q = q_ref[0, rows, :]
        for start, n in ((8 * j, n_even), (8 * j + half, n_odd)):
            k3 = kc[pl.ds(start, 8 * n), :].reshape(n, 8, HEAD_DIM)
            v3 = vc[pl.ds(start, 8 * n), :].reshape(n, 8, HEAD_DIM)
            parts.append((_score(q[None], k3), v3))
        for g, (k_ref, v_ref, kn_ref, vn_ref) in ((1, (k1, v1, kn1, vn1)), (2, (k2, v2, kn2, vn2))):
            q = q_ref[g, rows, :]
            parts.append((_score(q[None], k_ref[0, :, rows, :]), v_ref[0, :, rows, :]))
            parts.append((_score(q, kn_ref[0, rows, :]), vn_ref[0, rows, :]))
        o_ref[rows, :] = _joint_softmax(parts)


def _attn_s_call(q_s, kv, caches):
    new = pl.BlockSpec((1, NEW_ROWS, HEAD_DIM), lambda b: (0, b, 0))
    in_specs = [pl.BlockSpec((N_GROUPS, NEW_ROWS, HEAD_DIM), lambda b: (0, b, 0))] + [new] * (2 * N_GROUPS)
    args = [q_s] + [kv[g][t] for g in range(N_GROUPS) for t in (1, 3)]
    for g in range(N_GROUPS):
        width = HEADS_PER_GROUP * DILATIONS[g]
        if DILATIONS[g] == 1:
            shape = (DEC_BATCH, WINDOWS[g] * HEADS_PER_GROUP, HEAD_DIM)
            spec = pl.BlockSpec((1,) + shape[1:], lambda b: (b, 0, 0))
        else:
            shape = (DEC_BATCH, SPAN, width, HEAD_DIM)
            spec = pl.BlockSpec((1, SPAN, NEW_ROWS, HEAD_DIM), lambda b: (b, 0, 0, 0))
        in_specs += [spec, spec]
        args += [caches[g][0].reshape(shape), caches[g][1].reshape(shape)]
    rows0 = WINDOWS[0] * HEADS_PER_GROUP + NEW_ROWS
    return pl.pallas_call(
        _attn_s_kernel,
        grid=(DEC_BATCH,),
        in_specs=in_specs,
        out_specs=pl.BlockSpec((NEW_ROWS, HEAD_DIM), lambda b: (b, 0)),
        out_shape=jax.ShapeDtypeStruct((4 * ROWS_S, HEAD_DIM), F32),
        scratch_shapes=[pltpu.VMEM((rows0, HEAD_DIM), F32)] * 2,
        compiler_params=_params(("arbitrary",)),
        name="attn_sample",
    )(*args)


def _mix_kernel(*refs):
    (yp_ref, ys_ref, gp_ref, gs_ref) = refs[0:4]
    op_refs, lp_refs = refs[4:7], refs[7:10]
    os_ref, wpw_ref, bpw_ref, wo_ref, mp_ref, ms_ref = refs[10:16]
    i = pl.program_id(0)

    def combined_prompt_head(h):
        sl = pl.ds(h, TM2, stride=HEADS_PER_GROUP)
        ls = [l[sl, :] for l in lp_refs]
        mx = jnp.maximum(jnp.maximum(ls[0], ls[1]), ls[2])
        es = [jnp.exp(l - mx) for l in ls]
        num = es[0] * op_refs[0][sl, :] + es[1] * op_refs[1][sl, :] + es[2] * op_refs[2][sl, :]
        return num / (es[0] + es[1] + es[2])

    def sample_head(h):
        return os_ref[pl.ds(h, ROWS_S, stride=HEADS_PER_GROUP), :]

    def mix(y_ref, g_ref, head, m_ref, rows):
        conv = jnp.dot(y_ref[...], wpw_ref[...], preferred_element_type=F32) + bpw_ref[...]
        att = jnp.zeros((rows, D_MODEL), F32)
        for h in range(HEADS_PER_GROUP):
            att = att + jnp.dot(head(h).astype(BF16), wo_ref[h * HEAD_DIM:(h + 1) * HEAD_DIM, :],
                                preferred_element_type=F32)
        gates = g_ref[...].astype(F32)
        m_ref[...] = (gates[:, :D_MODEL] * conv + gates[:, D_MODEL:] * att).astype(BF16)

    @pl.when(i < NP2)
    def _():
        mix(yp_ref, gp_ref, combined_prompt_head, mp_ref, TM2)

    @pl.when(i == NP2)
    def _():
        mix(ys_ref, gs_ref, sample_head, ms_ref, ROWS_S)


def _mix_call(y_p, y_s, gate_p, gate_s, o_p, l_p, o_s, w_pw, b_pw, w_o):
    pi = lambda i: (_pidx(i, NP2), 0)
    zero = lambda i: (0, 0)
    hp = pl.BlockSpec((4 * TM2, HEAD_DIM), pi)
    hs = pl.BlockSpec((4 * ROWS_S, HEAD_DIM), zero)
    return pl.pallas_call(
        _mix_kernel,
        grid=(NP2 + 1,),
        in_specs=[
            pl.BlockSpec((TM2, D_CONV), pi), pl.BlockSpec((ROWS_S, D_CONV), zero),
            pl.BlockSpec((TM2, 2 * D_MODEL), pi), pl.BlockSpec((ROWS_S, 2 * D_MODEL), zero),
            hp, hp, hp, hp, hp, hp, hs,
            pl.BlockSpec((D_CONV, D_MODEL), zero), pl.BlockSpec((1, D_MODEL), zero),
            pl.BlockSpec((GROUP_COLS, D_MODEL), zero),
        ],
        out_specs=[pl.BlockSpec((TM2, D_MODEL), pi), pl.BlockSpec((ROWS_S, D_MODEL), zero)],
        out_shape=[jax.ShapeDtypeStruct((ROWS_P, D_MODEL), BF16),
                   jax.ShapeDtypeStruct((ROWS_S, D_MODEL), BF16)],
        compiler_params=_params(("arbitrary",)),
        name="mix",
    )(y_p, y_s, gate_p, gate_s, *o_p, *l_p, o_s, w_pw, b_pw, w_o)


def _outproj_kernel(mp_ref, ms_ref, xp_ref, xs_ref, w_ref, op_ref, os_ref):
    i = pl.program_id(0)

    @pl.when(i < NP2)
    def _():
        op_ref[...] = xp_ref[...] + jnp.dot(mp_ref[...], w_ref[...], preferred_element_type=F32)

    @pl.when(i == NP2)
    def _():
        os_ref[...] = xs_ref[...] + jnp.dot(ms_ref[...], w_ref[...], preferred_element_type=F32)


def _outproj_call(m_p, m_s, xp, xs, w_out):
    pi = lambda i: (_pidx(i, NP2), 0)
    zero = lambda i: (0, 0)
    return pl.pallas_call(
        _outproj_kernel,
        grid=(NP2 + 1,),
        in_specs=[
            pl.BlockSpec((TM2, D_MODEL), pi), pl.BlockSpec((ROWS_S, D_MODEL), zero),
            pl.BlockSpec((TM2, D_MODEL), pi), pl.BlockSpec((ROWS_S, D_MODEL), zero),
            pl.BlockSpec((D_MODEL, D_MODEL), zero),
        ],
        out_specs=[pl.BlockSpec((TM2, D_MODEL), pi), pl.BlockSpec((ROWS_S, D_MODEL), zero)],
        out_shape=[jax.ShapeDtypeStruct((ROWS_P, D_MODEL), F32),
                   jax.ShapeDtypeStruct((ROWS_S, D_MODEL), F32)],
        compiler_params=_params(("arbitrary",)),
        name="out_proj",
    )(m_p, m_s, xp, xs, w_out)


N_CACHE = 2 * N_GROUPS
COPY_STEPS = 2 * DEC_BATCH


def _shift_copies(c, w, cache_refs, new_refs, out_refs, bufs, sems):
    b = c // 2
    loads, stores = [], []
    for g in range(N_GROUPS):
        n = 2 * g + w
        keep = WINDOWS[g] * HEADS_PER_GROUP - NEW_ROWS
        loads.append(pltpu.make_async_copy(cache_refs[n].at[b, pl.ds(NEW_ROWS, keep)], bufs[g].at[w],
                                           sems.at[0, w, g]))
        stores.append(pltpu.make_async_copy(bufs[g].at[w], out_refs[n].at[b, pl.ds(0, keep)],
                                            sems.at[1, w, g]))
        stores.append(pltpu.make_async_copy(new_refs[n].at[b], out_refs[n].at[b, pl.ds(keep, NEW_ROWS)],
                                            sems.at[2, w, g]))
    return loads, stores


def _ffn_kernel(*refs):
    x_hbm, xs_ref, gf_ref, wg_ref, wu_ref, wd_ref, gl_ref = refs[0:7]
    cache_refs = refs[7:7 + N_CACHE]
    new_refs = refs[7 + N_CACHE:7 + 2 * N_CACHE]
    yp_ref, ys_ref = refs[7 + 2 * N_CACHE:9 + 2 * N_CACHE]
    out_refs = refs[9 + 2 * N_CACHE:9 + 3 * N_CACHE]
    hp_s, hs_s = refs[9 + 3 * N_CACHE:11 + 3 * N_CACHE]
    bufs = refs[11 + 3 * N_CACHE:11 + 3 * N_CACHE + N_GROUPS]
    sems, x_sem = refs[11 + 3 * N_CACHE + N_GROUPS:]
    i = pl.program_id(0)
    k = pl.program_id(1)

    step = i * KH + k
    for w in range(2):
        @pl.when(step % 2 == w)
        def _():
            copies = lambda c, slot: _shift_copies(c, slot, cache_refs, new_refs, out_refs, bufs, sems)

            @pl.when(step == 0)
            def _():
                for d in copies(step, w)[0]:
                    d.start()

            @pl.when(step < COPY_STEPS)
            def _():
                for d in copies(step, w)[0]:
                    d.wait()

            @pl.when((step >= 1) & (step <= COPY_STEPS))
            def _():
                for d in copies(step - 1, 1 - w)[1]:
                    d.wait()

            @pl.when(step < COPY_STEPS)
            def _():
                for d in copies(step, w)[1]:
                    d.start()

            @pl.when(step + 1 < COPY_STEPS)
            def _():
                for d in copies(step + 1, 1 - w)[0]:
                    d.start()

    def ffn(load_x, h_s, y_ref):
        @pl.when(k == 0)
        def _():
            load_x()
            h_s[...] = _rms(y_ref[...], gf_ref[...]).astype(BF16)

        h = h_s[...]
        a = jnp.dot(h, wg_ref[...], preferred_element_type=F32)
        u = jnp.dot(h, wu_ref[...], preferred_element_type=F32)
        act = (a * _sigmoid(a) * u).astype(BF16)
        y_ref[...] += jnp.dot(act, wd_ref[...], preferred_element_type=F32)

        @pl.when(k == KH - 1)
        def _():
            y_ref[...] = _rms(y_ref[...], gl_ref[...])

    def load_prompt_tile():
        cp = pltpu.make_async_copy(x_hbm.at[pl.ds(pl.multiple_of(i * TM, TM), TM)], yp_ref, x_sem)
        cp.start()
        cp.wait()

    def load_sample_rows():
        ys_ref[...] = xs_ref[...]

    ffn(load_prompt_tile, hp_s, yp_ref)

    @pl.when(i == 0)
    def _():
        ffn(load_sample_rows, hs_s, ys_ref)


def _ffn_call(x1_p, x1_s, g_ffn, w_gate, w_up, w_down, g_final, caches, new_rows):
    assert NP * KH >= COPY_STEPS
    pi = lambda i, k: (i, 0)
    zero = lambda i, k: (0, 0)
    hbm = pl.BlockSpec(memory_space=pl.ANY)
    return pl.pallas_call(
        _ffn_kernel,
        grid=(NP, KH),
        in_specs=[
            hbm,
            pl.BlockSpec((ROWS_S, D_MODEL), zero, pipeline_mode=pl.Buffered(1)),
            pl.BlockSpec((1, D_MODEL), zero),
            pl.BlockSpec((D_MODEL, TH), lambda i, k: (0, k)),
            pl.BlockSpec((D_MODEL, TH), lambda i, k: (0, k)),
            pl.BlockSpec((TH, D_MODEL), lambda i, k: (k, 0)),
            pl.BlockSpec((1, D_MODEL), zero),
        ] + [hbm] * (2 * N_CACHE),
        out_specs=[pl.BlockSpec((TM, D_MODEL), pi), pl.BlockSpec((ROWS_S, D_MODEL), zero)] + [hbm] * N_CACHE,
        out_shape=[jax.ShapeDtypeStruct((ROWS_P, D_MODEL), F32),
                   jax.ShapeDtypeStruct((ROWS_S, D_MODEL), F32)]
        + [jax.ShapeDtypeStruct(c.shape, F32) for c in caches],
        scratch_shapes=[pltpu.VMEM((TM, D_MODEL), BF16), pltpu.VMEM((ROWS_S, D_MODEL), BF16)]
        + [pltpu.VMEM((2, WINDOWS[g] * HEADS_PER_GROUP - NEW_ROWS, HEAD_DIM), F32) for g in range(N_GROUPS)]
        + [pltpu.SemaphoreType.DMA((3, 2, N_GROUPS)), pltpu.SemaphoreType.DMA(())],
        compiler_params=_params(("arbitrary", "arbitrary"), FFN_VMEM_LIMIT),
        name="ffn",
    )(x1_p, x1_s, g_ffn, w_gate, w_up, w_down, g_final, *caches, *new_rows)


def _rope_tables(pos):
    half = ROT_DIM // 2
    inv = jnp.power(ROPE_THETA, -jnp.arange(0, ROT_DIM, 2, dtype=F32) / ROT_DIM)
    ang = pos.astype(F32)[:, None] * inv[None, :]
    cos, sin = jnp.cos(ang), jnp.sin(ang)
    n = pos.shape[0]
    ones = jnp.ones((n, HEAD_DIM - ROT_DIM), F32)
    zeros = jnp.zeros((n, HEAD_DIM - ROT_DIM), F32)
    zh = jnp.zeros((n, half), F32)
    c = jnp.concatenate([cos, cos, ones], axis=1)
    s_lo = jnp.concatenate([-sin, zh, zeros], axis=1)
    s_hi = jnp.concatenate([zh, sin, zeros], axis=1)
    return jnp.stack([c, s_lo, s_hi])


def kernel(x_prompt, x_sample, state_conv, cache_k_w128, cache_v_w128, cache_k_w512, cache_v_w512,
           cache_k_w2048, cache_v_w2048, g_mix, w_in, b_glu, w_dw, b_dw, ln_g, ln_b, w_pw, b_pw,
           w_o_att, w_out, g_ffn, w_gate, w_up, w_down, g_final):
    xp = x_prompt.reshape(ROWS_P, D_MODEL)
    xs = x_sample.reshape(ROWS_S, D_MODEL)
    w_in2 = w_in.reshape(D_MODEL, IN_COLS)

    hp, hs = _rms_call(xp, xs, g_mix)
    u_p, u_s = _glu_call(hp, hs, w_in2, b_glu)
    tab_p = _rope_tables(jnp.arange(SEQ, dtype=jnp.int32))
    tab_s = _rope_tables(PAST_LEN + jnp.arange(ROWS_S, dtype=jnp.int32) % DEC_SEQ)
    q_p, q_s = _q_call(hp, hs, w_in2, tab_p, tab_s)
    kv = [_kv_call(hp, hs, w_in2, tab_p, tab_s, g) for g in range(N_GROUPS)]
    gate_p, gate_s = _gate_call(hp, hs, w_in2)

    w_dw2 = w_dw.reshape(CONV_WIDTH, D_CONV)
    y_p = _conv_p_call(u_p, w_dw2, b_dw, ln_g, ln_b)
    ext_s = jnp.concatenate([state_conv[0], u_s.reshape(DEC_BATCH, DEC_SEQ, D_CONV)], axis=1)
    y_s = _conv_s_call(ext_s, w_dw2, b_dw, ln_g, ln_b).reshape(ROWS_S, D_CONV).astype(BF16)
    conv_p = u_p.reshape(BATCH, SEQ, D_CONV)[:, SEQ - (CONV_WIDTH - 1):][None]
    conv_s = ext_s[:, DEC_SEQ:][None]

    caches = ((cache_k_w128, cache_v_w128), (cache_k_w512, cache_v_w512), (cache_k_w2048, cache_v_w2048))
    o_p, l_p = [], []
    for g in range(N_GROUPS):
        k_p, _, v_p, _ = kv[g]
        o, l = _attn_p_call(q_p, k_p, v_p, g)
        o_p.append(o)
        l_p.append(l)
    o_s = _attn_s_call(q_s, kv, caches)

    m_p, m_s = _mix_call(y_p, y_s, gate_p, gate_s, o_p, l_p, o_s,
                         w_pw.reshape(D_CONV, D_MODEL).astype(BF16), b_pw,
                         w_o_att.reshape(GROUP_COLS, D_MODEL).astype(BF16))
    x1_p, x1_s = _outproj_call(m_p, m_s, xp, xs, w_out.reshape(D_MODEL, D_MODEL).astype(BF16))

    flat = lambda g, a: a.reshape(DEC_BATCH, WINDOWS[g] * HEADS_PER_GROUP, HEAD_DIM)
    cache_flat = [flat(g, caches[g][t]) for g in range(N_GROUPS) for t in range(2)]
    new_rows = [kv[g][t].reshape(DEC_BATCH, NEW_ROWS, HEAD_DIM) for g in range(N_GROUPS) for t in (1, 3)]
    y_p2, y_s2, *shifted = _ffn_call(x1_p, x1_s, g_ffn,
                                     w_gate.reshape(D_MODEL, FFN_HIDDEN).astype(BF16),
                                     w_up.reshape(D_MODEL, FFN_HIDDEN).astype(BF16),
                                     w_down.reshape(FFN_HIDDEN, D_MODEL).astype(BF16),
                                     g_final.reshape(1, D_MODEL), cache_flat, new_rows)

    cache_out = []
    for g in range(N_GROUPS):
        keep = min(WINDOWS[g], SEQ)
        for t in range(2):
            full = kv[g][2 * t].reshape(BATCH, SEQ, HEADS_PER_GROUP, HEAD_DIM)
            cache_out.append(full[:, SEQ - keep:][None])
            cache_out.append(shifted[2 * g + t].reshape(1, DEC_BATCH, WINDOWS[g], HEADS_PER_GROUP, HEAD_DIM))
    return (y_p2.reshape(BATCH, SEQ, D_MODEL), y_s2.reshape(DEC_BATCH, DEC_SEQ, D_MODEL),
            conv_p, conv_s, *cache_out)
```

```python
import functools

import jax
import jax.numpy as jnp
from jax import lax
from jax.experimental import pallas as pl
from jax.experimental.pallas import tpu as pltpu

F32 = jnp.float32
BF16 = jnp.bfloat16

D_MODEL = 2048
BATCH = 4
SEQ = 2048
DEC_BATCH = 32
DEC_SEQ = 4
PAST_LEN = 8192
HEAD_DIM = 128
HEADS_PER_GROUP = 4
DILATIONS = (1, 4, 16)
WINDOWS = (128, 512, 2048)
N_GROUPS = 3
GROUP_COLS = HEADS_PER_GROUP * HEAD_DIM
ATT_WIDTH = N_GROUPS * GROUP_COLS
SPAN = 128
ATT_SCALE = HEAD_DIM ** -0.5
ROT_DIM = HEAD_DIM // 4
ROPE_THETA = 500000.0
D_CONV = D_MODEL // 2
CONV_WIDTH = 31
FFN_HIDDEN = 5632
NORM_EPS = 1e-6
LN_EPS = 1e-5
IN_COLS = 2 * D_CONV + 3 * ATT_WIDTH + 2 * D_MODEL

ROWS_P = BATCH * SEQ
ROWS_S = DEC_BATCH * DEC_SEQ
NEG = -1e30

VMEM_LIMIT = 56 * 1024 * 1024
FFN_VMEM_LIMIT = 62 * 1024 * 1024

TM = 1024
NP = ROWS_P // TM
TN = 512
NC = 256
TM2 = 512
NP2 = ROWS_P // TM2
TH = 512
KH = FFN_HIDDEN // TH
TC = 256
HALO = 32
CS = 4


def _sigmoid(x):
    return 1.0 / (1.0 + jnp.exp(-x))


def _rms(x, g):
    return x * lax.rsqrt(jnp.mean(x * x, axis=-1, keepdims=True) + NORM_EPS) * g


def _params(sem, vmem_limit=VMEM_LIMIT):
    return pltpu.CompilerParams(dimension_semantics=sem, vmem_limit_bytes=vmem_limit)


def _pidx(i, n):
    return jnp.minimum(i, n - 1)


def _rms_kernel(xp_ref, xs_ref, g_ref, hp_ref, hs_ref):
    i = pl.program_id(0)

    @pl.when(i < NP)
    def _():
        hp_ref[...] = _rms(xp_ref[...], g_ref[...]).astype(BF16)

    @pl.when(i == NP)
    def _():
        hs_ref[...] = _rms(xs_ref[...], g_ref[...]).astype(BF16)


def _rms_call(xp, xs, g):
    return pl.pallas_call(
        _rms_kernel,
        grid=(NP + 1,),
        in_specs=[
            pl.BlockSpec((TM, D_MODEL), lambda i: (_pidx(i, NP), 0)),
            pl.BlockSpec((ROWS_S, D_MODEL), lambda i: (0, 0)),
            pl.BlockSpec((1, D_MODEL), lambda i: (0, 0)),
        ],
        out_specs=[
            pl.BlockSpec((TM, D_MODEL), lambda i: (_pidx(i, NP), 0)),
            pl.BlockSpec((ROWS_S, D_MODEL), lambda i: (0, 0)),
        ],
        out_shape=[
            jax.ShapeDtypeStruct((ROWS_P, D_MODEL), BF16),
            jax.ShapeDtypeStruct((ROWS_S, D_MODEL), BF16),
        ],
        compiler_params=_params(("arbitrary",)),
        name="rms_in",
    )(xp, xs, g)


def _glu_kernel(hp_ref, hs_ref, wa_ref, wb_ref, ba_ref, bb_ref, wd32_ref, up_ref, us_ref, wd16_ref, wa_s, wb_s):
    i = pl.program_id(1)

    @pl.when(i == 0)
    def _():
        wa_s[...] = wa_ref[...].astype(BF16)
        wb_s[...] = wb_ref[...].astype(BF16)

    @pl.when(i < NP)
    def _():
        wd16_ref[...] = wd32_ref[...].astype(BF16)

    def glu(h, u_ref):
        for c in range(TN // NC):
            sl = slice(c * NC, (c + 1) * NC)
            za = jnp.dot(h, wa_s[:, sl], preferred_element_type=F32) + ba_ref[:, sl]
            zb = jnp.dot(h, wb_s[:, sl], preferred_element_type=F32) + bb_ref[:, sl]
            u_ref[:, sl] = za * _sigmoid(zb)

    @pl.when(i < NP)
    def _():
        glu(hp_ref[...], up_ref)

    @pl.when(i == NP)
    def _():
        glu(hs_ref[...], us_ref)


def _slab_spec(rows, cols, n_row_tiles):
    return pl.BlockSpec((rows, cols), lambda j, i: (j * n_row_tiles + _pidx(i, n_row_tiles), 0))


def _glu_call(hp, hs, w_in, b_glu, w_down):
    nj = D_CONV // TN
    slab = FFN_HIDDEN // (nj * NP)
    return pl.pallas_call(
        _glu_kernel,
        grid=(nj, NP + 1),
        in_specs=[
            pl.BlockSpec((TM, D_MODEL), lambda j, i: (_pidx(i, NP), 0)),
            pl.BlockSpec((ROWS_S, D_MODEL), lambda j, i: (0, 0)),
            pl.BlockSpec((D_MODEL, TN), lambda j, i: (0, j)),
            pl.BlockSpec((D_MODEL, TN), lambda j, i: (0, j + nj)),
            pl.BlockSpec((1, TN), lambda j, i: (0, j)),
            pl.BlockSpec((1, TN), lambda j, i: (0, j + nj)),
            _slab_spec(slab, D_MODEL, NP),
        ],
        out_specs=[
            pl.BlockSpec((TM, TN), lambda j, i: (_pidx(i, NP), j)),
            pl.BlockSpec((ROWS_S, TN), lambda j, i: (0, j)),
            _slab_spec(slab, D_MODEL, NP),
        ],
        out_shape=[
            jax.ShapeDtypeStruct((ROWS_P, D_CONV), F32),
            jax.ShapeDtypeStruct((ROWS_S, D_CONV), F32),
            jax.ShapeDtypeStruct((FFN_HIDDEN, D_MODEL), BF16),
        ],
        scratch_shapes=[pltpu.VMEM((D_MODEL, TN), BF16), pltpu.VMEM((D_MODEL, TN), BF16)],
        compiler_params=_params(("arbitrary", "arbitrary")),
        name="in_glu",
    )(hp, hs, w_in, w_in, b_glu, b_glu, w_down)


def _project_heads(h, w_s, t_ref, o_ref, rows):
    if t_ref is not None:
        cos, sin_lo, sin_hi = t_ref[0], t_ref[1], t_ref[2]
    for c in range(TN // NC):
        z = jnp.dot(h, w_s[:, c * NC:(c + 1) * NC], preferred_element_type=F32)
        for hc in range(NC // HEAD_DIM):
            t = z[:, hc * HEAD_DIM:(hc + 1) * HEAD_DIM]
            if t_ref is not None:
                t = (t * cos + pltpu.roll(t, HEAD_DIM - ROT_DIM // 2, 1) * sin_lo
                     + pltpu.roll(t, ROT_DIM // 2, 1) * sin_hi)
            head = c * (NC // HEAD_DIM) + hc
            o_ref[0, pl.ds(head, rows, stride=HEADS_PER_GROUP), :] = t


def _q_kernel(hp_ref, hs_ref, w_ref, tp_ref, ts_ref, op_ref, os_ref, w_s):
    i = pl.program_id(1)

    @pl.when(i == 0)
    def _():
        w_s[...] = w_ref[...].astype(BF16)

    @pl.when(i < NP)
    def _():
        _project_heads(hp_ref[...], w_s, tp_ref, op_ref, TM)

    @pl.when(i == NP)
    def _():
        _project_heads(hs_ref[...], w_s, ts_ref, os_ref, ROWS_S)


def _kv_kernel(hp_ref, hs_ref, wk_ref, wv_ref, tp_ref, ts_ref, kp_ref, ks_ref, vp_ref, vs_ref, wk_s, wv_s):
    i = pl.program_id(0)

    @pl.when(i == 0)
    def _():
        wk_s[...] = wk_ref[...].astype(BF16)
        wv_s[...] = wv_ref[...].astype(BF16)

    @pl.when(i < NP)
    def _():
        h = hp_ref[...]
        _project_heads(h, wk_s, tp_ref, kp_ref, TM)
        _project_heads(h, wv_s, None, vp_ref, TM)

    @pl.when(i == NP)
    def _():
        h = hs_ref[...]
        _project_heads(h, wk_s, ts_ref, ks_ref, ROWS_S)
        _project_heads(h, wv_s, None, vs_ref, ROWS_S)


Q_PLANE0 = 2 * D_CONV // TN


def _q_call(hp, hs, w_in, tab_p, tab_s):
    return pl.pallas_call(
        _q_kernel,
        grid=(N_GROUPS, NP + 1),
        in_specs=[
            pl.BlockSpec((TM, D_MODEL), lambda j, i: (_pidx(i, NP), 0)),
            pl.BlockSpec((ROWS_S, D_MODEL), lambda j, i: (0, 0)),
            pl.BlockSpec((D_MODEL, TN), lambda j, i: (0, j + Q_PLANE0)),
            pl.BlockSpec((3, TM, HEAD_DIM), lambda j, i: (0, _pidx(i, NP) % (SEQ // TM), 0)),
            pl.BlockSpec((3, ROWS_S, HEAD_DIM), lambda j, i: (0, 0, 0)),
        ],
        out_specs=[
            pl.BlockSpec((1, 4 * TM, HEAD_DIM), lambda j, i: (j, _pidx(i, NP), 0)),
            pl.BlockSpec((1, 4 * ROWS_S, HEAD_DIM), lambda j, i: (j, 0, 0)),
        ],
        out_shape=[
            jax.ShapeDtypeStruct((N_GROUPS, 4 * ROWS_P, HEAD_DIM), F32),
            jax.ShapeDtypeStruct((N_GROUPS, 4 * ROWS_S, HEAD_DIM), F32),
        ],
        scratch_shapes=[pltpu.VMEM((D_MODEL, TN), BF16)],
        compiler_params=_params(("arbitrary", "arbitrary")),
        name="in_q",
    )(hp, hs, w_in, tab_p, tab_s)


def _kv_call(hp, hs, w_in, tab_p, tab_s, g):
    wspec = lambda plane: pl.BlockSpec((D_MODEL, TN), lambda i: (0, Q_PLANE0 + plane))
    outp = pl.BlockSpec((1, 4 * TM, HEAD_DIM), lambda i: (0, _pidx(i, NP), 0))
    outs = pl.BlockSpec((1, 4 * ROWS_S, HEAD_DIM), lambda i: (0, 0, 0))
    shp = jax.ShapeDtypeStruct((1, 4 * ROWS_P, HEAD_DIM), F32)
    shs = jax.ShapeDtypeStruct((1, 4 * ROWS_S, HEAD_DIM), F32)
    return pl.pallas_call(
        _kv_kernel,
        grid=(NP + 1,),
        in_specs=[
            pl.BlockSpec((TM, D_MODEL), lambda i: (_pidx(i, NP), 0)),
            pl.BlockSpec((ROWS_S, D_MODEL), lambda i: (0, 0)),
            wspec(N_GROUPS + g), wspec(2 * N_GROUPS + g),
            pl.BlockSpec((3, TM, HEAD_DIM), lambda i: (0, _pidx(i, NP) % (SEQ // TM), 0)),
            pl.BlockSpec((3, ROWS_S, HEAD_DIM), lambda i: (0, 0, 0)),
        ],
        out_specs=[outp, outs, outp, outs],
        out_shape=[shp, shs, shp, shs],
        scratch_shapes=[pltpu.VMEM((D_MODEL, TN), BF16), pltpu.VMEM((D_MODEL, TN), BF16)],
        compiler_params=_params(("arbitrary",)),
        name="in_kv%d" % g,
    )(hp, hs, w_in, w_in, tab_p, tab_s)


def _gate_kernel(hp_ref, hs_ref, wa_ref, wb_ref, wg32_ref, wu32_ref, op_ref, os_ref, wg16_ref, wu16_ref, w_s):
    i = pl.program_id(1)

    @pl.when(i == 0)
    def _():
        w_s[:, 0:TN] = wa_ref[...].astype(BF16)
        w_s[:, TN:2 * TN] = wb_ref[...].astype(BF16)

    @pl.when(i < NP)
    def _():
        wg16_ref[...] = wg32_ref[...].astype(BF16)
        wu16_ref[...] = wu32_ref[...].astype(BF16)

    def gate(h, o_ref):
        for c in range(2 * TN // NC):
            sl = slice(c * NC, (c + 1) * NC)
            o_ref[:, sl] = _sigmoid(jnp.dot(h, w_s[:, sl], preferred_element_type=F32)).astype(BF16)

    @pl.when(i < NP)
    def _():
        gate(hp_ref[...], op_ref)

    @pl.when(i == NP)
    def _():
        gate(hs_ref[...], os_ref)


def _gate_call(hp, hs, w_in, w_gate, w_up):
    nj = 2 * D_MODEL // (2 * TN)
    off = (2 * D_CONV + 3 * ATT_WIDTH) // TN
    slab = D_MODEL // (nj * NP)
    wspec = _slab_spec(slab, FFN_HIDDEN, NP)
    return pl.pallas_call(
        _gate_kernel,
        grid=(nj, NP + 1),
        in_specs=[
            pl.BlockSpec((TM, D_MODEL), lambda j, i: (_pidx(i, NP), 0)),
            pl.BlockSpec((ROWS_S, D_MODEL), lambda j, i: (0, 0)),
            pl.BlockSpec((D_MODEL, TN), lambda j, i: (0, 2 * j + off)),
            pl.BlockSpec((D_MODEL, TN), lambda j, i: (0, 2 * j + 1 + off)),
            wspec, wspec,
        ],
        out_specs=[
            pl.BlockSpec((TM, 2 * TN), lambda j, i: (_pidx(i, NP), j)),
            pl.BlockSpec((ROWS_S, 2 * TN), lambda j, i: (0, j)),
            wspec, wspec,
        ],
        out_shape=[
            jax.ShapeDtypeStruct((ROWS_P, 2 * D_MODEL), BF16),
            jax.ShapeDtypeStruct((ROWS_S, 2 * D_MODEL), BF16),
            jax.ShapeDtypeStruct((D_MODEL, FFN_HIDDEN), BF16),
            jax.ShapeDtypeStruct((D_MODEL, FFN_HIDDEN), BF16),
        ],
        scratch_shapes=[pltpu.VMEM((D_MODEL, 2 * TN), BF16)],
        compiler_params=_params(("arbitrary", "arbitrary")),
        name="in_gate",
    )(hp, hs, w_in, w_in, w_gate, w_up)


def _ln_silu(y, g, b):
    mu = jnp.mean(y, axis=-1, keepdims=True)
    yc = y - mu
    var = jnp.mean(yc * yc, axis=-1, keepdims=True)
    z = yc * lax.rsqrt(var + LN_EPS) * g + b
    return z * _sigmoid(z)


def _conv_p_kernel(cur_ref, prev_ref, w_ref, bdw_ref, g_ref, b_ref, y_ref, ext, ypre):
    c = pl.program_id(1)
    n_lc = D_CONV // 128
    for lc in range(n_lc):
        sl = slice(lc * 128, (lc + 1) * 128)
        ext[lc, pl.ds(HALO, TC), :] = cur_ref[:, sl]

    @pl.when(c > 0)
    def _():
        for lc in range(n_lc):
            ext[lc, 0:HALO, :] = prev_ref[TC - HALO:TC, lc * 128:(lc + 1) * 128]

    @pl.when(c == 0)
    def _():
        for lc in range(n_lc):
            ext[lc, 0:HALO, :] = jnp.zeros((HALO, 128), F32)

    base = HALO - (CONV_WIDTH - 1)
    for lc in range(n_lc):
        sl = slice(lc * 128, (lc + 1) * 128)
        for t0 in range(CS):
            acc = jnp.zeros((TC // CS, 128), F32)
            for j in range(CONV_WIDTH):
                acc = acc + ext[lc, pl.ds(base + t0 + j, TC // CS, stride=CS), :] * w_ref[j:j + 1, sl]
            ypre[lc, pl.ds(t0, TC // CS, stride=CS), :] = acc + bdw_ref[:, sl]

    total = jnp.zeros((TC, 1), F32)
    for lc in range(n_lc):
        total = total + jnp.sum(ypre[lc], axis=-1, keepdims=True)
    mu = total * (1.0 / D_CONV)
    sq = jnp.zeros((TC, 1), F32)
    for lc in range(n_lc):
        yc = ypre[lc] - mu
        sq = sq + jnp.sum(yc * yc, axis=-1, keepdims=True)
    inv = lax.rsqrt(sq * (1.0 / D_CONV) + LN_EPS)
    for lc in range(n_lc):
        sl = slice(lc * 128, (lc + 1) * 128)
        z = (ypre[lc] - mu) * inv * g_ref[:, sl] + b_ref[:, sl]
        y_ref[:, sl] = (z * _sigmoid(z)).astype(BF16)


def _conv_p_call(u_p, w_dw, b_dw, ln_g, ln_b):
    nc = SEQ // TC
    vec = pl.BlockSpec((1, D_CONV), lambda b, c: (0, 0))
    return pl.pallas_call(
        _conv_p_kernel,
        grid=(BATCH, nc),
        in_specs=[
            pl.BlockSpec((TC, D_CONV), lambda b, c: (b * nc + c, 0)),
            pl.BlockSpec((TC, D_CONV), lambda b, c: (b * nc + jnp.maximum(c - 1, 0), 0)),
            pl.BlockSpec((CONV_WIDTH, D_CONV), lambda b, c: (0, 0)),
            vec, vec, vec,
        ],
        out_specs=pl.BlockSpec((TC, D_CONV), lambda b, c: (b * nc + c, 0)),
        out_shape=jax.ShapeDtypeStruct((ROWS_P, D_CONV), BF16),
        scratch_shapes=[pltpu.VMEM((D_CONV // 128, HALO + TC, 128), F32),
                        pltpu.VMEM((D_CONV // 128, TC, 128), F32)],
        compiler_params=_params(("arbitrary", "arbitrary")),
        name="conv_prompt",
    )(u_p, u_p, w_dw, b_dw, ln_g, ln_b)


CONV_SB = 8


def _conv_s_kernel(ext_ref, w_ref, bdw_ref, g_ref, b_ref, y_ref):
    acc = jnp.zeros((CONV_SB, DEC_SEQ, D_CONV), F32)
    for j in range(CONV_WIDTH):
        acc = acc + ext_ref[:, pl.ds(j, DEC_SEQ), :] * w_ref[j:j + 1, :]
    y_ref[...] = _ln_silu(acc + bdw_ref[...], g_ref[...], b_ref[...])


def _conv_s_call(ext_s, w_dw, b_dw, ln_g, ln_b):
    vec = pl.BlockSpec((1, D_CONV), lambda b: (0, 0))
    rows = CONV_WIDTH - 1 + DEC_SEQ
    return pl.pallas_call(
        _conv_s_kernel,
        grid=(DEC_BATCH // CONV_SB,),
        in_specs=[
            pl.BlockSpec((CONV_SB, rows, D_CONV), lambda b: (b, 0, 0)),
            pl.BlockSpec((CONV_WIDTH, D_CONV), lambda b: (0, 0)),
            vec, vec, vec,
        ],
        out_specs=pl.BlockSpec((CONV_SB, DEC_SEQ, D_CONV), lambda b: (b, 0, 0)),
        out_shape=jax.ShapeDtypeStruct((DEC_BATCH, DEC_SEQ, D_CONV), F32),
        compiler_params=_params(("arbitrary",)),
        name="conv_sample",
    )(ext_s, w_dw, b_dw, ln_g, ln_b)


def _softmax_block(q, k, v, mask):
    s = lax.dot_general(q.astype(BF16), k.astype(BF16), (((2,), (2,)), ((0,), (0,))),
                        preferred_element_type=F32) * ATT_SCALE
    s = jnp.where(mask[None], s, NEG)
    m = jnp.max(s, axis=-1, keepdims=True)
    p = jnp.exp(s - m)
    den = jnp.sum(p, axis=-1, keepdims=True)
    o = lax.dot_general(p.astype(BF16), v.astype(BF16), (((2,), (1,)), ((0,), (0,))),
                        preferred_element_type=F32) / den
    lse = m + jnp.log(den)
    return o, jnp.broadcast_to(lse, o.shape)


def _make_attn_p_kernel(dil):
    stride = HEADS_PER_GROUP * dil
    nblk = SEQ // dil // SPAN
    blk_rows = SPAN * stride
    pair = min(dil, 2)
    group_rows = pair * HEADS_PER_GROUP
    n_r2 = dil // pair

    def kern(q_ref, k_ref, v_ref, o_ref, l_ref):
        qi = lax.broadcasted_iota(jnp.int32, (SPAN, SPAN), 0)
        ki = lax.broadcasted_iota(jnp.int32, (SPAN, SPAN), 1)
        mask_first = ki <= qi
        qi2 = lax.broadcasted_iota(jnp.int32, (SPAN, 2 * SPAN), 0)
        ki2 = lax.broadcasted_iota(jnp.int32, (SPAN, 2 * SPAN), 1)
        mask_band = (ki2 >= qi2) & (ki2 <= qi2 + SPAN)

        def block(base, key_base, n_keys, mask):
            qs = [pl.ds(base + off, SPAN, stride=stride) for off in range(group_rows)]
            ks = [pl.ds(key_base + off, n_keys, stride=stride) for off in range(group_rows)]
            o, lse = _softmax_block(jnp.stack([q_ref[0, s, :] for s in qs]),
                                    jnp.stack([k_ref[0, s, :] for s in ks]),
                                    jnp.stack([v_ref[0, s, :] for s in ks]), mask)
            for u, s in enumerate(qs):
                o_ref[s, :] = o[u]
                l_ref[s, :] = lse[u]

        def first(r2, carry):
            base = pl.multiple_of(r2 * group_rows, max(group_rows, 8))
            block(base, base, SPAN, mask_first)
            return carry

        def band(t, carry):
            jb = 1 + t // n_r2
            base = pl.multiple_of(jb * blk_rows + (t % n_r2) * group_rows, max(group_rows, 8))
            block(base, base - blk_rows, 2 * SPAN, mask_band)
            return carry

        lax.fori_loop(0, n_r2, first, 0)
        if nblk > 1:
            lax.fori_loop(0, n_r2 * (nblk - 1), band, 0)

    return kern


def _attn_p_wide_kernel(q_ref, k_ref, v_ref, o_ref, l_ref):
    qi = lax.broadcasted_iota(jnp.int32, (SPAN, SPAN), 0)
    ki = lax.broadcasted_iota(jnp.int32, (SPAN, SPAN), 1)
    mask_first = ki <= qi

    def tile(c, carry):
        q8 = jnp.swapaxes(q_ref[0, 0, :, c], 0, 1)
        k8 = jnp.swapaxes(k_ref[0, 0, :, c], 0, 1)
        v8 = jnp.swapaxes(v_ref[0, 0, :, c], 0, 1)
        o, lse = _softmax_block(q8, k8, v8, mask_first)
        o_ref[0, :, c] = jnp.swapaxes(o, 0, 1)
        l_ref[0, :, c] = jnp.swapaxes(lse, 0, 1)
        return carry

    lax.fori_loop(0, q_ref.shape[3], tile, 0)


def _attn_p_call(q_p, k_p, v_p, g):
    dil = DILATIONS[g]
    rows = 4 * SEQ
    if SEQ // dil == SPAN:
        tiles = HEADS_PER_GROUP * dil // 8
        view = lambda a: a.reshape(a.shape[0], BATCH, SPAN, tiles, 8, HEAD_DIM)
        blk = lambda plane: pl.BlockSpec((1, 1, SPAN, tiles, 8, HEAD_DIM), lambda b: (plane, b, 0, 0, 0, 0))
        out = pl.BlockSpec((1, SPAN, tiles, 8, HEAD_DIM), lambda b: (b, 0, 0, 0, 0))
        o, l = pl.pallas_call(
            _attn_p_wide_kernel,
            grid=(BATCH,),
            in_specs=[blk(g), blk(0), blk(0)],
            out_specs=[out, out],
            out_shape=[jax.ShapeDtypeStruct((BATCH, SPAN, tiles, 8, HEAD_DIM), F32)] * 2,
            compiler_params=_params(("arbitrary",)),
            name="attn_prompt_g%d" % g,
        )(view(q_p), view(k_p), view(v_p))
        return o.reshape(4 * ROWS_P, HEAD_DIM), l.reshape(4 * ROWS_P, HEAD_DIM)
    blk = lambda plane: pl.BlockSpec((1, rows, HEAD_DIM), lambda b: (plane, b, 0))
    out = pl.BlockSpec((rows, HEAD_DIM), lambda b: (b, 0))
    return pl.pallas_call(
        _make_attn_p_kernel(dil),
        grid=(BATCH,),
        in_specs=[blk(g), blk(0), blk(0)],
        out_specs=[out, out],
        out_shape=[jax.ShapeDtypeStruct((4 * ROWS_P, HEAD_DIM), F32)] * 2,
        compiler_params=_params(("arbitrary",)),
        name="attn_prompt_g%d" % g,
    )(q_p, k_p, v_p)


NEW_ROWS = DEC_SEQ * HEADS_PER_GROUP


def _joint_softmax(parts):
    mx = None
    for s, _ in parts:
        m = jnp.max(s, axis=0) if s.ndim == 3 else s
        mx = m if mx is None else jnp.maximum(mx, m)
    den = jnp.zeros((8, 1), F32)
    acc = jnp.zeros((8, HEAD_DIM), F32)
    for s, v in parts:
        p = jnp.exp(s - mx)
        if s.ndim == 3:
            den = den + jnp.sum(p, axis=0)
            acc = acc + jnp.sum(p * v, axis=0)
        else:
            den = den + p
            acc = acc + p * v
    return acc / den


def _score(q, k):
    return jnp.sum(q * k, axis=-1, keepdims=True) * ATT_SCALE


def _attn_s_kernel(q_ref, kn0, vn0, kn1, vn1, kn2, vn2, k0, v0, k1, v1, k2, v2, o_ref, kc, vc):
    nb = WINDOWS[0] * HEADS_PER_GROUP
    kc[0:nb, :] = k0[0]
    kc[nb:nb + NEW_ROWS, :] = kn0[0]
    vc[0:nb, :] = v0[0]
    vc[nb:nb + NEW_ROWS, :] = vn0[0]
    half = HEADS_PER_GROUP
    n_even, n_odd = SPAN // 2 + 1, SPAN // 2
    for j in range(2):
        rows = slice(8 * j, 8 * j + 8)
        parts = []
        q = q_ref[0, rows, :]
        for start, n in ((8 * j, n_even), (8 * j + half, n_odd)):
            k3 = kc[pl.ds(start, 8 * n), :].reshape(n, 8, HEAD_DIM)
            v3 = vc[pl.ds(start, 8 * n), :].reshape(n, 8, HEAD_DIM)
            parts.append((_score(q[None], k3), v3))
        for g, (k_ref, v_ref, kn_ref, vn_ref) in ((1, (k1, v1, kn1, vn1)), (2, (k2, v2, kn2, vn2))):
            q = q_ref[g, rows, :]
            parts.append((_score(q[None], k_ref[0, :, rows, :]), v_ref[0, :, rows, :]))
            parts.append((_score(q, kn_ref[0, rows, :]), vn_ref[0, rows, :]))
        o_ref[rows, :] = _joint_softmax(parts)


def _attn_s_call(q_s, kv, caches):
    new = pl.BlockSpec((1, NEW_ROWS, HEAD_DIM), lambda b: (0, b, 0))
    in_specs = [pl.BlockSpec((N_GROUPS, NEW_ROWS, HEAD_DIM), lambda b: (0, b, 0))] + [new] * (2 * N_GROUPS)
    args = [q_s] + [kv[g][t] for g in range(N_GROUPS) for t in (1, 3)]
    for g in range(N_GROUPS):
        width = HEADS_PER_GROUP * DILATIONS[g]
        if DILATIONS[g] == 1:
            shape = (DEC_BATCH, WINDOWS[g] * HEADS_PER_GROUP, HEAD_DIM)
            spec = pl.BlockSpec((1,) + shape[1:], lambda b: (b, 0, 0))
        else:
            shape = (DEC_BATCH, SPAN, width, HEAD_DIM)
            spec = pl.BlockSpec((1, SPAN, NEW_ROWS, HEAD_DIM), lambda b: (b, 0, 0, 0))
        in_specs += [spec, spec]
        args += [caches[g][0].reshape(shape), caches[g][1].reshape(shape)]
    rows0 = WINDOWS[0] * HEADS_PER_GROUP + NEW_ROWS
    return pl.pallas_call(
        _attn_s_kernel,
        grid=(DEC_BATCH,),
        in_specs=in_specs,
        out_specs=pl.BlockSpec((NEW_ROWS, HEAD_DIM), lambda b: (b, 0)),
        out_shape=jax.ShapeDtypeStruct((4 * ROWS_S, HEAD_DIM), F32),
        scratch_shapes=[pltpu.VMEM((rows0, HEAD_DIM), F32)] * 2,
        compiler_params=_params(("arbitrary",)),
        name="attn_sample",
    )(*args)


def _mix_kernel(*refs):
    (yp_ref, ys_ref, gp_ref, gs_ref) = refs[0:4]
    op_refs, lp_refs = refs[4:7], refs[7:10]
    os_ref, wpw_ref, bpw_ref, wo_ref, mp_ref, ms_ref = refs[10:16]
    i = pl.program_id(0)

    def combined_prompt_head(h):
        sl = pl.ds(h, TM2, stride=HEADS_PER_GROUP)
        ls = [l[sl, :] for l in lp_refs]
        mx = jnp.maximum(jnp.maximum(ls[0], ls[1]), ls[2])
        es = [jnp.exp(l - mx) for l in ls]
        num = es[0] * op_refs[0][sl, :] + es[1] * op_refs[1][sl, :] + es[2] * op_refs[2][sl, :]
        return num / (es[0] + es[1] + es[2])

    def sample_head(h):
        return os_ref[pl.ds(h, ROWS_S, stride=HEADS_PER_GROUP), :]

    def mix(y_ref, g_ref, head, m_ref, rows):
        conv = jnp.dot(y_ref[...], wpw_ref[...], preferred_element_type=F32) + bpw_ref[...]
        o_att = jnp.concatenate([head(h).astype(BF16) for h in range(HEADS_PER_GROUP)], axis=1)
        att = jnp.dot(o_att, wo_ref[...], preferred_element_type=F32)
        gates = g_ref[...].astype(F32)
        m_ref[...] = (gates[:, :D_MODEL] * conv + gates[:, D_MODEL:] * att).astype(BF16)

    @pl.when(i < NP2)
    def _():
        mix(yp_ref, gp_ref, combined_prompt_head, mp_ref, TM2)

    @pl.when(i == NP2)
    def _():
        mix(ys_ref, gs_ref, sample_head, ms_ref, ROWS_S)


def _mix_call(y_p, y_s, gate_p, gate_s, o_p, l_p, o_s, w_pw, b_pw, w_o):
    pi = lambda i: (_pidx(i, NP2), 0)
    zero = lambda i: (0, 0)
    hp = pl.BlockSpec((4 * TM2, HEAD_DIM), pi)
    hs = pl.BlockSpec((4 * ROWS_S, HEAD_DIM), zero)
    return pl.pallas_call(
        _mix_kernel,
        grid=(NP2 + 1,),
        in_specs=[
            pl.BlockSpec((TM2, D_CONV), pi), pl.BlockSpec((ROWS_S, D_CONV), zero),
            pl.BlockSpec((TM2, 2 * D_MODEL), pi), pl.BlockSpec((ROWS_S, 2 * D_MODEL), zero),
            hp, hp, hp, hp, hp, hp, hs,
            pl.BlockSpec((D_CONV, D_MODEL), zero), pl.BlockSpec((1, D_MODEL), zero),
            pl.BlockSpec((GROUP_COLS, D_MODEL), zero),
        ],
        out_specs=[pl.BlockSpec((TM2, D_MODEL), pi), pl.BlockSpec((ROWS_S, D_MODEL), zero)],
        out_shape=[jax.ShapeDtypeStruct((ROWS_P, D_MODEL), BF16),
                   jax.ShapeDtypeStruct((ROWS_S, D_MODEL), BF16)],
        compiler_params=_params(("arbitrary",)),
        name="mix",
    )(y_p, y_s, gate_p, gate_s, *o_p, *l_p, o_s, w_pw, b_pw, w_o)


def _outproj_kernel(mp_ref, ms_ref, xp_ref, xs_ref, w_ref, g_ref, op_ref, os_ref, hp_ref, hs_ref):
    i = pl.program_id(0)

    def project(m_ref, x_ref, o_ref, h_ref):
        x1 = x_ref[...] + jnp.dot(m_ref[...], w_ref[...], preferred_element_type=F32)
        o_ref[...] = x1
        h_ref[...] = _rms(x1, g_ref[...]).astype(BF16)

    @pl.when(i < NP2)
    def _():
        project(mp_ref, xp_ref, op_ref, hp_ref)

    @pl.when(i == NP2)
    def _():
        project(ms_ref, xs_ref, os_ref, hs_ref)


def _outproj_call(m_p, m_s, xp, xs, w_out, g_ffn):
    pi = lambda i: (_pidx(i, NP2), 0)
    zero = lambda i: (0, 0)
    tile_p, tile_s = pl.BlockSpec((TM2, D_MODEL), pi), pl.BlockSpec((ROWS_S, D_MODEL), zero)
    return pl.pallas_call(
        _outproj_kernel,
        grid=(NP2 + 1,),
        in_specs=[tile_p, tile_s, tile_p, tile_s, pl.BlockSpec((D_MODEL, D_MODEL), zero),
                  pl.BlockSpec((1, D_MODEL), zero)],
        out_specs=[tile_p, tile_s, tile_p, tile_s],
        out_shape=[jax.ShapeDtypeStruct((ROWS_P, D_MODEL), F32), jax.ShapeDtypeStruct((ROWS_S, D_MODEL), F32),
                   jax.ShapeDtypeStruct((ROWS_P, D_MODEL), BF16), jax.ShapeDtypeStruct((ROWS_S, D_MODEL), BF16)],
        compiler_params=_params(("arbitrary",)),
        name="out_proj",
    )(m_p, m_s, xp, xs, w_out, g_ffn)


N_CACHE = 2 * N_GROUPS
COPY_STEPS = 2 * DEC_BATCH


def _shift_copies(c, w, cache_refs, new_refs, out_refs, bufs, sems):
    b = c // 2
    loads, stores = [], []
    for g in range(N_GROUPS):
        n = 2 * g + w
        keep = WINDOWS[g] * HEADS_PER_GROUP - NEW_ROWS
        loads.append(pltpu.make_async_copy(cache_refs[n].at[b, pl.ds(NEW_ROWS, keep)], bufs[g].at[w],
                                           sems.at[0, w, g]))
        stores.append(pltpu.make_async_copy(bufs[g].at[w], out_refs[n].at[b, pl.ds(0, keep)],
                                            sems.at[1, w, g]))
        stores.append(pltpu.make_async_copy(new_refs[n].at[b], out_refs[n].at[b, pl.ds(keep, NEW_ROWS)],
                                            sems.at[2, w, g]))
    return loads, stores


def _ffn_kernel(*refs):
    x_hbm, xs_ref, hp_ref, hs_ref, wg_ref, wu_ref, wd_ref, gl_ref = refs[0:8]
    cache_refs = refs[8:8 + N_CACHE]
    new_refs = refs[8 + N_CACHE:8 + 2 * N_CACHE]
    yp_ref, ys_ref = refs[8 + 2 * N_CACHE:10 + 2 * N_CACHE]
    out_refs = refs[10 + 2 * N_CACHE:10 + 3 * N_CACHE]
    bufs = refs[10 + 3 * N_CACHE:10 + 3 * N_CACHE + N_GROUPS]
    sems, x_sem = refs[10 + 3 * N_CACHE + N_GROUPS:]
    i = pl.program_id(0)
    k = pl.program_id(1)

    step = i * KH + k
    for w in range(2):
        @pl.when(step % 2 == w)
        def _():
            copies = lambda c, slot: _shift_copies(c, slot, cache_refs, new_refs, out_refs, bufs, sems)

            @pl.when(step == 0)
            def _():
                for d in copies(step, w)[0]:
                    d.start()

            @pl.when(step < COPY_STEPS)
            def _():
                for d in copies(step, w)[0]:
                    d.wait()

            @pl.when((step >= 1) & (step <= COPY_STEPS))
            def _():
                for d in copies(step - 1, 1 - w)[1]:
                    d.wait()

            @pl.when(step < COPY_STEPS)
            def _():
                for d in copies(step, w)[1]:
                    d.start()

            @pl.when(step + 1 < COPY_STEPS)
            def _():
                for d in copies(step + 1, 1 - w)[0]:
                    d.start()

    def ffn(h_ref, y_ref, start_residual, finish_residual):
        @pl.when(k == 0)
        def _():
            start_residual()

        h = h_ref[...]
        a = jnp.dot(h, wg_ref[...], preferred_element_type=F32)
        u = jnp.dot(h, wu_ref[...], preferred_element_type=F32)
        act = (a * _sigmoid(a) * u).astype(BF16)

        @pl.when(k == 0)
        def _():
            finish_residual()

        y_ref[...] += jnp.dot(act, wd_ref[...], preferred_element_type=F32)

        @pl.when(k == KH - 1)
        def _():
            y_ref[...] = _rms(y_ref[...], gl_ref[...])

    residual = pltpu.make_async_copy(x_hbm.at[pl.ds(pl.multiple_of(i * TM, TM), TM)], yp_ref, x_sem)

    def copy_sample_rows():
        ys_ref[...] = xs_ref[...]

    ffn(hp_ref, yp_ref, residual.start, residual.wait)

    @pl.when(i == 0)
    def _():
        ffn(hs_ref, ys_ref, copy_sample_rows, lambda: None)


def _ffn_call(x1_p, x1_s, h_p, h_s, w_gate, w_up, w_down, g_final, caches, new_rows):
    assert NP * KH >= COPY_STEPS
    pi = lambda i, k: (i, 0)
    zero = lambda i, k: (0, 0)
    hbm = pl.BlockSpec(memory_space=pl.ANY)
    return pl.pallas_call(
        _ffn_kernel,
        grid=(NP, KH),
        in_specs=[
            hbm,
            pl.BlockSpec((ROWS_S, D_MODEL), zero, pipeline_mode=pl.Buffered(1)),
            pl.BlockSpec((TM, D_MODEL), pi),
            pl.BlockSpec((ROWS_S, D_MODEL), zero, pipeline_mode=pl.Buffered(1)),
            pl.BlockSpec((D_MODEL, TH), lambda i, k: (0, k)),
            pl.BlockSpec((D_MODEL, TH), lambda i, k: (0, k)),
            pl.BlockSpec((TH, D_MODEL), lambda i, k: (k, 0)),
            pl.BlockSpec((1, D_MODEL), zero),
        ] + [hbm] * (2 * N_CACHE),
        out_specs=[pl.BlockSpec((TM, D_MODEL), pi), pl.BlockSpec((ROWS_S, D_MODEL), zero)] + [hbm] * N_CACHE,
        out_shape=[jax.ShapeDtypeStruct((ROWS_P, D_MODEL), F32),
                   jax.ShapeDtypeStruct((ROWS_S, D_MODEL), F32)]
        + [jax.ShapeDtypeStruct(c.shape, F32) for c in caches],
        scratch_shapes=[pltpu.VMEM((2, WINDOWS[g] * HEADS_PER_GROUP - NEW_ROWS, HEAD_DIM), F32)
                        for g in range(N_GROUPS)]
        + [pltpu.SemaphoreType.DMA((3, 2, N_GROUPS)), pltpu.SemaphoreType.DMA(())],
        compiler_params=_params(("arbitrary", "arbitrary"), FFN_VMEM_LIMIT),
        name="ffn",
    )(x1_p, x1_s, h_p, h_s, w_gate, w_up, w_down, g_final, *caches, *new_rows)


def _rope_tables(pos):
    half = ROT_DIM // 2
    inv = jnp.power(ROPE_THETA, -jnp.arange(0, ROT_DIM, 2, dtype=F32) / ROT_DIM)
    ang = pos.astype(F32)[:, None] * inv[None, :]
    cos, sin = jnp.cos(ang), jnp.sin(ang)
    n = pos.shape[0]
    ones = jnp.ones((n, HEAD_DIM - ROT_DIM), F32)
    zeros = jnp.zeros((n, HEAD_DIM - ROT_DIM), F32)
    zh = jnp.zeros((n, half), F32)
    c = jnp.concatenate([cos, cos, ones], axis=1)
    s_lo = jnp.concatenate([-sin, zh, zeros], axis=1)
    s_hi = jnp.concatenate([zh, sin, zeros], axis=1)
    return jnp.stack([c, s_lo, s_hi])


def kernel(x_prompt, x_sample, state_conv, cache_k_w128, cache_v_w128, cache_k_w512, cache_v_w512,
           cache_k_w2048, cache_v_w2048, g_mix, w_in, b_glu, w_dw, b_dw, ln_g, ln_b, w_pw, b_pw,
           w_o_att, w_out, g_ffn, w_gate, w_up, w_down, g_final):
    xp = x_prompt.reshape(ROWS_P, D_MODEL)
    xs = x_sample.reshape(ROWS_S, D_MODEL)
    w_in2 = w_in.reshape(D_MODEL, IN_COLS)

    hp, hs = _rms_call(xp, xs, g_mix)
    u_p, u_s, w_down16 = _glu_call(hp, hs, w_in2, b_glu, w_down.reshape(FFN_HIDDEN, D_MODEL))
    tab_p = _rope_tables(jnp.arange(SEQ, dtype=jnp.int32))
    tab_s = _rope_tables(PAST_LEN + jnp.arange(ROWS_S, dtype=jnp.int32) % DEC_SEQ)
    q_p, q_s = _q_call(hp, hs, w_in2, tab_p, tab_s)
    kv = [_kv_call(hp, hs, w_in2, tab_p, tab_s, g) for g in range(N_GROUPS)]
    gate_p, gate_s, w_gate16, w_up16 = _gate_call(hp, hs, w_in2, w_gate.reshape(D_MODEL, FFN_HIDDEN),
                                                  w_up.reshape(D_MODEL, FFN_HIDDEN))

    w_dw2 = w_dw.reshape(CONV_WIDTH, D_CONV)
    y_p = _conv_p_call(u_p, w_dw2, b_dw, ln_g, ln_b)
    ext_s = jnp.concatenate([state_conv[0], u_s.reshape(DEC_BATCH, DEC_SEQ, D_CONV)], axis=1)
    y_s = _conv_s_call(ext_s, w_dw2, b_dw, ln_g, ln_b).reshape(ROWS_S, D_CONV).astype(BF16)
    conv_p = u_p.reshape(BATCH, SEQ, D_CONV)[:, SEQ - (CONV_WIDTH - 1):][None]
    conv_s = ext_s[:, DEC_SEQ:][None]

    caches = ((cache_k_w128, cache_v_w128), (cache_k_w512, cache_v_w512), (cache_k_w2048, cache_v_w2048))
    o_p, l_p = [], []
    for g in range(N_GROUPS):
        k_p, _, v_p, _ = kv[g]
        o, l = _attn_p_call(q_p, k_p, v_p, g)
        o_p.append(o)
        l_p.append(l)
    o_s = _attn_s_call(q_s, kv, caches)

    m_p, m_s = _mix_call(y_p, y_s, gate_p, gate_s, o_p, l_p, o_s,
                         w_pw.reshape(D_CONV, D_MODEL).astype(BF16), b_pw,
                         w_o_att.reshape(GROUP_COLS, D_MODEL).astype(BF16))
    x1_p, x1_s, hf_p, hf_s = _outproj_call(m_p, m_s, xp, xs, w_out.reshape(D_MODEL, D_MODEL).astype(BF16), g_ffn)

    flat = lambda g, a: a.reshape(DEC_BATCH, WINDOWS[g] * HEADS_PER_GROUP, HEAD_DIM)
    cache_flat = [flat(g, caches[g][t]) for g in range(N_GROUPS) for t in range(2)]
    new_rows = [kv[g][t].reshape(DEC_BATCH, NEW_ROWS, HEAD_DIM) for g in range(N_GROUPS) for t in (1, 3)]
    y_p2, y_s2, *shifted = _ffn_call(x1_p, x1_s, hf_p, hf_s, w_gate16, w_up16, w_down16,
                                     g_final.reshape(1, D_MODEL), cache_flat, new_rows)

    cache_out = []
    for g in range(N_GROUPS):
        keep = min(WINDOWS[g], SEQ)
        for t in range(2):
            full = kv[g][2 * t].reshape(BATCH, SEQ, HEADS_PER_GROUP, HEAD_DIM)
            cache_out.append(full[:, SEQ - keep:][None])
            cache_out.append(shifted[2 * g + t].reshape(1, DEC_BATCH, WINDOWS[g], HEADS_PER_GROUP, HEAD_DIM))
    return (y_p2.reshape(BATCH, SEQ, D_MODEL), y_s2.reshape(DEC_BATCH, DEC_SEQ, D_MODEL),
            conv_p, conv_s, *cache_out)
```

```python
import functools

import jax
import jax.numpy as jnp
import numpy as np
from jax import lax
from jax.experimental import pallas as pl
from jax.experimental.pallas import tpu as pltpu

F32 = jnp.float32
BF16 = jnp.bfloat16

D_MODEL = 2048
BATCH = 4
SEQ = 2048
DEC_BATCH = 32
DEC_SEQ = 4
PAST_LEN = 8192
HEAD_DIM = 128
HEADS_PER_GROUP = 4
DILATIONS = (1, 4, 16)
WINDOWS = (128, 512, 2048)
N_GROUPS = 3
GROUP_COLS = HEADS_PER_GROUP * HEAD_DIM
ATT_WIDTH = N_GROUPS * GROUP_COLS
SPAN = 128
ATT_SCALE = HEAD_DIM ** -0.5
ROT_DIM = HEAD_DIM // 4
ROPE_THETA = 500000.0
D_CONV = D_MODEL // 2
CONV_WIDTH = 31
FFN_HIDDEN = 5632
NORM_EPS = 1e-6
LN_EPS = 1e-5
IN_COLS = 2 * D_CONV + 3 * ATT_WIDTH + 2 * D_MODEL

ROWS_P = BATCH * SEQ
ROWS_S = DEC_BATCH * DEC_SEQ
NEG = -1e30

VMEM_LIMIT = 56 * 1024 * 1024
FFN_VMEM_LIMIT = 62 * 1024 * 1024

TM = 1024
NP = ROWS_P // TM
TN = 512
NC = 256
TM2 = 512
NP2 = ROWS_P // TM2
TH = 512
KH = FFN_HIDDEN // TH
TC = 256
HALO = 32
CS = 4


def _sigmoid(x):
    return 1.0 / (1.0 + jnp.exp(-x))


def _rms(x, g):
    return x * lax.rsqrt(jnp.mean(x * x, axis=-1, keepdims=True) + NORM_EPS) * g


def _params(sem, vmem_limit=VMEM_LIMIT):
    return pltpu.CompilerParams(dimension_semantics=sem, vmem_limit_bytes=vmem_limit)


def _pidx(i, n):
    return jnp.minimum(i, n - 1)


def _rms_kernel(xp_ref, xs_ref, g_ref, hp_ref, hs_ref):
    i = pl.program_id(0)

    @pl.when(i < NP)
    def _():
        hp_ref[...] = _rms(xp_ref[...], g_ref[...]).astype(BF16)

    @pl.when(i == NP)
    def _():
        hs_ref[...] = _rms(xs_ref[...], g_ref[...]).astype(BF16)


def _rms_call(xp, xs, g):
    return pl.pallas_call(
        _rms_kernel,
        grid=(NP + 1,),
        in_specs=[
            pl.BlockSpec((TM, D_MODEL), lambda i: (_pidx(i, NP), 0)),
            pl.BlockSpec((ROWS_S, D_MODEL), lambda i: (0, 0)),
            pl.BlockSpec((1, D_MODEL), lambda i: (0, 0)),
        ],
        out_specs=[
            pl.BlockSpec((TM, D_MODEL), lambda i: (_pidx(i, NP), 0)),
            pl.BlockSpec((ROWS_S, D_MODEL), lambda i: (0, 0)),
        ],
        out_shape=[
            jax.ShapeDtypeStruct((ROWS_P, D_MODEL), BF16),
            jax.ShapeDtypeStruct((ROWS_S, D_MODEL), BF16),
        ],
        compiler_params=_params(("arbitrary",)),
        name="rms_in",
    )(xp, xs, g)


def _glu_kernel(hp_ref, hs_ref, wa_ref, wb_ref, ba_ref, bb_ref, wd32_ref, up_ref, us_ref, wd16_ref, wa_s, wb_s):
    i = pl.program_id(1)

    @pl.when(i == 0)
    def _():
        wa_s[...] = wa_ref[...].astype(BF16)
        wb_s[...] = wb_ref[...].astype(BF16)

    @pl.when(i < NP)
    def _():
        wd16_ref[...] = wd32_ref[...].astype(BF16)

    def glu(h, u_ref):
        for c in range(TN // NC):
            sl = slice(c * NC, (c + 1) * NC)
            za = jnp.dot(h, wa_s[:, sl], preferred_element_type=F32) + ba_ref[:, sl]
            zb = jnp.dot(h, wb_s[:, sl], preferred_element_type=F32) + bb_ref[:, sl]
            u_ref[:, sl] = za * _sigmoid(zb)

    @pl.when(i < NP)
    def _():
        glu(hp_ref[...], up_ref)

    @pl.when(i == NP)
    def _():
        glu(hs_ref[...], us_ref)


def _slab_spec(rows, cols, n_row_tiles):
    return pl.BlockSpec((rows, cols), lambda j, i: (j * n_row_tiles + _pidx(i, n_row_tiles), 0))


def _glu_call(hp, hs, w_in, b_glu, w_down):
    nj = D_CONV // TN
    slab = FFN_HIDDEN // (nj * NP)
    return pl.pallas_call(
        _glu_kernel,
        grid=(nj, NP + 1),
        in_specs=[
            pl.BlockSpec((TM, D_MODEL), lambda j, i: (_pidx(i, NP), 0)),
            pl.BlockSpec((ROWS_S, D_MODEL), lambda j, i: (0, 0)),
            pl.BlockSpec((D_MODEL, TN), lambda j, i: (0, j)),
            pl.BlockSpec((D_MODEL, TN), lambda j, i: (0, j + nj)),
            pl.BlockSpec((1, TN), lambda j, i: (0, j)),
            pl.BlockSpec((1, TN), lambda j, i: (0, j + nj)),
            _slab_spec(slab, D_MODEL, NP),
        ],
        out_specs=[
            pl.BlockSpec((TM, TN), lambda j, i: (_pidx(i, NP), j)),
            pl.BlockSpec((ROWS_S, TN), lambda j, i: (0, j)),
            _slab_spec(slab, D_MODEL, NP),
        ],
        out_shape=[
            jax.ShapeDtypeStruct((ROWS_P, D_CONV), F32),
            jax.ShapeDtypeStruct((ROWS_S, D_CONV), F32),
            jax.ShapeDtypeStruct((FFN_HIDDEN, D_MODEL), BF16),
        ],
        scratch_shapes=[pltpu.VMEM((D_MODEL, TN), BF16), pltpu.VMEM((D_MODEL, TN), BF16)],
        compiler_params=_params(("arbitrary", "arbitrary")),
        name="in_glu",
    )(hp, hs, w_in, w_in, b_glu, b_glu, w_down)


def _project_heads(h, w_s, t_ref, o_ref, rows):
    if t_ref is not None:
        cos, sin_lo, sin_hi = t_ref[0], t_ref[1], t_ref[2]
    for c in range(TN // NC):
        z = jnp.dot(h, w_s[:, c * NC:(c + 1) * NC], preferred_element_type=F32)
        for hc in range(NC // HEAD_DIM):
            t = z[:, hc * HEAD_DIM:(hc + 1) * HEAD_DIM]
            if t_ref is not None:
                t = (t * cos + pltpu.roll(t, HEAD_DIM - ROT_DIM // 2, 1) * sin_lo
                     + pltpu.roll(t, ROT_DIM // 2, 1) * sin_hi)
            head = c * (NC // HEAD_DIM) + hc
            o_ref[pl.ds(head, rows, stride=HEADS_PER_GROUP), :] = t


def _q_kernel(hp_ref, hs_ref, w0_ref, w1_ref, w2_ref, tp_ref, ts_ref, op_ref, os_ref, w_s):
    i = pl.program_id(0)

    @pl.when(i == 0)
    def _():
        for g, w_ref in enumerate((w0_ref, w1_ref, w2_ref)):
            w_s[g] = w_ref[...].astype(BF16)

    @pl.when(i < NP)
    def _():
        h = hp_ref[...]
        for g in range(N_GROUPS):
            _project_heads(h, w_s.at[g], tp_ref, op_ref.at[g], TM)

    @pl.when(i == NP)
    def _():
        h = hs_ref[...]
        for g in range(N_GROUPS):
            _project_heads(h, w_s.at[g], ts_ref, os_ref.at[g], ROWS_S)


def _make_kv_kernel(tail_rows):
    tiles_per_seq = SEQ // TM

    def kern(hp_ref, hs_ref, wk_ref, wv_ref, tp_ref, ts_ref, slab32_ref, *refs):
        kp_ref, ks_ref, vp_ref, vs_ref, slab16_ref = refs[0:5]
        kt_ref, vt_ref = refs[5:7] if tail_rows else (None, None)
        wk_s, wv_s = refs[-2:]
        i = pl.program_id(0)

        @pl.when(i == 0)
        def _():
            wk_s[...] = wk_ref[...].astype(BF16)
            wv_s[...] = wv_ref[...].astype(BF16)

        @pl.when(i < NP)
        def _():
            slab16_ref[...] = slab32_ref[...].astype(BF16)
            h = hp_ref[...]
            _project_heads(h, wk_s, tp_ref, kp_ref.at[0], TM)
            _project_heads(h, wv_s, None, vp_ref.at[0], TM)

        if tail_rows:
            @pl.when((i < NP) & (i % tiles_per_seq == tiles_per_seq - 1))
            def _():
                kt_ref[0] = kp_ref[0, 4 * TM - tail_rows:4 * TM, :]
                vt_ref[0] = vp_ref[0, 4 * TM - tail_rows:4 * TM, :]

        @pl.when(i == NP)
        def _():
            h = hs_ref[...]
            _project_heads(h, wk_s, ts_ref, ks_ref.at[0], ROWS_S)
            _project_heads(h, wv_s, None, vs_ref.at[0], ROWS_S)

    return kern


Q_PLANE0 = 2 * D_CONV // TN


def _q_call(hp, hs, w_in, tab_p, tab_s):
    wspec = lambda g: pl.BlockSpec((D_MODEL, TN), lambda i: (0, Q_PLANE0 + g), pipeline_mode=pl.Buffered(1))
    return pl.pallas_call(
        _q_kernel,
        grid=(NP + 1,),
        in_specs=[
            pl.BlockSpec((TM, D_MODEL), lambda i: (_pidx(i, NP), 0)),
            pl.BlockSpec((ROWS_S, D_MODEL), lambda i: (0, 0)),
            wspec(0), wspec(1), wspec(2),
            pl.BlockSpec((3, TM, HEAD_DIM), lambda i: (0, _pidx(i, NP) % (SEQ // TM), 0)),
            pl.BlockSpec((3, ROWS_S, HEAD_DIM), lambda i: (0, 0, 0)),
        ],
        out_specs=[
            pl.BlockSpec((N_GROUPS, 4 * TM, HEAD_DIM), lambda i: (0, _pidx(i, NP), 0)),
            pl.BlockSpec((N_GROUPS, 4 * ROWS_S, HEAD_DIM), lambda i: (0, 0, 0)),
        ],
        out_shape=[
            jax.ShapeDtypeStruct((N_GROUPS, 4 * ROWS_P, HEAD_DIM), F32),
            jax.ShapeDtypeStruct((N_GROUPS, 4 * ROWS_S, HEAD_DIM), F32),
        ],
        scratch_shapes=[pltpu.VMEM((N_GROUPS, D_MODEL, TN), BF16)],
        compiler_params=_params(("arbitrary",)),
        name="in_q",
    )(hp, hs, w_in, w_in, w_in, tab_p, tab_s)


def _kv_call(hp, hs, w_in, tab_p, tab_s, g, w_extra):
    wspec = lambda plane: pl.BlockSpec((D_MODEL, TN), lambda i: (0, Q_PLANE0 + plane), pipeline_mode=pl.Buffered(1))
    outp = pl.BlockSpec((1, 4 * TM, HEAD_DIM), lambda i: (0, _pidx(i, NP), 0))
    outs = pl.BlockSpec((1, 4 * ROWS_S, HEAD_DIM), lambda i: (0, 0, 0))
    shp = jax.ShapeDtypeStruct((1, 4 * ROWS_P, HEAD_DIM), F32)
    shs = jax.ShapeDtypeStruct((1, 4 * ROWS_S, HEAD_DIM), F32)
    slab = pl.BlockSpec((w_extra.shape[0] // NP, w_extra.shape[1]), lambda i: (_pidx(i, NP), 0))
    out_specs = [outp, outs, outp, outs, slab]
    out_shape = [shp, shs, shp, shs, jax.ShapeDtypeStruct(w_extra.shape, BF16)]
    tail_rows = HEADS_PER_GROUP * WINDOWS[g] if WINDOWS[g] < SEQ else 0
    if tail_rows:
        assert tail_rows <= 4 * TM
        tail = pl.BlockSpec((1, tail_rows, HEAD_DIM), lambda i: (_pidx(i, NP) // (SEQ // TM), 0, 0))
        out_specs += [tail, tail]
        out_shape += [jax.ShapeDtypeStruct((BATCH, tail_rows, HEAD_DIM), F32)] * 2
    return pl.pallas_call(
        _make_kv_kernel(tail_rows),
        grid=(NP + 1,),
        in_specs=[
            pl.BlockSpec((TM, D_MODEL), lambda i: (_pidx(i, NP), 0)),
            pl.BlockSpec((ROWS_S, D_MODEL), lambda i: (0, 0)),
            wspec(N_GROUPS + g), wspec(2 * N_GROUPS + g),
            pl.BlockSpec((3, TM, HEAD_DIM), lambda i: (0, _pidx(i, NP) % (SEQ // TM), 0)),
            pl.BlockSpec((3, ROWS_S, HEAD_DIM), lambda i: (0, 0, 0)),
            slab,
        ],
        out_specs=out_specs,
        out_shape=out_shape,
        scratch_shapes=[pltpu.VMEM((D_MODEL, TN), BF16), pltpu.VMEM((D_MODEL, TN), BF16)],
        compiler_params=_params(("arbitrary",)),
        name="in_kv%d" % g,
    )(hp, hs, w_in, w_in, tab_p, tab_s, w_extra)


def _gate_kernel(hp_ref, hs_ref, wa_ref, wb_ref, wg32_ref, wu32_ref, op_ref, os_ref, wg16_ref, wu16_ref, w_s):
    i = pl.program_id(1)

    @pl.when(i == 0)
    def _():
        w_s[:, 0:TN] = wa_ref[...].astype(BF16)
        w_s[:, TN:2 * TN] = wb_ref[...].astype(BF16)

    @pl.when(i < NP)
    def _():
        wg16_ref[...] = wg32_ref[...].astype(BF16)
        wu16_ref[...] = wu32_ref[...].astype(BF16)

    def gate(h, o_ref):
        for c in range(2 * TN // NC):
            sl = slice(c * NC, (c + 1) * NC)
            o_ref[:, sl] = _sigmoid(jnp.dot(h, w_s[:, sl], preferred_element_type=F32)).astype(BF16)

    @pl.when(i < NP)
    def _():
        gate(hp_ref[...], op_ref)

    @pl.when(i == NP)
    def _():
        gate(hs_ref[...], os_ref)


def _gate_call(hp, hs, w_in, w_gate, w_up):
    nj = 2 * D_MODEL // (2 * TN)
    off = (2 * D_CONV + 3 * ATT_WIDTH) // TN
    slab = D_MODEL // (nj * NP)
    wspec = _slab_spec(slab, FFN_HIDDEN, NP)
    return pl.pallas_call(
        _gate_kernel,
        grid=(nj, NP + 1),
        in_specs=[
            pl.BlockSpec((TM, D_MODEL), lambda j, i: (_pidx(i, NP), 0)),
            pl.BlockSpec((ROWS_S, D_MODEL), lambda j, i: (0, 0)),
            pl.BlockSpec((D_MODEL, TN), lambda j, i: (0, 2 * j + off)),
            pl.BlockSpec((D_MODEL, TN), lambda j, i: (0, 2 * j + 1 + off)),
            wspec, wspec,
        ],
        out_specs=[
            pl.BlockSpec((TM, 2 * TN), lambda j, i: (_pidx(i, NP), j)),
            pl.BlockSpec((ROWS_S, 2 * TN), lambda j, i: (0, j)),
            wspec, wspec,
        ],
        out_shape=[
            jax.ShapeDtypeStruct((ROWS_P, 2 * D_MODEL), BF16),
            jax.ShapeDtypeStruct((ROWS_S, 2 * D_MODEL), BF16),
            jax.ShapeDtypeStruct((D_MODEL, FFN_HIDDEN), BF16),
            jax.ShapeDtypeStruct((D_MODEL, FFN_HIDDEN), BF16),
        ],
        scratch_shapes=[pltpu.VMEM((D_MODEL, 2 * TN), BF16)],
        compiler_params=_params(("arbitrary", "arbitrary")),
        name="in_gate",
    )(hp, hs, w_in, w_in, w_gate, w_up)


def _ln_silu(y, g, b):
    mu = jnp.mean(y, axis=-1, keepdims=True)
    yc = y - mu
    var = jnp.mean(yc * yc, axis=-1, keepdims=True)
    z = yc * lax.rsqrt(var + LN_EPS) * g + b
    return z * _sigmoid(z)


def _conv_p_kernel(cur_ref, prev_ref, w_ref, bdw_ref, g_ref, b_ref, y_ref, ext, ypre):
    c = pl.program_id(1)
    n_lc = D_CONV // 128
    for lc in range(n_lc):
        sl = slice(lc * 128, (lc + 1) * 128)
        ext[lc, pl.ds(HALO, TC), :] = cur_ref[:, sl]

    @pl.when(c > 0)
    def _():
        for lc in range(n_lc):
            ext[lc, 0:HALO, :] = prev_ref[TC - HALO:TC, lc * 128:(lc + 1) * 128]

    @pl.when(c == 0)
    def _():
        for lc in range(n_lc):
            ext[lc, 0:HALO, :] = jnp.zeros((HALO, 128), F32)

    base = HALO - (CONV_WIDTH - 1)
    for lc in range(n_lc):
        sl = slice(lc * 128, (lc + 1) * 128)
        for t0 in range(CS):
            acc = jnp.zeros((TC // CS, 128), F32)
            for j in range(CONV_WIDTH):
                acc = acc + ext[lc, pl.ds(base + t0 + j, TC // CS, stride=CS), :] * w_ref[j:j + 1, sl]
            ypre[lc, pl.ds(t0, TC // CS, stride=CS), :] = acc + bdw_ref[:, sl]

    total = jnp.zeros((TC, 1), F32)
    for lc in range(n_lc):
        total = total + jnp.sum(ypre[lc], axis=-1, keepdims=True)
    mu = total * (1.0 / D_CONV)
    sq = jnp.zeros((TC, 1), F32)
    for lc in range(n_lc):
        yc = ypre[lc] - mu
        sq = sq + jnp.sum(yc * yc, axis=-1, keepdims=True)
    inv = lax.rsqrt(sq * (1.0 / D_CONV) + LN_EPS)
    for lc in range(n_lc):
        sl = slice(lc * 128, (lc + 1) * 128)
        z = (ypre[lc] - mu) * inv * g_ref[:, sl] + b_ref[:, sl]
        y_ref[:, sl] = (z * _sigmoid(z)).astype(BF16)


def _conv_p_call(u_p, w_dw, b_dw, ln_g, ln_b):
    nc = SEQ // TC
    vec = pl.BlockSpec((1, D_CONV), lambda b, c: (0, 0))
    return pl.pallas_call(
        _conv_p_kernel,
        grid=(BATCH, nc),
        in_specs=[
            pl.BlockSpec((TC, D_CONV), lambda b, c: (b * nc + c, 0)),
            pl.BlockSpec((TC, D_CONV), lambda b, c: (b * nc + jnp.maximum(c - 1, 0), 0)),
            pl.BlockSpec((CONV_WIDTH, D_CONV), lambda b, c: (0, 0)),
            vec, vec, vec,
        ],
        out_specs=pl.BlockSpec((TC, D_CONV), lambda b, c: (b * nc + c, 0)),
        out_shape=jax.ShapeDtypeStruct((ROWS_P, D_CONV), BF16),
        scratch_shapes=[pltpu.VMEM((D_CONV // 128, HALO + TC, 128), F32),
                        pltpu.VMEM((D_CONV // 128, TC, 128), F32)],
        compiler_params=_params(("arbitrary", "arbitrary")),
        name="conv_prompt",
    )(u_p, u_p, w_dw, b_dw, ln_g, ln_b)


CONV_SB = 8


def _conv_s_kernel(ext_ref, w_ref, bdw_ref, g_ref, b_ref, y_ref):
    acc = jnp.zeros((CONV_SB, DEC_SEQ, D_CONV), F32)
    for j in range(CONV_WIDTH):
        acc = acc + ext_ref[:, pl.ds(j, DEC_SEQ), :] * w_ref[j:j + 1, :]
    y_ref[...] = _ln_silu(acc + bdw_ref[...], g_ref[...], b_ref[...])


def _conv_s_call(ext_s, w_dw, b_dw, ln_g, ln_b):
    vec = pl.BlockSpec((1, D_CONV), lambda b: (0, 0))
    rows = CONV_WIDTH - 1 + DEC_SEQ
    return pl.pallas_call(
        _conv_s_kernel,
        grid=(DEC_BATCH // CONV_SB,),
        in_specs=[
            pl.BlockSpec((CONV_SB, rows, D_CONV), lambda b: (b, 0, 0)),
            pl.BlockSpec((CONV_WIDTH, D_CONV), lambda b: (0, 0)),
            vec, vec, vec,
        ],
        out_specs=pl.BlockSpec((CONV_SB, DEC_SEQ, D_CONV), lambda b: (b, 0, 0)),
        out_shape=jax.ShapeDtypeStruct((DEC_BATCH, DEC_SEQ, D_CONV), F32),
        compiler_params=_params(("arbitrary",)),
        name="conv_sample",
    )(ext_s, w_dw, b_dw, ln_g, ln_b)


def _softmax_block(q, k, v, mask):
    s = lax.dot_general(q.astype(BF16), k.astype(BF16), (((2,), (2,)), ((0,), (0,))),
                        preferred_element_type=F32) * ATT_SCALE
    s = jnp.where(mask[None], s, NEG)
    m = jnp.max(s, axis=-1, keepdims=True)
    p = jnp.exp(s - m)
    den = jnp.sum(p, axis=-1, keepdims=True)
    o = lax.dot_general(p.astype(BF16), v.astype(BF16), (((2,), (1,)), ((0,), (0,))),
                        preferred_element_type=F32) / den
    lse = m + jnp.log(den)
    return o, jnp.broadcast_to(lse, o.shape)


def _make_attn_p_kernel(dil):
    stride = HEADS_PER_GROUP * dil
    nblk = SEQ // dil // SPAN
    blk_rows = SPAN * stride
    pair = min(dil, 2)
    group_rows = pair * HEADS_PER_GROUP
    n_r2 = dil // pair

    def kern(q_ref, k_ref, v_ref, o_ref, l_ref):
        qi = lax.broadcasted_iota(jnp.int32, (SPAN, SPAN), 0)
        ki = lax.broadcasted_iota(jnp.int32, (SPAN, SPAN), 1)
        mask_first = ki <= qi
        qi2 = lax.broadcasted_iota(jnp.int32, (SPAN, 2 * SPAN), 0)
        ki2 = lax.broadcasted_iota(jnp.int32, (SPAN, 2 * SPAN), 1)
        mask_band = (ki2 >= qi2) & (ki2 <= qi2 + SPAN)

        def block(base, key_base, n_keys, mask):
            qs = [pl.ds(base + off, SPAN, stride=stride) for off in range(group_rows)]
            ks = [pl.ds(key_base + off, n_keys, stride=stride) for off in range(group_rows)]
            o, lse = _softmax_block(jnp.stack([q_ref[0, s, :] for s in qs]),
                                    jnp.stack([k_ref[0, s, :] for s in ks]),
                                    jnp.stack([v_ref[0, s, :] for s in ks]), mask)
            for u, s in enumerate(qs):
                o_ref[s, :] = o[u]
                l_ref[s, :] = lse[u]

        def first(r2, carry):
            base = pl.multiple_of(r2 * group_rows, max(group_rows, 8))
            block(base, base, SPAN, mask_first)
            return carry

        def band(t, carry):
            jb = 1 + t // n_r2
            base = pl.multiple_of(jb * blk_rows + (t % n_r2) * group_rows, max(group_rows, 8))
            block(base, base - blk_rows, 2 * SPAN, mask_band)
            return carry

        lax.fori_loop(0, n_r2, first, 0)
        if nblk > 1:
            lax.fori_loop(0, n_r2 * (nblk - 1), band, 0)

    return kern


def _attn_p_wide_kernel(q_ref, k_ref, v_ref, o_ref, l_ref):
    qi = lax.broadcasted_iota(jnp.int32, (SPAN, SPAN), 0)
    ki = lax.broadcasted_iota(jnp.int32, (SPAN, SPAN), 1)
    mask_first = ki <= qi

    def tile(c, carry):
        q8 = jnp.swapaxes(q_ref[0, 0, :, c], 0, 1)
        k8 = jnp.swapaxes(k_ref[0, 0, :, c], 0, 1)
        v8 = jnp.swapaxes(v_ref[0, 0, :, c], 0, 1)
        o, lse = _softmax_block(q8, k8, v8, mask_first)
        o_ref[0, :, c] = jnp.swapaxes(o, 0, 1)
        l_ref[0, :, c] = jnp.swapaxes(lse, 0, 1)
        return carry

    lax.fori_loop(0, q_ref.shape[3], tile, 0)


def _attn_p_call(q_p, k_p, v_p, g):
    dil = DILATIONS[g]
    rows = 4 * SEQ
    if SEQ // dil == SPAN:
        tiles = HEADS_PER_GROUP * dil // 8
        view = lambda a: a.reshape(a.shape[0], BATCH, SPAN, tiles, 8, HEAD_DIM)
        blk = lambda plane: pl.BlockSpec((1, 1, SPAN, tiles, 8, HEAD_DIM), lambda b: (plane, b, 0, 0, 0, 0))
        out = pl.BlockSpec((1, SPAN, tiles, 8, HEAD_DIM), lambda b: (b, 0, 0, 0, 0))
        o, l = pl.pallas_call(
            _attn_p_wide_kernel,
            grid=(BATCH,),
            in_specs=[blk(g), blk(0), blk(0)],
            out_specs=[out, out],
            out_shape=[jax.ShapeDtypeStruct((BATCH, SPAN, tiles, 8, HEAD_DIM), F32)] * 2,
            compiler_params=_params(("arbitrary",)),
            name="attn_prompt_g%d" % g,
        )(view(q_p), view(k_p), view(v_p))
        return o.reshape(4 * ROWS_P, HEAD_DIM), l.reshape(4 * ROWS_P, HEAD_DIM)
    blk = lambda plane: pl.BlockSpec((1, rows, HEAD_DIM), lambda b: (plane, b, 0))
    out = pl.BlockSpec((rows, HEAD_DIM), lambda b: (b, 0))
    return pl.pallas_call(
        _make_attn_p_kernel(dil),
        grid=(BATCH,),
        in_specs=[blk(g), blk(0), blk(0)],
        out_specs=[out, out],
        out_shape=[jax.ShapeDtypeStruct((4 * ROWS_P, HEAD_DIM), F32)] * 2,
        compiler_params=_params(("arbitrary",)),
        name="attn_prompt_g%d" % g,
    )(q_p, k_p, v_p)


NEW_ROWS = DEC_SEQ * HEADS_PER_GROUP


def _joint_softmax(parts):
    mx = None
    for s, _ in parts:
        m = jnp.max(s, axis=0) if s.ndim == 3 else s
        mx = m if mx is None else jnp.maximum(mx, m)
    den = jnp.zeros((8, 1), F32)
    acc = jnp.zeros((8, HEAD_DIM), F32)
    for s, v in parts:
        p = jnp.exp(s - mx)
        if s.ndim == 3:
            den = den + jnp.sum(p, axis=0)
            acc = acc + jnp.sum(p * v, axis=0)
        else:
            den = den + p
            acc = acc + p * v
    return acc / den


def _score(q, k):
    return jnp.sum(q * k, axis=-1, keepdims=True) * ATT_SCALE


def _attn_s_kernel(q_ref, kn0, vn0, kn1, vn1, kn2, vn2, k0, v0, k1, v1, k2, v2, o_ref, kc, vc):
    nb = WINDOWS[0] * HEADS_PER_GROUP
    kc[0:nb, :] = k0[0]
    kc[nb:nb + NEW_ROWS, :] = kn0[0]
    vc[0:nb, :] = v0[0]
    vc[nb:nb + NEW_ROWS, :] = vn0[0]
    half = HEADS_PER_GROUP
    n_even, n_odd = SPAN // 2 + 1, SPAN // 2
    for j in range(2):
        rows = slice(8 * j, 8 * j + 8)
        parts = []
        q = q_ref[0, rows, :]
        for start, n in ((8 * j, n_even), (8 * j + half, n_odd)):
            k3 = kc[pl.ds(start, 8 * n), :].reshape(n, 8, HEAD_DIM)
            v3 = vc[pl.ds(start, 8 * n), :].reshape(n, 8, HEAD_DIM)
            parts.append((_score(q[None], k3), v3))
        for g, (k_ref, v_ref, kn_ref, vn_ref) in ((1, (k1, v1, kn1, vn1)), (2, (k2, v2, kn2, vn2))):
            q = q_ref[g, rows, :]
            parts.append((_score(q[None], k_ref[0, :, rows, :]), v_ref[0, :, rows, :]))
            parts.append((_score(q, kn_ref[0, rows, :]), vn_ref[0, rows, :]))
        o_ref[rows, :] = _joint_softmax(parts)


def _attn_s_call(q_s, kv, caches):
    new = pl.BlockSpec((1, NEW_ROWS, HEAD_DIM), lambda b: (0, b, 0))
    in_specs = [pl.BlockSpec((N_GROUPS, NEW_ROWS, HEAD_DIM), lambda b: (0, b, 0))] + [new] * (2 * N_GROUPS)
    args = [q_s] + [kv[g][t] for g in range(N_GROUPS) for t in (1, 3)]
    for g in range(N_GROUPS):
        width = HEADS_PER_GROUP * DILATIONS[g]
        if DILATIONS[g] == 1:
            shape = (DEC_BATCH, WINDOWS[g] * HEADS_PER_GROUP, HEAD_DIM)
            spec = pl.BlockSpec((1,) + shape[1:], lambda b: (b, 0, 0))
        else:
            shape = (DEC_BATCH, SPAN, width, HEAD_DIM)
            spec = pl.BlockSpec((1, SPAN, NEW_ROWS, HEAD_DIM), lambda b: (b, 0, 0, 0))
        in_specs += [spec, spec]
        args += [caches[g][0].reshape(shape), caches[g][1].reshape(shape)]
    rows0 = WINDOWS[0] * HEADS_PER_GROUP + NEW_ROWS
    return pl.pallas_call(
        _attn_s_kernel,
        grid=(DEC_BATCH,),
        in_specs=in_specs,
        out_specs=pl.BlockSpec((NEW_ROWS, HEAD_DIM), lambda b: (b, 0)),
        out_shape=jax.ShapeDtypeStruct((4 * ROWS_S, HEAD_DIM), F32),
        scratch_shapes=[pltpu.VMEM((rows0, HEAD_DIM), F32)] * 2,
        compiler_params=_params(("arbitrary",)),
        name="attn_sample",
    )(*args)


def _mix_kernel(*refs):
    (yp_ref, ys_ref, gp_ref, gs_ref) = refs[0:4]
    op_refs, lp_refs = refs[4:7], refs[7:10]
    os_ref, wpw_ref, bpw_ref, wo_ref, mp_ref, ms_ref = refs[10:16]
    i = pl.program_id(0)

    def combined_prompt_head(h):
        sl = pl.ds(h, TM2, stride=HEADS_PER_GROUP)
        ls = [l[sl, :] for l in lp_refs]
        mx = jnp.maximum(jnp.maximum(ls[0], ls[1]), ls[2])
        es = [jnp.exp(l - mx) for l in ls]
        num = es[0] * op_refs[0][sl, :] + es[1] * op_refs[1][sl, :] + es[2] * op_refs[2][sl, :]
        return num / (es[0] + es[1] + es[2])

    def sample_head(h):
        return os_ref[pl.ds(h, ROWS_S, stride=HEADS_PER_GROUP), :]

    def mix(y_ref, g_ref, head, m_ref, rows):
        conv = jnp.dot(y_ref[...], wpw_ref[...], preferred_element_type=F32) + bpw_ref[...]
        o_att = jnp.concatenate([head(h).astype(BF16) for h in range(HEADS_PER_GROUP)], axis=1)
        att = jnp.dot(o_att, wo_ref[...], preferred_element_type=F32)
        gates = g_ref[...].astype(F32)
        m_ref[...] = (gates[:, :D_MODEL] * conv + gates[:, D_MODEL:] * att).astype(BF16)

    @pl.when(i < NP2)
    def _():
        mix(yp_ref, gp_ref, combined_prompt_head, mp_ref, TM2)

    @pl.when(i == NP2)
    def _():
        mix(ys_ref, gs_ref, sample_head, ms_ref, ROWS_S)


def _mix_call(y_p, y_s, gate_p, gate_s, o_p, l_p, o_s, w_pw, b_pw, w_o):
    pi = lambda i: (_pidx(i, NP2), 0)
    zero = lambda i: (0, 0)
    hp = pl.BlockSpec((4 * TM2, HEAD_DIM), pi)
    hs = pl.BlockSpec((4 * ROWS_S, HEAD_DIM), zero)
    return pl.pallas_call(
        _mix_kernel,
        grid=(NP2 + 1,),
        in_specs=[
            pl.BlockSpec((TM2, D_CONV), pi), pl.BlockSpec((ROWS_S, D_CONV), zero),
            pl.BlockSpec((TM2, 2 * D_MODEL), pi), pl.BlockSpec((ROWS_S, 2 * D_MODEL), zero),
            hp, hp, hp, hp, hp, hp, hs,
            pl.BlockSpec((D_CONV, D_MODEL), zero), pl.BlockSpec((1, D_MODEL), zero),
            pl.BlockSpec((GROUP_COLS, D_MODEL), zero),
        ],
        out_specs=[pl.BlockSpec((TM2, D_MODEL), pi), pl.BlockSpec((ROWS_S, D_MODEL), zero)],
        out_shape=[jax.ShapeDtypeStruct((ROWS_P, D_MODEL), BF16),
                   jax.ShapeDtypeStruct((ROWS_S, D_MODEL), BF16)],
        compiler_params=_params(("arbitrary",)),
        name="mix",
    )(y_p, y_s, gate_p, gate_s, *o_p, *l_p, o_s, w_pw, b_pw, w_o)


def _outproj_kernel(mp_ref, ms_ref, xp_ref, xs_ref, w_ref, g_ref, op_ref, os_ref, hp_ref, hs_ref):
    i = pl.program_id(0)

    def project(m_ref, x_ref, o_ref, h_ref):
        x1 = x_ref[...] + jnp.dot(m_ref[...], w_ref[...], preferred_element_type=F32)
        o_ref[...] = x1
        h_ref[...] = _rms(x1, g_ref[...]).astype(BF16)

    @pl.when(i < NP2)
    def _():
        project(mp_ref, xp_ref, op_ref, hp_ref)

    @pl.when(i == NP2)
    def _():
        project(ms_ref, xs_ref, os_ref, hs_ref)


def _outproj_call(m_p, m_s, xp, xs, w_out, g_ffn):
    pi = lambda i: (_pidx(i, NP2), 0)
    zero = lambda i: (0, 0)
    tile_p, tile_s = pl.BlockSpec((TM2, D_MODEL), pi), pl.BlockSpec((ROWS_S, D_MODEL), zero)
    return pl.pallas_call(
        _outproj_kernel,
        grid=(NP2 + 1,),
        in_specs=[tile_p, tile_s, tile_p, tile_s, pl.BlockSpec((D_MODEL, D_MODEL), zero),
                  pl.BlockSpec((1, D_MODEL), zero)],
        out_specs=[tile_p, tile_s, tile_p, tile_s],
        out_shape=[jax.ShapeDtypeStruct((ROWS_P, D_MODEL), F32), jax.ShapeDtypeStruct((ROWS_S, D_MODEL), F32),
                   jax.ShapeDtypeStruct((ROWS_P, D_MODEL), BF16), jax.ShapeDtypeStruct((ROWS_S, D_MODEL), BF16)],
        compiler_params=_params(("arbitrary",)),
        name="out_proj",
    )(m_p, m_s, xp, xs, w_out, g_ffn)


N_CACHE = 2 * N_GROUPS
COPY_STEPS = 2 * DEC_BATCH


def _shift_copies(c, w, cache_refs, new_refs, out_refs, bufs, sems):
    b = c // 2
    loads, stores = [], []
    for g in range(N_GROUPS):
        n = 2 * g + w
        keep = WINDOWS[g] * HEADS_PER_GROUP - NEW_ROWS
        loads.append(pltpu.make_async_copy(cache_refs[n].at[b, pl.ds(NEW_ROWS, keep)], bufs[g].at[w],
                                           sems.at[0, w, g]))
        stores.append(pltpu.make_async_copy(bufs[g].at[w], out_refs[n].at[b, pl.ds(0, keep)],
                                            sems.at[1, w, g]))
        stores.append(pltpu.make_async_copy(new_refs[n].at[b], out_refs[n].at[b, pl.ds(keep, NEW_ROWS)],
                                            sems.at[2, w, g]))
    return loads, stores


def _ffn_kernel(*refs):
    x_hbm, xs_ref, hp_ref, hs_ref, wg_ref, wu_ref, wd_ref, gl_ref = refs[0:8]
    cache_refs = refs[8:8 + N_CACHE]
    new_refs = refs[8 + N_CACHE:8 + 2 * N_CACHE]
    yp_ref, ys_ref = refs[8 + 2 * N_CACHE:10 + 2 * N_CACHE]
    out_refs = refs[10 + 2 * N_CACHE:10 + 3 * N_CACHE]
    bufs = refs[10 + 3 * N_CACHE:10 + 3 * N_CACHE + N_GROUPS]
    sems, x_sem = refs[10 + 3 * N_CACHE + N_GROUPS:]
    i = pl.program_id(0)
    k = pl.program_id(1)

    step = i * KH + k
    for w in range(2):
        @pl.when(step % 2 == w)
        def _():
            copies = lambda c, slot: _shift_copies(c, slot, cache_refs, new_refs, out_refs, bufs, sems)

            @pl.when(step == 0)
            def _():
                for d in copies(step, w)[0]:
                    d.start()

            @pl.when(step < COPY_STEPS)
            def _():
                for d in copies(step, w)[0]:
                    d.wait()

            @pl.when((step >= 1) & (step <= COPY_STEPS))
            def _():
                for d in copies(step - 1, 1 - w)[1]:
                    d.wait()

            @pl.when(step < COPY_STEPS)
            def _():
                for d in copies(step, w)[1]:
                    d.start()

            @pl.when(step + 1 < COPY_STEPS)
            def _():
                for d in copies(step + 1, 1 - w)[0]:
                    d.start()

    def ffn(h_ref, y_ref, start_residual, finish_residual):
        @pl.when(k == 0)
        def _():
            start_residual()

        h = h_ref[...]
        a = jnp.dot(h, wg_ref[...], preferred_element_type=F32)
        u = jnp.dot(h, wu_ref[...], preferred_element_type=F32)
        act = (a * _sigmoid(a) * u).astype(BF16)

        @pl.when(k == 0)
        def _():
            finish_residual()

        y_ref[...] += jnp.dot(act, wd_ref[...], preferred_element_type=F32)

        @pl.when(k == KH - 1)
        def _():
            y_ref[...] = _rms(y_ref[...], gl_ref[...])

    residual = pltpu.make_async_copy(x_hbm.at[pl.ds(pl.multiple_of(i * TM, TM), TM)], yp_ref, x_sem)

    def copy_sample_rows():
        ys_ref[...] = xs_ref[...]

    ffn(hp_ref, yp_ref, residual.start, residual.wait)

    @pl.when(i == 0)
    def _():
        ffn(hs_ref, ys_ref, copy_sample_rows, lambda: None)


def _ffn_call(x1_p, x1_s, h_p, h_s, w_gate, w_up, w_down, g_final, caches, new_rows):
    assert NP * KH >= COPY_STEPS
    pi = lambda i, k: (i, 0)
    zero = lambda i, k: (0, 0)
    hbm = pl.BlockSpec(memory_space=pl.ANY)
    return pl.pallas_call(
        _ffn_kernel,
        grid=(NP, KH),
        in_specs=[
            hbm,
            pl.BlockSpec((ROWS_S, D_MODEL), zero, pipeline_mode=pl.Buffered(1)),
            pl.BlockSpec((TM, D_MODEL), pi),
            pl.BlockSpec((ROWS_S, D_MODEL), zero, pipeline_mode=pl.Buffered(1)),
            pl.BlockSpec((D_MODEL, TH), lambda i, k: (0, k)),
            pl.BlockSpec((D_MODEL, TH), lambda i, k: (0, k)),
            pl.BlockSpec((TH, D_MODEL), lambda i, k: (k, 0)),
            pl.BlockSpec((1, D_MODEL), zero),
        ] + [hbm] * (2 * N_CACHE),
        out_specs=[pl.BlockSpec((TM, D_MODEL), pi), pl.BlockSpec((ROWS_S, D_MODEL), zero)] + [hbm] * N_CACHE,
        out_shape=[jax.ShapeDtypeStruct((ROWS_P, D_MODEL), F32),
                   jax.ShapeDtypeStruct((ROWS_S, D_MODEL), F32)]
        + [jax.ShapeDtypeStruct(c.shape, F32) for c in caches],
        scratch_shapes=[pltpu.VMEM((2, WINDOWS[g] * HEADS_PER_GROUP - NEW_ROWS, HEAD_DIM), F32)
                        for g in range(N_GROUPS)]
        + [pltpu.SemaphoreType.DMA((3, 2, N_GROUPS)), pltpu.SemaphoreType.DMA(())],
        compiler_params=_params(("arbitrary", "arbitrary"), FFN_VMEM_LIMIT),
        name="ffn",
    )(x1_p, x1_s, h_p, h_s, w_gate, w_up, w_down, g_final, *caches, *new_rows)


def _rope_tables(pos):
    half = ROT_DIM // 2
    inv = np.power(ROPE_THETA, -np.arange(0, ROT_DIM, 2, dtype=np.float64) / ROT_DIM)
    ang = np.asarray(pos, np.float64)[:, None] * inv[None, :]
    cos, sin = np.cos(ang), np.sin(ang)
    n = ang.shape[0]
    ones = np.ones((n, HEAD_DIM - ROT_DIM))
    zeros = np.zeros((n, HEAD_DIM - ROT_DIM))
    zh = np.zeros((n, half))
    c = np.concatenate([cos, cos, ones], axis=1)
    s_lo = np.concatenate([-sin, zh, zeros], axis=1)
    s_hi = np.concatenate([zh, sin, zeros], axis=1)
    return jnp.asarray(np.stack([c, s_lo, s_hi]), F32)


def kernel(x_prompt, x_sample, state_conv, cache_k_w128, cache_v_w128, cache_k_w512, cache_v_w512,
           cache_k_w2048, cache_v_w2048, g_mix, w_in, b_glu, w_dw, b_dw, ln_g, ln_b, w_pw, b_pw,
           w_o_att, w_out, g_ffn, w_gate, w_up, w_down, g_final):
    xp = x_prompt.reshape(ROWS_P, D_MODEL)
    xs = x_sample.reshape(ROWS_S, D_MODEL)
    w_in2 = w_in.reshape(D_MODEL, IN_COLS)

    hp, hs = _rms_call(xp, xs, g_mix)
    u_p, u_s, w_down16 = _glu_call(hp, hs, w_in2, b_glu, w_down.reshape(FFN_HIDDEN, D_MODEL))
    tab_p = _rope_tables(np.arange(SEQ))
    tab_s = _rope_tables(PAST_LEN + np.arange(ROWS_S) % DEC_SEQ)
    q_p, q_s = _q_call(hp, hs, w_in2, tab_p, tab_s)
    mixer_w = (w_out.reshape(D_MODEL, D_MODEL), w_pw.reshape(D_CONV, D_MODEL), w_o_att.reshape(GROUP_COLS, D_MODEL))
    kv = [_kv_call(hp, hs, w_in2, tab_p, tab_s, g, mixer_w[g]) for g in range(N_GROUPS)]
    w_out16, w_pw16, w_o16 = kv[0][4], kv[1][4], kv[2][4]
    gate_p, gate_s, w_gate16, w_up16 = _gate_call(hp, hs, w_in2, w_gate.reshape(D_MODEL, FFN_HIDDEN),
                                                  w_up.reshape(D_MODEL, FFN_HIDDEN))

    w_dw2 = w_dw.reshape(CONV_WIDTH, D_CONV)
    y_p = _conv_p_call(u_p, w_dw2, b_dw, ln_g, ln_b)
    ext_s = jnp.concatenate([state_conv[0], u_s.reshape(DEC_BATCH, DEC_SEQ, D_CONV)], axis=1)
    y_s = _conv_s_call(ext_s, w_dw2, b_dw, ln_g, ln_b).reshape(ROWS_S, D_CONV).astype(BF16)
    conv_p = u_p.reshape(BATCH, SEQ, D_CONV)[:, SEQ - (CONV_WIDTH - 1):][None]
    conv_s = ext_s[:, DEC_SEQ:][None]

    caches = ((cache_k_w128, cache_v_w128), (cache_k_w512, cache_v_w512), (cache_k_w2048, cache_v_w2048))
    o_p, l_p = [], []
    for g in range(N_GROUPS):
        o, l = _attn_p_call(q_p, kv[g][0], kv[g][2], g)
        o_p.append(o)
        l_p.append(l)
    o_s = _attn_s_call(q_s, kv, caches)

    m_p, m_s = _mix_call(y_p, y_s, gate_p, gate_s, o_p, l_p, o_s, w_pw16, b_pw, w_o16)
    x1_p, x1_s, hf_p, hf_s = _outproj_call(m_p, m_s, xp, xs, w_out16, g_ffn)

    flat = lambda g, a: a.reshape(DEC_BATCH, WINDOWS[g] * HEADS_PER_GROUP, HEAD_DIM)
    cache_flat = [flat(g, caches[g][t]) for g in range(N_GROUPS) for t in range(2)]
    new_rows = [kv[g][t].reshape(DEC_BATCH, NEW_ROWS, HEAD_DIM) for g in range(N_GROUPS) for t in (1, 3)]
    y_p2, y_s2, *shifted = _ffn_call(x1_p, x1_s, hf_p, hf_s, w_gate16, w_up16, w_down16,
                                     g_final.reshape(1, D_MODEL), cache_flat, new_rows)

    cache_out = []
    for g in range(N_GROUPS):
        keep = min(WINDOWS[g], SEQ)
        for t in range(2):
            prompt_rows = kv[g][2 * t] if keep == SEQ else kv[g][5 + t]
            cache_out.append(prompt_rows.reshape(1, BATCH, keep, HEADS_PER_GROUP, HEAD_DIM))
            cache_out.append(shifted[2 * g + t].reshape(1, DEC_BATCH, WINDOWS[g], HEADS_PER_GROUP, HEAD_DIM))
    return (y_p2.reshape(BATCH, SEQ, D_MODEL), y_s2.reshape(DEC_BATCH, DEC_SEQ, D_MODEL),
            conv_p, conv_s, *cache_out)
```

```python
import functools

import jax
import jax.numpy as jnp
import numpy as np
from jax import lax
from jax.experimental import pallas as pl
from jax.experimental.pallas import tpu as pltpu

F32 = jnp.float32
BF16 = jnp.bfloat16

D_MODEL = 2048
BATCH = 4
SEQ = 2048
DEC_BATCH = 32
DEC_SEQ = 4
PAST_LEN = 8192
HEAD_DIM = 128
HEADS_PER_GROUP = 4
DILATIONS = (1, 4, 16)
WINDOWS = (128, 512, 2048)
N_GROUPS = 3
GROUP_COLS = HEADS_PER_GROUP * HEAD_DIM
ATT_WIDTH = N_GROUPS * GROUP_COLS
SPAN = 128
ATT_SCALE = HEAD_DIM ** -0.5
ROT_DIM = HEAD_DIM // 4
ROPE_THETA = 500000.0
D_CONV = D_MODEL // 2
CONV_WIDTH = 31
FFN_HIDDEN = 5632
NORM_EPS = 1e-6
LN_EPS = 1e-5
IN_COLS = 2 * D_CONV + 3 * ATT_WIDTH + 2 * D_MODEL

ROWS_P = BATCH * SEQ
ROWS_S = DEC_BATCH * DEC_SEQ
NEG = -1e30

VMEM_LIMIT = 56 * 1024 * 1024
FFN_VMEM_LIMIT = 62 * 1024 * 1024

TM = 1024
NP = ROWS_P // TM
TN = 512
NC = 256
TM2 = 512
NP2 = ROWS_P // TM2
TH = 512
KH = FFN_HIDDEN // TH
TC = 256
HALO = 32
CS = 4


def _sigmoid(x):
    return 1.0 / (1.0 + jnp.exp(-x))


def _rms(x, g):
    return x * lax.rsqrt(jnp.mean(x * x, axis=-1, keepdims=True) + NORM_EPS) * g


def _params(sem, vmem_limit=VMEM_LIMIT):
    return pltpu.CompilerParams(dimension_semantics=sem, vmem_limit_bytes=vmem_limit)


def _pidx(i, n):
    return jnp.minimum(i, n - 1)


def _rms_kernel(xp_ref, xs_ref, g_ref, hp_ref, hs_ref):
    i = pl.program_id(0)

    @pl.when(i < NP)
    def _():
        hp_ref[...] = _rms(xp_ref[...], g_ref[...]).astype(BF16)

    @pl.when(i == NP)
    def _():
        hs_ref[...] = _rms(xs_ref[...], g_ref[...]).astype(BF16)


def _rms_call(xp, xs, g):
    return pl.pallas_call(
        _rms_kernel,
        grid=(NP + 1,),
        in_specs=[
            pl.BlockSpec((TM, D_MODEL), lambda i: (_pidx(i, NP), 0)),
            pl.BlockSpec((ROWS_S, D_MODEL), lambda i: (0, 0)),
            pl.BlockSpec((1, D_MODEL), lambda i: (0, 0)),
        ],
        out_specs=[
            pl.BlockSpec((TM, D_MODEL), lambda i: (_pidx(i, NP), 0)),
            pl.BlockSpec((ROWS_S, D_MODEL), lambda i: (0, 0)),
        ],
        out_shape=[
            jax.ShapeDtypeStruct((ROWS_P, D_MODEL), BF16),
            jax.ShapeDtypeStruct((ROWS_S, D_MODEL), BF16),
        ],
        compiler_params=_params(("arbitrary",)),
        name="rms_in",
    )(xp, xs, g)


N_GATE_BLOCKS = 2 * D_MODEL // TN


def _glu_kernel(hp_ref, hs_ref, wa_ref, wb_ref, ba_ref, bb_ref, *refs):
    gate32_refs = refs[:N_GATE_BLOCKS]
    up_ref, us_ref, gate16_ref, wa_s, wb_s = refs[N_GATE_BLOCKS:]
    i = pl.program_id(1)

    @pl.when(i == 0)
    def _():
        wa_s[...] = wa_ref[...].astype(BF16)
        wb_s[...] = wb_ref[...].astype(BF16)

    @pl.when(i < NP)
    def _():
        for c, g_ref in enumerate(gate32_refs):
            gate16_ref[:, c * TN:(c + 1) * TN] = g_ref[...].astype(BF16)

    def glu(h, u_ref):
        for c in range(TN // NC):
            sl = slice(c * NC, (c + 1) * NC)
            za = jnp.dot(h, wa_s[:, sl], preferred_element_type=F32) + ba_ref[:, sl]
            zb = jnp.dot(h, wb_s[:, sl], preferred_element_type=F32) + bb_ref[:, sl]
            u_ref[:, sl] = za * _sigmoid(zb)

    @pl.when(i < NP)
    def _():
        glu(hp_ref[...], up_ref)

    @pl.when(i == NP)
    def _():
        glu(hs_ref[...], us_ref)


def _glu_call(hp, hs, w_in, b_glu):
    nj = D_CONV // TN
    slab = D_MODEL // (nj * NP)
    gate0 = (2 * D_CONV + 3 * ATT_WIDTH) // TN
    slab_row = lambda j, i: j * NP + _pidx(i, NP)
    return pl.pallas_call(
        _glu_kernel,
        grid=(nj, NP + 1),
        in_specs=[
            pl.BlockSpec((TM, D_MODEL), lambda j, i: (_pidx(i, NP), 0)),
            pl.BlockSpec((ROWS_S, D_MODEL), lambda j, i: (0, 0)),
            pl.BlockSpec((D_MODEL, TN), lambda j, i: (0, j)),
            pl.BlockSpec((D_MODEL, TN), lambda j, i: (0, j + nj)),
            pl.BlockSpec((1, TN), lambda j, i: (0, j)),
            pl.BlockSpec((1, TN), lambda j, i: (0, j + nj)),
        ] + [pl.BlockSpec((slab, TN), lambda j, i, c=c: (slab_row(j, i), gate0 + c)) for c in range(N_GATE_BLOCKS)],
        out_specs=[
            pl.BlockSpec((TM, TN), lambda j, i: (_pidx(i, NP), j)),
            pl.BlockSpec((ROWS_S, TN), lambda j, i: (0, j)),
            pl.BlockSpec((slab, 2 * D_MODEL), lambda j, i: (slab_row(j, i), 0)),
        ],
        out_shape=[
            jax.ShapeDtypeStruct((ROWS_P, D_CONV), F32),
            jax.ShapeDtypeStruct((ROWS_S, D_CONV), F32),
            jax.ShapeDtypeStruct((D_MODEL, 2 * D_MODEL), BF16),
        ],
        scratch_shapes=[pltpu.VMEM((D_MODEL, TN), BF16), pltpu.VMEM((D_MODEL, TN), BF16)],
        compiler_params=_params(("arbitrary", "arbitrary")),
        name="in_glu",
    )(hp, hs, w_in, w_in, b_glu, b_glu, *([w_in] * N_GATE_BLOCKS))


def _project_heads(h, w_s, t_ref, o_ref, rows):
    if t_ref is not None:
        cos, sin_lo, sin_hi = t_ref[0], t_ref[1], t_ref[2]
    for c in range(TN // NC):
        z = jnp.dot(h, w_s[:, c * NC:(c + 1) * NC], preferred_element_type=F32)
        for hc in range(NC // HEAD_DIM):
            t = z[:, hc * HEAD_DIM:(hc + 1) * HEAD_DIM]
            if t_ref is not None:
                t = (t * cos + pltpu.roll(t, HEAD_DIM - ROT_DIM // 2, 1) * sin_lo
                     + pltpu.roll(t, ROT_DIM // 2, 1) * sin_hi)
            head = c * (NC // HEAD_DIM) + hc
            o_ref[pl.ds(head, rows, stride=HEADS_PER_GROUP), :] = t


def _q_kernel(hp_ref, hs_ref, w0_ref, w1_ref, w2_ref, tp_ref, ts_ref, op_ref, os_ref, w_s):
    i = pl.program_id(0)

    @pl.when(i == 0)
    def _():
        for g, w_ref in enumerate((w0_ref, w1_ref, w2_ref)):
            w_s[g] = w_ref[...].astype(BF16)

    @pl.when(i < NP)
    def _():
        h = hp_ref[...]
        for g in range(N_GROUPS):
            _project_heads(h, w_s.at[g], tp_ref, op_ref.at[g], TM)

    @pl.when(i == NP)
    def _():
        h = hs_ref[...]
        for g in range(N_GROUPS):
            _project_heads(h, w_s.at[g], ts_ref, os_ref.at[g], ROWS_S)


def _make_kv_kernel(tail_rows):
    tiles_per_seq = SEQ // TM

    def kern(hp_ref, hs_ref, wk_ref, wv_ref, tp_ref, ts_ref, slab32_ref, *refs):
        kp_ref, ks_ref, vp_ref, vs_ref, slab16_ref = refs[0:5]
        kt_ref, vt_ref = refs[5:7] if tail_rows else (None, None)
        wk_s, wv_s = refs[-2:]
        i = pl.program_id(0)

        @pl.when(i == 0)
        def _():
            wk_s[...] = wk_ref[...].astype(BF16)
            wv_s[...] = wv_ref[...].astype(BF16)

        @pl.when(i < NP)
        def _():
            slab16_ref[...] = slab32_ref[...].astype(BF16)
            h = hp_ref[...]
            _project_heads(h, wk_s, tp_ref, kp_ref.at[0], TM)
            _project_heads(h, wv_s, None, vp_ref.at[0], TM)

        if tail_rows:
            @pl.when((i < NP) & (i % tiles_per_seq == tiles_per_seq - 1))
            def _():
                kt_ref[0] = kp_ref[0, 4 * TM - tail_rows:4 * TM, :]
                vt_ref[0] = vp_ref[0, 4 * TM - tail_rows:4 * TM, :]

        @pl.when(i == NP)
        def _():
            h = hs_ref[...]
            _project_heads(h, wk_s, ts_ref, ks_ref.at[0], ROWS_S)
            _project_heads(h, wv_s, None, vs_ref.at[0], ROWS_S)

    return kern


Q_PLANE0 = 2 * D_CONV // TN


def _q_call(hp, hs, w_in, tab_p, tab_s):
    wspec = lambda g: pl.BlockSpec((D_MODEL, TN), lambda i: (0, Q_PLANE0 + g), pipeline_mode=pl.Buffered(1))
    return pl.pallas_call(
        _q_kernel,
        grid=(NP + 1,),
        in_specs=[
            pl.BlockSpec((TM, D_MODEL), lambda i: (_pidx(i, NP), 0)),
            pl.BlockSpec((ROWS_S, D_MODEL), lambda i: (0, 0)),
            wspec(0), wspec(1), wspec(2),
            pl.BlockSpec((3, TM, HEAD_DIM), lambda i: (0, _pidx(i, NP) % (SEQ // TM), 0)),
            pl.BlockSpec((3, ROWS_S, HEAD_DIM), lambda i: (0, 0, 0)),
        ],
        out_specs=[
            pl.BlockSpec((N_GROUPS, 4 * TM, HEAD_DIM), lambda i: (0, _pidx(i, NP), 0)),
            pl.BlockSpec((N_GROUPS, 4 * ROWS_S, HEAD_DIM), lambda i: (0, 0, 0)),
        ],
        out_shape=[
            jax.ShapeDtypeStruct((N_GROUPS, 4 * ROWS_P, HEAD_DIM), F32),
            jax.ShapeDtypeStruct((N_GROUPS, 4 * ROWS_S, HEAD_DIM), F32),
        ],
        scratch_shapes=[pltpu.VMEM((N_GROUPS, D_MODEL, TN), BF16)],
        compiler_params=_params(("arbitrary",)),
        name="in_q",
    )(hp, hs, w_in, w_in, w_in, tab_p, tab_s)


def _kv_call(hp, hs, w_in, tab_p, tab_s, g, w_extra):
    wspec = lambda plane: pl.BlockSpec((D_MODEL, TN), lambda i: (0, Q_PLANE0 + plane), pipeline_mode=pl.Buffered(1))
    outp = pl.BlockSpec((1, 4 * TM, HEAD_DIM), lambda i: (0, _pidx(i, NP), 0))
    outs = pl.BlockSpec((1, 4 * ROWS_S, HEAD_DIM), lambda i: (0, 0, 0))
    shp = jax.ShapeDtypeStruct((1, 4 * ROWS_P, HEAD_DIM), F32)
    shs = jax.ShapeDtypeStruct((1, 4 * ROWS_S, HEAD_DIM), F32)
    slab = pl.BlockSpec((w_extra.shape[0] // NP, w_extra.shape[1]), lambda i: (_pidx(i, NP), 0))
    out_specs = [outp, outs, outp, outs, slab]
    out_shape = [shp, shs, shp, shs, jax.ShapeDtypeStruct(w_extra.shape, BF16)]
    tail_rows = HEADS_PER_GROUP * WINDOWS[g] if WINDOWS[g] < SEQ else 0
    if tail_rows:
        assert tail_rows <= 4 * TM
        tail = pl.BlockSpec((1, tail_rows, HEAD_DIM), lambda i: (_pidx(i, NP) // (SEQ // TM), 0, 0))
        out_specs += [tail, tail]
        out_shape += [jax.ShapeDtypeStruct((BATCH, tail_rows, HEAD_DIM), F32)] * 2
    return pl.pallas_call(
        _make_kv_kernel(tail_rows),
        grid=(NP + 1,),
        in_specs=[
            pl.BlockSpec((TM, D_MODEL), lambda i: (_pidx(i, NP), 0)),
            pl.BlockSpec((ROWS_S, D_MODEL), lambda i: (0, 0)),
            wspec(N_GROUPS + g), wspec(2 * N_GROUPS + g),
            pl.BlockSpec((3, TM, HEAD_DIM), lambda i: (0, _pidx(i, NP) % (SEQ // TM), 0)),
            pl.BlockSpec((3, ROWS_S, HEAD_DIM), lambda i: (0, 0, 0)),
            slab,
        ],
        out_specs=out_specs,
        out_shape=out_shape,
        scratch_shapes=[pltpu.VMEM((D_MODEL, TN), BF16), pltpu.VMEM((D_MODEL, TN), BF16)],
        compiler_params=_params(("arbitrary",)),
        name="in_kv%d" % g,
    )(hp, hs, w_in, w_in, tab_p, tab_s, w_extra)


def _ln_silu(y, g, b):
    mu = jnp.mean(y, axis=-1, keepdims=True)
    yc = y - mu
    var = jnp.mean(yc * yc, axis=-1, keepdims=True)
    z = yc * lax.rsqrt(var + LN_EPS) * g + b
    return z * _sigmoid(z)


def _conv_p_kernel(cur_ref, prev_ref, w_ref, bdw_ref, g_ref, b_ref, wg32_ref, wu32_ref, wd32_ref,
                   y_ref, wg16_ref, wu16_ref, wd16_ref, ext, ypre):
    c = pl.program_id(1)
    n_lc = D_CONV // 128
    wg16_ref[...] = wg32_ref[...].astype(BF16)
    wu16_ref[...] = wu32_ref[...].astype(BF16)
    wd16_ref[...] = wd32_ref[...].astype(BF16)
    for lc in range(n_lc):
        sl = slice(lc * 128, (lc + 1) * 128)
        ext[lc, pl.ds(HALO, TC), :] = cur_ref[:, sl]

    @pl.when(c > 0)
    def _():
        for lc in range(n_lc):
            ext[lc, 0:HALO, :] = prev_ref[TC - HALO:TC, lc * 128:(lc + 1) * 128]

    @pl.when(c == 0)
    def _():
        for lc in range(n_lc):
            ext[lc, 0:HALO, :] = jnp.zeros((HALO, 128), F32)

    base = HALO - (CONV_WIDTH - 1)
    for lc in range(n_lc):
        sl = slice(lc * 128, (lc + 1) * 128)
        for t0 in range(CS):
            acc = jnp.zeros((TC // CS, 128), F32)
            for j in range(CONV_WIDTH):
                acc = acc + ext[lc, pl.ds(base + t0 + j, TC // CS, stride=CS), :] * w_ref[j:j + 1, sl]
            ypre[lc, pl.ds(t0, TC // CS, stride=CS), :] = acc + bdw_ref[:, sl]

    total = jnp.zeros((TC, 1), F32)
    for lc in range(n_lc):
        total = total + jnp.sum(ypre[lc], axis=-1, keepdims=True)
    mu = total * (1.0 / D_CONV)
    sq = jnp.zeros((TC, 1), F32)
    for lc in range(n_lc):
        yc = ypre[lc] - mu
        sq = sq + jnp.sum(yc * yc, axis=-1, keepdims=True)
    inv = lax.rsqrt(sq * (1.0 / D_CONV) + LN_EPS)
    for lc in range(n_lc):
        sl = slice(lc * 128, (lc + 1) * 128)
        z = (ypre[lc] - mu) * inv * g_ref[:, sl] + b_ref[:, sl]
        y_ref[:, sl] = (z * _sigmoid(z)).astype(BF16)


def _conv_p_call(u_p, w_dw, b_dw, ln_g, ln_b, w_gate, w_up, w_down):
    nc = SEQ // TC
    steps = BATCH * nc
    vec = pl.BlockSpec((1, D_CONV), lambda b, c: (0, 0))
    slab = lambda w: pl.BlockSpec((w.shape[0] // steps, w.shape[1]), lambda b, c: (b * nc + c, 0))
    weights = (w_gate, w_up, w_down)
    return pl.pallas_call(
        _conv_p_kernel,
        grid=(BATCH, nc),
        in_specs=[
            pl.BlockSpec((TC, D_CONV), lambda b, c: (b * nc + c, 0)),
            pl.BlockSpec((TC, D_CONV), lambda b, c: (b * nc + jnp.maximum(c - 1, 0), 0)),
            pl.BlockSpec((CONV_WIDTH, D_CONV), lambda b, c: (0, 0)),
            vec, vec, vec,
        ] + [slab(w) for w in weights],
        out_specs=[pl.BlockSpec((TC, D_CONV), lambda b, c: (b * nc + c, 0))] + [slab(w) for w in weights],
        out_shape=[jax.ShapeDtypeStruct((ROWS_P, D_CONV), BF16)]
        + [jax.ShapeDtypeStruct(w.shape, BF16) for w in weights],
        scratch_shapes=[pltpu.VMEM((D_CONV // 128, HALO + TC, 128), F32),
                        pltpu.VMEM((D_CONV // 128, TC, 128), F32)],
        compiler_params=_params(("arbitrary", "arbitrary")),
        name="conv_prompt",
    )(u_p, u_p, w_dw, b_dw, ln_g, ln_b, *weights)


CONV_SB = 8


def _conv_s_kernel(ext_ref, w_ref, bdw_ref, g_ref, b_ref, y_ref):
    acc = jnp.zeros((CONV_SB, DEC_SEQ, D_CONV), F32)
    for j in range(CONV_WIDTH):
        acc = acc + ext_ref[:, pl.ds(j, DEC_SEQ), :] * w_ref[j:j + 1, :]
    y_ref[...] = _ln_silu(acc + bdw_ref[...], g_ref[...], b_ref[...])


def _conv_s_call(ext_s, w_dw, b_dw, ln_g, ln_b):
    vec = pl.BlockSpec((1, D_CONV), lambda b: (0, 0))
    rows = CONV_WIDTH - 1 + DEC_SEQ
    return pl.pallas_call(
        _conv_s_kernel,
        grid=(DEC_BATCH // CONV_SB,),
        in_specs=[
            pl.BlockSpec((CONV_SB, rows, D_CONV), lambda b: (b, 0, 0)),
            pl.BlockSpec((CONV_WIDTH, D_CONV), lambda b: (0, 0)),
            vec, vec, vec,
        ],
        out_specs=pl.BlockSpec((CONV_SB, DEC_SEQ, D_CONV), lambda b: (b, 0, 0)),
        out_shape=jax.ShapeDtypeStruct((DEC_BATCH, DEC_SEQ, D_CONV), F32),
        compiler_params=_params(("arbitrary",)),
        name="conv_sample",
    )(ext_s, w_dw, b_dw, ln_g, ln_b)


def _softmax_block(q, k, v, mask):
    s = lax.dot_general(q.astype(BF16), k.astype(BF16), (((2,), (2,)), ((0,), (0,))),
                        preferred_element_type=F32) * ATT_SCALE
    s = jnp.where(mask[None], s, NEG)
    m = jnp.max(s, axis=-1, keepdims=True)
    p = jnp.exp(s - m)
    den = jnp.sum(p, axis=-1, keepdims=True)
    o = lax.dot_general(p.astype(BF16), v.astype(BF16), (((2,), (1,)), ((0,), (0,))),
                        preferred_element_type=F32) / den
    lse = m + jnp.log(den)
    return o, jnp.broadcast_to(lse, o.shape)


def _make_attn_p_kernel(dil):
    stride = HEADS_PER_GROUP * dil
    nblk = SEQ // dil // SPAN
    blk_rows = SPAN * stride
    pair = min(dil, 2)
    group_rows = pair * HEADS_PER_GROUP
    n_r2 = dil // pair

    def kern(q_ref, k_ref, v_ref, o_ref, l_ref):
        qi = lax.broadcasted_iota(jnp.int32, (SPAN, SPAN), 0)
        ki = lax.broadcasted_iota(jnp.int32, (SPAN, SPAN), 1)
        mask_first = ki <= qi
        qi2 = lax.broadcasted_iota(jnp.int32, (SPAN, 2 * SPAN), 0)
        ki2 = lax.broadcasted_iota(jnp.int32, (SPAN, 2 * SPAN), 1)
        mask_band = (ki2 >= qi2) & (ki2 <= qi2 + SPAN)

        def block(base, key_base, n_keys, mask):
            qs = [pl.ds(base + off, SPAN, stride=stride) for off in range(group_rows)]
            ks = [pl.ds(key_base + off, n_keys, stride=stride) for off in range(group_rows)]
            o, lse = _softmax_block(jnp.stack([q_ref[0, s, :] for s in qs]),
                                    jnp.stack([k_ref[0, s, :] for s in ks]),
                                    jnp.stack([v_ref[0, s, :] for s in ks]), mask)
            for u, s in enumerate(qs):
                o_ref[s, :] = o[u]
                l_ref[s, :] = lse[u]

        def first(r2, carry):
            base = pl.multiple_of(r2 * group_rows, max(group_rows, 8))
            block(base, base, SPAN, mask_first)
            return carry

        def band(t, carry):
            jb = 1 + t // n_r2
            base = pl.multiple_of(jb * blk_rows + (t % n_r2) * group_rows, max(group_rows, 8))
            block(base, base - blk_rows, 2 * SPAN, mask_band)
            return carry

        lax.fori_loop(0, n_r2, first, 0)
        if nblk > 1:
            lax.fori_loop(0, n_r2 * (nblk - 1), band, 0)

    return kern


def _attn_p_wide_kernel(q_ref, k_ref, v_ref, o_ref, l_ref):
    qi = lax.broadcasted_iota(jnp.int32, (SPAN, SPAN), 0)
    ki = lax.broadcasted_iota(jnp.int32, (SPAN, SPAN), 1)
    mask_first = ki <= qi

    def tile(c, carry):
        q8 = jnp.swapaxes(q_ref[0, 0, :, c], 0, 1)
        k8 = jnp.swapaxes(k_ref[0, 0, :, c], 0, 1)
        v8 = jnp.swapaxes(v_ref[0, 0, :, c], 0, 1)
        o, lse = _softmax_block(q8, k8, v8, mask_first)
        o_ref[0, :, c] = jnp.swapaxes(o, 0, 1)
        l_ref[0, :, c] = jnp.swapaxes(lse, 0, 1)
        return carry

    lax.fori_loop(0, q_ref.shape[3], tile, 0)


def _attn_p_call(q_p, k_p, v_p, g):
    dil = DILATIONS[g]
    rows = 4 * SEQ
    if SEQ // dil == SPAN:
        tiles = HEADS_PER_GROUP * dil // 8
        view = lambda a: a.reshape(a.shape[0], BATCH, SPAN, tiles, 8, HEAD_DIM)
        blk = lambda plane: pl.BlockSpec((1, 1, SPAN, tiles, 8, HEAD_DIM), lambda b: (plane, b, 0, 0, 0, 0))
        out = pl.BlockSpec((1, SPAN, tiles, 8, HEAD_DIM), lambda b: (b, 0, 0, 0, 0))
        o, l = pl.pallas_call(
            _attn_p_wide_kernel,
            grid=(BATCH,),
            in_specs=[blk(g), blk(0), blk(0)],
            out_specs=[out, out],
            out_shape=[jax.ShapeDtypeStruct((BATCH, SPAN, tiles, 8, HEAD_DIM), F32)] * 2,
            compiler_params=_params(("arbitrary",)),
            name="attn_prompt_g%d" % g,
        )(view(q_p), view(k_p), view(v_p))
        return o.reshape(4 * ROWS_P, HEAD_DIM), l.reshape(4 * ROWS_P, HEAD_DIM)
    blk = lambda plane: pl.BlockSpec((1, rows, HEAD_DIM), lambda b: (plane, b, 0))
    out = pl.BlockSpec((rows, HEAD_DIM), lambda b: (b, 0))
    return pl.pallas_call(
        _make_attn_p_kernel(dil),
        grid=(BATCH,),
        in_specs=[blk(g), blk(0), blk(0)],
        out_specs=[out, out],
        out_shape=[jax.ShapeDtypeStruct((4 * ROWS_P, HEAD_DIM), F32)] * 2,
        compiler_params=_params(("arbitrary",)),
        name="attn_prompt_g%d" % g,
    )(q_p, k_p, v_p)


NEW_ROWS = DEC_SEQ * HEADS_PER_GROUP


def _joint_softmax(parts):
    mx = None
    for s, _ in parts:
        m = jnp.max(s, axis=0) if s.ndim == 3 else s
        mx = m if mx is None else jnp.maximum(mx, m)
    den = jnp.zeros((8, 1), F32)
    acc = jnp.zeros((8, HEAD_DIM), F32)
    for s, v in parts:
        p = jnp.exp(s - mx)
        if s.ndim == 3:
            den = den + jnp.sum(p, axis=0)
            acc = acc + jnp.sum(p * v, axis=0)
        else:
            den = den + p
            acc = acc + p * v
    return acc / den


def _score(q, k):
    return jnp.sum(q * k, axis=-1, keepdims=True) * ATT_SCALE


def _attn_s_kernel(q_ref, kn0, vn0, kn1, vn1, kn2, vn2, k0, v0, k1, v1, k2, v2, o_ref, kc, vc):
    nb = WINDOWS[0] * HEADS_PER_GROUP
    kc[0:nb, :] = k0[0]
    kc[nb:nb + NEW_ROWS, :] = kn0[0]
    vc[0:nb, :] = v0[0]
    vc[nb:nb + NEW_ROWS, :] = vn0[0]
    half = HEADS_PER_GROUP
    n_even, n_odd = SPAN // 2 + 1, SPAN // 2
    for j in range(2):
        rows = slice(8 * j, 8 * j + 8)
        parts = []
        q = q_ref[0, rows, :]
        for start, n in ((8 * j, n_even), (8 * j + half, n_odd)):
            k3 = kc[pl.ds(start, 8 * n), :].reshape(n, 8, HEAD_DIM)
            v3 = vc[pl.ds(start, 8 * n), :].reshape(n, 8, HEAD_DIM)
            parts.append((_score(q[None], k3), v3))
        for g, (k_ref, v_ref, kn_ref, vn_ref) in ((1, (k1, v1, kn1, vn1)), (2, (k2, v2, kn2, vn2))):
            q = q_ref[g, rows, :]
            parts.append((_score(q[None], k_ref[0, :, rows, :]), v_ref[0, :, rows, :]))
            parts.append((_score(q, kn_ref[0, rows, :]), vn_ref[0, rows, :]))
        o_ref[rows, :] = _joint_softmax(parts)


def _attn_s_call(q_s, kv, caches):
    new = pl.BlockSpec((1, NEW_ROWS, HEAD_DIM), lambda b: (0, b, 0))
    in_specs = [pl.BlockSpec((N_GROUPS, NEW_ROWS, HEAD_DIM), lambda b: (0, b, 0))] + [new] * (2 * N_GROUPS)
    args = [q_s] + [kv[g][t] for g in range(N_GROUPS) for t in (1, 3)]
    for g in range(N_GROUPS):
        width = HEADS_PER_GROUP * DILATIONS[g]
        if DILATIONS[g] == 1:
            shape = (DEC_BATCH, WINDOWS[g] * HEADS_PER_GROUP, HEAD_DIM)
            spec = pl.BlockSpec((1,) + shape[1:], lambda b: (b, 0, 0))
        else:
            shape = (DEC_BATCH, SPAN, width, HEAD_DIM)
            spec = pl.BlockSpec((1, SPAN, NEW_ROWS, HEAD_DIM), lambda b: (b, 0, 0, 0))
        in_specs += [spec, spec]
        args += [caches[g][0].reshape(shape), caches[g][1].reshape(shape)]
    rows0 = WINDOWS[0] * HEADS_PER_GROUP + NEW_ROWS
    return pl.pallas_call(
        _attn_s_kernel,
        grid=(DEC_BATCH,),
        in_specs=in_specs,
        out_specs=pl.BlockSpec((NEW_ROWS, HEAD_DIM), lambda b: (b, 0)),
        out_shape=jax.ShapeDtypeStruct((4 * ROWS_S, HEAD_DIM), F32),
        scratch_shapes=[pltpu.VMEM((rows0, HEAD_DIM), F32)] * 2,
        compiler_params=_params(("arbitrary",)),
        name="attn_sample",
    )(*args)


def _mix_kernel(*refs):
    (yp_ref, ys_ref, hp_ref, hs_ref) = refs[0:4]
    op_refs, lp_refs = refs[4:7], refs[7:10]
    os_ref, wpw_ref, bpw_ref, wo_ref, wg_ref, mp_ref, ms_ref = refs[10:17]
    i = pl.program_id(0)

    def combined_prompt_head(h):
        sl = pl.ds(h, TM2, stride=HEADS_PER_GROUP)
        ls = [l[sl, :] for l in lp_refs]
        mx = jnp.maximum(jnp.maximum(ls[0], ls[1]), ls[2])
        es = [jnp.exp(l - mx) for l in ls]
        num = es[0] * op_refs[0][sl, :] + es[1] * op_refs[1][sl, :] + es[2] * op_refs[2][sl, :]
        return num / (es[0] + es[1] + es[2])

    def sample_head(h):
        return os_ref[pl.ds(h, ROWS_S, stride=HEADS_PER_GROUP), :]

    def mix(y_ref, h_ref, head, m_ref):
        y, h = y_ref[...], h_ref[...]
        o_att = jnp.concatenate([head(hh).astype(BF16) for hh in range(HEADS_PER_GROUP)], axis=1)
        for c in range(D_MODEL // NC):
            sl = slice(c * NC, (c + 1) * NC)
            gl = slice(D_MODEL + c * NC, D_MODEL + (c + 1) * NC)
            conv = jnp.dot(y, wpw_ref[:, sl], preferred_element_type=F32) + bpw_ref[:, sl]
            att = jnp.dot(o_att, wo_ref[:, sl], preferred_element_type=F32)
            gate_conv = _sigmoid(jnp.dot(h, wg_ref[:, sl], preferred_element_type=F32))
            gate_att = _sigmoid(jnp.dot(h, wg_ref[:, gl], preferred_element_type=F32))
            m_ref[:, sl] = (gate_conv * conv + gate_att * att).astype(BF16)

    @pl.when(i < NP2)
    def _():
        mix(yp_ref, hp_ref, combined_prompt_head, mp_ref)

    @pl.when(i == NP2)
    def _():
        mix(ys_ref, hs_ref, sample_head, ms_ref)


def _mix_call(y_p, y_s, h_p, h_s, o_p, l_p, o_s, w_pw, b_pw, w_o, w_gates):
    pi = lambda i: (_pidx(i, NP2), 0)
    zero = lambda i: (0, 0)
    hp = pl.BlockSpec((4 * TM2, HEAD_DIM), pi)
    hs = pl.BlockSpec((4 * ROWS_S, HEAD_DIM), zero)
    const = lambda shape: pl.BlockSpec(shape, zero, pipeline_mode=pl.Buffered(1))
    return pl.pallas_call(
        _mix_kernel,
        grid=(NP2 + 1,),
        in_specs=[
            pl.BlockSpec((TM2, D_CONV), pi), pl.BlockSpec((ROWS_S, D_CONV), zero),
            pl.BlockSpec((TM2, D_MODEL), pi), pl.BlockSpec((ROWS_S, D_MODEL), zero),
            hp, hp, hp, hp, hp, hp, hs,
            const((D_CONV, D_MODEL)), const((1, D_MODEL)), const((GROUP_COLS, D_MODEL)),
            const((D_MODEL, 2 * D_MODEL)),
        ],
        out_specs=[pl.BlockSpec((TM2, D_MODEL), pi), pl.BlockSpec((ROWS_S, D_MODEL), zero)],
        out_shape=[jax.ShapeDtypeStruct((ROWS_P, D_MODEL), BF16),
                   jax.ShapeDtypeStruct((ROWS_S, D_MODEL), BF16)],
        compiler_params=_params(("arbitrary",)),
        name="mix",
    )(y_p, y_s, h_p, h_s, *o_p, *l_p, o_s, w_pw, b_pw, w_o, w_gates)


def _outproj_kernel(mp_ref, ms_ref, xp_ref, xs_ref, w_ref, g_ref, op_ref, os_ref, hp_ref, hs_ref):
    i = pl.program_id(0)

    def project(m_ref, x_ref, o_ref, h_ref):
        x1 = x_ref[...] + jnp.dot(m_ref[...], w_ref[...], preferred_element_type=F32)
        o_ref[...] = x1
        h_ref[...] = _rms(x1, g_ref[...]).astype(BF16)

    @pl.when(i < NP2)
    def _():
        project(mp_ref, xp_ref, op_ref, hp_ref)

    @pl.when(i == NP2)
    def _():
        project(ms_ref, xs_ref, os_ref, hs_ref)


def _outproj_call(m_p, m_s, xp, xs, w_out, g_ffn):
    pi = lambda i: (_pidx(i, NP2), 0)
    zero = lambda i: (0, 0)
    tile_p, tile_s = pl.BlockSpec((TM2, D_MODEL), pi), pl.BlockSpec((ROWS_S, D_MODEL), zero)
    return pl.pallas_call(
        _outproj_kernel,
        grid=(NP2 + 1,),
        in_specs=[tile_p, tile_s, tile_p, tile_s, pl.BlockSpec((D_MODEL, D_MODEL), zero),
                  pl.BlockSpec((1, D_MODEL), zero)],
        out_specs=[tile_p, tile_s, tile_p, tile_s],
        out_shape=[jax.ShapeDtypeStruct((ROWS_P, D_MODEL), F32), jax.ShapeDtypeStruct((ROWS_S, D_MODEL), F32),
                   jax.ShapeDtypeStruct((ROWS_P, D_MODEL), BF16), jax.ShapeDtypeStruct((ROWS_S, D_MODEL), BF16)],
        compiler_params=_params(("arbitrary",)),
        name="out_proj",
    )(m_p, m_s, xp, xs, w_out, g_ffn)


N_CACHE = 2 * N_GROUPS
COPY_STEPS = 2 * DEC_BATCH


def _shift_copies(c, w, cache_refs, new_refs, out_refs, bufs, sems):
    b = c // 2
    loads, stores = [], []
    for g in range(N_GROUPS):
        n = 2 * g + w
        keep = WINDOWS[g] * HEADS_PER_GROUP - NEW_ROWS
        loads.append(pltpu.make_async_copy(cache_refs[n].at[b, pl.ds(NEW_ROWS, keep)], bufs[g].at[w],
                                           sems.at[0, w, g]))
        stores.append(pltpu.make_async_copy(bufs[g].at[w], out_refs[n].at[b, pl.ds(0, keep)],
                                            sems.at[1, w, g]))
        stores.append(pltpu.make_async_copy(new_refs[n].at[b], out_refs[n].at[b, pl.ds(keep, NEW_ROWS)],
                                            sems.at[2, w, g]))
    return loads, stores


def _ffn_kernel(*refs):
    x_hbm, xs_ref, hp_ref, hs_ref, wg_ref, wu_ref, wd_ref, gl_ref = refs[0:8]
    cache_refs = refs[8:8 + N_CACHE]
    new_refs = refs[8 + N_CACHE:8 + 2 * N_CACHE]
    yp_ref, ys_ref = refs[8 + 2 * N_CACHE:10 + 2 * N_CACHE]
    out_refs = refs[10 + 2 * N_CACHE:10 + 3 * N_CACHE]
    bufs = refs[10 + 3 * N_CACHE:10 + 3 * N_CACHE + N_GROUPS]
    sems, x_sem = refs[10 + 3 * N_CACHE + N_GROUPS:]
    i = pl.program_id(0)
    k = pl.program_id(1)

    step = i * KH + k
    for w in range(2):
        @pl.when(step % 2 == w)
        def _():
            copies = lambda c, slot: _shift_copies(c, slot, cache_refs, new_refs, out_refs, bufs, sems)

            @pl.when(step == 0)
            def _():
                for d in copies(step, w)[0]:
                    d.start()

            @pl.when(step < COPY_STEPS)
            def _():
                for d in copies(step, w)[0]:
                    d.wait()

            @pl.when((step >= 1) & (step <= COPY_STEPS))
            def _():
                for d in copies(step - 1, 1 - w)[1]:
                    d.wait()

            @pl.when(step < COPY_STEPS)
            def _():
                for d in copies(step, w)[1]:
                    d.start()

            @pl.when(step + 1 < COPY_STEPS)
            def _():
                for d in copies(step + 1, 1 - w)[0]:
                    d.start()

    def ffn(h_ref, y_ref, start_residual, finish_residual):
        @pl.when(k == 0)
        def _():
            start_residual()

        h = h_ref[...]
        a = jnp.dot(h, wg_ref[...], preferred_element_type=F32)
        u = jnp.dot(h, wu_ref[...], preferred_element_type=F32)
        act = (a * _sigmoid(a) * u).astype(BF16)

        @pl.when(k == 0)
        def _():
            finish_residual()

        y_ref[...] += jnp.dot(act, wd_ref[...], preferred_element_type=F32)

        @pl.when(k == KH - 1)
        def _():
            y_ref[...] = _rms(y_ref[...], gl_ref[...])

    residual = pltpu.make_async_copy(x_hbm.at[pl.ds(pl.multiple_of(i * TM, TM), TM)], yp_ref, x_sem)

    def copy_sample_rows():
        ys_ref[...] = xs_ref[...]

    ffn(hp_ref, yp_ref, residual.start, residual.wait)

    @pl.when(i == 0)
    def _():
        ffn(hs_ref, ys_ref, copy_sample_rows, lambda: None)


def _ffn_call(x1_p, x1_s, h_p, h_s, w_gate, w_up, w_down, g_final, caches, new_rows):
    assert NP * KH >= COPY_STEPS
    pi = lambda i, k: (i, 0)
    zero = lambda i, k: (0, 0)
    hbm = pl.BlockSpec(memory_space=pl.ANY)
    return pl.pallas_call(
        _ffn_kernel,
        grid=(NP, KH),
        in_specs=[
            hbm,
            pl.BlockSpec((ROWS_S, D_MODEL), zero, pipeline_mode=pl.Buffered(1)),
            pl.BlockSpec((TM, D_MODEL), pi),
            pl.BlockSpec((ROWS_S, D_MODEL), zero, pipeline_mode=pl.Buffered(1)),
            pl.BlockSpec((D_MODEL, TH), lambda i, k: (0, k)),
            pl.BlockSpec((D_MODEL, TH), lambda i, k: (0, k)),
            pl.BlockSpec((TH, D_MODEL), lambda i, k: (k, 0)),
            pl.BlockSpec((1, D_MODEL), zero),
        ] + [hbm] * (2 * N_CACHE),
        out_specs=[pl.BlockSpec((TM, D_MODEL), pi), pl.BlockSpec((ROWS_S, D_MODEL), zero)] + [hbm] * N_CACHE,
        out_shape=[jax.ShapeDtypeStruct((ROWS_P, D_MODEL), F32),
                   jax.ShapeDtypeStruct((ROWS_S, D_MODEL), F32)]
        + [jax.ShapeDtypeStruct(c.shape, F32) for c in caches],
        scratch_shapes=[pltpu.VMEM((2, WINDOWS[g] * HEADS_PER_GROUP - NEW_ROWS, HEAD_DIM), F32)
                        for g in range(N_GROUPS)]
        + [pltpu.SemaphoreType.DMA((3, 2, N_GROUPS)), pltpu.SemaphoreType.DMA(())],
        compiler_params=_params(("arbitrary", "arbitrary"), FFN_VMEM_LIMIT),
        name="ffn",
    )(x1_p, x1_s, h_p, h_s, w_gate, w_up, w_down, g_final, *caches, *new_rows)


def _rope_tables(pos):
    half = ROT_DIM // 2
    inv = np.power(ROPE_THETA, -np.arange(0, ROT_DIM, 2, dtype=np.float64) / ROT_DIM)
    ang = np.asarray(pos, np.float64)[:, None] * inv[None, :]
    cos, sin = np.cos(ang), np.sin(ang)
    n = ang.shape[0]
    ones = np.ones((n, HEAD_DIM - ROT_DIM))
    zeros = np.zeros((n, HEAD_DIM - ROT_DIM))
    zh = np.zeros((n, half))
    c = np.concatenate([cos, cos, ones], axis=1)
    s_lo = np.concatenate([-sin, zh, zeros], axis=1)
    s_hi = np.concatenate([zh, sin, zeros], axis=1)
    return jnp.asarray(np.stack([c, s_lo, s_hi]), F32)


def kernel(x_prompt, x_sample, state_conv, cache_k_w128, cache_v_w128, cache_k_w512, cache_v_w512,
           cache_k_w2048, cache_v_w2048, g_mix, w_in, b_glu, w_dw, b_dw, ln_g, ln_b, w_pw, b_pw,
           w_o_att, w_out, g_ffn, w_gate, w_up, w_down, g_final):
    xp = x_prompt.reshape(ROWS_P, D_MODEL)
    xs = x_sample.reshape(ROWS_S, D_MODEL)
    w_in2 = w_in.reshape(D_MODEL, IN_COLS)

    hp, hs = _rms_call(xp, xs, g_mix)
    u_p, u_s, w_gates16 = _glu_call(hp, hs, w_in2, b_glu)
    tab_p = _rope_tables(np.arange(SEQ))
    tab_s = _rope_tables(PAST_LEN + np.arange(ROWS_S) % DEC_SEQ)
    q_p, q_s = _q_call(hp, hs, w_in2, tab_p, tab_s)
    mixer_w = (w_out.reshape(D_MODEL, D_MODEL), w_pw.reshape(D_CONV, D_MODEL), w_o_att.reshape(GROUP_COLS, D_MODEL))
    kv = [_kv_call(hp, hs, w_in2, tab_p, tab_s, g, mixer_w[g]) for g in range(N_GROUPS)]
    w_out16, w_pw16, w_o16 = kv[0][4], kv[1][4], kv[2][4]

    w_dw2 = w_dw.reshape(CONV_WIDTH, D_CONV)
    y_p, w_gate16, w_up16, w_down16 = _conv_p_call(
        u_p, w_dw2, b_dw, ln_g, ln_b, w_gate.reshape(D_MODEL, FFN_HIDDEN), w_up.reshape(D_MODEL, FFN_HIDDEN),
        w_down.reshape(FFN_HIDDEN, D_MODEL))
    ext_s = jnp.concatenate([state_conv[0], u_s.reshape(DEC_BATCH, DEC_SEQ, D_CONV)], axis=1)
    y_s = _conv_s_call(ext_s, w_dw2, b_dw, ln_g, ln_b).reshape(ROWS_S, D_CONV).astype(BF16)
    conv_p = u_p.reshape(BATCH, SEQ, D_CONV)[:, SEQ - (CONV_WIDTH - 1):][None]
    conv_s = ext_s[:, DEC_SEQ:][None]

    caches = ((cache_k_w128, cache_v_w128), (cache_k_w512, cache_v_w512), (cache_k_w2048, cache_v_w2048))
    o_p, l_p = [], []
    for g in range(N_GROUPS):
        o, l = _attn_p_call(q_p, kv[g][0], kv[g][2], g)
        o_p.append(o)
        l_p.append(l)
    o_s = _attn_s_call(q_s, kv, caches)

    m_p, m_s = _mix_call(y_p, y_s, hp, hs, o_p, l_p, o_s, w_pw16, b_pw, w_o16, w_gates16)
    x1_p, x1_s, hf_p, hf_s = _outproj_call(m_p, m_s, xp, xs, w_out16, g_ffn)

    flat = lambda g, a: a.reshape(DEC_BATCH, WINDOWS[g] * HEADS_PER_GROUP, HEAD_DIM)
    cache_flat = [flat(g, caches[g][t]) for g in range(N_GROUPS) for t in range(2)]
    new_rows = [kv[g][t].reshape(DEC_BATCH, NEW_ROWS, HEAD_DIM) for g in range(N_GROUPS) for t in (1, 3)]
    y_p2, y_s2, *shifted = _ffn_call(x1_p, x1_s, hf_p, hf_s, w_gate16, w_up16, w_down16,
                                     g_final.reshape(1, D_MODEL), cache_flat, new_rows)

    cache_out = []
    for g in range(N_GROUPS):
        keep = min(WINDOWS[g], SEQ)
        for t in range(2):
            prompt_rows = kv[g][2 * t] if keep == SEQ else kv[g][5 + t]
            cache_out.append(prompt_rows.reshape(1, BATCH, keep, HEADS_PER_GROUP, HEAD_DIM))
            cache_out.append(shifted[2 * g + t].reshape(1, DEC_BATCH, WINDOWS[g], HEADS_PER_GROUP, HEAD_DIM))
    return (y_p2.reshape(BATCH, SEQ, D_MODEL), y_s2.reshape(DEC_BATCH, DEC_SEQ, D_MODEL),
            conv_p, conv_s, *cache_out)
```

```python
import functools

import jax
import jax.numpy as jnp
import numpy as np
from jax import lax
from jax.experimental import pallas as pl
from jax.experimental.pallas import tpu as pltpu

F32 = jnp.float32
BF16 = jnp.bfloat16

D_MODEL = 2048
BATCH = 4
SEQ = 2048
DEC_BATCH = 32
DEC_SEQ = 4
PAST_LEN = 8192
HEAD_DIM = 128
HEADS_PER_GROUP = 4
DILATIONS = (1, 4, 16)
WINDOWS = (128, 512, 2048)
N_GROUPS = 3
GROUP_COLS = HEADS_PER_GROUP * HEAD_DIM
ATT_WIDTH = N_GROUPS * GROUP_COLS
SPAN = 128
ATT_SCALE = HEAD_DIM ** -0.5
ROT_DIM = HEAD_DIM // 4
ROPE_THETA = 500000.0
D_CONV = D_MODEL // 2
CONV_WIDTH = 31
FFN_HIDDEN = 5632
NORM_EPS = 1e-6
LN_EPS = 1e-5
IN_COLS = 2 * D_CONV + 3 * ATT_WIDTH + 2 * D_MODEL

ROWS_P = BATCH * SEQ
ROWS_S = DEC_BATCH * DEC_SEQ
NEG = -1e30

VMEM_LIMIT = 56 * 1024 * 1024
FFN_VMEM_LIMIT = 62 * 1024 * 1024

TM = 1024
NP = ROWS_P // TM
TN = 512
NC = 256
TM2 = 512
NP2 = ROWS_P // TM2
TH = 512
KH = FFN_HIDDEN // TH
TC = 256
HALO = 32
CS = 4


def _sigmoid(x):
    return 1.0 / (1.0 + jnp.exp(-x))


def _rms(x, g):
    return x * lax.rsqrt(jnp.mean(x * x, axis=-1, keepdims=True) + NORM_EPS) * g


def _params(sem, vmem_limit=VMEM_LIMIT):
    return pltpu.CompilerParams(dimension_semantics=sem, vmem_limit_bytes=vmem_limit)


def _pidx(i, n):
    return jnp.minimum(i, n - 1)


def _rms_kernel(xp_ref, xs_ref, g_ref, hp_ref, hs_ref):
    i = pl.program_id(0)

    @pl.when(i < NP)
    def _():
        hp_ref[...] = _rms(xp_ref[...], g_ref[...]).astype(BF16)

    @pl.when(i == NP)
    def _():
        hs_ref[...] = _rms(xs_ref[...], g_ref[...]).astype(BF16)


def _rms_call(xp, xs, g):
    return pl.pallas_call(
        _rms_kernel,
        grid=(NP + 1,),
        in_specs=[
            pl.BlockSpec((TM, D_MODEL), lambda i: (_pidx(i, NP), 0)),
            pl.BlockSpec((ROWS_S, D_MODEL), lambda i: (0, 0)),
            pl.BlockSpec((1, D_MODEL), lambda i: (0, 0)),
        ],
        out_specs=[
            pl.BlockSpec((TM, D_MODEL), lambda i: (_pidx(i, NP), 0)),
            pl.BlockSpec((ROWS_S, D_MODEL), lambda i: (0, 0)),
        ],
        out_shape=[
            jax.ShapeDtypeStruct((ROWS_P, D_MODEL), BF16),
            jax.ShapeDtypeStruct((ROWS_S, D_MODEL), BF16),
        ],
        compiler_params=_params(("arbitrary",)),
        name="rms_in",
    )(xp, xs, g)


N_GATE_BLOCKS = 2 * D_MODEL // TN


def _glu_kernel(hp_ref, hs_ref, wa_ref, wb_ref, ba_ref, bb_ref, *refs):
    gate32_refs = refs[:N_GATE_BLOCKS]
    up_ref, us_ref, gate16_ref, wa_s, wb_s = refs[N_GATE_BLOCKS:]
    i = pl.program_id(1)

    @pl.when(i == 0)
    def _():
        wa_s[...] = wa_ref[...].astype(BF16)
        wb_s[...] = wb_ref[...].astype(BF16)

    @pl.when(i < NP)
    def _():
        for c, g_ref in enumerate(gate32_refs):
            gate16_ref[:, c * TN:(c + 1) * TN] = g_ref[...].astype(BF16)

    def glu(h, u_ref):
        for c in range(TN // NC):
            sl = slice(c * NC, (c + 1) * NC)
            za = jnp.dot(h, wa_s[:, sl], preferred_element_type=F32) + ba_ref[:, sl]
            zb = jnp.dot(h, wb_s[:, sl], preferred_element_type=F32) + bb_ref[:, sl]
            u_ref[:, sl] = za * _sigmoid(zb)

    @pl.when(i < NP)
    def _():
        glu(hp_ref[...], up_ref)

    @pl.when(i == NP)
    def _():
        glu(hs_ref[...], us_ref)


def _glu_call(hp, hs, w_in, b_glu):
    nj = D_CONV // TN
    slab = D_MODEL // (nj * NP)
    gate0 = (2 * D_CONV + 3 * ATT_WIDTH) // TN
    slab_row = lambda j, i: j * NP + _pidx(i, NP)
    return pl.pallas_call(
        _glu_kernel,
        grid=(nj, NP + 1),
        in_specs=[
            pl.BlockSpec((TM, D_MODEL), lambda j, i: (_pidx(i, NP), 0)),
            pl.BlockSpec((ROWS_S, D_MODEL), lambda j, i: (0, 0)),
            pl.BlockSpec((D_MODEL, TN), lambda j, i: (0, j)),
            pl.BlockSpec((D_MODEL, TN), lambda j, i: (0, j + nj)),
            pl.BlockSpec((1, TN), lambda j, i: (0, j)),
            pl.BlockSpec((1, TN), lambda j, i: (0, j + nj)),
        ] + [pl.BlockSpec((slab, TN), lambda j, i, c=c: (slab_row(j, i), gate0 + c)) for c in range(N_GATE_BLOCKS)],
        out_specs=[
            pl.BlockSpec((TM, TN), lambda j, i: (_pidx(i, NP), j)),
            pl.BlockSpec((ROWS_S, TN), lambda j, i: (0, j)),
            pl.BlockSpec((slab, 2 * D_MODEL), lambda j, i: (slab_row(j, i), 0)),
        ],
        out_shape=[
            jax.ShapeDtypeStruct((ROWS_P, D_CONV), F32),
            jax.ShapeDtypeStruct((ROWS_S, D_CONV), F32),
            jax.ShapeDtypeStruct((D_MODEL, 2 * D_MODEL), BF16),
        ],
        scratch_shapes=[pltpu.VMEM((D_MODEL, TN), BF16), pltpu.VMEM((D_MODEL, TN), BF16)],
        compiler_params=_params(("arbitrary", "arbitrary")),
        name="in_glu",
    )(hp, hs, w_in, w_in, b_glu, b_glu, *([w_in] * N_GATE_BLOCKS))


def _project_heads(h, w_s, t_ref, o_ref, rows):
    if t_ref is not None:
        cos, sin_lo, sin_hi = t_ref[0], t_ref[1], t_ref[2]
    for c in range(TN // NC):
        z = jnp.dot(h, w_s[:, c * NC:(c + 1) * NC], preferred_element_type=F32)
        for hc in range(NC // HEAD_DIM):
            t = z[:, hc * HEAD_DIM:(hc + 1) * HEAD_DIM]
            if t_ref is not None:
                t = (t * cos + pltpu.roll(t, HEAD_DIM - ROT_DIM // 2, 1) * sin_lo
                     + pltpu.roll(t, ROT_DIM // 2, 1) * sin_hi)
            head = c * (NC // HEAD_DIM) + hc
            o_ref[pl.ds(head, rows, stride=HEADS_PER_GROUP), :] = t


def _q_kernel(hp_ref, hs_ref, w0_ref, w1_ref, w2_ref, tp_ref, ts_ref, op_ref, os_ref, w_s):
    i = pl.program_id(0)

    @pl.when(i == 0)
    def _():
        for g, w_ref in enumerate((w0_ref, w1_ref, w2_ref)):
            w_s[g] = w_ref[...].astype(BF16)

    @pl.when(i < NP)
    def _():
        h = hp_ref[...]
        for g in range(N_GROUPS):
            _project_heads(h, w_s.at[g], tp_ref, op_ref.at[g], TM)

    @pl.when(i == NP)
    def _():
        h = hs_ref[...]
        for g in range(N_GROUPS):
            _project_heads(h, w_s.at[g], ts_ref, os_ref.at[g], ROWS_S)


def _make_kv_kernel(tail_rows):
    tiles_per_seq = SEQ // TM

    def kern(hp_ref, hs_ref, wk_ref, wv_ref, tp_ref, ts_ref, slab32_ref, *refs):
        kp_ref, ks_ref, vp_ref, vs_ref, slab16_ref = refs[0:5]
        kt_ref, vt_ref = refs[5:7] if tail_rows else (None, None)
        wk_s, wv_s = refs[-2:]
        i = pl.program_id(0)

        @pl.when(i == 0)
        def _():
            wk_s[...] = wk_ref[...].astype(BF16)
            wv_s[...] = wv_ref[...].astype(BF16)

        @pl.when(i < NP)
        def _():
            slab16_ref[...] = slab32_ref[...].astype(BF16)
            h = hp_ref[...]
            _project_heads(h, wk_s, tp_ref, kp_ref.at[0], TM)
            _project_heads(h, wv_s, None, vp_ref.at[0], TM)

        if tail_rows:
            @pl.when((i < NP) & (i % tiles_per_seq == tiles_per_seq - 1))
            def _():
                kt_ref[0] = kp_ref[0, 4 * TM - tail_rows:4 * TM, :]
                vt_ref[0] = vp_ref[0, 4 * TM - tail_rows:4 * TM, :]

        @pl.when(i == NP)
        def _():
            h = hs_ref[...]
            _project_heads(h, wk_s, ts_ref, ks_ref.at[0], ROWS_S)
            _project_heads(h, wv_s, None, vs_ref.at[0], ROWS_S)

    return kern


Q_PLANE0 = 2 * D_CONV // TN


def _q_call(hp, hs, w_in, tab_p, tab_s):
    wspec = lambda g: pl.BlockSpec((D_MODEL, TN), lambda i: (0, Q_PLANE0 + g), pipeline_mode=pl.Buffered(1))
    return pl.pallas_call(
        _q_kernel,
        grid=(NP + 1,),
        in_specs=[
            pl.BlockSpec((TM, D_MODEL), lambda i: (_pidx(i, NP), 0)),
            pl.BlockSpec((ROWS_S, D_MODEL), lambda i: (0, 0)),
            wspec(0), wspec(1), wspec(2),
            pl.BlockSpec((3, TM, HEAD_DIM), lambda i: (0, _pidx(i, NP) % (SEQ // TM), 0)),
            pl.BlockSpec((3, ROWS_S, HEAD_DIM), lambda i: (0, 0, 0)),
        ],
        out_specs=[
            pl.BlockSpec((N_GROUPS, 4 * TM, HEAD_DIM), lambda i: (0, _pidx(i, NP), 0)),
            pl.BlockSpec((N_GROUPS, 4 * ROWS_S, HEAD_DIM), lambda i: (0, 0, 0)),
        ],
        out_shape=[
            jax.ShapeDtypeStruct((N_GROUPS, 4 * ROWS_P, HEAD_DIM), F32),
            jax.ShapeDtypeStruct((N_GROUPS, 4 * ROWS_S, HEAD_DIM), F32),
        ],
        scratch_shapes=[pltpu.VMEM((N_GROUPS, D_MODEL, TN), BF16)],
        compiler_params=_params(("arbitrary",)),
        name="in_q",
    )(hp, hs, w_in, w_in, w_in, tab_p, tab_s)


def _kv_call(hp, hs, w_in, tab_p, tab_s, g, w_extra):
    wspec = lambda plane: pl.BlockSpec((D_MODEL, TN), lambda i: (0, Q_PLANE0 + plane), pipeline_mode=pl.Buffered(1))
    outp = pl.BlockSpec((1, 4 * TM, HEAD_DIM), lambda i: (0, _pidx(i, NP), 0))
    outs = pl.BlockSpec((1, 4 * ROWS_S, HEAD_DIM), lambda i: (0, 0, 0))
    shp = jax.ShapeDtypeStruct((1, 4 * ROWS_P, HEAD_DIM), F32)
    shs = jax.ShapeDtypeStruct((1, 4 * ROWS_S, HEAD_DIM), F32)
    slab = pl.BlockSpec((w_extra.shape[0] // NP, w_extra.shape[1]), lambda i: (_pidx(i, NP), 0))
    out_specs = [outp, outs, outp, outs, slab]
    out_shape = [shp, shs, shp, shs, jax.ShapeDtypeStruct(w_extra.shape, BF16)]
    tail_rows = HEADS_PER_GROUP * WINDOWS[g] if WINDOWS[g] < SEQ else 0
    if tail_rows:
        assert tail_rows <= 4 * TM
        tail = pl.BlockSpec((1, tail_rows, HEAD_DIM), lambda i: (_pidx(i, NP) // (SEQ // TM), 0, 0))
        out_specs += [tail, tail]
        out_shape += [jax.ShapeDtypeStruct((BATCH, tail_rows, HEAD_DIM), F32)] * 2
    return pl.pallas_call(
        _make_kv_kernel(tail_rows),
        grid=(NP + 1,),
        in_specs=[
            pl.BlockSpec((TM, D_MODEL), lambda i: (_pidx(i, NP), 0)),
            pl.BlockSpec((ROWS_S, D_MODEL), lambda i: (0, 0)),
            wspec(N_GROUPS + g), wspec(2 * N_GROUPS + g),
            pl.BlockSpec((3, TM, HEAD_DIM), lambda i: (0, _pidx(i, NP) % (SEQ // TM), 0)),
            pl.BlockSpec((3, ROWS_S, HEAD_DIM), lambda i: (0, 0, 0)),
            slab,
        ],
        out_specs=out_specs,
        out_shape=out_shape,
        scratch_shapes=[pltpu.VMEM((D_MODEL, TN), BF16), pltpu.VMEM((D_MODEL, TN), BF16)],
        compiler_params=_params(("arbitrary",)),
        name="in_kv%d" % g,
    )(hp, hs, w_in, w_in, tab_p, tab_s, w_extra)


def _ln_silu(y, g, b):
    mu = jnp.mean(y, axis=-1, keepdims=True)
    yc = y - mu
    var = jnp.mean(yc * yc, axis=-1, keepdims=True)
    z = yc * lax.rsqrt(var + LN_EPS) * g + b
    return z * _sigmoid(z)


def _conv_p_kernel(cur_ref, prev_ref, w_ref, bdw_ref, g_ref, b_ref, wg32_ref, wu32_ref, wd32_ref,
                   y_ref, wgu16_ref, wd16_ref, ext, ypre):
    c = pl.program_id(1)
    n_lc = D_CONV // 128
    for k in range(KH):
        src = slice(k * TH, (k + 1) * TH)
        wgu16_ref[:, 2 * k * TH:(2 * k + 1) * TH] = wg32_ref[:, src].astype(BF16)
        wgu16_ref[:, (2 * k + 1) * TH:(2 * k + 2) * TH] = wu32_ref[:, src].astype(BF16)
    wd16_ref[...] = wd32_ref[...].astype(BF16)
    for lc in range(n_lc):
        sl = slice(lc * 128, (lc + 1) * 128)
        ext[lc, pl.ds(HALO, TC), :] = cur_ref[:, sl]

    @pl.when(c > 0)
    def _():
        for lc in range(n_lc):
            ext[lc, 0:HALO, :] = prev_ref[TC - HALO:TC, lc * 128:(lc + 1) * 128]

    @pl.when(c == 0)
    def _():
        for lc in range(n_lc):
            ext[lc, 0:HALO, :] = jnp.zeros((HALO, 128), F32)

    base = HALO - (CONV_WIDTH - 1)
    for lc in range(n_lc):
        sl = slice(lc * 128, (lc + 1) * 128)
        for t0 in range(CS):
            acc = jnp.zeros((TC // CS, 128), F32)
            for j in range(CONV_WIDTH):
                acc = acc + ext[lc, pl.ds(base + t0 + j, TC // CS, stride=CS), :] * w_ref[j:j + 1, sl]
            ypre[lc, pl.ds(t0, TC // CS, stride=CS), :] = acc + bdw_ref[:, sl]

    total = jnp.zeros((TC, 1), F32)
    for lc in range(n_lc):
        total = total + jnp.sum(ypre[lc], axis=-1, keepdims=True)
    mu = total * (1.0 / D_CONV)
    sq = jnp.zeros((TC, 1), F32)
    for lc in range(n_lc):
        yc = ypre[lc] - mu
        sq = sq + jnp.sum(yc * yc, axis=-1, keepdims=True)
    inv = lax.rsqrt(sq * (1.0 / D_CONV) + LN_EPS)
    for lc in range(n_lc):
        sl = slice(lc * 128, (lc + 1) * 128)
        z = (ypre[lc] - mu) * inv * g_ref[:, sl] + b_ref[:, sl]
        y_ref[:, sl] = (z * _sigmoid(z)).astype(BF16)


def _conv_p_call(u_p, w_dw, b_dw, ln_g, ln_b, w_gate, w_up, w_down):
    nc = SEQ // TC
    steps = BATCH * nc
    vec = pl.BlockSpec((1, D_CONV), lambda b, c: (0, 0))
    slab = lambda rows, cols: pl.BlockSpec((rows // steps, cols), lambda b, c: (b * nc + c, 0))
    return pl.pallas_call(
        _conv_p_kernel,
        grid=(BATCH, nc),
        in_specs=[
            pl.BlockSpec((TC, D_CONV), lambda b, c: (b * nc + c, 0)),
            pl.BlockSpec((TC, D_CONV), lambda b, c: (b * nc + jnp.maximum(c - 1, 0), 0)),
            pl.BlockSpec((CONV_WIDTH, D_CONV), lambda b, c: (0, 0)),
            vec, vec, vec,
            slab(D_MODEL, FFN_HIDDEN), slab(D_MODEL, FFN_HIDDEN), slab(FFN_HIDDEN, D_MODEL),
        ],
        out_specs=[pl.BlockSpec((TC, D_CONV), lambda b, c: (b * nc + c, 0)),
                   slab(D_MODEL, 2 * FFN_HIDDEN), slab(FFN_HIDDEN, D_MODEL)],
        out_shape=[jax.ShapeDtypeStruct((ROWS_P, D_CONV), BF16),
                   jax.ShapeDtypeStruct((D_MODEL, 2 * FFN_HIDDEN), BF16),
                   jax.ShapeDtypeStruct((FFN_HIDDEN, D_MODEL), BF16)],
        scratch_shapes=[pltpu.VMEM((D_CONV // 128, HALO + TC, 128), F32),
                        pltpu.VMEM((D_CONV // 128, TC, 128), F32)],
        compiler_params=_params(("arbitrary", "arbitrary")),
        name="conv_prompt",
    )(u_p, u_p, w_dw, b_dw, ln_g, ln_b, w_gate, w_up, w_down)


CONV_SB = 8


def _conv_s_kernel(state_ref, u_ref, w_ref, bdw_ref, g_ref, b_ref, y_ref, new_state_ref, ext):
    past = CONV_WIDTH - 1
    ext[:, 0:past, :] = state_ref[0]
    ext[:, past:past + DEC_SEQ, :] = u_ref[...]
    new_state_ref[0] = ext[:, DEC_SEQ:past + DEC_SEQ, :]
    acc = jnp.zeros((CONV_SB, DEC_SEQ, D_CONV), F32)
    for j in range(CONV_WIDTH):
        acc = acc + ext[:, pl.ds(j, DEC_SEQ), :] * w_ref[j:j + 1, :]
    y_ref[...] = _ln_silu(acc + bdw_ref[...], g_ref[...], b_ref[...])


def _conv_s_call(state_conv, u_s, w_dw, b_dw, ln_g, ln_b):
    vec = pl.BlockSpec((1, D_CONV), lambda b: (0, 0))
    past = CONV_WIDTH - 1
    state = pl.BlockSpec((1, CONV_SB, past, D_CONV), lambda b: (0, b, 0, 0))
    rows = pl.BlockSpec((CONV_SB, DEC_SEQ, D_CONV), lambda b: (b, 0, 0))
    return pl.pallas_call(
        _conv_s_kernel,
        grid=(DEC_BATCH // CONV_SB,),
        in_specs=[state, rows, pl.BlockSpec((CONV_WIDTH, D_CONV), lambda b: (0, 0)), vec, vec, vec],
        out_specs=[rows, state],
        out_shape=[jax.ShapeDtypeStruct((DEC_BATCH, DEC_SEQ, D_CONV), F32),
                   jax.ShapeDtypeStruct(state_conv.shape, F32)],
        scratch_shapes=[pltpu.VMEM((CONV_SB, past + DEC_SEQ, D_CONV), F32)],
        compiler_params=_params(("arbitrary",)),
        name="conv_sample",
    )(state_conv, u_s, w_dw, b_dw, ln_g, ln_b)


def _softmax_block(q, k, v, mask):
    s = lax.dot_general(q.astype(BF16), k.astype(BF16), (((2,), (2,)), ((0,), (0,))),
                        preferred_element_type=F32) * ATT_SCALE
    s = jnp.where(mask[None], s, NEG)
    m = jnp.max(s, axis=-1, keepdims=True)
    p = jnp.exp(s - m)
    den = jnp.sum(p, axis=-1, keepdims=True)
    o = lax.dot_general(p.astype(BF16), v.astype(BF16), (((2,), (1,)), ((0,), (0,))),
                        preferred_element_type=F32) / den
    lse = m + jnp.log(den)
    return o, jnp.broadcast_to(lse, o.shape)


def _make_attn_p_kernel(dil):
    stride = HEADS_PER_GROUP * dil
    nblk = SEQ // dil // SPAN
    blk_rows = SPAN * stride
    pair = min(dil, 2)
    group_rows = pair * HEADS_PER_GROUP
    n_r2 = dil // pair

    def kern(q_ref, k_ref, v_ref, o_ref, l_ref):
        qi = lax.broadcasted_iota(jnp.int32, (SPAN, SPAN), 0)
        ki = lax.broadcasted_iota(jnp.int32, (SPAN, SPAN), 1)
        mask_first = ki <= qi
        qi2 = lax.broadcasted_iota(jnp.int32, (SPAN, 2 * SPAN), 0)
        ki2 = lax.broadcasted_iota(jnp.int32, (SPAN, 2 * SPAN), 1)
        mask_band = (ki2 >= qi2) & (ki2 <= qi2 + SPAN)

        def block(base, key_base, n_keys, mask):
            qs = [pl.ds(base + off, SPAN, stride=stride) for off in range(group_rows)]
            ks = [pl.ds(key_base + off, n_keys, stride=stride) for off in range(group_rows)]
            o, lse = _softmax_block(jnp.stack([q_ref[0, s, :] for s in qs]),
                                    jnp.stack([k_ref[0, s, :] for s in ks]),
                                    jnp.stack([v_ref[0, s, :] for s in ks]), mask)
            for u, s in enumerate(qs):
                o_ref[s, :] = o[u]
                l_ref[s, :] = lse[u]

        def first(r2, carry):
            base = pl.multiple_of(r2 * group_rows, max(group_rows, 8))
            block(base, base, SPAN, mask_first)
            return carry

        def band(t, carry):
            jb = 1 + t // n_r2
            base = pl.multiple_of(jb * blk_rows + (t % n_r2) * group_rows, max(group_rows, 8))
            block(base, base - blk_rows, 2 * SPAN, mask_band)
            return carry

        lax.fori_loop(0, n_r2, first, 0)
        if nblk > 1:
            lax.fori_loop(0, n_r2 * (nblk - 1), band, 0)

    return kern


def _attn_p_wide_kernel(q_ref, k_ref, v_ref, o_ref, l_ref):
    qi = lax.broadcasted_iota(jnp.int32, (SPAN, SPAN), 0)
    ki = lax.broadcasted_iota(jnp.int32, (SPAN, SPAN), 1)
    mask_first = ki <= qi

    def tile(c, carry):
        q8 = jnp.swapaxes(q_ref[0, 0, :, c], 0, 1)
        k8 = jnp.swapaxes(k_ref[0, 0, :, c], 0, 1)
        v8 = jnp.swapaxes(v_ref[0, 0, :, c], 0, 1)
        o, lse = _softmax_block(q8, k8, v8, mask_first)
        o_ref[0, :, c] = jnp.swapaxes(o, 0, 1)
        l_ref[0, :, c] = jnp.swapaxes(lse, 0, 1)
        return carry

    lax.fori_loop(0, q_ref.shape[3], tile, 0)


def _attn_p_call(q_p, k_p, v_p, g):
    dil = DILATIONS[g]
    rows = 4 * SEQ
    if SEQ // dil == SPAN:
        tiles = HEADS_PER_GROUP * dil // 8
        view = lambda a: a.reshape(a.shape[0], BATCH, SPAN, tiles, 8, HEAD_DIM)
        blk = lambda plane: pl.BlockSpec((1, 1, SPAN, tiles, 8, HEAD_DIM), lambda b: (plane, b, 0, 0, 0, 0))
        out = pl.BlockSpec((1, SPAN, tiles, 8, HEAD_DIM), lambda b: (b, 0, 0, 0, 0))
        o, l = pl.pallas_call(
            _attn_p_wide_kernel,
            grid=(BATCH,),
            in_specs=[blk(g), blk(0), blk(0)],
            out_specs=[out, out],
            out_shape=[jax.ShapeDtypeStruct((BATCH, SPAN, tiles, 8, HEAD_DIM), F32)] * 2,
            compiler_params=_params(("arbitrary",)),
            name="attn_prompt_g%d" % g,
        )(view(q_p), view(k_p), view(v_p))
        return o.reshape(4 * ROWS_P, HEAD_DIM), l.reshape(4 * ROWS_P, HEAD_DIM)
    blk = lambda plane: pl.BlockSpec((1, rows, HEAD_DIM), lambda b: (plane, b, 0))
    out = pl.BlockSpec((rows, HEAD_DIM), lambda b: (b, 0))
    return pl.pallas_call(
        _make_attn_p_kernel(dil),
        grid=(BATCH,),
        in_specs=[blk(g), blk(0), blk(0)],
        out_specs=[out, out],
        out_shape=[jax.ShapeDtypeStruct((4 * ROWS_P, HEAD_DIM), F32)] * 2,
        compiler_params=_params(("arbitrary",)),
        name="attn_prompt_g%d" % g,
    )(q_p, k_p, v_p)


NEW_ROWS = DEC_SEQ * HEADS_PER_GROUP


def _joint_softmax(parts):
    mx = None
    for s, _ in parts:
        m = jnp.max(s, axis=0) if s.ndim == 3 else s
        mx = m if mx is None else jnp.maximum(mx, m)
    den = jnp.zeros((8, 1), F32)
    acc = jnp.zeros((8, HEAD_DIM), F32)
    for s, v in parts:
        p = jnp.exp(s - mx)
        if s.ndim == 3:
            den = den + jnp.sum(p, axis=0)
            acc = acc + jnp.sum(p * v, axis=0)
        else:
            den = den + p
            acc = acc + p * v
    return acc / den


def _score(q, k):
    return jnp.sum(q * k, axis=-1, keepdims=True) * ATT_SCALE


def _attn_s_kernel(q_ref, kn0, vn0, kn1, vn1, kn2, vn2, k0, v0, k1, v1, k2, v2, o_ref, kc, vc):
    nb = WINDOWS[0] * HEADS_PER_GROUP
    kc[0:nb, :] = k0[0]
    kc[nb:nb + NEW_ROWS, :] = kn0[0]
    vc[0:nb, :] = v0[0]
    vc[nb:nb + NEW_ROWS, :] = vn0[0]
    half = HEADS_PER_GROUP
    n_even, n_odd = SPAN // 2 + 1, SPAN // 2
    for j in range(2):
        rows = slice(8 * j, 8 * j + 8)
        parts = []
        q = q_ref[0, rows, :]
        for start, n in ((8 * j, n_even), (8 * j + half, n_odd)):
            k3 = kc[pl.ds(start, 8 * n), :].reshape(n, 8, HEAD_DIM)
            v3 = vc[pl.ds(start, 8 * n), :].reshape(n, 8, HEAD_DIM)
            parts.append((_score(q[None], k3), v3))
        for g, (k_ref, v_ref, kn_ref, vn_ref) in ((1, (k1, v1, kn1, vn1)), (2, (k2, v2, kn2, vn2))):
            q = q_ref[g, rows, :]
            parts.append((_score(q[None], k_ref[0, :, rows, :]), v_ref[0, :, rows, :]))
            parts.append((_score(q, kn_ref[0, rows, :]), vn_ref[0, rows, :]))
        o_ref[rows, :] = _joint_softmax(parts)


def _attn_s_call(q_s, kv, caches):
    new = pl.BlockSpec((1, NEW_ROWS, HEAD_DIM), lambda b: (0, b, 0))
    in_specs = [pl.BlockSpec((N_GROUPS, NEW_ROWS, HEAD_DIM), lambda b: (0, b, 0))] + [new] * (2 * N_GROUPS)
    args = [q_s] + [kv[g][t] for g in range(N_GROUPS) for t in (1, 3)]
    for g in range(N_GROUPS):
        width = HEADS_PER_GROUP * DILATIONS[g]
        if DILATIONS[g] == 1:
            shape = (DEC_BATCH, WINDOWS[g] * HEADS_PER_GROUP, HEAD_DIM)
            spec = pl.BlockSpec((1,) + shape[1:], lambda b: (b, 0, 0))
        else:
            shape = (DEC_BATCH, SPAN, width, HEAD_DIM)
            spec = pl.BlockSpec((1, SPAN, NEW_ROWS, HEAD_DIM), lambda b: (b, 0, 0, 0))
        in_specs += [spec, spec]
        args += [caches[g][0].reshape(shape), caches[g][1].reshape(shape)]
    rows0 = WINDOWS[0] * HEADS_PER_GROUP + NEW_ROWS
    return pl.pallas_call(
        _attn_s_kernel,
        grid=(DEC_BATCH,),
        in_specs=in_specs,
        out_specs=pl.BlockSpec((NEW_ROWS, HEAD_DIM), lambda b: (b, 0)),
        out_shape=jax.ShapeDtypeStruct((4 * ROWS_S, HEAD_DIM), F32),
        scratch_shapes=[pltpu.VMEM((rows0, HEAD_DIM), F32)] * 2,
        compiler_params=_params(("arbitrary",)),
        name="attn_sample",
    )(*args)


def _mix_kernel(*refs):
    (yp_ref, ys_ref, hp_ref, hs_ref) = refs[0:4]
    op_refs, lp_refs = refs[4:7], refs[7:10]
    os_ref, wpw_ref, bpw_ref, wo_ref, wg_ref, mp_ref, ms_ref = refs[10:17]
    i = pl.program_id(0)

    def combined_prompt_head(h):
        sl = pl.ds(h, TM2, stride=HEADS_PER_GROUP)
        ls = [l[sl, :] for l in lp_refs]
        mx = jnp.maximum(jnp.maximum(ls[0], ls[1]), ls[2])
        es = [jnp.exp(l - mx) for l in ls]
        num = es[0] * op_refs[0][sl, :] + es[1] * op_refs[1][sl, :] + es[2] * op_refs[2][sl, :]
        return num / (es[0] + es[1] + es[2])

    def sample_head(h):
        return os_ref[pl.ds(h, ROWS_S, stride=HEADS_PER_GROUP), :]

    def mix(y_ref, h_ref, head, m_ref):
        y, h = y_ref[...], h_ref[...]
        o_att = jnp.concatenate([head(hh).astype(BF16) for hh in range(HEADS_PER_GROUP)], axis=1)
        for c in range(D_MODEL // NC):
            sl = slice(c * NC, (c + 1) * NC)
            gl = slice(D_MODEL + c * NC, D_MODEL + (c + 1) * NC)
            conv = jnp.dot(y, wpw_ref[:, sl], preferred_element_type=F32) + bpw_ref[:, sl]
            att = jnp.dot(o_att, wo_ref[:, sl], preferred_element_type=F32)
            gate_conv = _sigmoid(jnp.dot(h, wg_ref[:, sl], preferred_element_type=F32))
            gate_att = _sigmoid(jnp.dot(h, wg_ref[:, gl], preferred_element_type=F32))
            m_ref[:, sl] = (gate_conv * conv + gate_att * att).astype(BF16)

    @pl.when(i < NP2)
    def _():
        mix(yp_ref, hp_ref, combined_prompt_head, mp_ref)

    @pl.when(i == NP2)
    def _():
        mix(ys_ref, hs_ref, sample_head, ms_ref)


def _mix_call(y_p, y_s, h_p, h_s, o_p, l_p, o_s, w_pw, b_pw, w_o, w_gates):
    pi = lambda i: (_pidx(i, NP2), 0)
    zero = lambda i: (0, 0)
    hp = pl.BlockSpec((4 * TM2, HEAD_DIM), pi)
    hs = pl.BlockSpec((4 * ROWS_S, HEAD_DIM), zero)
    const = lambda shape: pl.BlockSpec(shape, zero, pipeline_mode=pl.Buffered(1))
    return pl.pallas_call(
        _mix_kernel,
        grid=(NP2 + 1,),
        in_specs=[
            pl.BlockSpec((TM2, D_CONV), pi), pl.BlockSpec((ROWS_S, D_CONV), zero),
            pl.BlockSpec((TM2, D_MODEL), pi), pl.BlockSpec((ROWS_S, D_MODEL), zero),
            hp, hp, hp, hp, hp, hp, hs,
            const((D_CONV, D_MODEL)), const((1, D_MODEL)), const((GROUP_COLS, D_MODEL)),
            const((D_MODEL, 2 * D_MODEL)),
        ],
        out_specs=[pl.BlockSpec((TM2, D_MODEL), pi), pl.BlockSpec((ROWS_S, D_MODEL), zero)],
        out_shape=[jax.ShapeDtypeStruct((ROWS_P, D_MODEL), BF16),
                   jax.ShapeDtypeStruct((ROWS_S, D_MODEL), BF16)],
        compiler_params=_params(("arbitrary",)),
        name="mix",
    )(y_p, y_s, h_p, h_s, *o_p, *l_p, o_s, w_pw, b_pw, w_o, w_gates)


def _outproj_kernel(mp_ref, ms_ref, xp_ref, xs_ref, w_ref, g_ref, op_ref, os_ref, hp_ref, hs_ref):
    i = pl.program_id(0)

    def project(m_ref, x_ref, o_ref, h_ref):
        x1 = x_ref[...] + jnp.dot(m_ref[...], w_ref[...], preferred_element_type=F32)
        o_ref[...] = x1
        h_ref[...] = _rms(x1, g_ref[...]).astype(BF16)

    @pl.when(i < NP2)
    def _():
        project(mp_ref, xp_ref, op_ref, hp_ref)

    @pl.when(i == NP2)
    def _():
        project(ms_ref, xs_ref, os_ref, hs_ref)


def _outproj_call(m_p, m_s, xp, xs, w_out, g_ffn):
    pi = lambda i: (_pidx(i, NP2), 0)
    zero = lambda i: (0, 0)
    tile_p, tile_s = pl.BlockSpec((TM2, D_MODEL), pi), pl.BlockSpec((ROWS_S, D_MODEL), zero)
    return pl.pallas_call(
        _outproj_kernel,
        grid=(NP2 + 1,),
        in_specs=[tile_p, tile_s, tile_p, tile_s, pl.BlockSpec((D_MODEL, D_MODEL), zero),
                  pl.BlockSpec((1, D_MODEL), zero)],
        out_specs=[tile_p, tile_s, tile_p, tile_s],
        out_shape=[jax.ShapeDtypeStruct((ROWS_P, D_MODEL), F32), jax.ShapeDtypeStruct((ROWS_S, D_MODEL), F32),
                   jax.ShapeDtypeStruct((ROWS_P, D_MODEL), BF16), jax.ShapeDtypeStruct((ROWS_S, D_MODEL), BF16)],
        compiler_params=_params(("arbitrary",)),
        name="out_proj",
    )(m_p, m_s, xp, xs, w_out, g_ffn)


N_CACHE = 2 * N_GROUPS
COPY_STEPS = 2 * DEC_BATCH


def _shift_copies(c, w, cache_refs, new_refs, out_refs, bufs, sems):
    b = c // 2
    loads, stores = [], []
    for g in range(N_GROUPS):
        n = 2 * g + w
        keep = WINDOWS[g] * HEADS_PER_GROUP - NEW_ROWS
        loads.append(pltpu.make_async_copy(cache_refs[n].at[b, pl.ds(NEW_ROWS, keep)], bufs[g].at[w],
                                           sems.at[0, w, g]))
        stores.append(pltpu.make_async_copy(bufs[g].at[w], out_refs[n].at[b, pl.ds(0, keep)],
                                            sems.at[1, w, g]))
        stores.append(pltpu.make_async_copy(new_refs[n].at[b], out_refs[n].at[b, pl.ds(keep, NEW_ROWS)],
                                            sems.at[2, w, g]))
    return loads, stores


def _ffn_kernel(*refs):
    x_hbm, xs_ref, hp_ref, hs_ref, wgu_ref, wd_ref, gl_ref = refs[0:7]
    cache_refs = refs[7:7 + N_CACHE]
    new_refs = refs[7 + N_CACHE:7 + 2 * N_CACHE]
    yp_ref, ys_ref = refs[7 + 2 * N_CACHE:9 + 2 * N_CACHE]
    out_refs = refs[9 + 2 * N_CACHE:9 + 3 * N_CACHE]
    bufs = refs[9 + 3 * N_CACHE:9 + 3 * N_CACHE + N_GROUPS]
    sems, x_sem = refs[9 + 3 * N_CACHE + N_GROUPS:]
    i = pl.program_id(0)
    k = pl.program_id(1)

    step = i * KH + k
    for w in range(2):
        @pl.when(step % 2 == w)
        def _():
            copies = lambda c, slot: _shift_copies(c, slot, cache_refs, new_refs, out_refs, bufs, sems)

            @pl.when(step == 0)
            def _():
                for d in copies(step, w)[0]:
                    d.start()

            @pl.when(step < COPY_STEPS)
            def _():
                for d in copies(step, w)[0]:
                    d.wait()

            @pl.when((step >= 1) & (step <= COPY_STEPS))
            def _():
                for d in copies(step - 1, 1 - w)[1]:
                    d.wait()

            @pl.when(step < COPY_STEPS)
            def _():
                for d in copies(step, w)[1]:
                    d.start()

            @pl.when(step + 1 < COPY_STEPS)
            def _():
                for d in copies(step + 1, 1 - w)[0]:
                    d.start()

    def ffn(h_ref, y_ref, start_residual, finish_residual):
        @pl.when(k == 0)
        def _():
            start_residual()

        h = h_ref[...]
        a = jnp.dot(h, wgu_ref[:, 0:TH], preferred_element_type=F32)
        u = jnp.dot(h, wgu_ref[:, TH:2 * TH], preferred_element_type=F32)
        act = (a * _sigmoid(a) * u).astype(BF16)

        @pl.when(k == 0)
        def _():
            finish_residual()

        y_ref[...] += jnp.dot(act, wd_ref[...], preferred_element_type=F32)

        @pl.when(k == KH - 1)
        def _():
            y_ref[...] = _rms(y_ref[...], gl_ref[...])

    residual = pltpu.make_async_copy(x_hbm.at[pl.ds(pl.multiple_of(i * TM, TM), TM)], yp_ref, x_sem)

    def copy_sample_rows():
        ys_ref[...] = xs_ref[...]

    ffn(hp_ref, yp_ref, residual.start, residual.wait)

    @pl.when(i == 0)
    def _():
        ffn(hs_ref, ys_ref, copy_sample_rows, lambda: None)


def _ffn_call(x1_p, x1_s, h_p, h_s, w_gate_up, w_down, g_final, caches, new_rows):
    assert NP * KH >= COPY_STEPS
    pi = lambda i, k: (i, 0)
    zero = lambda i, k: (0, 0)
    hbm = pl.BlockSpec(memory_space=pl.ANY)
    return pl.pallas_call(
        _ffn_kernel,
        grid=(NP, KH),
        in_specs=[
            hbm,
            pl.BlockSpec((ROWS_S, D_MODEL), zero, pipeline_mode=pl.Buffered(1)),
            pl.BlockSpec((TM, D_MODEL), pi),
            pl.BlockSpec((ROWS_S, D_MODEL), zero, pipeline_mode=pl.Buffered(1)),
            pl.BlockSpec((D_MODEL, 2 * TH), lambda i, k: (0, k)),
            pl.BlockSpec((TH, D_MODEL), lambda i, k: (k, 0)),
            pl.BlockSpec((1, D_MODEL), zero),
        ] + [hbm] * (2 * N_CACHE),
        out_specs=[pl.BlockSpec((TM, D_MODEL), pi), pl.BlockSpec((ROWS_S, D_MODEL), zero)] + [hbm] * N_CACHE,
        out_shape=[jax.ShapeDtypeStruct((ROWS_P, D_MODEL), F32),
                   jax.ShapeDtypeStruct((ROWS_S, D_MODEL), F32)]
        + [jax.ShapeDtypeStruct(c.shape, F32) for c in caches],
        scratch_shapes=[pltpu.VMEM((2, WINDOWS[g] * HEADS_PER_GROUP - NEW_ROWS, HEAD_DIM), F32)
                        for g in range(N_GROUPS)]
        + [pltpu.SemaphoreType.DMA((3, 2, N_GROUPS)), pltpu.SemaphoreType.DMA(())],
        compiler_params=_params(("arbitrary", "arbitrary"), FFN_VMEM_LIMIT),
        name="ffn",
    )(x1_p, x1_s, h_p, h_s, w_gate_up, w_down, g_final, *caches, *new_rows)


def _rope_tables(pos):
    half = ROT_DIM // 2
    inv = np.power(ROPE_THETA, -np.arange(0, ROT_DIM, 2, dtype=np.float64) / ROT_DIM)
    ang = np.asarray(pos, np.float64)[:, None] * inv[None, :]
    cos, sin = np.cos(ang), np.sin(ang)
    n = ang.shape[0]
    ones = np.ones((n, HEAD_DIM - ROT_DIM))
    zeros = np.zeros((n, HEAD_DIM - ROT_DIM))
    zh = np.zeros((n, half))
    c = np.concatenate([cos, cos, ones], axis=1)
    s_lo = np.concatenate([-sin, zh, zeros], axis=1)
    s_hi = np.concatenate([zh, sin, zeros], axis=1)
    return jnp.asarray(np.stack([c, s_lo, s_hi]), F32)


def kernel(x_prompt, x_sample, state_conv, cache_k_w128, cache_v_w128, cache_k_w512, cache_v_w512,
           cache_k_w2048, cache_v_w2048, g_mix, w_in, b_glu, w_dw, b_dw, ln_g, ln_b, w_pw, b_pw,
           w_o_att, w_out, g_ffn, w_gate, w_up, w_down, g_final):
    xp = x_prompt.reshape(ROWS_P, D_MODEL)
    xs = x_sample.reshape(ROWS_S, D_MODEL)
    w_in2 = w_in.reshape(D_MODEL, IN_COLS)

    hp, hs = _rms_call(xp, xs, g_mix)
    u_p, u_s, w_gates16 = _glu_call(hp, hs, w_in2, b_glu)
    tab_p = _rope_tables(np.arange(SEQ))
    tab_s = _rope_tables(PAST_LEN + np.arange(ROWS_S) % DEC_SEQ)
    q_p, q_s = _q_call(hp, hs, w_in2, tab_p, tab_s)
    mixer_w = (w_out.reshape(D_MODEL, D_MODEL), w_pw.reshape(D_CONV, D_MODEL), w_o_att.reshape(GROUP_COLS, D_MODEL))
    kv = [_kv_call(hp, hs, w_in2, tab_p, tab_s, g, mixer_w[g]) for g in range(N_GROUPS)]
    w_out16, w_pw16, w_o16 = kv[0][4], kv[1][4], kv[2][4]

    w_dw2 = w_dw.reshape(CONV_WIDTH, D_CONV)
    y_p, w_gate_up16, w_down16 = _conv_p_call(
        u_p, w_dw2, b_dw, ln_g, ln_b, w_gate.reshape(D_MODEL, FFN_HIDDEN), w_up.reshape(D_MODEL, FFN_HIDDEN),
        w_down.reshape(FFN_HIDDEN, D_MODEL))
    y_s, conv_s = _conv_s_call(state_conv, u_s.reshape(DEC_BATCH, DEC_SEQ, D_CONV), w_dw2, b_dw, ln_g, ln_b)
    y_s = y_s.reshape(ROWS_S, D_CONV).astype(BF16)
    conv_p = u_p.reshape(BATCH, SEQ, D_CONV)[:, SEQ - (CONV_WIDTH - 1):][None]

    caches = ((cache_k_w128, cache_v_w128), (cache_k_w512, cache_v_w512), (cache_k_w2048, cache_v_w2048))
    o_p, l_p = [], []
    for g in range(N_GROUPS):
        o, l = _attn_p_call(q_p, kv[g][0], kv[g][2], g)
        o_p.append(o)
        l_p.append(l)
    o_s = _attn_s_call(q_s, kv, caches)

    m_p, m_s = _mix_call(y_p, y_s, hp, hs, o_p, l_p, o_s, w_pw16, b_pw, w_o16, w_gates16)
    x1_p, x1_s, hf_p, hf_s = _outproj_call(m_p, m_s, xp, xs, w_out16, g_ffn)

    flat = lambda g, a: a.reshape(DEC_BATCH, WINDOWS[g] * HEADS_PER_GROUP, HEAD_DIM)
    cache_flat = [flat(g, caches[g][t]) for g in range(N_GROUPS) for t in range(2)]
    new_rows = [kv[g][t].reshape(DEC_BATCH, NEW_ROWS, HEAD_DIM) for g in range(N_GROUPS) for t in (1, 3)]
    y_p2, y_s2, *shifted = _ffn_call(x1_p, x1_s, hf_p, hf_s, w_gate_up16, w_down16,
                                     g_final.reshape(1, D_MODEL), cache_flat, new_rows)

    cache_out = []
    for g in range(N_GROUPS):
        keep = min(WINDOWS[g], SEQ)
        for t in range(2):
            prompt_rows = kv[g][2 * t] if keep == SEQ else kv[g][5 + t]
            cache_out.append(prompt_rows.reshape(1, BATCH, keep, HEADS_PER_GROUP, HEAD_DIM))
            cache_out.append(shifted[2 * g + t].reshape(1, DEC_BATCH, WINDOWS[g], HEADS_PER_GROUP, HEAD_DIM))
    return (y_p2.reshape(BATCH, SEQ, D_MODEL), y_s2.reshape(DEC_BATCH, DEC_SEQ, D_MODEL),
            conv_p, conv_s, *cache_out)
```

```python
import functools

import jax
import jax.numpy as jnp
import numpy as np
from jax import lax
from jax.experimental import pallas as pl
from jax.experimental.pallas import tpu as pltpu

F32 = jnp.float32
BF16 = jnp.bfloat16

D_MODEL = 2048
BATCH = 4
SEQ = 2048
DEC_BATCH = 32
DEC_SEQ = 4
PAST_LEN = 8192
HEAD_DIM = 128
HEADS_PER_GROUP = 4
DILATIONS = (1, 4, 16)
WINDOWS = (128, 512, 2048)
N_GROUPS = 3
GROUP_COLS = HEADS_PER_GROUP * HEAD_DIM
ATT_WIDTH = N_GROUPS * GROUP_COLS
SPAN = 128
ATT_SCALE = HEAD_DIM ** -0.5
ROT_DIM = HEAD_DIM // 4
ROPE_THETA = 500000.0
D_CONV = D_MODEL // 2
CONV_WIDTH = 31
FFN_HIDDEN = 5632
NORM_EPS = 1e-6
LN_EPS = 1e-5
IN_COLS = 2 * D_CONV + 3 * ATT_WIDTH + 2 * D_MODEL

ROWS_P = BATCH * SEQ
ROWS_S = DEC_BATCH * DEC_SEQ
NEG = -1e30

VMEM_LIMIT = 56 * 1024 * 1024
FFN_VMEM_LIMIT = 62 * 1024 * 1024

TM = 1024
NP = ROWS_P // TM
TN = 512
NC = 256
TM2 = 512
NP2 = ROWS_P // TM2
TH = 512
KH = FFN_HIDDEN // TH
TC = 256
HALO = 32
CS = 4


def _sigmoid(x):
    return 1.0 / (1.0 + jnp.exp(-x))


def _rms(x, g):
    return x * lax.rsqrt(jnp.mean(x * x, axis=-1, keepdims=True) + NORM_EPS) * g


def _params(sem, vmem_limit=VMEM_LIMIT):
    return pltpu.CompilerParams(dimension_semantics=sem, vmem_limit_bytes=vmem_limit)


def _pidx(i, n):
    return jnp.minimum(i, n - 1)


def _rms_kernel(xp_ref, xs_ref, g_ref, hp_ref, hs_ref):
    i = pl.program_id(0)

    @pl.when(i < NP)
    def _():
        hp_ref[...] = _rms(xp_ref[...], g_ref[...]).astype(BF16)

    @pl.when(i == NP)
    def _():
        hs_ref[...] = _rms(xs_ref[...], g_ref[...]).astype(BF16)


def _rms_call(xp, xs, g):
    return pl.pallas_call(
        _rms_kernel,
        grid=(NP + 1,),
        in_specs=[
            pl.BlockSpec((TM, D_MODEL), lambda i: (_pidx(i, NP), 0)),
            pl.BlockSpec((ROWS_S, D_MODEL), lambda i: (0, 0)),
            pl.BlockSpec((1, D_MODEL), lambda i: (0, 0)),
        ],
        out_specs=[
            pl.BlockSpec((TM, D_MODEL), lambda i: (_pidx(i, NP), 0)),
            pl.BlockSpec((ROWS_S, D_MODEL), lambda i: (0, 0)),
        ],
        out_shape=[
            jax.ShapeDtypeStruct((ROWS_P, D_MODEL), BF16),
            jax.ShapeDtypeStruct((ROWS_S, D_MODEL), BF16),
        ],
        compiler_params=_params(("arbitrary",)),
        name="rms_in",
    )(xp, xs, g)


N_GATE_BLOCKS = 2 * D_MODEL // TN


def _glu_kernel(hp_ref, hs_ref, wa_ref, wb_ref, ba_ref, bb_ref, *refs):
    gate32_refs = refs[:N_GATE_BLOCKS]
    up_ref, us_ref, gate16_ref, wa_s, wb_s = refs[N_GATE_BLOCKS:]
    i = pl.program_id(1)

    @pl.when(i == 0)
    def _():
        wa_s[...] = wa_ref[...].astype(BF16)
        wb_s[...] = wb_ref[...].astype(BF16)

    @pl.when(i < NP)
    def _():
        for c, g_ref in enumerate(gate32_refs):
            gate16_ref[:, c * TN:(c + 1) * TN] = g_ref[...].astype(BF16)

    def glu(h, u_ref):
        for c in range(TN // NC):
            sl = slice(c * NC, (c + 1) * NC)
            za = jnp.dot(h, wa_s[:, sl], preferred_element_type=F32) + ba_ref[:, sl]
            zb = jnp.dot(h, wb_s[:, sl], preferred_element_type=F32) + bb_ref[:, sl]
            u_ref[:, sl] = za * _sigmoid(zb)

    @pl.when(i < NP)
    def _():
        glu(hp_ref[...], up_ref)

    @pl.when(i == NP)
    def _():
        glu(hs_ref[...], us_ref)


def _glu_call(hp, hs, w_in, b_glu):
    nj = D_CONV // TN
    slab = D_MODEL // (nj * NP)
    gate0 = (2 * D_CONV + 3 * ATT_WIDTH) // TN
    slab_row = lambda j, i: j * NP + _pidx(i, NP)
    return pl.pallas_call(
        _glu_kernel,
        grid=(nj, NP + 1),
        in_specs=[
            pl.BlockSpec((TM, D_MODEL), lambda j, i: (_pidx(i, NP), 0)),
            pl.BlockSpec((ROWS_S, D_MODEL), lambda j, i: (0, 0)),
            pl.BlockSpec((D_MODEL, TN), lambda j, i: (0, j)),
            pl.BlockSpec((D_MODEL, TN), lambda j, i: (0, j + nj)),
            pl.BlockSpec((1, TN), lambda j, i: (0, j)),
            pl.BlockSpec((1, TN), lambda j, i: (0, j + nj)),
        ] + [pl.BlockSpec((slab, TN), lambda j, i, c=c: (slab_row(j, i), gate0 + c)) for c in range(N_GATE_BLOCKS)],
        out_specs=[
            pl.BlockSpec((TM, TN), lambda j, i: (_pidx(i, NP), j)),
            pl.BlockSpec((ROWS_S, TN), lambda j, i: (0, j)),
            pl.BlockSpec((slab, 2 * D_MODEL), lambda j, i: (slab_row(j, i), 0)),
        ],
        out_shape=[
            jax.ShapeDtypeStruct((ROWS_P, D_CONV), F32),
            jax.ShapeDtypeStruct((ROWS_S, D_CONV), F32),
            jax.ShapeDtypeStruct((D_MODEL, 2 * D_MODEL), BF16),
        ],
        scratch_shapes=[pltpu.VMEM((D_MODEL, TN), BF16), pltpu.VMEM((D_MODEL, TN), BF16)],
        compiler_params=_params(("arbitrary", "arbitrary")),
        name="in_glu",
    )(hp, hs, w_in, w_in, b_glu, b_glu, *([w_in] * N_GATE_BLOCKS))


def _project_heads(h, w_s, t_ref, o_ref, rows):
    if t_ref is not None:
        cos, sin_lo, sin_hi = t_ref[0], t_ref[1], t_ref[2]
    for c in range(TN // NC):
        z = jnp.dot(h, w_s[:, c * NC:(c + 1) * NC], preferred_element_type=F32)
        for hc in range(NC // HEAD_DIM):
            t = z[:, hc * HEAD_DIM:(hc + 1) * HEAD_DIM]
            if t_ref is not None:
                t = (t * cos + pltpu.roll(t, HEAD_DIM - ROT_DIM // 2, 1) * sin_lo
                     + pltpu.roll(t, ROT_DIM // 2, 1) * sin_hi)
            head = c * (NC // HEAD_DIM) + hc
            o_ref[pl.ds(head, rows, stride=HEADS_PER_GROUP), :] = t


def _q_kernel(hp_ref, hs_ref, w0_ref, w1_ref, w2_ref, tp_ref, ts_ref, op_ref, os_ref, w_s):
    i = pl.program_id(0)

    @pl.when(i == 0)
    def _():
        for g, w_ref in enumerate((w0_ref, w1_ref, w2_ref)):
            w_s[g] = w_ref[...].astype(BF16)

    @pl.when(i < NP)
    def _():
        h = hp_ref[...]
        for g in range(N_GROUPS):
            _project_heads(h, w_s.at[g], tp_ref, op_ref.at[g], TM)

    @pl.when(i == NP)
    def _():
        h = hs_ref[...]
        for g in range(N_GROUPS):
            _project_heads(h, w_s.at[g], ts_ref, os_ref.at[g], ROWS_S)


def _make_kv_kernel(tail_rows):
    tiles_per_seq = SEQ // TM

    def kern(hp_ref, hs_ref, wk_ref, wv_ref, tp_ref, ts_ref, slab32_ref, *refs):
        kp_ref, ks_ref, vp_ref, vs_ref, slab16_ref = refs[0:5]
        kt_ref, vt_ref = refs[5:7] if tail_rows else (None, None)
        wk_s, wv_s = refs[-2:]
        i = pl.program_id(0)

        @pl.when(i == 0)
        def _():
            wk_s[...] = wk_ref[...].astype(BF16)
            wv_s[...] = wv_ref[...].astype(BF16)

        @pl.when(i < NP)
        def _():
            slab16_ref[...] = slab32_ref[...].astype(BF16)
            h = hp_ref[...]
            _project_heads(h, wk_s, tp_ref, kp_ref.at[0], TM)
            _project_heads(h, wv_s, None, vp_ref.at[0], TM)

        if tail_rows:
            @pl.when((i < NP) & (i % tiles_per_seq == tiles_per_seq - 1))
            def _():
                kt_ref[0] = kp_ref[0, 4 * TM - tail_rows:4 * TM, :]
                vt_ref[0] = vp_ref[0, 4 * TM - tail_rows:4 * TM, :]

        @pl.when(i == NP)
        def _():
            h = hs_ref[...]
            _project_heads(h, wk_s, ts_ref, ks_ref.at[0], ROWS_S)
            _project_heads(h, wv_s, None, vs_ref.at[0], ROWS_S)

    return kern


Q_PLANE0 = 2 * D_CONV // TN


def _q_call(hp, hs, w_in, tab_p, tab_s):
    wspec = lambda g: pl.BlockSpec((D_MODEL, TN), lambda i: (0, Q_PLANE0 + g), pipeline_mode=pl.Buffered(1))
    return pl.pallas_call(
        _q_kernel,
        grid=(NP + 1,),
        in_specs=[
            pl.BlockSpec((TM, D_MODEL), lambda i: (_pidx(i, NP), 0)),
            pl.BlockSpec((ROWS_S, D_MODEL), lambda i: (0, 0)),
            wspec(0), wspec(1), wspec(2),
            pl.BlockSpec((3, TM, HEAD_DIM), lambda i: (0, _pidx(i, NP) % (SEQ // TM), 0)),
            pl.BlockSpec((3, ROWS_S, HEAD_DIM), lambda i: (0, 0, 0)),
        ],
        out_specs=[
            pl.BlockSpec((N_GROUPS, 4 * TM, HEAD_DIM), lambda i: (0, _pidx(i, NP), 0)),
            pl.BlockSpec((N_GROUPS, 4 * ROWS_S, HEAD_DIM), lambda i: (0, 0, 0)),
        ],
        out_shape=[
            jax.ShapeDtypeStruct((N_GROUPS, 4 * ROWS_P, HEAD_DIM), F32),
            jax.ShapeDtypeStruct((N_GROUPS, 4 * ROWS_S, HEAD_DIM), F32),
        ],
        scratch_shapes=[pltpu.VMEM((N_GROUPS, D_MODEL, TN), BF16)],
        compiler_params=_params(("arbitrary",)),
        name="in_q",
    )(hp, hs, w_in, w_in, w_in, tab_p, tab_s)


def _kv_call(hp, hs, w_in, tab_p, tab_s, g, w_extra):
    wspec = lambda plane: pl.BlockSpec((D_MODEL, TN), lambda i: (0, Q_PLANE0 + plane), pipeline_mode=pl.Buffered(1))
    outp = pl.BlockSpec((1, 4 * TM, HEAD_DIM), lambda i: (0, _pidx(i, NP), 0))
    outs = pl.BlockSpec((1, 4 * ROWS_S, HEAD_DIM), lambda i: (0, 0, 0))
    shp = jax.ShapeDtypeStruct((1, 4 * ROWS_P, HEAD_DIM), F32)
    shs = jax.ShapeDtypeStruct((1, 4 * ROWS_S, HEAD_DIM), F32)
    slab = pl.BlockSpec((w_extra.shape[0] // NP, w_extra.shape[1]), lambda i: (_pidx(i, NP), 0))
    out_specs = [outp, outs, outp, outs, slab]
    out_shape = [shp, shs, shp, shs, jax.ShapeDtypeStruct(w_extra.shape, BF16)]
    tail_rows = HEADS_PER_GROUP * WINDOWS[g] if WINDOWS[g] < SEQ else 0
    if tail_rows:
        assert tail_rows <= 4 * TM
        tail = pl.BlockSpec((1, tail_rows, HEAD_DIM), lambda i: (_pidx(i, NP) // (SEQ // TM), 0, 0))
        out_specs += [tail, tail]
        out_shape += [jax.ShapeDtypeStruct((BATCH, tail_rows, HEAD_DIM), F32)] * 2
    return pl.pallas_call(
        _make_kv_kernel(tail_rows),
        grid=(NP + 1,),
        in_specs=[
            pl.BlockSpec((TM, D_MODEL), lambda i: (_pidx(i, NP), 0)),
            pl.BlockSpec((ROWS_S, D_MODEL), lambda i: (0, 0)),
            wspec(N_GROUPS + g), wspec(2 * N_GROUPS + g),
            pl.BlockSpec((3, TM, HEAD_DIM), lambda i: (0, _pidx(i, NP) % (SEQ // TM), 0)),
            pl.BlockSpec((3, ROWS_S, HEAD_DIM), lambda i: (0, 0, 0)),
            slab,
        ],
        out_specs=out_specs,
        out_shape=out_shape,
        scratch_shapes=[pltpu.VMEM((D_MODEL, TN), BF16), pltpu.VMEM((D_MODEL, TN), BF16)],
        compiler_params=_params(("arbitrary",)),
        name="in_kv%d" % g,
    )(hp, hs, w_in, w_in, tab_p, tab_s, w_extra)


def _ln_silu(y, g, b):
    mu = jnp.mean(y, axis=-1, keepdims=True)
    yc = y - mu
    var = jnp.mean(yc * yc, axis=-1, keepdims=True)
    z = yc * lax.rsqrt(var + LN_EPS) * g + b
    return z * _sigmoid(z)


def _conv_p_kernel(cur_ref, prev_ref, w_ref, bdw_ref, g_ref, b_ref, wg32_ref, wu32_ref, wd32_ref,
                   y_ref, wgu16_ref, wd16_ref, ext, ypre):
    c = pl.program_id(1)
    n_lc = D_CONV // 128
    for k in range(KH):
        src = slice(k * TH, (k + 1) * TH)
        wgu16_ref[:, 2 * k * TH:(2 * k + 1) * TH] = wg32_ref[:, src].astype(BF16)
        wgu16_ref[:, (2 * k + 1) * TH:(2 * k + 2) * TH] = wu32_ref[:, src].astype(BF16)
    wd16_ref[...] = wd32_ref[...].astype(BF16)
    for lc in range(n_lc):
        sl = slice(lc * 128, (lc + 1) * 128)
        ext[lc, pl.ds(HALO, TC), :] = cur_ref[:, sl]

    @pl.when(c > 0)
    def _():
        for lc in range(n_lc):
            ext[lc, 0:HALO, :] = prev_ref[TC - HALO:TC, lc * 128:(lc + 1) * 128]

    @pl.when(c == 0)
    def _():
        for lc in range(n_lc):
            ext[lc, 0:HALO, :] = jnp.zeros((HALO, 128), F32)

    base = HALO - (CONV_WIDTH - 1)
    for lc in range(n_lc):
        sl = slice(lc * 128, (lc + 1) * 128)
        for t0 in range(CS):
            acc = jnp.zeros((TC // CS, 128), F32)
            for j in range(CONV_WIDTH):
                acc = acc + ext[lc, pl.ds(base + t0 + j, TC // CS, stride=CS), :] * w_ref[j:j + 1, sl]
            ypre[lc, pl.ds(t0, TC // CS, stride=CS), :] = acc + bdw_ref[:, sl]

    total = jnp.zeros((TC, 1), F32)
    for lc in range(n_lc):
        total = total + jnp.sum(ypre[lc], axis=-1, keepdims=True)
    mu = total * (1.0 / D_CONV)
    sq = jnp.zeros((TC, 1), F32)
    for lc in range(n_lc):
        yc = ypre[lc] - mu
        sq = sq + jnp.sum(yc * yc, axis=-1, keepdims=True)
    inv = lax.rsqrt(sq * (1.0 / D_CONV) + LN_EPS)
    for lc in range(n_lc):
        sl = slice(lc * 128, (lc + 1) * 128)
        z = (ypre[lc] - mu) * inv * g_ref[:, sl] + b_ref[:, sl]
        y_ref[:, sl] = (z * _sigmoid(z)).astype(BF16)


def _conv_p_call(u_p, w_dw, b_dw, ln_g, ln_b, w_gate, w_up, w_down):
    nc = SEQ // TC
    steps = BATCH * nc
    vec = pl.BlockSpec((1, D_CONV), lambda b, c: (0, 0))
    slab = lambda rows, cols: pl.BlockSpec((rows // steps, cols), lambda b, c: (b * nc + c, 0))
    return pl.pallas_call(
        _conv_p_kernel,
        grid=(BATCH, nc),
        in_specs=[
            pl.BlockSpec((TC, D_CONV), lambda b, c: (b * nc + c, 0)),
            pl.BlockSpec((TC, D_CONV), lambda b, c: (b * nc + jnp.maximum(c - 1, 0), 0)),
            pl.BlockSpec((CONV_WIDTH, D_CONV), lambda b, c: (0, 0)),
            vec, vec, vec,
            slab(D_MODEL, FFN_HIDDEN), slab(D_MODEL, FFN_HIDDEN), slab(FFN_HIDDEN, D_MODEL),
        ],
        out_specs=[pl.BlockSpec((TC, D_CONV), lambda b, c: (b * nc + c, 0)),
                   slab(D_MODEL, 2 * FFN_HIDDEN), slab(FFN_HIDDEN, D_MODEL)],
        out_shape=[jax.ShapeDtypeStruct((ROWS_P, D_CONV), BF16),
                   jax.ShapeDtypeStruct((D_MODEL, 2 * FFN_HIDDEN), BF16),
                   jax.ShapeDtypeStruct((FFN_HIDDEN, D_MODEL), BF16)],
        scratch_shapes=[pltpu.VMEM((D_CONV // 128, HALO + TC, 128), F32),
                        pltpu.VMEM((D_CONV // 128, TC, 128), F32)],
        compiler_params=_params(("arbitrary", "arbitrary")),
        name="conv_prompt",
    )(u_p, u_p, w_dw, b_dw, ln_g, ln_b, w_gate, w_up, w_down)


CONV_SB = 8


def _conv_s_kernel(state_ref, u_ref, w_ref, bdw_ref, g_ref, b_ref, y_ref, new_state_ref, ext):
    past = CONV_WIDTH - 1
    ext[:, 0:past, :] = state_ref[0]
    ext[:, past:past + DEC_SEQ, :] = u_ref[...]
    new_state_ref[0] = ext[:, DEC_SEQ:past + DEC_SEQ, :]
    acc = jnp.zeros((CONV_SB, DEC_SEQ, D_CONV), F32)
    for j in range(CONV_WIDTH):
        acc = acc + ext[:, pl.ds(j, DEC_SEQ), :] * w_ref[j:j + 1, :]
    y_ref[...] = _ln_silu(acc + bdw_ref[...], g_ref[...], b_ref[...])


def _conv_s_call(state_conv, u_s, w_dw, b_dw, ln_g, ln_b):
    vec = pl.BlockSpec((1, D_CONV), lambda b: (0, 0))
    past = CONV_WIDTH - 1
    state = pl.BlockSpec((1, CONV_SB, past, D_CONV), lambda b: (0, b, 0, 0))
    rows = pl.BlockSpec((CONV_SB, DEC_SEQ, D_CONV), lambda b: (b, 0, 0))
    return pl.pallas_call(
        _conv_s_kernel,
        grid=(DEC_BATCH // CONV_SB,),
        in_specs=[state, rows, pl.BlockSpec((CONV_WIDTH, D_CONV), lambda b: (0, 0)), vec, vec, vec],
        out_specs=[rows, state],
        out_shape=[jax.ShapeDtypeStruct((DEC_BATCH, DEC_SEQ, D_CONV), F32),
                   jax.ShapeDtypeStruct(state_conv.shape, F32)],
        scratch_shapes=[pltpu.VMEM((CONV_SB, past + DEC_SEQ, D_CONV), F32)],
        compiler_params=_params(("arbitrary",)),
        name="conv_sample",
    )(state_conv, u_s, w_dw, b_dw, ln_g, ln_b)


def _softmax_block(q, k, v, mask):
    s = lax.dot_general(q.astype(BF16), k.astype(BF16), (((2,), (2,)), ((0,), (0,))),
                        preferred_element_type=F32) * ATT_SCALE
    s = jnp.where(mask[None], s, NEG)
    m = jnp.max(s, axis=-1, keepdims=True)
    p = jnp.exp(s - m)
    den = jnp.sum(p, axis=-1, keepdims=True)
    o = lax.dot_general(p.astype(BF16), v.astype(BF16), (((2,), (1,)), ((0,), (0,))),
                        preferred_element_type=F32) / den
    lse = m + jnp.log(den)
    return o, jnp.broadcast_to(lse, o.shape)


def _make_attn_p_kernel(dil):
    stride = HEADS_PER_GROUP * dil
    nblk = SEQ // dil // SPAN
    blk_rows = SPAN * stride
    pair = min(dil, 2)
    group_rows = pair * HEADS_PER_GROUP
    n_r2 = dil // pair

    def kern(q_ref, k_ref, v_ref, o_ref, l_ref):
        qi = lax.broadcasted_iota(jnp.int32, (SPAN, SPAN), 0)
        ki = lax.broadcasted_iota(jnp.int32, (SPAN, SPAN), 1)
        mask_first = ki <= qi
        qi2 = lax.broadcasted_iota(jnp.int32, (SPAN, 2 * SPAN), 0)
        ki2 = lax.broadcasted_iota(jnp.int32, (SPAN, 2 * SPAN), 1)
        mask_band = (ki2 >= qi2) & (ki2 <= qi2 + SPAN)

        def block(base, key_base, n_keys, mask):
            qs = [pl.ds(base + off, SPAN, stride=stride) for off in range(group_rows)]
            ks = [pl.ds(key_base + off, n_keys, stride=stride) for off in range(group_rows)]
            o, lse = _softmax_block(jnp.stack([q_ref[0, s, :] for s in qs]),
                                    jnp.stack([k_ref[0, s, :] for s in ks]),
                                    jnp.stack([v_ref[0, s, :] for s in ks]), mask)
            for u, s in enumerate(qs):
                o_ref[s, :] = o[u]
                l_ref[s, :] = lse[u]

        def first(r2, carry):
            base = pl.multiple_of(r2 * group_rows, max(group_rows, 8))
            block(base, base, SPAN, mask_first)
            return carry

        def band(t, carry):
            jb = 1 + t // n_r2
            base = pl.multiple_of(jb * blk_rows + (t % n_r2) * group_rows, max(group_rows, 8))
            block(base, base - blk_rows, 2 * SPAN, mask_band)
            return carry

        lax.fori_loop(0, n_r2, first, 0)
        if nblk > 1:
            lax.fori_loop(0, n_r2 * (nblk - 1), band, 0)

    return kern


def _attn_p_wide_kernel(q_ref, k_ref, v_ref, o_ref, l_ref):
    qi = lax.broadcasted_iota(jnp.int32, (SPAN, SPAN), 0)
    ki = lax.broadcasted_iota(jnp.int32, (SPAN, SPAN), 1)
    mask_first = ki <= qi

    def tile(c, carry):
        q8 = jnp.swapaxes(q_ref[0, 0, :, c], 0, 1)
        k8 = jnp.swapaxes(k_ref[0, 0, :, c], 0, 1)
        v8 = jnp.swapaxes(v_ref[0, 0, :, c], 0, 1)
        o, lse = _softmax_block(q8, k8, v8, mask_first)
        o_ref[0, :, c] = jnp.swapaxes(o, 0, 1)
        l_ref[0, :, c] = jnp.swapaxes(lse, 0, 1)
        return carry

    lax.fori_loop(0, q_ref.shape[3], tile, 0)


def _attn_p_call(q_p, k_p, v_p, g):
    dil = DILATIONS[g]
    rows = 4 * SEQ
    if SEQ // dil == SPAN:
        tiles = HEADS_PER_GROUP * dil // 8
        view = lambda a: a.reshape(a.shape[0], BATCH, SPAN, tiles, 8, HEAD_DIM)
        blk = lambda plane: pl.BlockSpec((1, 1, SPAN, tiles, 8, HEAD_DIM), lambda b: (plane, b, 0, 0, 0, 0))
        out = pl.BlockSpec((1, SPAN, tiles, 8, HEAD_DIM), lambda b: (b, 0, 0, 0, 0))
        o, l = pl.pallas_call(
            _attn_p_wide_kernel,
            grid=(BATCH,),
            in_specs=[blk(g), blk(0), blk(0)],
            out_specs=[out, out],
            out_shape=[jax.ShapeDtypeStruct((BATCH, SPAN, tiles, 8, HEAD_DIM), F32)] * 2,
            compiler_params=_params(("arbitrary",)),
            name="attn_prompt_g%d" % g,
        )(view(q_p), view(k_p), view(v_p))
        return o.reshape(4 * ROWS_P, HEAD_DIM), l.reshape(4 * ROWS_P, HEAD_DIM)
    blk = lambda plane: pl.BlockSpec((1, rows, HEAD_DIM), lambda b: (plane, b, 0))
    out = pl.BlockSpec((rows, HEAD_DIM), lambda b: (b, 0))
    return pl.pallas_call(
        _make_attn_p_kernel(dil),
        grid=(BATCH,),
        in_specs=[blk(g), blk(0), blk(0)],
        out_specs=[out, out],
        out_shape=[jax.ShapeDtypeStruct((4 * ROWS_P, HEAD_DIM), F32)] * 2,
        compiler_params=_params(("arbitrary",)),
        name="attn_prompt_g%d" % g,
    )(q_p, k_p, v_p)


NEW_ROWS = DEC_SEQ * HEADS_PER_GROUP


def _joint_softmax(parts):
    mx = None
    for s, _ in parts:
        m = jnp.max(s, axis=0) if s.ndim == 3 else s
        mx = m if mx is None else jnp.maximum(mx, m)
    den = jnp.zeros((8, 1), F32)
    acc = jnp.zeros((8, HEAD_DIM), F32)
    for s, v in parts:
        p = jnp.exp(s - mx)
        if s.ndim == 3:
            den = den + jnp.sum(p, axis=0)
            acc = acc + jnp.sum(p * v, axis=0)
        else:
            den = den + p
            acc = acc + p * v
    return acc / den


def _score(q, k):
    return jnp.sum(q * k, axis=-1, keepdims=True) * ATT_SCALE


def _attn_s_kernel(q_ref, kn0, vn0, kn1, vn1, kn2, vn2, k0, v0, k1, v1, k2, v2, o_ref, kc, vc):
    nb = WINDOWS[0] * HEADS_PER_GROUP
    kc[0:nb, :] = k0[0]
    kc[nb:nb + NEW_ROWS, :] = kn0[0]
    vc[0:nb, :] = v0[0]
    vc[nb:nb + NEW_ROWS, :] = vn0[0]
    half = HEADS_PER_GROUP
    n_even, n_odd = SPAN // 2 + 1, SPAN // 2
    for j in range(2):
        rows = slice(8 * j, 8 * j + 8)
        parts = []
        q = q_ref[0, rows, :]
        for start, n in ((8 * j, n_even), (8 * j + half, n_odd)):
            k3 = kc[pl.ds(start, 8 * n), :].reshape(n, 8, HEAD_DIM)
            v3 = vc[pl.ds(start, 8 * n), :].reshape(n, 8, HEAD_DIM)
            parts.append((_score(q[None], k3), v3))
        for g, (k_ref, v_ref, kn_ref, vn_ref) in ((1, (k1, v1, kn1, vn1)), (2, (k2, v2, kn2, vn2))):
            q = q_ref[g, rows, :]
            parts.append((_score(q[None], k_ref[0, :, rows, :]), v_ref[0, :, rows, :]))
            parts.append((_score(q, kn_ref[0, rows, :]), vn_ref[0, rows, :]))
        o_ref[rows, :] = _joint_softmax(parts)


def _attn_s_call(q_s, kv, caches):
    new = pl.BlockSpec((1, NEW_ROWS, HEAD_DIM), lambda b: (0, b, 0))
    in_specs = [pl.BlockSpec((N_GROUPS, NEW_ROWS, HEAD_DIM), lambda b: (0, b, 0))] + [new] * (2 * N_GROUPS)
    args = [q_s] + [kv[g][t] for g in range(N_GROUPS) for t in (1, 3)]
    for g in range(N_GROUPS):
        width = HEADS_PER_GROUP * DILATIONS[g]
        if DILATIONS[g] == 1:
            shape = (DEC_BATCH, WINDOWS[g] * HEADS_PER_GROUP, HEAD_DIM)
            spec = pl.BlockSpec((1,) + shape[1:], lambda b: (b, 0, 0))
        else:
            shape = (DEC_BATCH, SPAN, width, HEAD_DIM)
            spec = pl.BlockSpec((1, SPAN, NEW_ROWS, HEAD_DIM), lambda b: (b, 0, 0, 0))
        in_specs += [spec, spec]
        args += [caches[g][0].reshape(shape), caches[g][1].reshape(shape)]
    rows0 = WINDOWS[0] * HEADS_PER_GROUP + NEW_ROWS
    return pl.pallas_call(
        _attn_s_kernel,
        grid=(DEC_BATCH,),
        in_specs=in_specs,
        out_specs=pl.BlockSpec((NEW_ROWS, HEAD_DIM), lambda b: (b, 0)),
        out_shape=jax.ShapeDtypeStruct((4 * ROWS_S, HEAD_DIM), F32),
        scratch_shapes=[pltpu.VMEM((rows0, HEAD_DIM), F32)] * 2,
        compiler_params=_params(("arbitrary",)),
        name="attn_sample",
    )(*args)


def _mix_kernel(*refs):
    (yp_ref, ys_ref, hp_ref, hs_ref) = refs[0:4]
    op_refs, lp_refs = refs[4:7], refs[7:10]
    os_ref, wpw_ref, bpw_ref, wo_ref, wg_ref, mp_ref, ms_ref = refs[10:17]
    i = pl.program_id(0)

    def combined_prompt_head(h):
        sl = pl.ds(h, TM2, stride=HEADS_PER_GROUP)
        ls = [l[sl, :] for l in lp_refs]
        mx = jnp.maximum(jnp.maximum(ls[0], ls[1]), ls[2])
        es = [jnp.exp(l - mx) for l in ls]
        num = es[0] * op_refs[0][sl, :] + es[1] * op_refs[1][sl, :] + es[2] * op_refs[2][sl, :]
        return num / (es[0] + es[1] + es[2])

    def sample_head(h):
        return os_ref[pl.ds(h, ROWS_S, stride=HEADS_PER_GROUP), :]

    def mix(y_ref, h_ref, head, m_ref):
        y, h = y_ref[...], h_ref[...]
        o_att = jnp.concatenate([head(hh).astype(BF16) for hh in range(HEADS_PER_GROUP)], axis=1)
        for c in range(D_MODEL // NC):
            sl = slice(c * NC, (c + 1) * NC)
            gl = slice(D_MODEL + c * NC, D_MODEL + (c + 1) * NC)
            conv = jnp.dot(y, wpw_ref[:, sl], preferred_element_type=F32) + bpw_ref[:, sl]
            att = jnp.dot(o_att, wo_ref[:, sl], preferred_element_type=F32)
            gate_conv = _sigmoid(jnp.dot(h, wg_ref[:, sl], preferred_element_type=F32))
            gate_att = _sigmoid(jnp.dot(h, wg_ref[:, gl], preferred_element_type=F32))
            m_ref[:, sl] = (gate_conv * conv + gate_att * att).astype(BF16)

    @pl.when(i < NP2)
    def _():
        mix(yp_ref, hp_ref, combined_prompt_head, mp_ref)

    @pl.when(i == NP2)
    def _():
        mix(ys_ref, hs_ref, sample_head, ms_ref)


def _mix_call(y_p, y_s, h_p, h_s, o_p, l_p, o_s, w_pw, b_pw, w_o, w_gates):
    pi = lambda i: (_pidx(i, NP2), 0)
    zero = lambda i: (0, 0)
    hp = pl.BlockSpec((4 * TM2, HEAD_DIM), pi)
    hs = pl.BlockSpec((4 * ROWS_S, HEAD_DIM), zero)
    const = lambda shape: pl.BlockSpec(shape, zero, pipeline_mode=pl.Buffered(1))
    return pl.pallas_call(
        _mix_kernel,
        grid=(NP2 + 1,),
        in_specs=[
            pl.BlockSpec((TM2, D_CONV), pi), pl.BlockSpec((ROWS_S, D_CONV), zero),
            pl.BlockSpec((TM2, D_MODEL), pi), pl.BlockSpec((ROWS_S, D_MODEL), zero),
            hp, hp, hp, hp, hp, hp, hs,
            const((D_CONV, D_MODEL)), const((1, D_MODEL)), const((GROUP_COLS, D_MODEL)),
            const((D_MODEL, 2 * D_MODEL)),
        ],
        out_specs=[pl.BlockSpec((TM2, D_MODEL), pi), pl.BlockSpec((ROWS_S, D_MODEL), zero)],
        out_shape=[jax.ShapeDtypeStruct((ROWS_P, D_MODEL), BF16),
                   jax.ShapeDtypeStruct((ROWS_S, D_MODEL), BF16)],
        compiler_params=_params(("arbitrary",)),
        name="mix",
    )(y_p, y_s, h_p, h_s, *o_p, *l_p, o_s, w_pw, b_pw, w_o, w_gates)


def _outproj_kernel(mp_ref, ms_ref, xp_ref, xs_ref, w_ref, g_ref, op_ref, os_ref, hp_ref, hs_ref):
    i = pl.program_id(0)

    def project(m_ref, x_ref, o_ref, h_ref):
        x1 = x_ref[...] + jnp.dot(m_ref[...], w_ref[...], preferred_element_type=F32)
        o_ref[...] = x1
        h_ref[...] = _rms(x1, g_ref[...]).astype(BF16)

    @pl.when(i < NP2)
    def _():
        project(mp_ref, xp_ref, op_ref, hp_ref)

    @pl.when(i == NP2)
    def _():
        project(ms_ref, xs_ref, os_ref, hs_ref)


def _outproj_call(m_p, m_s, xp, xs, w_out, g_ffn):
    pi = lambda i: (_pidx(i, NP2), 0)
    zero = lambda i: (0, 0)
    tile_p, tile_s = pl.BlockSpec((TM2, D_MODEL), pi), pl.BlockSpec((ROWS_S, D_MODEL), zero)
    return pl.pallas_call(
        _outproj_kernel,
        grid=(NP2 + 1,),
        in_specs=[tile_p, tile_s, tile_p, tile_s, pl.BlockSpec((D_MODEL, D_MODEL), zero),
                  pl.BlockSpec((1, D_MODEL), zero)],
        out_specs=[tile_p, tile_s, tile_p, tile_s],
        out_shape=[jax.ShapeDtypeStruct((ROWS_P, D_MODEL), F32), jax.ShapeDtypeStruct((ROWS_S, D_MODEL), F32),
                   jax.ShapeDtypeStruct((ROWS_P, D_MODEL), BF16), jax.ShapeDtypeStruct((ROWS_S, D_MODEL), BF16)],
        compiler_params=_params(("arbitrary",)),
        name="out_proj",
    )(m_p, m_s, xp, xs, w_out, g_ffn)


N_CACHE = 2 * N_GROUPS
COPY_STEPS = 2 * DEC_BATCH


def _shift_copies(c, w, cache_refs, new_refs, out_refs, bufs, sems):
    b = c // 2
    loads, stores = [], []
    for g in range(N_GROUPS):
        n = 2 * g + w
        keep = WINDOWS[g] * HEADS_PER_GROUP - NEW_ROWS
        loads.append(pltpu.make_async_copy(cache_refs[n].at[b, pl.ds(NEW_ROWS, keep)], bufs[g].at[w],
                                           sems.at[0, w, g]))
        stores.append(pltpu.make_async_copy(bufs[g].at[w], out_refs[n].at[b, pl.ds(0, keep)],
                                            sems.at[1, w, g]))
        stores.append(pltpu.make_async_copy(new_refs[n].at[b], out_refs[n].at[b, pl.ds(keep, NEW_ROWS)],
                                            sems.at[2, w, g]))
    return loads, stores


def _ffn_kernel(*refs):
    x_hbm, xs_ref, hp_ref, hs_ref, wgu_hbm, wd_hbm, gl_ref = refs[0:7]
    cache_refs = refs[7:7 + N_CACHE]
    new_refs = refs[7 + N_CACHE:7 + 2 * N_CACHE]
    yp_ref, ys_ref = refs[7 + 2 * N_CACHE:9 + 2 * N_CACHE]
    out_refs = refs[9 + 2 * N_CACHE:9 + 3 * N_CACHE]
    bufs = refs[9 + 3 * N_CACHE:9 + 3 * N_CACHE + N_GROUPS]
    wgu_buf, wd_buf, sems, x_sem, w_sems = refs[9 + 3 * N_CACHE + N_GROUPS:]
    i = pl.program_id(0)

    def weight_copies(chunk, slot):
        return (pltpu.make_async_copy(wgu_hbm.at[:, pl.ds(pl.multiple_of(chunk * 2 * TH, 2 * TH), 2 * TH)],
                                      wgu_buf.at[slot], w_sems.at[0, slot]),
                pltpu.make_async_copy(wd_hbm.at[pl.ds(pl.multiple_of(chunk * TH, TH), TH), :],
                                      wd_buf.at[slot], w_sems.at[1, slot]))

    @pl.when(i == 0)
    def _():
        for d in weight_copies(0, 0):
            d.start()
        ys_ref[...] = xs_ref[...]

    residual = pltpu.make_async_copy(x_hbm.at[pl.ds(pl.multiple_of(i * TM, TM), TM)], yp_ref, x_sem)
    residual.start()

    def hidden_chunk(k, carry):
        step = i * KH + k
        for w in range(2):
            @pl.when(step % 2 == w)
            def _():
                for d in weight_copies(k, w):
                    d.wait()

                @pl.when(step + 1 < NP * KH)
                def _():
                    for d in weight_copies(jnp.where(k + 1 == KH, 0, k + 1), 1 - w):
                        d.start()

                copies = lambda c, slot: _shift_copies(c, slot, cache_refs, new_refs, out_refs, bufs, sems)

                @pl.when((step == 0) & (step < COPY_STEPS))
                def _():
                    for d in copies(step, w)[0]:
                        d.start()

                @pl.when(step < COPY_STEPS)
                def _():
                    for d in copies(step, w)[0]:
                        d.wait()

                @pl.when((step >= 1) & (step <= COPY_STEPS))
                def _():
                    for d in copies(step - 1, 1 - w)[1]:
                        d.wait()

                @pl.when(step < COPY_STEPS)
                def _():
                    for d in copies(step, w)[1]:
                        d.start()

                @pl.when(step + 1 < COPY_STEPS)
                def _():
                    for d in copies(step + 1, 1 - w)[0]:
                        d.start()

        slot = step % 2

        def swiglu(h):
            a = jnp.dot(h, wgu_buf[slot, :, 0:TH], preferred_element_type=F32)
            u = jnp.dot(h, wgu_buf[slot, :, TH:2 * TH], preferred_element_type=F32)
            return (a * _sigmoid(a) * u).astype(BF16)

        act = swiglu(hp_ref[...])

        @pl.when(k == 0)
        def _():
            residual.wait()

        yp_ref[...] += jnp.dot(act, wd_buf[slot], preferred_element_type=F32)

        @pl.when(i == 0)
        def _():
            ys_ref[...] += jnp.dot(swiglu(hs_ref[...]), wd_buf[slot], preferred_element_type=F32)

        return carry

    lax.fori_loop(0, KH, hidden_chunk, 0)

    yp_ref[...] = _rms(yp_ref[...], gl_ref[...])

    @pl.when(i == 0)
    def _():
        ys_ref[...] = _rms(ys_ref[...], gl_ref[...])


def _ffn_call(x1_p, x1_s, h_p, h_s, w_gate_up, w_down, g_final, caches, new_rows):
    assert NP * KH >= COPY_STEPS
    pi = lambda i: (i, 0)
    zero = lambda i: (0, 0)
    hbm = pl.BlockSpec(memory_space=pl.ANY)
    return pl.pallas_call(
        _ffn_kernel,
        grid=(NP,),
        in_specs=[
            hbm,
            pl.BlockSpec((ROWS_S, D_MODEL), zero, pipeline_mode=pl.Buffered(1)),
            pl.BlockSpec((TM, D_MODEL), pi),
            pl.BlockSpec((ROWS_S, D_MODEL), zero, pipeline_mode=pl.Buffered(1)),
            hbm, hbm,
            pl.BlockSpec((1, D_MODEL), zero),
        ] + [hbm] * (2 * N_CACHE),
        out_specs=[pl.BlockSpec((TM, D_MODEL), pi), pl.BlockSpec((ROWS_S, D_MODEL), zero)] + [hbm] * N_CACHE,
        out_shape=[jax.ShapeDtypeStruct((ROWS_P, D_MODEL), F32),
                   jax.ShapeDtypeStruct((ROWS_S, D_MODEL), F32)]
        + [jax.ShapeDtypeStruct(c.shape, F32) for c in caches],
        scratch_shapes=[pltpu.VMEM((2, WINDOWS[g] * HEADS_PER_GROUP - NEW_ROWS, HEAD_DIM), F32)
                        for g in range(N_GROUPS)]
        + [pltpu.VMEM((2, D_MODEL, 2 * TH), BF16), pltpu.VMEM((2, TH, D_MODEL), BF16)]
        + [pltpu.SemaphoreType.DMA((3, 2, N_GROUPS)), pltpu.SemaphoreType.DMA(()),
           pltpu.SemaphoreType.DMA((2, 2))],
        compiler_params=_params(("arbitrary",), FFN_VMEM_LIMIT),
        name="ffn",
    )(x1_p, x1_s, h_p, h_s, w_gate_up, w_down, g_final, *caches, *new_rows)


def _rope_tables(pos):
    half = ROT_DIM // 2
    inv = np.power(ROPE_THETA, -np.arange(0, ROT_DIM, 2, dtype=np.float64) / ROT_DIM)
    ang = np.asarray(pos, np.float64)[:, None] * inv[None, :]
    cos, sin = np.cos(ang), np.sin(ang)
    n = ang.shape[0]
    ones = np.ones((n, HEAD_DIM - ROT_DIM))
    zeros = np.zeros((n, HEAD_DIM - ROT_DIM))
    zh = np.zeros((n, half))
    c = np.concatenate([cos, cos, ones], axis=1)
    s_lo = np.concatenate([-sin, zh, zeros], axis=1)
    s_hi = np.concatenate([zh, sin, zeros], axis=1)
    return jnp.asarray(np.stack([c, s_lo, s_hi]), F32)


def kernel(x_prompt, x_sample, state_conv, cache_k_w128, cache_v_w128, cache_k_w512, cache_v_w512,
           cache_k_w2048, cache_v_w2048, g_mix, w_in, b_glu, w_dw, b_dw, ln_g, ln_b, w_pw, b_pw,
           w_o_att, w_out, g_ffn, w_gate, w_up, w_down, g_final):
    xp = x_prompt.reshape(ROWS_P, D_MODEL)
    xs = x_sample.reshape(ROWS_S, D_MODEL)
    w_in2 = w_in.reshape(D_MODEL, IN_COLS)

    hp, hs = _rms_call(xp, xs, g_mix)
    u_p, u_s, w_gates16 = _glu_call(hp, hs, w_in2, b_glu)
    tab_p = _rope_tables(np.arange(SEQ))
    tab_s = _rope_tables(PAST_LEN + np.arange(ROWS_S) % DEC_SEQ)
    q_p, q_s = _q_call(hp, hs, w_in2, tab_p, tab_s)
    mixer_w = (w_out.reshape(D_MODEL, D_MODEL), w_pw.reshape(D_CONV, D_MODEL), w_o_att.reshape(GROUP_COLS, D_MODEL))
    kv = [_kv_call(hp, hs, w_in2, tab_p, tab_s, g, mixer_w[g]) for g in range(N_GROUPS)]
    w_out16, w_pw16, w_o16 = kv[0][4], kv[1][4], kv[2][4]

    w_dw2 = w_dw.reshape(CONV_WIDTH, D_CONV)
    y_p, w_gate_up16, w_down16 = _conv_p_call(
        u_p, w_dw2, b_dw, ln_g, ln_b, w_gate.reshape(D_MODEL, FFN_HIDDEN), w_up.reshape(D_MODEL, FFN_HIDDEN),
        w_down.reshape(FFN_HIDDEN, D_MODEL))
    y_s, conv_s = _conv_s_call(state_conv, u_s.reshape(DEC_BATCH, DEC_SEQ, D_CONV), w_dw2, b_dw, ln_g, ln_b)
    y_s = y_s.reshape(ROWS_S, D_CONV).astype(BF16)
    conv_p = u_p.reshape(BATCH, SEQ, D_CONV)[:, SEQ - (CONV_WIDTH - 1):][None]

    caches = ((cache_k_w128, cache_v_w128), (cache_k_w512, cache_v_w512), (cache_k_w2048, cache_v_w2048))
    o_p, l_p = [], []
    for g in range(N_GROUPS):
        o, l = _attn_p_call(q_p, kv[g][0], kv[g][2], g)
        o_p.append(o)
        l_p.append(l)
    o_s = _attn_s_call(q_s, kv, caches)

    m_p, m_s = _mix_call(y_p, y_s, hp, hs, o_p, l_p, o_s, w_pw16, b_pw, w_o16, w_gates16)
    x1_p, x1_s, hf_p, hf_s = _outproj_call(m_p, m_s, xp, xs, w_out16, g_ffn)

    flat = lambda g, a: a.reshape(DEC_BATCH, WINDOWS[g] * HEADS_PER_GROUP, HEAD_DIM)
    cache_flat = [flat(g, caches[g][t]) for g in range(N_GROUPS) for t in range(2)]
    new_rows = [kv[g][t].reshape(DEC_BATCH, NEW_ROWS, HEAD_DIM) for g in range(N_GROUPS) for t in (1, 3)]
    y_p2, y_s2, *shifted = _ffn_call(x1_p, x1_s, hf_p, hf_s, w_gate_up16, w_down16,
                                     g_final.reshape(1, D_MODEL), cache_flat, new_rows)

    cache_out = []
    for g in range(N_GROUPS):
        keep = min(WINDOWS[g], SEQ)
        for t in range(2):
            prompt_rows = kv[g][2 * t] if keep == SEQ else kv[g][5 + t]
            cache_out.append(prompt_rows.reshape(1, BATCH, keep, HEADS_PER_GROUP, HEAD_DIM))
            cache_out.append(shifted[2 * g + t].reshape(1, DEC_BATCH, WINDOWS[g], HEADS_PER_GROUP, HEAD_DIM))
    return (y_p2.reshape(BATCH, SEQ, D_MODEL), y_s2.reshape(DEC_BATCH, DEC_SEQ, D_MODEL),
            conv_p, conv_s, *cache_out)
```

```python
import functools

import jax
import jax.numpy as jnp
import numpy as np
from jax import lax
from jax.experimental import pallas as pl
from jax.experimental.pallas import tpu as pltpu

F32 = jnp.float32
BF16 = jnp.bfloat16

D_MODEL = 2048
BATCH = 4
SEQ = 2048
DEC_BATCH = 32
DEC_SEQ = 4
PAST_LEN = 8192
HEAD_DIM = 128
HEADS_PER_GROUP = 4
DILATIONS = (1, 4, 16)
WINDOWS = (128, 512, 2048)
N_GROUPS = 3
GROUP_COLS = HEADS_PER_GROUP * HEAD_DIM
ATT_WIDTH = N_GROUPS * GROUP_COLS
SPAN = 128
ATT_SCALE = HEAD_DIM ** -0.5
ROT_DIM = HEAD_DIM // 4
ROPE_THETA = 500000.0
D_CONV = D_MODEL // 2
CONV_WIDTH = 31
FFN_HIDDEN = 5632
NORM_EPS = 1e-6
LN_EPS = 1e-5
IN_COLS = 2 * D_CONV + 3 * ATT_WIDTH + 2 * D_MODEL

ROWS_P = BATCH * SEQ
ROWS_S = DEC_BATCH * DEC_SEQ
NEG = -1e30

VMEM_LIMIT = 56 * 1024 * 1024
FFN_VMEM_LIMIT = 62 * 1024 * 1024

TM = 1024
NP = ROWS_P // TM
TN = 512
NC = 256
TM2 = 512
NP2 = ROWS_P // TM2
TH = 512
KH = FFN_HIDDEN // TH
TC = 512
HALO = 32
CS = 4


def _sigmoid(x):
    return 1.0 / (1.0 + jnp.exp(-x))


def _rms(x, g):
    return x * lax.rsqrt(jnp.mean(x * x, axis=-1, keepdims=True) + NORM_EPS) * g


def _params(sem, vmem_limit=VMEM_LIMIT):
    return pltpu.CompilerParams(dimension_semantics=sem, vmem_limit_bytes=vmem_limit)


def _pidx(i, n):
    return jnp.minimum(i, n - 1)


def _rms_kernel(xp_ref, xs_ref, g_ref, hp_ref, hs_ref):
    i = pl.program_id(0)

    @pl.when(i < NP)
    def _():
        hp_ref[...] = _rms(xp_ref[...], g_ref[...]).astype(BF16)

    @pl.when(i == NP)
    def _():
        hs_ref[...] = _rms(xs_ref[...], g_ref[...]).astype(BF16)


def _rms_call(xp, xs, g):
    return pl.pallas_call(
        _rms_kernel,
        grid=(NP + 1,),
        in_specs=[
            pl.BlockSpec((TM, D_MODEL), lambda i: (_pidx(i, NP), 0)),
            pl.BlockSpec((ROWS_S, D_MODEL), lambda i: (0, 0)),
            pl.BlockSpec((1, D_MODEL), lambda i: (0, 0)),
        ],
        out_specs=[
            pl.BlockSpec((TM, D_MODEL), lambda i: (_pidx(i, NP), 0)),
            pl.BlockSpec((ROWS_S, D_MODEL), lambda i: (0, 0)),
        ],
        out_shape=[
            jax.ShapeDtypeStruct((ROWS_P, D_MODEL), BF16),
            jax.ShapeDtypeStruct((ROWS_S, D_MODEL), BF16),
        ],
        compiler_params=_params(("arbitrary",)),
        name="rms_in",
    )(xp, xs, g)


N_GATE_BLOCKS = 2 * D_MODEL // TN


def _glu_kernel(hp_ref, hs_ref, wa_ref, wb_ref, ba_ref, bb_ref, *refs):
    gate32_refs = refs[:N_GATE_BLOCKS]
    up_ref, us_ref, gate16_ref, wa_s, wb_s = refs[N_GATE_BLOCKS:]
    i = pl.program_id(1)

    @pl.when(i == 0)
    def _():
        wa_s[...] = wa_ref[...].astype(BF16)
        wb_s[...] = wb_ref[...].astype(BF16)

    @pl.when(i < NP)
    def _():
        for c, g_ref in enumerate(gate32_refs):
            gate16_ref[:, c * TN:(c + 1) * TN] = g_ref[...].astype(BF16)

    def glu(h, u_ref):
        for c in range(TN // NC):
            sl = slice(c * NC, (c + 1) * NC)
            za = jnp.dot(h, wa_s[:, sl], preferred_element_type=F32) + ba_ref[:, sl]
            zb = jnp.dot(h, wb_s[:, sl], preferred_element_type=F32) + bb_ref[:, sl]
            u_ref[:, sl] = za * _sigmoid(zb)

    @pl.when(i < NP)
    def _():
        glu(hp_ref[...], up_ref)

    @pl.when(i == NP)
    def _():
        glu(hs_ref[...], us_ref)


def _glu_call(hp, hs, w_in, b_glu):
    nj = D_CONV // TN
    slab = D_MODEL // (nj * NP)
    gate0 = (2 * D_CONV + 3 * ATT_WIDTH) // TN
    slab_row = lambda j, i: j * NP + _pidx(i, NP)
    return pl.pallas_call(
        _glu_kernel,
        grid=(nj, NP + 1),
        in_specs=[
            pl.BlockSpec((TM, D_MODEL), lambda j, i: (_pidx(i, NP), 0)),
            pl.BlockSpec((ROWS_S, D_MODEL), lambda j, i: (0, 0)),
            pl.BlockSpec((D_MODEL, TN), lambda j, i: (0, j)),
            pl.BlockSpec((D_MODEL, TN), lambda j, i: (0, j + nj)),
            pl.BlockSpec((1, TN), lambda j, i: (0, j)),
            pl.BlockSpec((1, TN), lambda j, i: (0, j + nj)),
        ] + [pl.BlockSpec((slab, TN), lambda j, i, c=c: (slab_row(j, i), gate0 + c)) for c in range(N_GATE_BLOCKS)],
        out_specs=[
            pl.BlockSpec((TM, TN), lambda j, i: (_pidx(i, NP), j)),
            pl.BlockSpec((ROWS_S, TN), lambda j, i: (0, j)),
            pl.BlockSpec((slab, 2 * D_MODEL), lambda j, i: (slab_row(j, i), 0)),
        ],
        out_shape=[
            jax.ShapeDtypeStruct((ROWS_P, D_CONV), F32),
            jax.ShapeDtypeStruct((ROWS_S, D_CONV), F32),
            jax.ShapeDtypeStruct((D_MODEL, 2 * D_MODEL), BF16),
        ],
        scratch_shapes=[pltpu.VMEM((D_MODEL, TN), BF16), pltpu.VMEM((D_MODEL, TN), BF16)],
        compiler_params=_params(("arbitrary", "arbitrary")),
        name="in_glu",
    )(hp, hs, w_in, w_in, b_glu, b_glu, *([w_in] * N_GATE_BLOCKS))


def _project_heads(h, w_s, t_ref, o_ref, rows):
    if t_ref is not None:
        cos, sin_lo, sin_hi = t_ref[0], t_ref[1], t_ref[2]
    for c in range(TN // NC):
        z = jnp.dot(h, w_s[:, c * NC:(c + 1) * NC], preferred_element_type=F32)
        for hc in range(NC // HEAD_DIM):
            t = z[:, hc * HEAD_DIM:(hc + 1) * HEAD_DIM]
            if t_ref is not None:
                t = (t * cos + pltpu.roll(t, HEAD_DIM - ROT_DIM // 2, 1) * sin_lo
                     + pltpu.roll(t, ROT_DIM // 2, 1) * sin_hi)
            head = c * (NC // HEAD_DIM) + hc
            o_ref[pl.ds(head, rows, stride=HEADS_PER_GROUP), :] = t


def _q_kernel(hp_ref, hs_ref, w0_ref, w1_ref, w2_ref, tp_ref, ts_ref, op_ref, os_ref, w_s):
    i = pl.program_id(0)

    @pl.when(i == 0)
    def _():
        for g, w_ref in enumerate((w0_ref, w1_ref, w2_ref)):
            w_s[g] = w_ref[...].astype(BF16)

    @pl.when(i < NP)
    def _():
        h = hp_ref[...]
        for g in range(N_GROUPS):
            _project_heads(h, w_s.at[g], tp_ref, op_ref.at[g], TM)

    @pl.when(i == NP)
    def _():
        h = hs_ref[...]
        for g in range(N_GROUPS):
            _project_heads(h, w_s.at[g], ts_ref, os_ref.at[g], ROWS_S)


def _make_kv_kernel(tail_rows):
    tiles_per_seq = SEQ // TM

    def kern(hp_ref, hs_ref, wk_ref, wv_ref, tp_ref, ts_ref, slab32_ref, *refs):
        kp_ref, ks_ref, vp_ref, vs_ref, slab16_ref = refs[0:5]
        kt_ref, vt_ref = refs[5:7] if tail_rows else (None, None)
        wk_s, wv_s = refs[-2:]
        i = pl.program_id(0)

        @pl.when(i == 0)
        def _():
            wk_s[...] = wk_ref[...].astype(BF16)
            wv_s[...] = wv_ref[...].astype(BF16)

        @pl.when(i < NP)
        def _():
            slab16_ref[...] = slab32_ref[...].astype(BF16)
            h = hp_ref[...]
            _project_heads(h, wk_s, tp_ref, kp_ref.at[0], TM)
            _project_heads(h, wv_s, None, vp_ref.at[0], TM)

        if tail_rows:
            @pl.when((i < NP) & (i % tiles_per_seq == tiles_per_seq - 1))
            def _():
                kt_ref[0] = kp_ref[0, 4 * TM - tail_rows:4 * TM, :]
                vt_ref[0] = vp_ref[0, 4 * TM - tail_rows:4 * TM, :]

        @pl.when(i == NP)
        def _():
            h = hs_ref[...]
            _project_heads(h, wk_s, ts_ref, ks_ref.at[0], ROWS_S)
            _project_heads(h, wv_s, None, vs_ref.at[0], ROWS_S)

    return kern


Q_PLANE0 = 2 * D_CONV // TN


def _q_call(hp, hs, w_in, tab_p, tab_s):
    wspec = lambda g: pl.BlockSpec((D_MODEL, TN), lambda i: (0, Q_PLANE0 + g), pipeline_mode=pl.Buffered(1))
    return pl.pallas_call(
        _q_kernel,
        grid=(NP + 1,),
        in_specs=[
            pl.BlockSpec((TM, D_MODEL), lambda i: (_pidx(i, NP), 0)),
            pl.BlockSpec((ROWS_S, D_MODEL), lambda i: (0, 0)),
            wspec(0), wspec(1), wspec(2),
            pl.BlockSpec((3, TM, HEAD_DIM), lambda i: (0, _pidx(i, NP) % (SEQ // TM), 0)),
            pl.BlockSpec((3, ROWS_S, HEAD_DIM), lambda i: (0, 0, 0)),
        ],
        out_specs=[
            pl.BlockSpec((N_GROUPS, 4 * TM, HEAD_DIM), lambda i: (0, _pidx(i, NP), 0)),
            pl.BlockSpec((N_GROUPS, 4 * ROWS_S, HEAD_DIM), lambda i: (0, 0, 0)),
        ],
        out_shape=[
            jax.ShapeDtypeStruct((N_GROUPS, 4 * ROWS_P, HEAD_DIM), F32),
            jax.ShapeDtypeStruct((N_GROUPS, 4 * ROWS_S, HEAD_DIM), F32),
        ],
        scratch_shapes=[pltpu.VMEM((N_GROUPS, D_MODEL, TN), BF16)],
        compiler_params=_params(("arbitrary",)),
        name="in_q",
    )(hp, hs, w_in, w_in, w_in, tab_p, tab_s)


def _kv_call(hp, hs, w_in, tab_p, tab_s, g, w_extra):
    wspec = lambda plane: pl.BlockSpec((D_MODEL, TN), lambda i: (0, Q_PLANE0 + plane), pipeline_mode=pl.Buffered(1))
    outp = pl.BlockSpec((1, 4 * TM, HEAD_DIM), lambda i: (0, _pidx(i, NP), 0))
    outs = pl.BlockSpec((1, 4 * ROWS_S, HEAD_DIM), lambda i: (0, 0, 0))
    shp = jax.ShapeDtypeStruct((1, 4 * ROWS_P, HEAD_DIM), F32)
    shs = jax.ShapeDtypeStruct((1, 4 * ROWS_S, HEAD_DIM), F32)
    slab = pl.BlockSpec((w_extra.shape[0] // NP, w_extra.shape[1]), lambda i: (_pidx(i, NP), 0))
    out_specs = [outp, outs, outp, outs, slab]
    out_shape = [shp, shs, shp, shs, jax.ShapeDtypeStruct(w_extra.shape, BF16)]
    tail_rows = HEADS_PER_GROUP * WINDOWS[g] if WINDOWS[g] < SEQ else 0
    if tail_rows:
        assert tail_rows <= 4 * TM
        tail = pl.BlockSpec((1, tail_rows, HEAD_DIM), lambda i: (_pidx(i, NP) // (SEQ // TM), 0, 0))
        out_specs += [tail, tail]
        out_shape += [jax.ShapeDtypeStruct((BATCH, tail_rows, HEAD_DIM), F32)] * 2
    return pl.pallas_call(
        _make_kv_kernel(tail_rows),
        grid=(NP + 1,),
        in_specs=[
            pl.BlockSpec((TM, D_MODEL), lambda i: (_pidx(i, NP), 0)),
            pl.BlockSpec((ROWS_S, D_MODEL), lambda i: (0, 0)),
            wspec(N_GROUPS + g), wspec(2 * N_GROUPS + g),
            pl.BlockSpec((3, TM, HEAD_DIM), lambda i: (0, _pidx(i, NP) % (SEQ // TM), 0)),
            pl.BlockSpec((3, ROWS_S, HEAD_DIM), lambda i: (0, 0, 0)),
            slab,
        ],
        out_specs=out_specs,
        out_shape=out_shape,
        scratch_shapes=[pltpu.VMEM((D_MODEL, TN), BF16), pltpu.VMEM((D_MODEL, TN), BF16)],
        compiler_params=_params(("arbitrary",)),
        name="in_kv%d" % g,
    )(hp, hs, w_in, w_in, tab_p, tab_s, w_extra)


def _ln_silu(y, g, b):
    mu = jnp.mean(y, axis=-1, keepdims=True)
    yc = y - mu
    var = jnp.mean(yc * yc, axis=-1, keepdims=True)
    z = yc * lax.rsqrt(var + LN_EPS) * g + b
    return z * _sigmoid(z)


def _conv_p_kernel(cur_ref, prev_ref, w_ref, bdw_ref, g_ref, b_ref, wg32_ref, wu32_ref, wd32_ref,
                   y_ref, wgu16_ref, wd16_ref, ext, ypre):
    c = pl.program_id(1)
    n_lc = D_CONV // 128
    for k in range(KH):
        src = slice(k * TH, (k + 1) * TH)
        wgu16_ref[:, 2 * k * TH:(2 * k + 1) * TH] = wg32_ref[:, src].astype(BF16)
        wgu16_ref[:, (2 * k + 1) * TH:(2 * k + 2) * TH] = wu32_ref[:, src].astype(BF16)
    wd16_ref[...] = wd32_ref[...].astype(BF16)
    for lc in range(n_lc):
        sl = slice(lc * 128, (lc + 1) * 128)
        ext[lc, pl.ds(HALO, TC), :] = cur_ref[:, sl]

    @pl.when(c > 0)
    def _():
        for lc in range(n_lc):
            ext[lc, 0:HALO, :] = prev_ref[TC - HALO:TC, lc * 128:(lc + 1) * 128]

    @pl.when(c == 0)
    def _():
        for lc in range(n_lc):
            ext[lc, 0:HALO, :] = jnp.zeros((HALO, 128), F32)

    base = HALO - (CONV_WIDTH - 1)
    for lc in range(n_lc):
        sl = slice(lc * 128, (lc + 1) * 128)
        for t0 in range(CS):
            acc = jnp.zeros((TC // CS, 128), F32)
            for j in range(CONV_WIDTH):
                acc = acc + ext[lc, pl.ds(base + t0 + j, TC // CS, stride=CS), :] * w_ref[j:j + 1, sl]
            ypre[lc, pl.ds(t0, TC // CS, stride=CS), :] = acc + bdw_ref[:, sl]

    total = jnp.zeros((TC, 1), F32)
    for lc in range(n_lc):
        total = total + jnp.sum(ypre[lc], axis=-1, keepdims=True)
    mu = total * (1.0 / D_CONV)
    sq = jnp.zeros((TC, 1), F32)
    for lc in range(n_lc):
        yc = ypre[lc] - mu
        sq = sq + jnp.sum(yc * yc, axis=-1, keepdims=True)
    inv = lax.rsqrt(sq * (1.0 / D_CONV) + LN_EPS)
    for lc in range(n_lc):
        sl = slice(lc * 128, (lc + 1) * 128)
        z = (ypre[lc] - mu) * inv * g_ref[:, sl] + b_ref[:, sl]
        y_ref[:, sl] = (z * _sigmoid(z)).astype(BF16)


def _conv_p_call(u_p, w_dw, b_dw, ln_g, ln_b, w_gate, w_up, w_down):
    nc = SEQ // TC
    steps = BATCH * nc
    vec = pl.BlockSpec((1, D_CONV), lambda b, c: (0, 0))
    slab = lambda rows, cols: pl.BlockSpec((rows // steps, cols), lambda b, c: (b * nc + c, 0))
    return pl.pallas_call(
        _conv_p_kernel,
        grid=(BATCH, nc),
        in_specs=[
            pl.BlockSpec((TC, D_CONV), lambda b, c: (b * nc + c, 0)),
            pl.BlockSpec((TC, D_CONV), lambda b, c: (b * nc + jnp.maximum(c - 1, 0), 0)),
            pl.BlockSpec((CONV_WIDTH, D_CONV), lambda b, c: (0, 0)),
            vec, vec, vec,
            slab(D_MODEL, FFN_HIDDEN), slab(D_MODEL, FFN_HIDDEN), slab(FFN_HIDDEN, D_MODEL),
        ],
        out_specs=[pl.BlockSpec((TC, D_CONV), lambda b, c: (b * nc + c, 0)),
                   slab(D_MODEL, 2 * FFN_HIDDEN), slab(FFN_HIDDEN, D_MODEL)],
        out_shape=[jax.ShapeDtypeStruct((ROWS_P, D_CONV), BF16),
                   jax.ShapeDtypeStruct((D_MODEL, 2 * FFN_HIDDEN), BF16),
                   jax.ShapeDtypeStruct((FFN_HIDDEN, D_MODEL), BF16)],
        scratch_shapes=[pltpu.VMEM((D_CONV // 128, HALO + TC, 128), F32),
                        pltpu.VMEM((D_CONV // 128, TC, 128), F32)],
        compiler_params=_params(("arbitrary", "arbitrary")),
        name="conv_prompt",
    )(u_p, u_p, w_dw, b_dw, ln_g, ln_b, w_gate, w_up, w_down)


CONV_SB = 8


def _conv_s_kernel(state_ref, u_ref, w_ref, bdw_ref, g_ref, b_ref, y_ref, new_state_ref, ext):
    past = CONV_WIDTH - 1
    ext[:, 0:past, :] = state_ref[0]
    ext[:, past:past + DEC_SEQ, :] = u_ref[...]
    new_state_ref[0] = ext[:, DEC_SEQ:past + DEC_SEQ, :]
    acc = jnp.zeros((CONV_SB, DEC_SEQ, D_CONV), F32)
    for j in range(CONV_WIDTH):
        acc = acc + ext[:, pl.ds(j, DEC_SEQ), :] * w_ref[j:j + 1, :]
    y_ref[...] = _ln_silu(acc + bdw_ref[...], g_ref[...], b_ref[...])


def _conv_s_call(state_conv, u_s, w_dw, b_dw, ln_g, ln_b):
    vec = pl.BlockSpec((1, D_CONV), lambda b: (0, 0))
    past = CONV_WIDTH - 1
    state = pl.BlockSpec((1, CONV_SB, past, D_CONV), lambda b: (0, b, 0, 0))
    rows = pl.BlockSpec((CONV_SB, DEC_SEQ, D_CONV), lambda b: (b, 0, 0))
    return pl.pallas_call(
        _conv_s_kernel,
        grid=(DEC_BATCH // CONV_SB,),
        in_specs=[state, rows, pl.BlockSpec((CONV_WIDTH, D_CONV), lambda b: (0, 0)), vec, vec, vec],
        out_specs=[rows, state],
        out_shape=[jax.ShapeDtypeStruct((DEC_BATCH, DEC_SEQ, D_CONV), F32),
                   jax.ShapeDtypeStruct(state_conv.shape, F32)],
        scratch_shapes=[pltpu.VMEM((CONV_SB, past + DEC_SEQ, D_CONV), F32)],
        compiler_params=_params(("arbitrary",)),
        name="conv_sample",
    )(state_conv, u_s, w_dw, b_dw, ln_g, ln_b)


def _softmax_block(q, k, v, mask):
    s = lax.dot_general(q.astype(BF16), k.astype(BF16), (((2,), (2,)), ((0,), (0,))),
                        preferred_element_type=F32) * ATT_SCALE
    s = jnp.where(mask[None], s, NEG)
    m = jnp.max(s, axis=-1, keepdims=True)
    p = jnp.exp(s - m)
    den = jnp.sum(p, axis=-1, keepdims=True)
    o = lax.dot_general(p.astype(BF16), v.astype(BF16), (((2,), (1,)), ((0,), (0,))),
                        preferred_element_type=F32) / den
    lse = m + jnp.log(den)
    return o, jnp.broadcast_to(lse, o.shape)


def _make_attn_p_kernel(dil):
    stride = HEADS_PER_GROUP * dil
    nblk = SEQ // dil // SPAN
    blk_rows = SPAN * stride
    pair = min(dil, 2)
    group_rows = pair * HEADS_PER_GROUP
    n_r2 = dil // pair

    def kern(q_ref, k_ref, v_ref, o_ref, l_ref):
        qi = lax.broadcasted_iota(jnp.int32, (SPAN, SPAN), 0)
        ki = lax.broadcasted_iota(jnp.int32, (SPAN, SPAN), 1)
        mask_first = ki <= qi
        qi2 = lax.broadcasted_iota(jnp.int32, (SPAN, 2 * SPAN), 0)
        ki2 = lax.broadcasted_iota(jnp.int32, (SPAN, 2 * SPAN), 1)
        mask_band = (ki2 >= qi2) & (ki2 <= qi2 + SPAN)

        def block(base, key_base, n_keys, mask):
            qs = [pl.ds(base + off, SPAN, stride=stride) for off in range(group_rows)]
            ks = [pl.ds(key_base + off, n_keys, stride=stride) for off in range(group_rows)]
            o, lse = _softmax_block(jnp.stack([q_ref[0, s, :] for s in qs]),
                                    jnp.stack([k_ref[0, s, :] for s in ks]),
                                    jnp.stack([v_ref[0, s, :] for s in ks]), mask)
            for u, s in enumerate(qs):
                o_ref[s, :] = o[u]
                l_ref[s, :] = lse[u]

        def first(r2, carry):
            base = pl.multiple_of(r2 * group_rows, max(group_rows, 8))
            block(base, base, SPAN, mask_first)
            return carry

        def band(t, carry):
            jb = 1 + t // n_r2
            base = pl.multiple_of(jb * blk_rows + (t % n_r2) * group_rows, max(group_rows, 8))
            block(base, base - blk_rows, 2 * SPAN, mask_band)
            return carry

        lax.fori_loop(0, n_r2, first, 0)
        if nblk > 1:
            lax.fori_loop(0, n_r2 * (nblk - 1), band, 0)

    return kern


def _attn_p_wide_kernel(q_ref, k_ref, v_ref, o_ref, l_ref):
    qi = lax.broadcasted_iota(jnp.int32, (SPAN, SPAN), 0)
    ki = lax.broadcasted_iota(jnp.int32, (SPAN, SPAN), 1)
    mask_first = ki <= qi

    def tile(c, carry):
        q8 = jnp.swapaxes(q_ref[0, 0, :, c], 0, 1)
        k8 = jnp.swapaxes(k_ref[0, 0, :, c], 0, 1)
        v8 = jnp.swapaxes(v_ref[0, 0, :, c], 0, 1)
        o, lse = _softmax_block(q8, k8, v8, mask_first)
        o_ref[0, :, c] = jnp.swapaxes(o, 0, 1)
        l_ref[0, :, c] = jnp.swapaxes(lse, 0, 1)
        return carry

    lax.fori_loop(0, q_ref.shape[3], tile, 0)


def _attn_p_call(q_p, k_p, v_p, g):
    dil = DILATIONS[g]
    rows = 4 * SEQ
    if SEQ // dil == SPAN:
        tiles = HEADS_PER_GROUP * dil // 8
        view = lambda a: a.reshape(a.shape[0], BATCH, SPAN, tiles, 8, HEAD_DIM)
        blk = lambda plane: pl.BlockSpec((1, 1, SPAN, tiles, 8, HEAD_DIM), lambda b: (plane, b, 0, 0, 0, 0))
        out = pl.BlockSpec((1, SPAN, tiles, 8, HEAD_DIM), lambda b: (b, 0, 0, 0, 0))
        o, l = pl.pallas_call(
            _attn_p_wide_kernel,
            grid=(BATCH,),
            in_specs=[blk(g), blk(0), blk(0)],
            out_specs=[out, out],
            out_shape=[jax.ShapeDtypeStruct((BATCH, SPAN, tiles, 8, HEAD_DIM), F32)] * 2,
            compiler_params=_params(("arbitrary",)),
            name="attn_prompt_g%d" % g,
        )(view(q_p), view(k_p), view(v_p))
        return o.reshape(4 * ROWS_P, HEAD_DIM), l.reshape(4 * ROWS_P, HEAD_DIM)
    blk = lambda plane: pl.BlockSpec((1, rows, HEAD_DIM), lambda b: (plane, b, 0))
    out = pl.BlockSpec((rows, HEAD_DIM), lambda b: (b, 0))
    return pl.pallas_call(
        _make_attn_p_kernel(dil),
        grid=(BATCH,),
        in_specs=[blk(g), blk(0), blk(0)],
        out_specs=[out, out],
        out_shape=[jax.ShapeDtypeStruct((4 * ROWS_P, HEAD_DIM), F32)] * 2,
        compiler_params=_params(("arbitrary",)),
        name="attn_prompt_g%d" % g,
    )(q_p, k_p, v_p)


NEW_ROWS = DEC_SEQ * HEADS_PER_GROUP


def _joint_softmax(parts):
    mx = None
    for s, _ in parts:
        m = jnp.max(s, axis=0) if s.ndim == 3 else s
        mx = m if mx is None else jnp.maximum(mx, m)
    den = jnp.zeros((8, 1), F32)
    acc = jnp.zeros((8, HEAD_DIM), F32)
    for s, v in parts:
        p = jnp.exp(s - mx)
        if s.ndim == 3:
            den = den + jnp.sum(p, axis=0)
            acc = acc + jnp.sum(p * v, axis=0)
        else:
            den = den + p
            acc = acc + p * v
    return acc / den


def _score(q, k):
    return jnp.sum(q * k, axis=-1, keepdims=True) * ATT_SCALE


ATT_SB = 4


def _attn_s_kernel(q_ref, kn0, vn0, kn1, vn1, kn2, vn2, k0, v0, k1, v1, k2, v2, o_ref, kc, vc):
    nb = WINDOWS[0] * HEADS_PER_GROUP
    half = HEADS_PER_GROUP
    n_even, n_odd = SPAN // 2 + 1, SPAN // 2
    for s in range(ATT_SB):
        new = slice(s * NEW_ROWS, (s + 1) * NEW_ROWS)
        kc[s, 0:nb, :] = k0[s]
        kc[s, nb:nb + NEW_ROWS, :] = kn0[0, new, :]
        vc[s, 0:nb, :] = v0[s]
        vc[s, nb:nb + NEW_ROWS, :] = vn0[0, new, :]
        for j in range(2):
            rows = slice(8 * j, 8 * j + 8)
            qrows = slice(s * NEW_ROWS + 8 * j, s * NEW_ROWS + 8 * j + 8)
            parts = []
            q = q_ref[0, qrows, :]
            for start, n in ((8 * j, n_even), (8 * j + half, n_odd)):
                k3 = kc[s, pl.ds(start, 8 * n), :].reshape(n, 8, HEAD_DIM)
                v3 = vc[s, pl.ds(start, 8 * n), :].reshape(n, 8, HEAD_DIM)
                parts.append((_score(q[None], k3), v3))
            for g, (k_ref, v_ref, kn_ref, vn_ref) in ((1, (k1, v1, kn1, vn1)), (2, (k2, v2, kn2, vn2))):
                q = q_ref[g, qrows, :]
                parts.append((_score(q[None], k_ref[s, :, rows, :]), v_ref[s, :, rows, :]))
                parts.append((_score(q, kn_ref[0, qrows, :]), vn_ref[0, qrows, :]))
            o_ref[qrows, :] = _joint_softmax(parts)


def _attn_s_call(q_s, kv, caches):
    step_rows = ATT_SB * NEW_ROWS
    new = pl.BlockSpec((1, step_rows, HEAD_DIM), lambda b: (0, b, 0))
    in_specs = [pl.BlockSpec((N_GROUPS, step_rows, HEAD_DIM), lambda b: (0, b, 0))] + [new] * (2 * N_GROUPS)
    args = [q_s] + [kv[g][t] for g in range(N_GROUPS) for t in (1, 3)]
    for g in range(N_GROUPS):
        width = HEADS_PER_GROUP * DILATIONS[g]
        if DILATIONS[g] == 1:
            shape = (DEC_BATCH, WINDOWS[g] * HEADS_PER_GROUP, HEAD_DIM)
            spec = pl.BlockSpec((ATT_SB,) + shape[1:], lambda b: (b, 0, 0))
        else:
            shape = (DEC_BATCH, SPAN, width, HEAD_DIM)
            spec = pl.BlockSpec((ATT_SB, SPAN, NEW_ROWS, HEAD_DIM), lambda b: (b, 0, 0, 0))
        in_specs += [spec, spec]
        args += [caches[g][0].reshape(shape), caches[g][1].reshape(shape)]
    rows0 = WINDOWS[0] * HEADS_PER_GROUP + NEW_ROWS
    return pl.pallas_call(
        _attn_s_kernel,
        grid=(DEC_BATCH // ATT_SB,),
        in_specs=in_specs,
        out_specs=pl.BlockSpec((step_rows, HEAD_DIM), lambda b: (b, 0)),
        out_shape=jax.ShapeDtypeStruct((4 * ROWS_S, HEAD_DIM), F32),
        scratch_shapes=[pltpu.VMEM((ATT_SB, rows0, HEAD_DIM), F32)] * 2,
        compiler_params=_params(("arbitrary",)),
        name="attn_sample",
    )(*args)


def _mix_kernel(*refs):
    (yp_ref, ys_ref, hp_ref, hs_ref) = refs[0:4]
    op_refs, lp_refs = refs[4:7], refs[7:10]
    os_ref, wpw_ref, bpw_ref, wo_ref, wg_ref, mp_ref, ms_ref = refs[10:17]
    i = pl.program_id(0)

    def combined_prompt_head(h):
        sl = pl.ds(h, TM2, stride=HEADS_PER_GROUP)
        ls = [l[sl, :] for l in lp_refs]
        mx = jnp.maximum(jnp.maximum(ls[0], ls[1]), ls[2])
        es = [jnp.exp(l - mx) for l in ls]
        num = es[0] * op_refs[0][sl, :] + es[1] * op_refs[1][sl, :] + es[2] * op_refs[2][sl, :]
        return num / (es[0] + es[1] + es[2])

    def sample_head(h):
        return os_ref[pl.ds(h, ROWS_S, stride=HEADS_PER_GROUP), :]

    def mix(y_ref, h_ref, head, m_ref):
        y, h = y_ref[...], h_ref[...]
        o_att = jnp.concatenate([head(hh).astype(BF16) for hh in range(HEADS_PER_GROUP)], axis=1)
        for c in range(D_MODEL // NC):
            sl = slice(c * NC, (c + 1) * NC)
            gl = slice(D_MODEL + c * NC, D_MODEL + (c + 1) * NC)
            conv = jnp.dot(y, wpw_ref[:, sl], preferred_element_type=F32) + bpw_ref[:, sl]
            att = jnp.dot(o_att, wo_ref[:, sl], preferred_element_type=F32)
            gate_conv = _sigmoid(jnp.dot(h, wg_ref[:, sl], preferred_element_type=F32))
            gate_att = _sigmoid(jnp.dot(h, wg_ref[:, gl], preferred_element_type=F32))
            m_ref[:, sl] = (gate_conv * conv + gate_att * att).astype(BF16)

    @pl.when(i < NP2)
    def _():
        mix(yp_ref, hp_ref, combined_prompt_head, mp_ref)

    @pl.when(i == NP2)
    def _():
        mix(ys_ref, hs_ref, sample_head, ms_ref)


def _mix_call(y_p, y_s, h_p, h_s, o_p, l_p, o_s, w_pw, b_pw, w_o, w_gates):
    pi = lambda i: (_pidx(i, NP2), 0)
    zero = lambda i: (0, 0)
    hp = pl.BlockSpec((4 * TM2, HEAD_DIM), pi)
    hs = pl.BlockSpec((4 * ROWS_S, HEAD_DIM), zero)
    const = lambda shape: pl.BlockSpec(shape, zero, pipeline_mode=pl.Buffered(1))
    return pl.pallas_call(
        _mix_kernel,
        grid=(NP2 + 1,),
        in_specs=[
            pl.BlockSpec((TM2, D_CONV), pi), pl.BlockSpec((ROWS_S, D_CONV), zero),
            pl.BlockSpec((TM2, D_MODEL), pi), pl.BlockSpec((ROWS_S, D_MODEL), zero),
            hp, hp, hp, hp, hp, hp, hs,
            const((D_CONV, D_MODEL)), const((1, D_MODEL)), const((GROUP_COLS, D_MODEL)),
            const((D_MODEL, 2 * D_MODEL)),
        ],
        out_specs=[pl.BlockSpec((TM2, D_MODEL), pi), pl.BlockSpec((ROWS_S, D_MODEL), zero)],
        out_shape=[jax.ShapeDtypeStruct((ROWS_P, D_MODEL), BF16),
                   jax.ShapeDtypeStruct((ROWS_S, D_MODEL), BF16)],
        compiler_params=_params(("arbitrary",)),
        name="mix",
    )(y_p, y_s, h_p, h_s, *o_p, *l_p, o_s, w_pw, b_pw, w_o, w_gates)


def _outproj_kernel(mp_ref, ms_ref, xp_ref, xs_ref, w_ref, g_ref, op_ref, os_ref, hp_ref, hs_ref):
    i = pl.program_id(0)

    def project(m_ref, x_ref, o_ref, h_ref):
        x1 = x_ref[...] + jnp.dot(m_ref[...], w_ref[...], preferred_element_type=F32)
        o_ref[...] = x1
        h_ref[...] = _rms(x1, g_ref[...]).astype(BF16)

    @pl.when(i < NP2)
    def _():
        project(mp_ref, xp_ref, op_ref, hp_ref)

    @pl.when(i == NP2)
    def _():
        project(ms_ref, xs_ref, os_ref, hs_ref)


def _outproj_call(m_p, m_s, xp, xs, w_out, g_ffn):
    pi = lambda i: (_pidx(i, NP2), 0)
    zero = lambda i: (0, 0)
    tile_p, tile_s = pl.BlockSpec((TM2, D_MODEL), pi), pl.BlockSpec((ROWS_S, D_MODEL), zero)
    return pl.pallas_call(
        _outproj_kernel,
        grid=(NP2 + 1,),
        in_specs=[tile_p, tile_s, tile_p, tile_s, pl.BlockSpec((D_MODEL, D_MODEL), zero),
                  pl.BlockSpec((1, D_MODEL), zero)],
        out_specs=[tile_p, tile_s, tile_p, tile_s],
        out_shape=[jax.ShapeDtypeStruct((ROWS_P, D_MODEL), F32), jax.ShapeDtypeStruct((ROWS_S, D_MODEL), F32),
                   jax.ShapeDtypeStruct((ROWS_P, D_MODEL), BF16), jax.ShapeDtypeStruct((ROWS_S, D_MODEL), BF16)],
        compiler_params=_params(("arbitrary",)),
        name="out_proj",
    )(m_p, m_s, xp, xs, w_out, g_ffn)


N_CACHE = 2 * N_GROUPS
COPY_STEPS = 2 * DEC_BATCH


def _shift_copies(c, w, cache_refs, new_refs, out_refs, bufs, sems):
    b = c // 2
    loads, stores = [], []
    for g in range(N_GROUPS):
        n = 2 * g + w
        keep = WINDOWS[g] * HEADS_PER_GROUP - NEW_ROWS
        loads.append(pltpu.make_async_copy(cache_refs[n].at[b, pl.ds(NEW_ROWS, keep)], bufs[g].at[w],
                                           sems.at[0, w, g]))
        stores.append(pltpu.make_async_copy(bufs[g].at[w], out_refs[n].at[b, pl.ds(0, keep)],
                                            sems.at[1, w, g]))
        stores.append(pltpu.make_async_copy(new_refs[n].at[b], out_refs[n].at[b, pl.ds(keep, NEW_ROWS)],
                                            sems.at[2, w, g]))
    return loads, stores


def _ffn_kernel(*refs):
    x_hbm, xs_ref, hp_ref, hs_ref, wgu_hbm, wd_hbm, gl_ref = refs[0:7]
    cache_refs = refs[7:7 + N_CACHE]
    new_refs = refs[7 + N_CACHE:7 + 2 * N_CACHE]
    yp_ref, ys_ref = refs[7 + 2 * N_CACHE:9 + 2 * N_CACHE]
    out_refs = refs[9 + 2 * N_CACHE:9 + 3 * N_CACHE]
    bufs = refs[9 + 3 * N_CACHE:9 + 3 * N_CACHE + N_GROUPS]
    wgu_buf, wd_buf, sems, x_sem, w_sems = refs[9 + 3 * N_CACHE + N_GROUPS:]
    i = pl.program_id(0)

    def weight_copies(chunk, slot):
        return (pltpu.make_async_copy(wgu_hbm.at[:, pl.ds(pl.multiple_of(chunk * 2 * TH, 2 * TH), 2 * TH)],
                                      wgu_buf.at[slot], w_sems.at[0, slot]),
                pltpu.make_async_copy(wd_hbm.at[pl.ds(pl.multiple_of(chunk * TH, TH), TH), :],
                                      wd_buf.at[slot], w_sems.at[1, slot]))

    @pl.when(i == 0)
    def _():
        for d in weight_copies(0, 0):
            d.start()
        ys_ref[...] = xs_ref[...]

    residual = pltpu.make_async_copy(x_hbm.at[pl.ds(pl.multiple_of(i * TM, TM), TM)], yp_ref, x_sem)
    residual.start()

    def hidden_chunk(k, carry):
        step = i * KH + k
        for w in range(2):
            @pl.when(step % 2 == w)
            def _():
                for d in weight_copies(k, w):
                    d.wait()

                @pl.when(step + 1 < NP * KH)
                def _():
                    for d in weight_copies(jnp.where(k + 1 == KH, 0, k + 1), 1 - w):
                        d.start()

                copies = lambda c, slot: _shift_copies(c, slot, cache_refs, new_refs, out_refs, bufs, sems)

                @pl.when((step == 0) & (step < COPY_STEPS))
                def _():
                    for d in copies(step, w)[0]:
                        d.start()

                @pl.when(step < COPY_STEPS)
                def _():
                    for d in copies(step, w)[0]:
                        d.wait()

                @pl.when((step >= 1) & (step <= COPY_STEPS))
                def _():
                    for d in copies(step - 1, 1 - w)[1]:
                        d.wait()

                @pl.when(step < COPY_STEPS)
                def _():
                    for d in copies(step, w)[1]:
                        d.start()

                @pl.when(step + 1 < COPY_STEPS)
                def _():
                    for d in copies(step + 1, 1 - w)[0]:
                        d.start()

        slot = step % 2

        def swiglu(h):
            a = jnp.dot(h, wgu_buf[slot, :, 0:TH], preferred_element_type=F32)
            u = jnp.dot(h, wgu_buf[slot, :, TH:2 * TH], preferred_element_type=F32)
            return (a * _sigmoid(a) * u).astype(BF16)

        act = swiglu(hp_ref[...])

        @pl.when(k == 0)
        def _():
            residual.wait()

        yp_ref[...] += jnp.dot(act, wd_buf[slot], preferred_element_type=F32)

        @pl.when(i == 0)
        def _():
            ys_ref[...] += jnp.dot(swiglu(hs_ref[...]), wd_buf[slot], preferred_element_type=F32)

        return carry

    lax.fori_loop(0, KH, hidden_chunk, 0)

    yp_ref[...] = _rms(yp_ref[...], gl_ref[...])

    @pl.when(i == 0)
    def _():
        ys_ref[...] = _rms(ys_ref[...], gl_ref[...])


def _ffn_call(x1_p, x1_s, h_p, h_s, w_gate_up, w_down, g_final, caches, new_rows):
    assert NP * KH >= COPY_STEPS
    pi = lambda i: (i, 0)
    zero = lambda i: (0, 0)
    hbm = pl.BlockSpec(memory_space=pl.ANY)
    return pl.pallas_call(
        _ffn_kernel,
        grid=(NP,),
        in_specs=[
            hbm,
            pl.BlockSpec((ROWS_S, D_MODEL), zero, pipeline_mode=pl.Buffered(1)),
            pl.BlockSpec((TM, D_MODEL), pi),
            pl.BlockSpec((ROWS_S, D_MODEL), zero, pipeline_mode=pl.Buffered(1)),
            hbm, hbm,
            pl.BlockSpec((1, D_MODEL), zero),
        ] + [hbm] * (2 * N_CACHE),
        out_specs=[pl.BlockSpec((TM, D_MODEL), pi), pl.BlockSpec((ROWS_S, D_MODEL), zero)] + [hbm] * N_CACHE,
        out_shape=[jax.ShapeDtypeStruct((ROWS_P, D_MODEL), F32),
                   jax.ShapeDtypeStruct((ROWS_S, D_MODEL), F32)]
        + [jax.ShapeDtypeStruct(c.shape, F32) for c in caches],
        scratch_shapes=[pltpu.VMEM((2, WINDOWS[g] * HEADS_PER_GROUP - NEW_ROWS, HEAD_DIM), F32)
                        for g in range(N_GROUPS)]
        + [pltpu.VMEM((2, D_MODEL, 2 * TH), BF16), pltpu.VMEM((2, TH, D_MODEL), BF16)]
        + [pltpu.SemaphoreType.DMA((3, 2, N_GROUPS)), pltpu.SemaphoreType.DMA(()),
           pltpu.SemaphoreType.DMA((2, 2))],
        compiler_params=_params(("arbitrary",), FFN_VMEM_LIMIT),
        name="ffn",
    )(x1_p, x1_s, h_p, h_s, w_gate_up, w_down, g_final, *caches, *new_rows)


def _rope_tables(pos):
    half = ROT_DIM // 2
    inv = np.power(ROPE_THETA, -np.arange(0, ROT_DIM, 2, dtype=np.float64) / ROT_DIM)
    ang = np.asarray(pos, np.float64)[:, None] * inv[None, :]
    cos, sin = np.cos(ang), np.sin(ang)
    n = ang.shape[0]
    ones = np.ones((n, HEAD_DIM - ROT_DIM))
    zeros = np.zeros((n, HEAD_DIM - ROT_DIM))
    zh = np.zeros((n, half))
    c = np.concatenate([cos, cos, ones], axis=1)
    s_lo = np.concatenate([-sin, zh, zeros], axis=1)
    s_hi = np.concatenate([zh, sin, zeros], axis=1)
    return jnp.asarray(np.stack([c, s_lo, s_hi]), F32)


def kernel(x_prompt, x_sample, state_conv, cache_k_w128, cache_v_w128, cache_k_w512, cache_v_w512,
           cache_k_w2048, cache_v_w2048, g_mix, w_in, b_glu, w_dw, b_dw, ln_g, ln_b, w_pw, b_pw,
           w_o_att, w_out, g_ffn, w_gate, w_up, w_down, g_final):
    xp = x_prompt.reshape(ROWS_P, D_MODEL)
    xs = x_sample.reshape(ROWS_S, D_MODEL)
    w_in2 = w_in.reshape(D_MODEL, IN_COLS)

    hp, hs = _rms_call(xp, xs, g_mix)
    u_p, u_s, w_gates16 = _glu_call(hp, hs, w_in2, b_glu)
    tab_p = _rope_tables(np.arange(SEQ))
    tab_s = _rope_tables(PAST_LEN + np.arange(ROWS_S) % DEC_SEQ)
    q_p, q_s = _q_call(hp, hs, w_in2, tab_p, tab_s)
    mixer_w = (w_out.reshape(D_MODEL, D_MODEL), w_pw.reshape(D_CONV, D_MODEL), w_o_att.reshape(GROUP_COLS, D_MODEL))
    kv = [_kv_call(hp, hs, w_in2, tab_p, tab_s, g, mixer_w[g]) for g in range(N_GROUPS)]
    w_out16, w_pw16, w_o16 = kv[0][4], kv[1][4], kv[2][4]

    w_dw2 = w_dw.reshape(CONV_WIDTH, D_CONV)
    y_p, w_gate_up16, w_down16 = _conv_p_call(
        u_p, w_dw2, b_dw, ln_g, ln_b, w_gate.reshape(D_MODEL, FFN_HIDDEN), w_up.reshape(D_MODEL, FFN_HIDDEN),
        w_down.reshape(FFN_HIDDEN, D_MODEL))
    y_s, conv_s = _conv_s_call(state_conv, u_s.reshape(DEC_BATCH, DEC_SEQ, D_CONV), w_dw2, b_dw, ln_g, ln_b)
    y_s = y_s.reshape(ROWS_S, D_CONV).astype(BF16)
    conv_p = u_p.reshape(BATCH, SEQ, D_CONV)[:, SEQ - (CONV_WIDTH - 1):][None]

    caches = ((cache_k_w128, cache_v_w128), (cache_k_w512, cache_v_w512), (cache_k_w2048, cache_v_w2048))
    o_p, l_p = [], []
    for g in range(N_GROUPS):
        o, l = _attn_p_call(q_p, kv[g][0], kv[g][2], g)
        o_p.append(o)
        l_p.append(l)
    o_s = _attn_s_call(q_s, kv, caches)

    m_p, m_s = _mix_call(y_p, y_s, hp, hs, o_p, l_p, o_s, w_pw16, b_pw, w_o16, w_gates16)
    x1_p, x1_s, hf_p, hf_s = _outproj_call(m_p, m_s, xp, xs, w_out16, g_ffn)

    flat = lambda g, a: a.reshape(DEC_BATCH, WINDOWS[g] * HEADS_PER_GROUP, HEAD_DIM)
    cache_flat = [flat(g, caches[g][t]) for g in range(N_GROUPS) for t in range(2)]
    new_rows = [kv[g][t].reshape(DEC_BATCH, NEW_ROWS, HEAD_DIM) for g in range(N_GROUPS) for t in (1, 3)]
    y_p2, y_s2, *shifted = _ffn_call(x1_p, x1_s, hf_p, hf_s, w_gate_up16, w_down16,
                                     g_final.reshape(1, D_MODEL), cache_flat, new_rows)

    cache_out = []
    for g in range(N_GROUPS):
        keep = min(WINDOWS[g], SEQ)
        for t in range(2):
            prompt_rows = kv[g][2 * t] if keep == SEQ else kv[g][5 + t]
            cache_out.append(prompt_rows.reshape(1, BATCH, keep, HEADS_PER_GROUP, HEAD_DIM))
            cache_out.append(shifted[2 * g + t].reshape(1, DEC_BATCH, WINDOWS[g], HEADS_PER_GROUP, HEAD_DIM))
    return (y_p2.reshape(BATCH, SEQ, D_MODEL), y_s2.reshape(DEC_BATCH, DEC_SEQ, D_MODEL),
            conv_p, conv_s, *cache_out)
```

```python
import jax
import jax.numpy as jnp
import numpy as np
from jax import lax
from jax.experimental import pallas as pl
from jax.experimental.pallas import tpu as pltpu

F32 = jnp.float32
BF16 = jnp.bfloat16

D_MODEL = 2048
BATCH = 4
SEQ = 2048
DEC_BATCH = 32
DEC_SEQ = 4
PAST_LEN = 8192
HEAD_DIM = 128
HEADS_PER_GROUP = 4
DILATIONS = (1, 4, 16)
WINDOWS = (128, 512, 2048)
N_GROUPS = 3
GROUP_COLS = HEADS_PER_GROUP * HEAD_DIM
ATT_WIDTH = N_GROUPS * GROUP_COLS
SPAN = 128
ATT_SCALE = HEAD_DIM ** -0.5
ROT_DIM = HEAD_DIM // 4
ROPE_THETA = 500000.0
D_CONV = D_MODEL // 2
CONV_WIDTH = 31
FFN_HIDDEN = 5632
NORM_EPS = 1e-6
LN_EPS = 1e-5
IN_COLS = 2 * D_CONV + 3 * ATT_WIDTH + 2 * D_MODEL

ROWS_P = BATCH * SEQ
ROWS_S = DEC_BATCH * DEC_SEQ
NEG = -1e30

VMEM_LIMIT = 56 * 1024 * 1024
FFN_VMEM_LIMIT = 62 * 1024 * 1024

TM = 1024
NP = ROWS_P // TM
TN = 512
NC = 256
TM2 = 512
NP2 = ROWS_P // TM2
TH = 512
KH = FFN_HIDDEN // TH
TC = 512
HALO = 32
CS = 4


def _sigmoid(x):
    return 1.0 / (1.0 + jnp.exp(-x))


def _rms(x, g):
    return x * lax.rsqrt(jnp.mean(x * x, axis=-1, keepdims=True) + NORM_EPS) * g


def _params(sem, vmem_limit=VMEM_LIMIT):
    return pltpu.CompilerParams(dimension_semantics=sem, vmem_limit_bytes=vmem_limit)


def _pidx(i, n):
    return jnp.minimum(i, n - 1)


def _rms_kernel(xp_ref, xs_ref, g_ref, hp_ref, hs_ref):
    i = pl.program_id(0)

    @pl.when(i < NP)
    def _():
        hp_ref[...] = _rms(xp_ref[...], g_ref[...]).astype(BF16)

    @pl.when(i == NP)
    def _():
        hs_ref[...] = _rms(xs_ref[...], g_ref[...]).astype(BF16)


def _rms_call(xp, xs, g):
    return pl.pallas_call(
        _rms_kernel,
        grid=(NP + 1,),
        in_specs=[
            pl.BlockSpec((TM, D_MODEL), lambda i: (_pidx(i, NP), 0)),
            pl.BlockSpec((ROWS_S, D_MODEL), lambda i: (0, 0)),
            pl.BlockSpec((1, D_MODEL), lambda i: (0, 0)),
        ],
        out_specs=[
            pl.BlockSpec((TM, D_MODEL), lambda i: (_pidx(i, NP), 0)),
            pl.BlockSpec((ROWS_S, D_MODEL), lambda i: (0, 0)),
        ],
        out_shape=[
            jax.ShapeDtypeStruct((ROWS_P, D_MODEL), BF16),
            jax.ShapeDtypeStruct((ROWS_S, D_MODEL), BF16),
        ],
        compiler_params=_params(("arbitrary",)),
        name="rms_in",
    )(xp, xs, g)


N_GATE_BLOCKS = 2 * D_MODEL // TN


def _glu_kernel(hp_ref, hs_ref, wa_ref, wb_ref, ba_ref, bb_ref, *refs):
    gate32_refs = refs[:N_GATE_BLOCKS]
    up_ref, us_ref, gate16_ref, wa_s, wb_s = refs[N_GATE_BLOCKS:]
    i = pl.program_id(1)

    @pl.when(i == 0)
    def _():
        wa_s[...] = wa_ref[...].astype(BF16)
        wb_s[...] = wb_ref[...].astype(BF16)

    @pl.when(i < NP)
    def _():
        for c, g_ref in enumerate(gate32_refs):
            gate16_ref[:, c * TN:(c + 1) * TN] = g_ref[...].astype(BF16)

    def glu(h, u_ref):
        for c in range(TN // NC):
            sl = slice(c * NC, (c + 1) * NC)
            za = jnp.dot(h, wa_s[:, sl], preferred_element_type=F32) + ba_ref[:, sl]
            zb = jnp.dot(h, wb_s[:, sl], preferred_element_type=F32) + bb_ref[:, sl]
            u_ref[:, sl] = za * _sigmoid(zb)

    @pl.when(i < NP)
    def _():
        glu(hp_ref[...], up_ref)

    @pl.when(i == NP)
    def _():
        glu(hs_ref[...], us_ref)


def _glu_call(hp, hs, w_in, b_glu):
    nj = D_CONV // TN
    slab = D_MODEL // (nj * NP)
    gate0 = (2 * D_CONV + 3 * ATT_WIDTH) // TN
    slab_row = lambda j, i: j * NP + _pidx(i, NP)
    return pl.pallas_call(
        _glu_kernel,
        grid=(nj, NP + 1),
        in_specs=[
            pl.BlockSpec((TM, D_MODEL), lambda j, i: (_pidx(i, NP), 0)),
            pl.BlockSpec((ROWS_S, D_MODEL), lambda j, i: (0, 0)),
            pl.BlockSpec((D_MODEL, TN), lambda j, i: (0, j)),
            pl.BlockSpec((D_MODEL, TN), lambda j, i: (0, j + nj)),
            pl.BlockSpec((1, TN), lambda j, i: (0, j)),
            pl.BlockSpec((1, TN), lambda j, i: (0, j + nj)),
        ] + [pl.BlockSpec((slab, TN), lambda j, i, c=c: (slab_row(j, i), gate0 + c)) for c in range(N_GATE_BLOCKS)],
        out_specs=[
            pl.BlockSpec((TM, TN), lambda j, i: (_pidx(i, NP), j)),
            pl.BlockSpec((ROWS_S, TN), lambda j, i: (0, j)),
            pl.BlockSpec((slab, 2 * D_MODEL), lambda j, i: (slab_row(j, i), 0)),
        ],
        out_shape=[
            jax.ShapeDtypeStruct((ROWS_P, D_CONV), F32),
            jax.ShapeDtypeStruct((ROWS_S, D_CONV), F32),
            jax.ShapeDtypeStruct((D_MODEL, 2 * D_MODEL), BF16),
        ],
        scratch_shapes=[pltpu.VMEM((D_MODEL, TN), BF16), pltpu.VMEM((D_MODEL, TN), BF16)],
        compiler_params=_params(("arbitrary", "arbitrary")),
        name="in_glu",
    )(hp, hs, w_in, w_in, b_glu, b_glu, *([w_in] * N_GATE_BLOCKS))


def _project_heads(h, w_s, t_ref, o_ref, rows):
    if t_ref is not None:
        cos, sin_lo, sin_hi = t_ref[0], t_ref[1], t_ref[2]
    for c in range(TN // NC):
        z = jnp.dot(h, w_s[:, c * NC:(c + 1) * NC], preferred_element_type=F32)
        for hc in range(NC // HEAD_DIM):
            t = z[:, hc * HEAD_DIM:(hc + 1) * HEAD_DIM]
            if t_ref is not None:
                t = (t * cos + pltpu.roll(t, HEAD_DIM - ROT_DIM // 2, 1) * sin_lo
                     + pltpu.roll(t, ROT_DIM // 2, 1) * sin_hi)
            head = c * (NC // HEAD_DIM) + hc
            o_ref[pl.ds(head, rows, stride=HEADS_PER_GROUP), :] = t


def _q_kernel(hp_ref, hs_ref, w0_ref, w1_ref, w2_ref, tp_ref, ts_ref, op_ref, os_ref, w_s):
    i = pl.program_id(0)

    @pl.when(i == 0)
    def _():
        for g, w_ref in enumerate((w0_ref, w1_ref, w2_ref)):
            w_s[g] = w_ref[...].astype(BF16)

    @pl.when(i < NP)
    def _():
        h = hp_ref[...]
        for g in range(N_GROUPS):
            _project_heads(h, w_s.at[g], tp_ref, op_ref.at[g], TM)

    @pl.when(i == NP)
    def _():
        h = hs_ref[...]
        for g in range(N_GROUPS):
            _project_heads(h, w_s.at[g], ts_ref, os_ref.at[g], ROWS_S)


def _make_kv_kernel(tail_rows):
    tiles_per_seq = SEQ // TM

    def kern(hp_ref, hs_ref, wk_ref, wv_ref, tp_ref, ts_ref, slab32_ref, *refs):
        kp_ref, ks_ref, vp_ref, vs_ref, slab16_ref = refs[0:5]
        kt_ref, vt_ref = refs[5:7] if tail_rows else (None, None)
        wk_s, wv_s = refs[-2:]
        i = pl.program_id(0)

        @pl.when(i == 0)
        def _():
            wk_s[...] = wk_ref[...].astype(BF16)
            wv_s[...] = wv_ref[...].astype(BF16)

        @pl.when(i < NP)
        def _():
            slab16_ref[...] = slab32_ref[...].astype(BF16)
            h = hp_ref[...]
            _project_heads(h, wk_s, tp_ref, kp_ref.at[0], TM)
            _project_heads(h, wv_s, None, vp_ref.at[0], TM)

        if tail_rows:
            @pl.when((i < NP) & (i % tiles_per_seq == tiles_per_seq - 1))
            def _():
                kt_ref[0] = kp_ref[0, 4 * TM - tail_rows:4 * TM, :]
                vt_ref[0] = vp_ref[0, 4 * TM - tail_rows:4 * TM, :]

        @pl.when(i == NP)
        def _():
            h = hs_ref[...]
            _project_heads(h, wk_s, ts_ref, ks_ref.at[0], ROWS_S)
            _project_heads(h, wv_s, None, vs_ref.at[0], ROWS_S)

    return kern


Q_PLANE0 = 2 * D_CONV // TN


def _q_call(hp, hs, w_in, tab_p, tab_s):
    wspec = lambda g: pl.BlockSpec((D_MODEL, TN), lambda i: (0, Q_PLANE0 + g), pipeline_mode=pl.Buffered(1))
    return pl.pallas_call(
        _q_kernel,
        grid=(NP + 1,),
        in_specs=[
            pl.BlockSpec((TM, D_MODEL), lambda i: (_pidx(i, NP), 0)),
            pl.BlockSpec((ROWS_S, D_MODEL), lambda i: (0, 0)),
            wspec(0), wspec(1), wspec(2),
            pl.BlockSpec((3, TM, HEAD_DIM), lambda i: (0, _pidx(i, NP) % (SEQ // TM), 0)),
            pl.BlockSpec((3, ROWS_S, HEAD_DIM), lambda i: (0, 0, 0)),
        ],
        out_specs=[
            pl.BlockSpec((N_GROUPS, 4 * TM, HEAD_DIM), lambda i: (0, _pidx(i, NP), 0)),
            pl.BlockSpec((N_GROUPS, 4 * ROWS_S, HEAD_DIM), lambda i: (0, 0, 0)),
        ],
        out_shape=[
            jax.ShapeDtypeStruct((N_GROUPS, 4 * ROWS_P, HEAD_DIM), F32),
            jax.ShapeDtypeStruct((N_GROUPS, 4 * ROWS_S, HEAD_DIM), F32),
        ],
        scratch_shapes=[pltpu.VMEM((N_GROUPS, D_MODEL, TN), BF16)],
        compiler_params=_params(("arbitrary",)),
        name="in_q",
    )(hp, hs, w_in, w_in, w_in, tab_p, tab_s)


def _kv_call(hp, hs, w_in, tab_p, tab_s, g, w_extra):
    wspec = lambda plane: pl.BlockSpec((D_MODEL, TN), lambda i: (0, Q_PLANE0 + plane), pipeline_mode=pl.Buffered(1))
    outp = pl.BlockSpec((1, 4 * TM, HEAD_DIM), lambda i: (0, _pidx(i, NP), 0))
    outs = pl.BlockSpec((1, 4 * ROWS_S, HEAD_DIM), lambda i: (0, 0, 0))
    shp = jax.ShapeDtypeStruct((1, 4 * ROWS_P, HEAD_DIM), F32)
    shs = jax.ShapeDtypeStruct((1, 4 * ROWS_S, HEAD_DIM), F32)
    slab = pl.BlockSpec((w_extra.shape[0] // NP, w_extra.shape[1]), lambda i: (_pidx(i, NP), 0))
    out_specs = [outp, outs, outp, outs, slab]
    out_shape = [shp, shs, shp, shs, jax.ShapeDtypeStruct(w_extra.shape, BF16)]
    tail_rows = HEADS_PER_GROUP * WINDOWS[g] if WINDOWS[g] < SEQ else 0
    if tail_rows:
        assert tail_rows <= 4 * TM
        tail = pl.BlockSpec((1, tail_rows, HEAD_DIM), lambda i: (_pidx(i, NP) // (SEQ // TM), 0, 0))
        out_specs += [tail, tail]
        out_shape += [jax.ShapeDtypeStruct((BATCH, tail_rows, HEAD_DIM), F32)] * 2
    return pl.pallas_call(
        _make_kv_kernel(tail_rows),
        grid=(NP + 1,),
        in_specs=[
            pl.BlockSpec((TM, D_MODEL), lambda i: (_pidx(i, NP), 0)),
            pl.BlockSpec((ROWS_S, D_MODEL), lambda i: (0, 0)),
            wspec(N_GROUPS + g), wspec(2 * N_GROUPS + g),
            pl.BlockSpec((3, TM, HEAD_DIM), lambda i: (0, _pidx(i, NP) % (SEQ // TM), 0)),
            pl.BlockSpec((3, ROWS_S, HEAD_DIM), lambda i: (0, 0, 0)),
            slab,
        ],
        out_specs=out_specs,
        out_shape=out_shape,
        scratch_shapes=[pltpu.VMEM((D_MODEL, TN), BF16), pltpu.VMEM((D_MODEL, TN), BF16)],
        compiler_params=_params(("arbitrary",)),
        name="in_kv%d" % g,
    )(hp, hs, w_in, w_in, tab_p, tab_s, w_extra)


def _ln_silu(y, g, b):
    mu = jnp.mean(y, axis=-1, keepdims=True)
    yc = y - mu
    var = jnp.mean(yc * yc, axis=-1, keepdims=True)
    z = yc * lax.rsqrt(var + LN_EPS) * g + b
    return z * _sigmoid(z)


def _conv_p_kernel(cur_ref, prev_ref, w_ref, bdw_ref, g_ref, b_ref, wg32_ref, wu32_ref, wd32_ref,
                   y_ref, wgu16_ref, wd16_ref, ext, ypre):
    c = pl.program_id(1)
    n_lc = D_CONV // 128
    for k in range(KH):
        src = slice(k * TH, (k + 1) * TH)
        wgu16_ref[:, 2 * k * TH:(2 * k + 1) * TH] = wg32_ref[:, src].astype(BF16)
        wgu16_ref[:, (2 * k + 1) * TH:(2 * k + 2) * TH] = wu32_ref[:, src].astype(BF16)
    wd16_ref[...] = wd32_ref[...].astype(BF16)
    for lc in range(n_lc):
        sl = slice(lc * 128, (lc + 1) * 128)
        ext[lc, pl.ds(HALO, TC), :] = cur_ref[:, sl]

    @pl.when(c > 0)
    def _():
        for lc in range(n_lc):
            ext[lc, 0:HALO, :] = prev_ref[TC - HALO:TC, lc * 128:(lc + 1) * 128]

    @pl.when(c == 0)
    def _():
        for lc in range(n_lc):
            ext[lc, 0:HALO, :] = jnp.zeros((HALO, 128), F32)

    base = HALO - (CONV_WIDTH - 1)
    for lc in range(n_lc):
        sl = slice(lc * 128, (lc + 1) * 128)
        for t0 in range(CS):
            acc = jnp.zeros((TC // CS, 128), F32)
            for j in range(CONV_WIDTH):
                acc = acc + ext[lc, pl.ds(base + t0 + j, TC // CS, stride=CS), :] * w_ref[j:j + 1, sl]
            ypre[lc, pl.ds(t0, TC // CS, stride=CS), :] = acc + bdw_ref[:, sl]

    total = jnp.zeros((TC, 1), F32)
    for lc in range(n_lc):
        total = total + jnp.sum(ypre[lc], axis=-1, keepdims=True)
    mu = total * (1.0 / D_CONV)
    sq = jnp.zeros((TC, 1), F32)
    for lc in range(n_lc):
        yc = ypre[lc] - mu
        sq = sq + jnp.sum(yc * yc, axis=-1, keepdims=True)
    inv = lax.rsqrt(sq * (1.0 / D_CONV) + LN_EPS)
    for lc in range(n_lc):
        sl = slice(lc * 128, (lc + 1) * 128)
        z = (ypre[lc] - mu) * inv * g_ref[:, sl] + b_ref[:, sl]
        y_ref[:, sl] = (z * _sigmoid(z)).astype(BF16)


def _conv_p_call(u_p, w_dw, b_dw, ln_g, ln_b, w_gate, w_up, w_down):
    nc = SEQ // TC
    steps = BATCH * nc
    vec = pl.BlockSpec((1, D_CONV), lambda b, c: (0, 0))
    slab = lambda rows, cols: pl.BlockSpec((rows // steps, cols), lambda b, c: (b * nc + c, 0))
    return pl.pallas_call(
        _conv_p_kernel,
        grid=(BATCH, nc),
        in_specs=[
            pl.BlockSpec((TC, D_CONV), lambda b, c: (b * nc + c, 0)),
            pl.BlockSpec((TC, D_CONV), lambda b, c: (b * nc + jnp.maximum(c - 1, 0), 0)),
            pl.BlockSpec((CONV_WIDTH, D_CONV), lambda b, c: (0, 0)),
            vec, vec, vec,
            slab(D_MODEL, FFN_HIDDEN), slab(D_MODEL, FFN_HIDDEN), slab(FFN_HIDDEN, D_MODEL),
        ],
        out_specs=[pl.BlockSpec((TC, D_CONV), lambda b, c: (b * nc + c, 0)),
                   slab(D_MODEL, 2 * FFN_HIDDEN), slab(FFN_HIDDEN, D_MODEL)],
        out_shape=[jax.ShapeDtypeStruct((ROWS_P, D_CONV), BF16),
                   jax.ShapeDtypeStruct((D_MODEL, 2 * FFN_HIDDEN), BF16),
                   jax.ShapeDtypeStruct((FFN_HIDDEN, D_MODEL), BF16)],
        scratch_shapes=[pltpu.VMEM((D_CONV // 128, HALO + TC, 128), F32),
                        pltpu.VMEM((D_CONV // 128, TC, 128), F32)],
        compiler_params=_params(("arbitrary", "arbitrary")),
        name="conv_prompt",
    )(u_p, u_p, w_dw, b_dw, ln_g, ln_b, w_gate, w_up, w_down)


CONV_SB = 8


def _conv_s_kernel(state_ref, u_ref, w_ref, bdw_ref, g_ref, b_ref, y_ref, new_state_ref, ext):
    past = CONV_WIDTH - 1
    ext[:, 0:past, :] = state_ref[0]
    ext[:, past:past + DEC_SEQ, :] = u_ref[...]
    new_state_ref[0] = ext[:, DEC_SEQ:past + DEC_SEQ, :]
    acc = jnp.zeros((CONV_SB, DEC_SEQ, D_CONV), F32)
    for j in range(CONV_WIDTH):
        acc = acc + ext[:, pl.ds(j, DEC_SEQ), :] * w_ref[j:j + 1, :]
    y_ref[...] = _ln_silu(acc + bdw_ref[...], g_ref[...], b_ref[...])


def _conv_s_call(state_conv, u_s, w_dw, b_dw, ln_g, ln_b):
    vec = pl.BlockSpec((1, D_CONV), lambda b: (0, 0))
    past = CONV_WIDTH - 1
    state = pl.BlockSpec((1, CONV_SB, past, D_CONV), lambda b: (0, b, 0, 0))
    rows = pl.BlockSpec((CONV_SB, DEC_SEQ, D_CONV), lambda b: (b, 0, 0))
    return pl.pallas_call(
        _conv_s_kernel,
        grid=(DEC_BATCH // CONV_SB,),
        in_specs=[state, rows, pl.BlockSpec((CONV_WIDTH, D_CONV), lambda b: (0, 0)), vec, vec, vec],
        out_specs=[rows, state],
        out_shape=[jax.ShapeDtypeStruct((DEC_BATCH, DEC_SEQ, D_CONV), F32),
                   jax.ShapeDtypeStruct(state_conv.shape, F32)],
        scratch_shapes=[pltpu.VMEM((CONV_SB, past + DEC_SEQ, D_CONV), F32)],
        compiler_params=_params(("arbitrary",)),
        name="conv_sample",
    )(state_conv, u_s, w_dw, b_dw, ln_g, ln_b)


def _softmax_block(q, k, v, mask):
    s = lax.dot_general(q.astype(BF16), k.astype(BF16), (((2,), (2,)), ((0,), (0,))),
                        preferred_element_type=F32) * ATT_SCALE
    s = jnp.where(mask[None], s, NEG)
    m = jnp.max(s, axis=-1, keepdims=True)
    p = jnp.exp(s - m)
    den = jnp.sum(p, axis=-1, keepdims=True)
    o = lax.dot_general(p.astype(BF16), v.astype(BF16), (((2,), (1,)), ((0,), (0,))),
                        preferred_element_type=F32) / den
    lse = m + jnp.log(den)
    return o, jnp.broadcast_to(lse, o.shape)


def _make_attn_p_kernel(dil):
    stride = HEADS_PER_GROUP * dil
    nblk = SEQ // dil // SPAN
    blk_rows = SPAN * stride
    pair = min(dil, 2)
    group_rows = pair * HEADS_PER_GROUP
    n_r2 = dil // pair

    def kern(q_ref, k_ref, v_ref, o_ref, l_ref):
        qi = lax.broadcasted_iota(jnp.int32, (SPAN, SPAN), 0)
        ki = lax.broadcasted_iota(jnp.int32, (SPAN, SPAN), 1)
        mask_first = ki <= qi
        qi2 = lax.broadcasted_iota(jnp.int32, (SPAN, 2 * SPAN), 0)
        ki2 = lax.broadcasted_iota(jnp.int32, (SPAN, 2 * SPAN), 1)
        mask_band = (ki2 >= qi2) & (ki2 <= qi2 + SPAN)

        def block(base, key_base, n_keys, mask):
            qs = [pl.ds(base + off, SPAN, stride=stride) for off in range(group_rows)]
            ks = [pl.ds(key_base + off, n_keys, stride=stride) for off in range(group_rows)]
            o, lse = _softmax_block(jnp.stack([q_ref[0, s, :] for s in qs]),
                                    jnp.stack([k_ref[0, s, :] for s in ks]),
                                    jnp.stack([v_ref[0, s, :] for s in ks]), mask)
            for u, s in enumerate(qs):
                o_ref[s, :] = o[u]
                l_ref[s, :] = lse[u]

        def first(r2, carry):
            base = pl.multiple_of(r2 * group_rows, max(group_rows, 8))
            block(base, base, SPAN, mask_first)
            return carry

        def band(t, carry):
            jb = 1 + t // n_r2
            base = pl.multiple_of(jb * blk_rows + (t % n_r2) * group_rows, max(group_rows, 8))
            block(base, base - blk_rows, 2 * SPAN, mask_band)
            return carry

        lax.fori_loop(0, n_r2, first, 0)
        if nblk > 1:
            lax.fori_loop(0, n_r2 * (nblk - 1), band, 0)

    return kern


def _attn_p_wide_kernel(q_ref, k_ref, v_ref, o_ref, l_ref):
    qi = lax.broadcasted_iota(jnp.int32, (SPAN, SPAN), 0)
    ki = lax.broadcasted_iota(jnp.int32, (SPAN, SPAN), 1)
    mask_first = ki <= qi

    def tile(c, carry):
        q8 = jnp.swapaxes(q_ref[0, 0, :, c], 0, 1)
        k8 = jnp.swapaxes(k_ref[0, 0, :, c], 0, 1)
        v8 = jnp.swapaxes(v_ref[0, 0, :, c], 0, 1)
        o, lse = _softmax_block(q8, k8, v8, mask_first)
        o_ref[0, :, c] = jnp.swapaxes(o, 0, 1)
        l_ref[0, :, c] = jnp.swapaxes(lse, 0, 1)
        return carry

    lax.fori_loop(0, q_ref.shape[3], tile, 0)


def _attn_p_call(q_p, k_p, v_p, g):
    dil = DILATIONS[g]
    rows = 4 * SEQ
    if SEQ // dil == SPAN:
        tiles = HEADS_PER_GROUP * dil // 8
        view = lambda a: a.reshape(a.shape[0], BATCH, SPAN, tiles, 8, HEAD_DIM)
        blk = lambda plane: pl.BlockSpec((1, 1, SPAN, tiles, 8, HEAD_DIM), lambda b: (plane, b, 0, 0, 0, 0))
        out = pl.BlockSpec((1, SPAN, tiles, 8, HEAD_DIM), lambda b: (b, 0, 0, 0, 0))
        o, l = pl.pallas_call(
            _attn_p_wide_kernel,
            grid=(BATCH,),
            in_specs=[blk(g), blk(0), blk(0)],
            out_specs=[out, out],
            out_shape=[jax.ShapeDtypeStruct((BATCH, SPAN, tiles, 8, HEAD_DIM), F32)] * 2,
            compiler_params=_params(("arbitrary",)),
            name="attn_prompt_g%d" % g,
        )(view(q_p), view(k_p), view(v_p))
        return o.reshape(4 * ROWS_P, HEAD_DIM), l.reshape(4 * ROWS_P, HEAD_DIM)
    blk = lambda plane: pl.BlockSpec((1, rows, HEAD_DIM), lambda b: (plane, b, 0))
    out = pl.BlockSpec((rows, HEAD_DIM), lambda b: (b, 0))
    return pl.pallas_call(
        _make_attn_p_kernel(dil),
        grid=(BATCH,),
        in_specs=[blk(g), blk(0), blk(0)],
        out_specs=[out, out],
        out_shape=[jax.ShapeDtypeStruct((4 * ROWS_P, HEAD_DIM), F32)] * 2,
        compiler_params=_params(("arbitrary",)),
        name="attn_prompt_g%d" % g,
    )(q_p, k_p, v_p)


NEW_ROWS = DEC_SEQ * HEADS_PER_GROUP


def _joint_softmax(parts):
    mx = None
    for s, _ in parts:
        m = jnp.max(s, axis=0) if s.ndim == 3 else s
        mx = m if mx is None else jnp.maximum(mx, m)
    den = jnp.zeros((8, 1), F32)
    acc = jnp.zeros((8, HEAD_DIM), F32)
    for s, v in parts:
        p = jnp.exp(s - mx)
        if s.ndim == 3:
            den = den + jnp.sum(p, axis=0)
            acc = acc + jnp.sum(p * v, axis=0)
        else:
            den = den + p
            acc = acc + p * v
    return acc / den


def _score(q, k):
    return jnp.sum(q * k, axis=-1, keepdims=True) * ATT_SCALE


ATT_SB = 4


def _attn_s_kernel(q_ref, kn0, vn0, kn1, vn1, kn2, vn2, k0, v0, k1, v1, k2, v2, o_ref, kc, vc):
    nb = WINDOWS[0] * HEADS_PER_GROUP
    half = HEADS_PER_GROUP
    n_even, n_odd = SPAN // 2 + 1, SPAN // 2
    for s in range(ATT_SB):
        new = slice(s * NEW_ROWS, (s + 1) * NEW_ROWS)
        kc[s, 0:nb, :] = k0[s]
        kc[s, nb:nb + NEW_ROWS, :] = kn0[0, new, :]
        vc[s, 0:nb, :] = v0[s]
        vc[s, nb:nb + NEW_ROWS, :] = vn0[0, new, :]
        for j in range(2):
            rows = slice(8 * j, 8 * j + 8)
            qrows = slice(s * NEW_ROWS + 8 * j, s * NEW_ROWS + 8 * j + 8)
            parts = []
            q = q_ref[0, qrows, :]
            for start, n in ((8 * j, n_even), (8 * j + half, n_odd)):
                k3 = kc[s, pl.ds(start, 8 * n), :].reshape(n, 8, HEAD_DIM)
                v3 = vc[s, pl.ds(start, 8 * n), :].reshape(n, 8, HEAD_DIM)
                parts.append((_score(q[None], k3), v3))
            for g, (k_ref, v_ref, kn_ref, vn_ref) in ((1, (k1, v1, kn1, vn1)), (2, (k2, v2, kn2, vn2))):
                q = q_ref[g, qrows, :]
                parts.append((_score(q[None], k_ref[s, :, rows, :]), v_ref[s, :, rows, :]))
                parts.append((_score(q, kn_ref[0, qrows, :]), vn_ref[0, qrows, :]))
            o_ref[qrows, :] = _joint_softmax(parts)


def _attn_s_call(q_s, kv, caches):
    step_rows = ATT_SB * NEW_ROWS
    new = pl.BlockSpec((1, step_rows, HEAD_DIM), lambda b: (0, b, 0))
    in_specs = [pl.BlockSpec((N_GROUPS, step_rows, HEAD_DIM), lambda b: (0, b, 0))] + [new] * (2 * N_GROUPS)
    args = [q_s] + [kv[g][t] for g in range(N_GROUPS) for t in (1, 3)]
    for g in range(N_GROUPS):
        width = HEADS_PER_GROUP * DILATIONS[g]
        if DILATIONS[g] == 1:
            shape = (DEC_BATCH, WINDOWS[g] * HEADS_PER_GROUP, HEAD_DIM)
            spec = pl.BlockSpec((ATT_SB,) + shape[1:], lambda b: (b, 0, 0))
        else:
            shape = (DEC_BATCH, SPAN, width, HEAD_DIM)
            spec = pl.BlockSpec((ATT_SB, SPAN, NEW_ROWS, HEAD_DIM), lambda b: (b, 0, 0, 0))
        in_specs += [spec, spec]
        args += [caches[g][0].reshape(shape), caches[g][1].reshape(shape)]
    rows0 = WINDOWS[0] * HEADS_PER_GROUP + NEW_ROWS
    return pl.pallas_call(
        _attn_s_kernel,
        grid=(DEC_BATCH // ATT_SB,),
        in_specs=in_specs,
        out_specs=pl.BlockSpec((step_rows, HEAD_DIM), lambda b: (b, 0)),
        out_shape=jax.ShapeDtypeStruct((4 * ROWS_S, HEAD_DIM), F32),
        scratch_shapes=[pltpu.VMEM((ATT_SB, rows0, HEAD_DIM), F32)] * 2,
        compiler_params=_params(("arbitrary",)),
        name="attn_sample",
    )(*args)


def _mix_kernel(*refs):
    (yp_ref, ys_ref, hp_ref, hs_ref) = refs[0:4]
    op_refs, lp_refs = refs[4:7], refs[7:10]
    os_ref, wpw_ref, bpw_ref, wo_ref, wg_ref, mp_ref, ms_ref = refs[10:17]
    i = pl.program_id(0)

    def combined_prompt_head(h):
        sl = pl.ds(h, TM2, stride=HEADS_PER_GROUP)
        ls = [l[sl, :] for l in lp_refs]
        mx = jnp.maximum(jnp.maximum(ls[0], ls[1]), ls[2])
        es = [jnp.exp(l - mx) for l in ls]
        num = es[0] * op_refs[0][sl, :] + es[1] * op_refs[1][sl, :] + es[2] * op_refs[2][sl, :]
        return num / (es[0] + es[1] + es[2])

    def sample_head(h):
        return os_ref[pl.ds(h, ROWS_S, stride=HEADS_PER_GROUP), :]

    def mix(y_ref, h_ref, head, m_ref):
        y, h = y_ref[...], h_ref[...]
        o_att = jnp.concatenate([head(hh).astype(BF16) for hh in range(HEADS_PER_GROUP)], axis=1)
        for c in range(D_MODEL // NC):
            sl = slice(c * NC, (c + 1) * NC)
            gl = slice(D_MODEL + c * NC, D_MODEL + (c + 1) * NC)
            conv = jnp.dot(y, wpw_ref[:, sl], preferred_element_type=F32) + bpw_ref[:, sl]
            att = jnp.dot(o_att, wo_ref[:, sl], preferred_element_type=F32)
            gate_conv = _sigmoid(jnp.dot(h, wg_ref[:, sl], preferred_element_type=F32))
            gate_att = _sigmoid(jnp.dot(h, wg_ref[:, gl], preferred_element_type=F32))
            m_ref[:, sl] = (gate_conv * conv + gate_att * att).astype(BF16)

    @pl.when(i < NP2)
    def _():
        mix(yp_ref, hp_ref, combined_prompt_head, mp_ref)

    @pl.when(i == NP2)
    def _():
        mix(ys_ref, hs_ref, sample_head, ms_ref)


def _mix_call(y_p, y_s, h_p, h_s, o_p, l_p, o_s, w_pw, b_pw, w_o, w_gates):
    pi = lambda i: (_pidx(i, NP2), 0)
    zero = lambda i: (0, 0)
    hp = pl.BlockSpec((4 * TM2, HEAD_DIM), pi)
    hs = pl.BlockSpec((4 * ROWS_S, HEAD_DIM), zero)
    const = lambda shape: pl.BlockSpec(shape, zero, pipeline_mode=pl.Buffered(1))
    return pl.pallas_call(
        _mix_kernel,
        grid=(NP2 + 1,),
        in_specs=[
            pl.BlockSpec((TM2, D_CONV), pi), pl.BlockSpec((ROWS_S, D_CONV), zero),
            pl.BlockSpec((TM2, D_MODEL), pi), pl.BlockSpec((ROWS_S, D_MODEL), zero),
            hp, hp, hp, hp, hp, hp, hs,
            const((D_CONV, D_MODEL)), const((1, D_MODEL)), const((GROUP_COLS, D_MODEL)),
            const((D_MODEL, 2 * D_MODEL)),
        ],
        out_specs=[pl.BlockSpec((TM2, D_MODEL), pi), pl.BlockSpec((ROWS_S, D_MODEL), zero)],
        out_shape=[jax.ShapeDtypeStruct((ROWS_P, D_MODEL), BF16),
                   jax.ShapeDtypeStruct((ROWS_S, D_MODEL), BF16)],
        compiler_params=_params(("arbitrary",)),
        name="mix",
    )(y_p, y_s, h_p, h_s, *o_p, *l_p, o_s, w_pw, b_pw, w_o, w_gates)


def _outproj_kernel(mp_ref, ms_ref, xp_ref, xs_ref, w_ref, g_ref, op_ref, os_ref, hp_ref, hs_ref):
    i = pl.program_id(0)

    def project(m_ref, x_ref, o_ref, h_ref):
        x1 = x_ref[...] + jnp.dot(m_ref[...], w_ref[...], preferred_element_type=F32)
        o_ref[...] = x1
        h_ref[...] = _rms(x1, g_ref[...]).astype(BF16)

    @pl.when(i < NP2)
    def _():
        project(mp_ref, xp_ref, op_ref, hp_ref)

    @pl.when(i == NP2)
    def _():
        project(ms_ref, xs_ref, os_ref, hs_ref)


def _outproj_call(m_p, m_s, xp, xs, w_out, g_ffn):
    pi = lambda i: (_pidx(i, NP2), 0)
    zero = lambda i: (0, 0)
    tile_p, tile_s = pl.BlockSpec((TM2, D_MODEL), pi), pl.BlockSpec((ROWS_S, D_MODEL), zero)
    return pl.pallas_call(
        _outproj_kernel,
        grid=(NP2 + 1,),
        in_specs=[tile_p, tile_s, tile_p, tile_s, pl.BlockSpec((D_MODEL, D_MODEL), zero),
                  pl.BlockSpec((1, D_MODEL), zero)],
        out_specs=[tile_p, tile_s, tile_p, tile_s],
        out_shape=[jax.ShapeDtypeStruct((ROWS_P, D_MODEL), F32), jax.ShapeDtypeStruct((ROWS_S, D_MODEL), F32),
                   jax.ShapeDtypeStruct((ROWS_P, D_MODEL), BF16), jax.ShapeDtypeStruct((ROWS_S, D_MODEL), BF16)],
        compiler_params=_params(("arbitrary",)),
        name="out_proj",
    )(m_p, m_s, xp, xs, w_out, g_ffn)


N_CACHE = 2 * N_GROUPS
COPY_STEPS = 2 * DEC_BATCH


def _shift_copies(c, w, cache_refs, new_refs, out_refs, bufs, sems):
    b = c // 2
    loads, stores = [], []
    for g in range(N_GROUPS):
        n = 2 * g + w
        keep = WINDOWS[g] * HEADS_PER_GROUP - NEW_ROWS
        loads.append(pltpu.make_async_copy(cache_refs[n].at[b, pl.ds(NEW_ROWS, keep)], bufs[g].at[w],
                                           sems.at[0, w, g]))
        stores.append(pltpu.make_async_copy(bufs[g].at[w], out_refs[n].at[b, pl.ds(0, keep)],
                                            sems.at[1, w, g]))
        stores.append(pltpu.make_async_copy(new_refs[n].at[b], out_refs[n].at[b, pl.ds(keep, NEW_ROWS)],
                                            sems.at[2, w, g]))
    return loads, stores


def _ffn_kernel(*refs):
    x_hbm, xs_ref, hp_ref, hs_ref, wgu_hbm, wd_hbm, gl_ref = refs[0:7]
    cache_refs = refs[7:7 + N_CACHE]
    new_refs = refs[7 + N_CACHE:7 + 2 * N_CACHE]
    yp_ref, ys_ref = refs[7 + 2 * N_CACHE:9 + 2 * N_CACHE]
    out_refs = refs[9 + 2 * N_CACHE:9 + 3 * N_CACHE]
    bufs = refs[9 + 3 * N_CACHE:9 + 3 * N_CACHE + N_GROUPS]
    wgu_buf, wd_buf, sems, x_sem, w_sems = refs[9 + 3 * N_CACHE + N_GROUPS:]
    i = pl.program_id(0)

    def weight_copies(chunk, slot):
        return (pltpu.make_async_copy(wgu_hbm.at[:, pl.ds(pl.multiple_of(chunk * 2 * TH, 2 * TH), 2 * TH)],
                                      wgu_buf.at[slot], w_sems.at[0, slot]),
                pltpu.make_async_copy(wd_hbm.at[pl.ds(pl.multiple_of(chunk * TH, TH), TH), :],
                                      wd_buf.at[slot], w_sems.at[1, slot]))

    @pl.when(i == 0)
    def _():
        for d in weight_copies(0, 0):
            d.start()
        ys_ref[...] = xs_ref[...]

    residual = pltpu.make_async_copy(x_hbm.at[pl.ds(pl.multiple_of(i * TM, TM), TM)], yp_ref, x_sem)
    residual.start()

    def hidden_chunk(k, carry):
        step = i * KH + k
        for w in range(2):
            @pl.when(step % 2 == w)
            def _():
                for d in weight_copies(k, w):
                    d.wait()

                @pl.when(step + 1 < NP * KH)
                def _():
                    for d in weight_copies(jnp.where(k + 1 == KH, 0, k + 1), 1 - w):
                        d.start()

                copies = lambda c, slot: _shift_copies(c, slot, cache_refs, new_refs, out_refs, bufs, sems)

                @pl.when((step == 0) & (step < COPY_STEPS))
                def _():
                    for d in copies(step, w)[0]:
                        d.start()

                @pl.when(step < COPY_STEPS)
                def _():
                    for d in copies(step, w)[0]:
                        d.wait()

                @pl.when((step >= 1) & (step <= COPY_STEPS))
                def _():
                    for d in copies(step - 1, 1 - w)[1]:
                        d.wait()

                @pl.when(step < COPY_STEPS)
                def _():
                    for d in copies(step, w)[1]:
                        d.start()

                @pl.when(step + 1 < COPY_STEPS)
                def _():
                    for d in copies(step + 1, 1 - w)[0]:
                        d.start()

        slot = step % 2

        def swiglu(h):
            a = jnp.dot(h, wgu_buf[slot, :, 0:TH], preferred_element_type=F32)
            u = jnp.dot(h, wgu_buf[slot, :, TH:2 * TH], preferred_element_type=F32)
            return (a * _sigmoid(a) * u).astype(BF16)

        act = swiglu(hp_ref[...])

        @pl.when(k == 0)
        def _():
            residual.wait()

        yp_ref[...] += jnp.dot(act, wd_buf[slot], preferred_element_type=F32)

        @pl.when(i == 0)
        def _():
            ys_ref[...] += jnp.dot(swiglu(hs_ref[...]), wd_buf[slot], preferred_element_type=F32)

        return carry

    lax.fori_loop(0, KH, hidden_chunk, 0)

    yp_ref[...] = _rms(yp_ref[...], gl_ref[...])

    @pl.when(i == 0)
    def _():
        ys_ref[...] = _rms(ys_ref[...], gl_ref[...])


def _ffn_call(x1_p, x1_s, h_p, h_s, w_gate_up, w_down, g_final, caches, new_rows):
    assert NP * KH >= COPY_STEPS
    pi = lambda i: (i, 0)
    zero = lambda i: (0, 0)
    hbm = pl.BlockSpec(memory_space=pl.ANY)
    return pl.pallas_call(
        _ffn_kernel,
        grid=(NP,),
        in_specs=[
            hbm,
            pl.BlockSpec((ROWS_S, D_MODEL), zero, pipeline_mode=pl.Buffered(1)),
            pl.BlockSpec((TM, D_MODEL), pi),
            pl.BlockSpec((ROWS_S, D_MODEL), zero, pipeline_mode=pl.Buffered(1)),
            hbm, hbm,
            pl.BlockSpec((1, D_MODEL), zero),
        ] + [hbm] * (2 * N_CACHE),
        out_specs=[pl.BlockSpec((TM, D_MODEL), pi), pl.BlockSpec((ROWS_S, D_MODEL), zero)] + [hbm] * N_CACHE,
        out_shape=[jax.ShapeDtypeStruct((ROWS_P, D_MODEL), F32),
                   jax.ShapeDtypeStruct((ROWS_S, D_MODEL), F32)]
        + [jax.ShapeDtypeStruct(c.shape, F32) for c in caches],
        scratch_shapes=[pltpu.VMEM((2, WINDOWS[g] * HEADS_PER_GROUP - NEW_ROWS, HEAD_DIM), F32)
                        for g in range(N_GROUPS)]
        + [pltpu.VMEM((2, D_MODEL, 2 * TH), BF16), pltpu.VMEM((2, TH, D_MODEL), BF16)]
        + [pltpu.SemaphoreType.DMA((3, 2, N_GROUPS)), pltpu.SemaphoreType.DMA(()),
           pltpu.SemaphoreType.DMA((2, 2))],
        compiler_params=_params(("arbitrary",), FFN_VMEM_LIMIT),
        name="ffn",
    )(x1_p, x1_s, h_p, h_s, w_gate_up, w_down, g_final, *caches, *new_rows)


def _rope_tables(pos):
    half = ROT_DIM // 2
    inv = np.power(ROPE_THETA, -np.arange(0, ROT_DIM, 2, dtype=np.float64) / ROT_DIM)
    ang = np.asarray(pos, np.float64)[:, None] * inv[None, :]
    cos, sin = np.cos(ang), np.sin(ang)
    n = ang.shape[0]
    ones = np.ones((n, HEAD_DIM - ROT_DIM))
    zeros = np.zeros((n, HEAD_DIM - ROT_DIM))
    zh = np.zeros((n, half))
    c = np.concatenate([cos, cos, ones], axis=1)
    s_lo = np.concatenate([-sin, zh, zeros], axis=1)
    s_hi = np.concatenate([zh, sin, zeros], axis=1)
    return jnp.asarray(np.stack([c, s_lo, s_hi]), F32)


def kernel(x_prompt, x_sample, state_conv, cache_k_w128, cache_v_w128, cache_k_w512, cache_v_w512,
           cache_k_w2048, cache_v_w2048, g_mix, w_in, b_glu, w_dw, b_dw, ln_g, ln_b, w_pw, b_pw,
           w_o_att, w_out, g_ffn, w_gate, w_up, w_down, g_final):
    xp = x_prompt.reshape(ROWS_P, D_MODEL)
    xs = x_sample.reshape(ROWS_S, D_MODEL)
    w_in2 = w_in.reshape(D_MODEL, IN_COLS)

    hp, hs = _rms_call(xp, xs, g_mix)
    u_p, u_s, w_gates16 = _glu_call(hp, hs, w_in2, b_glu)
    tab_p = _rope_tables(np.arange(SEQ))
    tab_s = _rope_tables(PAST_LEN + np.arange(ROWS_S) % DEC_SEQ)
    q_p, q_s = _q_call(hp, hs, w_in2, tab_p, tab_s)
    mixer_w = (w_out.reshape(D_MODEL, D_MODEL), w_pw.reshape(D_CONV, D_MODEL), w_o_att.reshape(GROUP_COLS, D_MODEL))
    kv = [_kv_call(hp, hs, w_in2, tab_p, tab_s, g, mixer_w[g]) for g in range(N_GROUPS)]
    w_out16, w_pw16, w_o16 = kv[0][4], kv[1][4], kv[2][4]

    w_dw2 = w_dw.reshape(CONV_WIDTH, D_CONV)
    y_p, w_gate_up16, w_down16 = _conv_p_call(
        u_p, w_dw2, b_dw, ln_g, ln_b, w_gate.reshape(D_MODEL, FFN_HIDDEN), w_up.reshape(D_MODEL, FFN_HIDDEN),
        w_down.reshape(FFN_HIDDEN, D_MODEL))
    y_s, conv_s = _conv_s_call(state_conv, u_s.reshape(DEC_BATCH, DEC_SEQ, D_CONV), w_dw2, b_dw, ln_g, ln_b)
    y_s = y_s.reshape(ROWS_S, D_CONV).astype(BF16)
    conv_p = u_p.reshape(BATCH, SEQ, D_CONV)[:, SEQ - (CONV_WIDTH - 1):][None]

    caches = ((cache_k_w128, cache_v_w128), (cache_k_w512, cache_v_w512), (cache_k_w2048, cache_v_w2048))
    o_p, l_p = [], []
    for g in range(N_GROUPS):
        o, l = _attn_p_call(q_p, kv[g][0], kv[g][2], g)
        o_p.append(o)
        l_p.append(l)
    o_s = _attn_s_call(q_s, kv, caches)

    m_p, m_s = _mix_call(y_p, y_s, hp, hs, o_p, l_p, o_s, w_pw16, b_pw, w_o16, w_gates16)
    x1_p, x1_s, hf_p, hf_s = _outproj_call(m_p, m_s, xp, xs, w_out16, g_ffn)

    flat = lambda g, a: a.reshape(DEC_BATCH, WINDOWS[g] * HEADS_PER_GROUP, HEAD_DIM)
    cache_flat = [flat(g, caches[g][t]) for g in range(N_GROUPS) for t in range(2)]
    new_rows = [kv[g][t].reshape(DEC_BATCH, NEW_ROWS, HEAD_DIM) for g in range(N_GROUPS) for t in (1, 3)]
    y_p2, y_s2, *shifted = _ffn_call(x1_p, x1_s, hf_p, hf_s, w_gate_up16, w_down16,
                                     g_final.reshape(1, D_MODEL), cache_flat, new_rows)

    cache_out = []
    for g in range(N_GROUPS):
        keep = min(WINDOWS[g], SEQ)
        for t in range(2):
            prompt_rows = kv[g][2 * t] if keep == SEQ else kv[g][5 + t]
            cache_out.append(prompt_rows.reshape(1, BATCH, keep, HEADS_PER_GROUP, HEAD_DIM))
            cache_out.append(shifted[2 * g + t].reshape(1, DEC_BATCH, WINDOWS[g], HEADS_PER_GROUP, HEAD_DIM))
    return (y_p2.reshape(BATCH, SEQ, D_MODEL), y_s2.reshape(DEC_BATCH, DEC_SEQ, D_MODEL),
            conv_p, conv_s, *cache_out)
```

```python
import jax
import jax.numpy as jnp
import numpy as np
from jax import lax
from jax.experimental import pallas as pl
from jax.experimental.pallas import tpu as pltpu

F32 = jnp.float32
BF16 = jnp.bfloat16

D_MODEL = 2048
BATCH = 4
SEQ = 2048
DEC_BATCH = 32
DEC_SEQ = 4
PAST_LEN = 8192
HEAD_DIM = 128
HEADS_PER_GROUP = 4
DILATIONS = (1, 4, 16)
WINDOWS = (128, 512, 2048)
N_GROUPS = 3
GROUP_COLS = HEADS_PER_GROUP * HEAD_DIM
ATT_WIDTH = N_GROUPS * GROUP_COLS
SPAN = 128
ATT_SCALE = HEAD_DIM ** -0.5
ROT_DIM = HEAD_DIM // 4
ROPE_THETA = 500000.0
D_CONV = D_MODEL // 2
CONV_WIDTH = 31
FFN_HIDDEN = 5632
NORM_EPS = 1e-6
LN_EPS = 1e-5
IN_COLS = 2 * D_CONV + 3 * ATT_WIDTH + 2 * D_MODEL

ROWS_P = BATCH * SEQ
ROWS_S = DEC_BATCH * DEC_SEQ
NEG = -1e30

VMEM_LIMIT = 56 * 1024 * 1024
FFN_VMEM_LIMIT = 62 * 1024 * 1024

TM = 1024
NP = ROWS_P // TM
TN = 512
NC = 256
TM2 = 512
NP2 = ROWS_P // TM2
TH = 512
KH = FFN_HIDDEN // TH
TC = 512
HALO = 32
CS = 4


def _sigmoid(x):
    return 1.0 / (1.0 + jnp.exp(-x))


def _rms(x, g):
    return x * lax.rsqrt(jnp.mean(x * x, axis=-1, keepdims=True) + NORM_EPS) * g


def _params(sem, vmem_limit=VMEM_LIMIT):
    return pltpu.CompilerParams(dimension_semantics=sem, vmem_limit_bytes=vmem_limit)


def _rms_kernel(xp_ref, xs_ref, g_ref, hp_ref, hs_ref):
    hp_ref[...] = _rms(xp_ref[...], g_ref[...]).astype(BF16)

    @pl.when(pl.program_id(0) == 0)
    def _():
        hs_ref[...] = _rms(xs_ref[...], g_ref[...]).astype(BF16)


def _rms_call(xp, xs, g):
    return pl.pallas_call(
        _rms_kernel,
        grid=(NP,),
        in_specs=[
            pl.BlockSpec((TM, D_MODEL), lambda i: (i, 0)),
            pl.BlockSpec((ROWS_S, D_MODEL), lambda i: (0, 0)),
            pl.BlockSpec((1, D_MODEL), lambda i: (0, 0)),
        ],
        out_specs=[
            pl.BlockSpec((TM, D_MODEL), lambda i: (i, 0)),
            pl.BlockSpec((ROWS_S, D_MODEL), lambda i: (0, 0)),
        ],
        out_shape=[
            jax.ShapeDtypeStruct((ROWS_P, D_MODEL), BF16),
            jax.ShapeDtypeStruct((ROWS_S, D_MODEL), BF16),
        ],
        compiler_params=_params(("arbitrary",)),
        name="rms_in",
    )(xp, xs, g)


N_GATE_BLOCKS = 2 * D_MODEL // TN


def _glu_kernel(hp_ref, hs_ref, wa_ref, wb_ref, ba_ref, bb_ref, *refs):
    gate32_refs = refs[:N_GATE_BLOCKS]
    up_ref, us_ref, gate16_ref, wa_s, wb_s = refs[N_GATE_BLOCKS:]
    i = pl.program_id(1)

    @pl.when(i == 0)
    def _():
        wa_s[...] = wa_ref[...].astype(BF16)
        wb_s[...] = wb_ref[...].astype(BF16)

    for c, g_ref in enumerate(gate32_refs):
        gate16_ref[:, c * TN:(c + 1) * TN] = g_ref[...].astype(BF16)

    def glu(h, u_ref):
        for c in range(TN // NC):
            sl = slice(c * NC, (c + 1) * NC)
            za = jnp.dot(h, wa_s[:, sl], preferred_element_type=F32) + ba_ref[:, sl]
            zb = jnp.dot(h, wb_s[:, sl], preferred_element_type=F32) + bb_ref[:, sl]
            u_ref[:, sl] = za * _sigmoid(zb)

    glu(hp_ref[...], up_ref)

    @pl.when(i == 0)
    def _():
        glu(hs_ref[...], us_ref)


def _glu_call(hp, hs, w_in, b_glu):
    nj = D_CONV // TN
    slab = D_MODEL // (nj * NP)
    gate0 = (2 * D_CONV + 3 * ATT_WIDTH) // TN
    slab_row = lambda j, i: j * NP + i
    return pl.pallas_call(
        _glu_kernel,
        grid=(nj, NP),
        in_specs=[
            pl.BlockSpec((TM, D_MODEL), lambda j, i: (i, 0)),
            pl.BlockSpec((ROWS_S, D_MODEL), lambda j, i: (0, 0)),
            pl.BlockSpec((D_MODEL, TN), lambda j, i: (0, j)),
            pl.BlockSpec((D_MODEL, TN), lambda j, i: (0, j + nj)),
            pl.BlockSpec((1, TN), lambda j, i: (0, j)),
            pl.BlockSpec((1, TN), lambda j, i: (0, j + nj)),
        ] + [pl.BlockSpec((slab, TN), lambda j, i, c=c: (slab_row(j, i), gate0 + c)) for c in range(N_GATE_BLOCKS)],
        out_specs=[
            pl.BlockSpec((TM, TN), lambda j, i: (i, j)),
            pl.BlockSpec((ROWS_S, TN), lambda j, i: (0, j)),
            pl.BlockSpec((slab, 2 * D_MODEL), lambda j, i: (slab_row(j, i), 0)),
        ],
        out_shape=[
            jax.ShapeDtypeStruct((ROWS_P, D_CONV), F32),
            jax.ShapeDtypeStruct((ROWS_S, D_CONV), F32),
            jax.ShapeDtypeStruct((D_MODEL, 2 * D_MODEL), BF16),
        ],
        scratch_shapes=[pltpu.VMEM((D_MODEL, TN), BF16), pltpu.VMEM((D_MODEL, TN), BF16)],
        compiler_params=_params(("arbitrary", "arbitrary")),
        name="in_glu",
    )(hp, hs, w_in, w_in, b_glu, b_glu, *([w_in] * N_GATE_BLOCKS))


def _project_heads(h, w_s, t_ref, o_ref, rows):
    if t_ref is not None:
        cos, sin_lo, sin_hi = t_ref[0], t_ref[1], t_ref[2]
    for c in range(TN // NC):
        z = jnp.dot(h, w_s[:, c * NC:(c + 1) * NC], preferred_element_type=F32)
        for hc in range(NC // HEAD_DIM):
            t = z[:, hc * HEAD_DIM:(hc + 1) * HEAD_DIM]
            if t_ref is not None:
                t = (t * cos + pltpu.roll(t, HEAD_DIM - ROT_DIM // 2, 1) * sin_lo
                     + pltpu.roll(t, ROT_DIM // 2, 1) * sin_hi)
            head = c * (NC // HEAD_DIM) + hc
            o_ref[pl.ds(head, rows, stride=HEADS_PER_GROUP), :] = t


def _q_kernel(hp_ref, hs_ref, w0_ref, w1_ref, w2_ref, tp_ref, ts_ref, op_ref, os_ref, w_s):
    i = pl.program_id(0)

    @pl.when(i == 0)
    def _():
        for g, w_ref in enumerate((w0_ref, w1_ref, w2_ref)):
            w_s[g] = w_ref[...].astype(BF16)

    h = hp_ref[...]
    for g in range(N_GROUPS):
        _project_heads(h, w_s.at[g], tp_ref, op_ref.at[g], TM)

    @pl.when(i == 0)
    def _():
        h = hs_ref[...]
        for g in range(N_GROUPS):
            _project_heads(h, w_s.at[g], ts_ref, os_ref.at[g], ROWS_S)


def _make_kv_kernel(tail_rows):
    tiles_per_seq = SEQ // TM

    def kern(hp_ref, hs_ref, wk_ref, wv_ref, tp_ref, ts_ref, slab32_ref, *refs):
        kp_ref, ks_ref, vp_ref, vs_ref, slab16_ref = refs[0:5]
        kt_ref, vt_ref = refs[5:7] if tail_rows else (None, None)
        wk_s, wv_s = refs[-2:]
        i = pl.program_id(0)

        @pl.when(i == 0)
        def _():
            wk_s[...] = wk_ref[...].astype(BF16)
            wv_s[...] = wv_ref[...].astype(BF16)

        slab16_ref[...] = slab32_ref[...].astype(BF16)
        h = hp_ref[...]
        _project_heads(h, wk_s, tp_ref, kp_ref.at[0], TM)
        _project_heads(h, wv_s, None, vp_ref.at[0], TM)

        if tail_rows:
            @pl.when(i % tiles_per_seq == tiles_per_seq - 1)
            def _():
                kt_ref[0] = kp_ref[0, 4 * TM - tail_rows:4 * TM, :]
                vt_ref[0] = vp_ref[0, 4 * TM - tail_rows:4 * TM, :]

        @pl.when(i == 0)
        def _():
            h = hs_ref[...]
            _project_heads(h, wk_s, ts_ref, ks_ref.at[0], ROWS_S)
            _project_heads(h, wv_s, None, vs_ref.at[0], ROWS_S)

    return kern


Q_PLANE0 = 2 * D_CONV // TN


def _q_call(hp, hs, w_in, tab_p, tab_s):
    wspec = lambda g: pl.BlockSpec((D_MODEL, TN), lambda i: (0, Q_PLANE0 + g), pipeline_mode=pl.Buffered(1))
    return pl.pallas_call(
        _q_kernel,
        grid=(NP,),
        in_specs=[
            pl.BlockSpec((TM, D_MODEL), lambda i: (i, 0)),
            pl.BlockSpec((ROWS_S, D_MODEL), lambda i: (0, 0)),
            wspec(0), wspec(1), wspec(2),
            pl.BlockSpec((3, TM, HEAD_DIM), lambda i: (0, i % (SEQ // TM), 0)),
            pl.BlockSpec((3, ROWS_S, HEAD_DIM), lambda i: (0, 0, 0)),
        ],
        out_specs=[
            pl.BlockSpec((N_GROUPS, 4 * TM, HEAD_DIM), lambda i: (0, i, 0)),
            pl.BlockSpec((N_GROUPS, 4 * ROWS_S, HEAD_DIM), lambda i: (0, 0, 0)),
        ],
        out_shape=[
            jax.ShapeDtypeStruct((N_GROUPS, 4 * ROWS_P, HEAD_DIM), F32),
            jax.ShapeDtypeStruct((N_GROUPS, 4 * ROWS_S, HEAD_DIM), F32),
        ],
        scratch_shapes=[pltpu.VMEM((N_GROUPS, D_MODEL, TN), BF16)],
        compiler_params=_params(("arbitrary",)),
        name="in_q",
    )(hp, hs, w_in, w_in, w_in, tab_p, tab_s)


def _kv_call(hp, hs, w_in, tab_p, tab_s, g, w_extra):
    wspec = lambda plane: pl.BlockSpec((D_MODEL, TN), lambda i: (0, Q_PLANE0 + plane), pipeline_mode=pl.Buffered(1))
    outp = pl.BlockSpec((1, 4 * TM, HEAD_DIM), lambda i: (0, i, 0))
    outs = pl.BlockSpec((1, 4 * ROWS_S, HEAD_DIM), lambda i: (0, 0, 0))
    shp = jax.ShapeDtypeStruct((1, 4 * ROWS_P, HEAD_DIM), F32)
    shs = jax.ShapeDtypeStruct((1, 4 * ROWS_S, HEAD_DIM), F32)
    slab = pl.BlockSpec((w_extra.shape[0] // NP, w_extra.shape[1]), lambda i: (i, 0))
    out_specs = [outp, outs, outp, outs, slab]
    out_shape = [shp, shs, shp, shs, jax.ShapeDtypeStruct(w_extra.shape, BF16)]
    tail_rows = HEADS_PER_GROUP * WINDOWS[g] if WINDOWS[g] < SEQ else 0
    if tail_rows:
        assert tail_rows <= 4 * TM
        tail = pl.BlockSpec((1, tail_rows, HEAD_DIM), lambda i: (i // (SEQ // TM), 0, 0))
        out_specs += [tail, tail]
        out_shape += [jax.ShapeDtypeStruct((BATCH, tail_rows, HEAD_DIM), F32)] * 2
    return pl.pallas_call(
        _make_kv_kernel(tail_rows),
        grid=(NP,),
        in_specs=[
            pl.BlockSpec((TM, D_MODEL), lambda i: (i, 0)),
            pl.BlockSpec((ROWS_S, D_MODEL), lambda i: (0, 0)),
            wspec(N_GROUPS + g), wspec(2 * N_GROUPS + g),
            pl.BlockSpec((3, TM, HEAD_DIM), lambda i: (0, i % (SEQ // TM), 0)),
            pl.BlockSpec((3, ROWS_S, HEAD_DIM), lambda i: (0, 0, 0)),
            slab,
        ],
        out_specs=out_specs,
        out_shape=out_shape,
        scratch_shapes=[pltpu.VMEM((D_MODEL, TN), BF16), pltpu.VMEM((D_MODEL, TN), BF16)],
        compiler_params=_params(("arbitrary",)),
        name="in_kv%d" % g,
    )(hp, hs, w_in, w_in, tab_p, tab_s, w_extra)


def _ln_silu(y, g, b):
    mu = jnp.mean(y, axis=-1, keepdims=True)
    yc = y - mu
    var = jnp.mean(yc * yc, axis=-1, keepdims=True)
    z = yc * lax.rsqrt(var + LN_EPS) * g + b
    return z * _sigmoid(z)


def _conv_p_kernel(cur_ref, prev_ref, w_ref, bdw_ref, g_ref, b_ref, wg32_ref, wu32_ref, wd32_ref,
                   y_ref, wgu16_ref, wd16_ref, ext, ypre):
    c = pl.program_id(1)
    n_lc = D_CONV // 128
    for k in range(KH):
        src = slice(k * TH, (k + 1) * TH)
        wgu16_ref[:, 2 * k * TH:(2 * k + 1) * TH] = wg32_ref[:, src].astype(BF16)
        wgu16_ref[:, (2 * k + 1) * TH:(2 * k + 2) * TH] = wu32_ref[:, src].astype(BF16)
    wd16_ref[...] = wd32_ref[...].astype(BF16)
    for lc in range(n_lc):
        sl = slice(lc * 128, (lc + 1) * 128)
        ext[lc, pl.ds(HALO, TC), :] = cur_ref[:, sl]

    @pl.when(c > 0)
    def _():
        for lc in range(n_lc):
            ext[lc, 0:HALO, :] = prev_ref[TC - HALO:TC, lc * 128:(lc + 1) * 128]

    @pl.when(c == 0)
    def _():
        for lc in range(n_lc):
            ext[lc, 0:HALO, :] = jnp.zeros((HALO, 128), F32)

    base = HALO - (CONV_WIDTH - 1)
    for lc in range(n_lc):
        sl = slice(lc * 128, (lc + 1) * 128)
        for t0 in range(CS):
            acc = jnp.zeros((TC // CS, 128), F32)
            for j in range(CONV_WIDTH):
                acc = acc + ext[lc, pl.ds(base + t0 + j, TC // CS, stride=CS), :] * w_ref[j:j + 1, sl]
            ypre[lc, pl.ds(t0, TC // CS, stride=CS), :] = acc + bdw_ref[:, sl]

    total = jnp.zeros((TC, 1), F32)
    for lc in range(n_lc):
        total = total + jnp.sum(ypre[lc], axis=-1, keepdims=True)
    mu = total * (1.0 / D_CONV)
    sq = jnp.zeros((TC, 1), F32)
    for lc in range(n_lc):
        yc = ypre[lc] - mu
        sq = sq + jnp.sum(yc * yc, axis=-1, keepdims=True)
    inv = lax.rsqrt(sq * (1.0 / D_CONV) + LN_EPS)
    for lc in range(n_lc):
        sl = slice(lc * 128, (lc + 1) * 128)
        z = (ypre[lc] - mu) * inv * g_ref[:, sl] + b_ref[:, sl]
        y_ref[:, sl] = (z * _sigmoid(z)).astype(BF16)


def _conv_p_call(u_p, w_dw, b_dw, ln_g, ln_b, w_gate, w_up, w_down):
    nc = SEQ // TC
    steps = BATCH * nc
    vec = pl.BlockSpec((1, D_CONV), lambda b, c: (0, 0))
    slab = lambda rows, cols: pl.BlockSpec((rows // steps, cols), lambda b, c: (b * nc + c, 0))
    return pl.pallas_call(
        _conv_p_kernel,
        grid=(BATCH, nc),
        in_specs=[
            pl.BlockSpec((TC, D_CONV), lambda b, c: (b * nc + c, 0)),
            pl.BlockSpec((TC, D_CONV), lambda b, c: (b * nc + jnp.maximum(c - 1, 0), 0)),
            pl.BlockSpec((CONV_WIDTH, D_CONV), lambda b, c: (0, 0)),
            vec, vec, vec,
            slab(D_MODEL, FFN_HIDDEN), slab(D_MODEL, FFN_HIDDEN), slab(FFN_HIDDEN, D_MODEL),
        ],
        out_specs=[pl.BlockSpec((TC, D_CONV), lambda b, c: (b * nc + c, 0)),
                   slab(D_MODEL, 2 * FFN_HIDDEN), slab(FFN_HIDDEN, D_MODEL)],
        out_shape=[jax.ShapeDtypeStruct((ROWS_P, D_CONV), BF16),
                   jax.ShapeDtypeStruct((D_MODEL, 2 * FFN_HIDDEN), BF16),
                   jax.ShapeDtypeStruct((FFN_HIDDEN, D_MODEL), BF16)],
        scratch_shapes=[pltpu.VMEM((D_CONV // 128, HALO + TC, 128), F32),
                        pltpu.VMEM((D_CONV // 128, TC, 128), F32)],
        compiler_params=_params(("arbitrary", "arbitrary")),
        name="conv_prompt",
    )(u_p, u_p, w_dw, b_dw, ln_g, ln_b, w_gate, w_up, w_down)


CONV_SB = 8


def _conv_s_kernel(state_ref, u_ref, w_ref, bdw_ref, g_ref, b_ref, y_ref, new_state_ref, ext):
    past = CONV_WIDTH - 1
    ext[:, 0:past, :] = state_ref[0]
    ext[:, past:past + DEC_SEQ, :] = u_ref[...]
    new_state_ref[0] = ext[:, DEC_SEQ:past + DEC_SEQ, :]
    acc = jnp.zeros((CONV_SB, DEC_SEQ, D_CONV), F32)
    for j in range(CONV_WIDTH):
        acc = acc + ext[:, pl.ds(j, DEC_SEQ), :] * w_ref[j:j + 1, :]
    y_ref[...] = _ln_silu(acc + bdw_ref[...], g_ref[...], b_ref[...])


def _conv_s_call(state_conv, u_s, w_dw, b_dw, ln_g, ln_b):
    vec = pl.BlockSpec((1, D_CONV), lambda b: (0, 0))
    past = CONV_WIDTH - 1
    state = pl.BlockSpec((1, CONV_SB, past, D_CONV), lambda b: (0, b, 0, 0))
    rows = pl.BlockSpec((CONV_SB, DEC_SEQ, D_CONV), lambda b: (b, 0, 0))
    return pl.pallas_call(
        _conv_s_kernel,
        grid=(DEC_BATCH // CONV_SB,),
        in_specs=[state, rows, pl.BlockSpec((CONV_WIDTH, D_CONV), lambda b: (0, 0)), vec, vec, vec],
        out_specs=[rows, state],
        out_shape=[jax.ShapeDtypeStruct((DEC_BATCH, DEC_SEQ, D_CONV), F32),
                   jax.ShapeDtypeStruct(state_conv.shape, F32)],
        scratch_shapes=[pltpu.VMEM((CONV_SB, past + DEC_SEQ, D_CONV), F32)],
        compiler_params=_params(("arbitrary",)),
        name="conv_sample",
    )(state_conv, u_s, w_dw, b_dw, ln_g, ln_b)


def _softmax_block(q, k, v, mask):
    s = lax.dot_general(q.astype(BF16), k.astype(BF16), (((2,), (2,)), ((0,), (0,))),
                        preferred_element_type=F32) * ATT_SCALE
    s = jnp.where(mask[None], s, NEG)
    m = jnp.max(s, axis=-1, keepdims=True)
    p = jnp.exp(s - m)
    den = jnp.sum(p, axis=-1, keepdims=True)
    o = lax.dot_general(p.astype(BF16), v.astype(BF16), (((2,), (1,)), ((0,), (0,))),
                        preferred_element_type=F32) / den
    lse = m + jnp.log(den)
    return o, jnp.broadcast_to(lse, o.shape)


def _make_attn_p_kernel(dil):
    stride = HEADS_PER_GROUP * dil
    nblk = SEQ // dil // SPAN
    blk_rows = SPAN * stride
    pair = min(dil, 2)
    group_rows = pair * HEADS_PER_GROUP
    n_r2 = dil // pair

    def kern(q_ref, k_ref, v_ref, o_ref, l_ref):
        qi = lax.broadcasted_iota(jnp.int32, (SPAN, SPAN), 0)
        ki = lax.broadcasted_iota(jnp.int32, (SPAN, SPAN), 1)
        mask_first = ki <= qi
        qi2 = lax.broadcasted_iota(jnp.int32, (SPAN, 2 * SPAN), 0)
        ki2 = lax.broadcasted_iota(jnp.int32, (SPAN, 2 * SPAN), 1)
        mask_band = (ki2 >= qi2) & (ki2 <= qi2 + SPAN)

        def block(base, key_base, n_keys, mask):
            qs = [pl.ds(base + off, SPAN, stride=stride) for off in range(group_rows)]
            ks = [pl.ds(key_base + off, n_keys, stride=stride) for off in range(group_rows)]
            o, lse = _softmax_block(jnp.stack([q_ref[0, s, :] for s in qs]),
                                    jnp.stack([k_ref[0, s, :] for s in ks]),
                                    jnp.stack([v_ref[0, s, :] for s in ks]), mask)
            for u, s in enumerate(qs):
                o_ref[s, :] = o[u]
                l_ref[s, :] = lse[u]

        def first(r2, carry):
            base = pl.multiple_of(r2 * group_rows, max(group_rows, 8))
            block(base, base, SPAN, mask_first)
            return carry

        def band(t, carry):
            jb = 1 + t // n_r2
            base = pl.multiple_of(jb * blk_rows + (t % n_r2) * group_rows, max(group_rows, 8))
            block(base, base - blk_rows, 2 * SPAN, mask_band)
            return carry

        lax.fori_loop(0, n_r2, first, 0)
        if nblk > 1:
            lax.fori_loop(0, n_r2 * (nblk - 1), band, 0)

    return kern


def _attn_p_wide_kernel(q_ref, k_ref, v_ref, o_ref, l_ref):
    qi = lax.broadcasted_iota(jnp.int32, (SPAN, SPAN), 0)
    ki = lax.broadcasted_iota(jnp.int32, (SPAN, SPAN), 1)
    mask_first = ki <= qi

    def tile(c, carry):
        q8 = jnp.swapaxes(q_ref[0, 0, :, c], 0, 1)
        k8 = jnp.swapaxes(k_ref[0, 0, :, c], 0, 1)
        v8 = jnp.swapaxes(v_ref[0, 0, :, c], 0, 1)
        o, lse = _softmax_block(q8, k8, v8, mask_first)
        o_ref[0, :, c] = jnp.swapaxes(o, 0, 1)
        l_ref[0, :, c] = jnp.swapaxes(lse, 0, 1)
        return carry

    lax.fori_loop(0, q_ref.shape[3], tile, 0)


def _attn_p_call(q_p, k_p, v_p, g):
    dil = DILATIONS[g]
    rows = 4 * SEQ
    if SEQ // dil == SPAN:
        tiles = HEADS_PER_GROUP * dil // 8
        view = lambda a: a.reshape(a.shape[0], BATCH, SPAN, tiles, 8, HEAD_DIM)
        blk = lambda plane: pl.BlockSpec((1, 1, SPAN, tiles, 8, HEAD_DIM), lambda b: (plane, b, 0, 0, 0, 0))
        out = pl.BlockSpec((1, SPAN, tiles, 8, HEAD_DIM), lambda b: (b, 0, 0, 0, 0))
        o, l = pl.pallas_call(
            _attn_p_wide_kernel,
            grid=(BATCH,),
            in_specs=[blk(g), blk(0), blk(0)],
            out_specs=[out, out],
            out_shape=[jax.ShapeDtypeStruct((BATCH, SPAN, tiles, 8, HEAD_DIM), F32)] * 2,
            compiler_params=_params(("arbitrary",)),
            name="attn_prompt_g%d" % g,
        )(view(q_p), view(k_p), view(v_p))
        return o.reshape(4 * ROWS_P, HEAD_DIM), l.reshape(4 * ROWS_P, HEAD_DIM)
    blk = lambda plane: pl.BlockSpec((1, rows, HEAD_DIM), lambda b: (plane, b, 0))
    out = pl.BlockSpec((rows, HEAD_DIM), lambda b: (b, 0))
    return pl.pallas_call(
        _make_attn_p_kernel(dil),
        grid=(BATCH,),
        in_specs=[blk(g), blk(0), blk(0)],
        out_specs=[out, out],
        out_shape=[jax.ShapeDtypeStruct((4 * ROWS_P, HEAD_DIM), F32)] * 2,
        compiler_params=_params(("arbitrary",)),
        name="attn_prompt_g%d" % g,
    )(q_p, k_p, v_p)


NEW_ROWS = DEC_SEQ * HEADS_PER_GROUP


def _joint_softmax(parts):
    mx = None
    for s, _ in parts:
        m = jnp.max(s, axis=0) if s.ndim == 3 else s
        mx = m if mx is None else jnp.maximum(mx, m)
    den = jnp.zeros((8, 1), F32)
    acc = jnp.zeros((8, HEAD_DIM), F32)
    for s, v in parts:
        p = jnp.exp(s - mx)
        if s.ndim == 3:
            den = den + jnp.sum(p, axis=0)
            acc = acc + jnp.sum(p * v, axis=0)
        else:
            den = den + p
            acc = acc + p * v
    return acc / den


def _score(q, k):
    return jnp.sum(q * k, axis=-1, keepdims=True) * ATT_SCALE


ATT_SB = 4


def _attn_s_kernel(q_ref, kn0, vn0, kn1, vn1, kn2, vn2, k0, v0, k1, v1, k2, v2, o_ref, kc, vc):
    nb = WINDOWS[0] * HEADS_PER_GROUP
    half = HEADS_PER_GROUP
    n_even, n_odd = SPAN // 2 + 1, SPAN // 2
    for s in range(ATT_SB):
        new = slice(s * NEW_ROWS, (s + 1) * NEW_ROWS)
        kc[s, 0:nb, :] = k0[s]
        kc[s, nb:nb + NEW_ROWS, :] = kn0[0, new, :]
        vc[s, 0:nb, :] = v0[s]
        vc[s, nb:nb + NEW_ROWS, :] = vn0[0, new, :]
        for j in range(2):
            rows = slice(8 * j, 8 * j + 8)
            qrows = slice(s * NEW_ROWS + 8 * j, s * NEW_ROWS + 8 * j + 8)
            parts = []
            q = q_ref[0, qrows, :]
            for start, n in ((8 * j, n_even), (8 * j + half, n_odd)):
                k3 = kc[s, pl.ds(start, 8 * n), :].reshape(n, 8, HEAD_DIM)
                v3 = vc[s, pl.ds(start, 8 * n), :].reshape(n, 8, HEAD_DIM)
                parts.append((_score(q[None], k3), v3))
            for g, (k_ref, v_ref, kn_ref, vn_ref) in ((1, (k1, v1, kn1, vn1)), (2, (k2, v2, kn2, vn2))):
                q = q_ref[g, qrows, :]
                parts.append((_score(q[None], k_ref[s, :, rows, :]), v_ref[s, :, rows, :]))
                parts.append((_score(q, kn_ref[0, qrows, :]), vn_ref[0, qrows, :]))
            o_ref[qrows, :] = _joint_softmax(parts)


def _attn_s_call(q_s, kv, caches):
    step_rows = ATT_SB * NEW_ROWS
    new = pl.BlockSpec((1, step_rows, HEAD_DIM), lambda b: (0, b, 0))
    in_specs = [pl.BlockSpec((N_GROUPS, step_rows, HEAD_DIM), lambda b: (0, b, 0))] + [new] * (2 * N_GROUPS)
    args = [q_s] + [kv[g][t] for g in range(N_GROUPS) for t in (1, 3)]
    for g in range(N_GROUPS):
        width = HEADS_PER_GROUP * DILATIONS[g]
        if DILATIONS[g] == 1:
            shape = (DEC_BATCH, WINDOWS[g] * HEADS_PER_GROUP, HEAD_DIM)
            spec = pl.BlockSpec((ATT_SB,) + shape[1:], lambda b: (b, 0, 0))
        else:
            shape = (DEC_BATCH, SPAN, width, HEAD_DIM)
            spec = pl.BlockSpec((ATT_SB, SPAN, NEW_ROWS, HEAD_DIM), lambda b: (b, 0, 0, 0))
        in_specs += [spec, spec]
        args += [caches[g][0].reshape(shape), caches[g][1].reshape(shape)]
    rows0 = WINDOWS[0] * HEADS_PER_GROUP + NEW_ROWS
    return pl.pallas_call(
        _attn_s_kernel,
        grid=(DEC_BATCH // ATT_SB,),
        in_specs=in_specs,
        out_specs=pl.BlockSpec((step_rows, HEAD_DIM), lambda b: (b, 0)),
        out_shape=jax.ShapeDtypeStruct((4 * ROWS_S, HEAD_DIM), F32),
        scratch_shapes=[pltpu.VMEM((ATT_SB, rows0, HEAD_DIM), F32)] * 2,
        compiler_params=_params(("arbitrary",)),
        name="attn_sample",
    )(*args)


def _mix_kernel(*refs):
    (yp_ref, ys_ref, hp_ref, hs_ref) = refs[0:4]
    op_refs, lp_refs = refs[4:7], refs[7:10]
    os_ref, wpw_ref, bpw_ref, wo_ref, wg_ref, mp_ref, ms_ref = refs[10:17]
    i = pl.program_id(0)

    def combined_prompt_head(h):
        sl = pl.ds(h, TM2, stride=HEADS_PER_GROUP)
        ls = [l[sl, :] for l in lp_refs]
        mx = jnp.maximum(jnp.maximum(ls[0], ls[1]), ls[2])
        es = [jnp.exp(l - mx) for l in ls]
        num = es[0] * op_refs[0][sl, :] + es[1] * op_refs[1][sl, :] + es[2] * op_refs[2][sl, :]
        return num / (es[0] + es[1] + es[2])

    def sample_head(h):
        return os_ref[pl.ds(h, ROWS_S, stride=HEADS_PER_GROUP), :]

    def mix(y_ref, h_ref, head, m_ref):
        y, h = y_ref[...], h_ref[...]
        o_att = jnp.concatenate([head(hh).astype(BF16) for hh in range(HEADS_PER_GROUP)], axis=1)
        for c in range(D_MODEL // NC):
            sl = slice(c * NC, (c + 1) * NC)
            gl = slice(D_MODEL + c * NC, D_MODEL + (c + 1) * NC)
            conv = jnp.dot(y, wpw_ref[:, sl], preferred_element_type=F32) + bpw_ref[:, sl]
            att = jnp.dot(o_att, wo_ref[:, sl], preferred_element_type=F32)
            gate_conv = _sigmoid(jnp.dot(h, wg_ref[:, sl], preferred_element_type=F32))
            gate_att = _sigmoid(jnp.dot(h, wg_ref[:, gl], preferred_element_type=F32))
            m_ref[:, sl] = (gate_conv * conv + gate_att * att).astype(BF16)

    mix(yp_ref, hp_ref, combined_prompt_head, mp_ref)

    @pl.when(i == 0)
    def _():
        mix(ys_ref, hs_ref, sample_head, ms_ref)


def _mix_call(y_p, y_s, h_p, h_s, o_p, l_p, o_s, w_pw, b_pw, w_o, w_gates):
    pi = lambda i: (i, 0)
    zero = lambda i: (0, 0)
    hp = pl.BlockSpec((4 * TM2, HEAD_DIM), pi)
    hs = pl.BlockSpec((4 * ROWS_S, HEAD_DIM), zero)
    const = lambda shape: pl.BlockSpec(shape, zero, pipeline_mode=pl.Buffered(1))
    return pl.pallas_call(
        _mix_kernel,
        grid=(NP2,),
        in_specs=[
            pl.BlockSpec((TM2, D_CONV), pi), pl.BlockSpec((ROWS_S, D_CONV), zero),
            pl.BlockSpec((TM2, D_MODEL), pi), pl.BlockSpec((ROWS_S, D_MODEL), zero),
            hp, hp, hp, hp, hp, hp, hs,
            const((D_CONV, D_MODEL)), const((1, D_MODEL)), const((GROUP_COLS, D_MODEL)),
            const((D_MODEL, 2 * D_MODEL)),
        ],
        out_specs=[pl.BlockSpec((TM2, D_MODEL), pi), pl.BlockSpec((ROWS_S, D_MODEL), zero)],
        out_shape=[jax.ShapeDtypeStruct((ROWS_P, D_MODEL), BF16),
                   jax.ShapeDtypeStruct((ROWS_S, D_MODEL), BF16)],
        compiler_params=_params(("arbitrary",)),
        name="mix",
    )(y_p, y_s, h_p, h_s, *o_p, *l_p, o_s, w_pw, b_pw, w_o, w_gates)


def _outproj_kernel(mp_ref, ms_ref, xp_ref, xs_ref, w_ref, g_ref, op_ref, os_ref, hp_ref, hs_ref):
    i = pl.program_id(0)

    def project(m_ref, x_ref, o_ref, h_ref):
        x1 = x_ref[...] + jnp.dot(m_ref[...], w_ref[...], preferred_element_type=F32)
        o_ref[...] = x1
        h_ref[...] = _rms(x1, g_ref[...]).astype(BF16)

    project(mp_ref, xp_ref, op_ref, hp_ref)

    @pl.when(i == 0)
    def _():
        project(ms_ref, xs_ref, os_ref, hs_ref)


def _outproj_call(m_p, m_s, xp, xs, w_out, g_ffn):
    pi = lambda i: (i, 0)
    zero = lambda i: (0, 0)
    tile_p, tile_s = pl.BlockSpec((TM2, D_MODEL), pi), pl.BlockSpec((ROWS_S, D_MODEL), zero)
    return pl.pallas_call(
        _outproj_kernel,
        grid=(NP2,),
        in_specs=[tile_p, tile_s, tile_p, tile_s, pl.BlockSpec((D_MODEL, D_MODEL), zero),
                  pl.BlockSpec((1, D_MODEL), zero)],
        out_specs=[tile_p, tile_s, tile_p, tile_s],
        out_shape=[jax.ShapeDtypeStruct((ROWS_P, D_MODEL), F32), jax.ShapeDtypeStruct((ROWS_S, D_MODEL), F32),
                   jax.ShapeDtypeStruct((ROWS_P, D_MODEL), BF16), jax.ShapeDtypeStruct((ROWS_S, D_MODEL), BF16)],
        compiler_params=_params(("arbitrary",)),
        name="out_proj",
    )(m_p, m_s, xp, xs, w_out, g_ffn)


N_CACHE = 2 * N_GROUPS
COPY_STEPS = 2 * DEC_BATCH


def _shift_copies(c, w, cache_refs, new_refs, out_refs, bufs, sems):
    b = c // 2
    loads, stores = [], []
    for g in range(N_GROUPS):
        n = 2 * g + w
        keep = WINDOWS[g] * HEADS_PER_GROUP - NEW_ROWS
        loads.append(pltpu.make_async_copy(cache_refs[n].at[b, pl.ds(NEW_ROWS, keep)], bufs[g].at[w],
                                           sems.at[0, w, g]))
        stores.append(pltpu.make_async_copy(bufs[g].at[w], out_refs[n].at[b, pl.ds(0, keep)],
                                            sems.at[1, w, g]))
        stores.append(pltpu.make_async_copy(new_refs[n].at[b], out_refs[n].at[b, pl.ds(keep, NEW_ROWS)],
                                            sems.at[2, w, g]))
    return loads, stores


def _ffn_kernel(*refs):
    x_hbm, xs_ref, hp_ref, hs_ref, wgu_hbm, wd_hbm, gl_ref = refs[0:7]
    cache_refs = refs[7:7 + N_CACHE]
    new_refs = refs[7 + N_CACHE:7 + 2 * N_CACHE]
    yp_ref, ys_ref = refs[7 + 2 * N_CACHE:9 + 2 * N_CACHE]
    out_refs = refs[9 + 2 * N_CACHE:9 + 3 * N_CACHE]
    bufs = refs[9 + 3 * N_CACHE:9 + 3 * N_CACHE + N_GROUPS]
    wgu_buf, wd_buf, sems, x_sem, w_sems = refs[9 + 3 * N_CACHE + N_GROUPS:]
    i = pl.program_id(0)

    def weight_copies(chunk, slot):
        return (pltpu.make_async_copy(wgu_hbm.at[:, pl.ds(pl.multiple_of(chunk * 2 * TH, 2 * TH), 2 * TH)],
                                      wgu_buf.at[slot], w_sems.at[0, slot]),
                pltpu.make_async_copy(wd_hbm.at[pl.ds(pl.multiple_of(chunk * TH, TH), TH), :],
                                      wd_buf.at[slot], w_sems.at[1, slot]))

    @pl.when(i == 0)
    def _():
        for d in weight_copies(0, 0):
            d.start()
        ys_ref[...] = xs_ref[...]

    residual = pltpu.make_async_copy(x_hbm.at[pl.ds(pl.multiple_of(i * TM, TM), TM)], yp_ref, x_sem)
    residual.start()

    def hidden_chunk(k, carry):
        step = i * KH + k
        for w in range(2):
            @pl.when(step % 2 == w)
            def _():
                for d in weight_copies(k, w):
                    d.wait()

                @pl.when(step + 1 < NP * KH)
                def _():
                    for d in weight_copies(jnp.where(k + 1 == KH, 0, k + 1), 1 - w):
                        d.start()

                copies = lambda c, slot: _shift_copies(c, slot, cache_refs, new_refs, out_refs, bufs, sems)

                @pl.when((step == 0) & (step < COPY_STEPS))
                def _():
                    for d in copies(step, w)[0]:
                        d.start()

                @pl.when(step < COPY_STEPS)
                def _():
                    for d in copies(step, w)[0]:
                        d.wait()

                @pl.when((step >= 1) & (step <= COPY_STEPS))
                def _():
                    for d in copies(step - 1, 1 - w)[1]:
                        d.wait()

                @pl.when(step < COPY_STEPS)
                def _():
                    for d in copies(step, w)[1]:
                        d.start()

                @pl.when(step + 1 < COPY_STEPS)
                def _():
                    for d in copies(step + 1, 1 - w)[0]:
                        d.start()

        slot = step % 2

        def swiglu(h):
            a = jnp.dot(h, wgu_buf[slot, :, 0:TH], preferred_element_type=F32)
            u = jnp.dot(h, wgu_buf[slot, :, TH:2 * TH], preferred_element_type=F32)
            return (a * _sigmoid(a) * u).astype(BF16)

        act = swiglu(hp_ref[...])

        @pl.when(k == 0)
        def _():
            residual.wait()

        yp_ref[...] += jnp.dot(act, wd_buf[slot], preferred_element_type=F32)

        @pl.when(i == 0)
        def _():
            ys_ref[...] += jnp.dot(swiglu(hs_ref[...]), wd_buf[slot], preferred_element_type=F32)

        return carry

    lax.fori_loop(0, KH, hidden_chunk, 0)

    yp_ref[...] = _rms(yp_ref[...], gl_ref[...])

    @pl.when(i == 0)
    def _():
        ys_ref[...] = _rms(ys_ref[...], gl_ref[...])


def _ffn_call(x1_p, x1_s, h_p, h_s, w_gate_up, w_down, g_final, caches, new_rows):
    assert NP * KH >= COPY_STEPS
    pi = lambda i: (i, 0)
    zero = lambda i: (0, 0)
    hbm = pl.BlockSpec(memory_space=pl.ANY)
    return pl.pallas_call(
        _ffn_kernel,
        grid=(NP,),
        in_specs=[
            hbm,
            pl.BlockSpec((ROWS_S, D_MODEL), zero, pipeline_mode=pl.Buffered(1)),
            pl.BlockSpec((TM, D_MODEL), pi),
            pl.BlockSpec((ROWS_S, D_MODEL), zero, pipeline_mode=pl.Buffered(1)),
            hbm, hbm,
            pl.BlockSpec((1, D_MODEL), zero),
        ] + [hbm] * (2 * N_CACHE),
        out_specs=[pl.BlockSpec((TM, D_MODEL), pi), pl.BlockSpec((ROWS_S, D_MODEL), zero)] + [hbm] * N_CACHE,
        out_shape=[jax.ShapeDtypeStruct((ROWS_P, D_MODEL), F32),
                   jax.ShapeDtypeStruct((ROWS_S, D_MODEL), F32)]
        + [jax.ShapeDtypeStruct(c.shape, F32) for c in caches],
        scratch_shapes=[pltpu.VMEM((2, WINDOWS[g] * HEADS_PER_GROUP - NEW_ROWS, HEAD_DIM), F32)
                        for g in range(N_GROUPS)]
        + [pltpu.VMEM((2, D_MODEL, 2 * TH), BF16), pltpu.VMEM((2, TH, D_MODEL), BF16)]
        + [pltpu.SemaphoreType.DMA((3, 2, N_GROUPS)), pltpu.SemaphoreType.DMA(()),
           pltpu.SemaphoreType.DMA((2, 2))],
        compiler_params=_params(("arbitrary",), FFN_VMEM_LIMIT),
        name="ffn",
    )(x1_p, x1_s, h_p, h_s, w_gate_up, w_down, g_final, *caches, *new_rows)


def _rope_tables(pos):
    half = ROT_DIM // 2
    inv = np.power(ROPE_THETA, -np.arange(0, ROT_DIM, 2, dtype=np.float64) / ROT_DIM)
    ang = np.asarray(pos, np.float64)[:, None] * inv[None, :]
    cos, sin = np.cos(ang), np.sin(ang)
    n = ang.shape[0]
    ones = np.ones((n, HEAD_DIM - ROT_DIM))
    zeros = np.zeros((n, HEAD_DIM - ROT_DIM))
    zh = np.zeros((n, half))
    c = np.concatenate([cos, cos, ones], axis=1)
    s_lo = np.concatenate([-sin, zh, zeros], axis=1)
    s_hi = np.concatenate([zh, sin, zeros], axis=1)
    return jnp.asarray(np.stack([c, s_lo, s_hi]), F32)


def kernel(x_prompt, x_sample, state_conv, cache_k_w128, cache_v_w128, cache_k_w512, cache_v_w512,
           cache_k_w2048, cache_v_w2048, g_mix, w_in, b_glu, w_dw, b_dw, ln_g, ln_b, w_pw, b_pw,
           w_o_att, w_out, g_ffn, w_gate, w_up, w_down, g_final):
    xp = x_prompt.reshape(ROWS_P, D_MODEL)
    xs = x_sample.reshape(ROWS_S, D_MODEL)
    w_in2 = w_in.reshape(D_MODEL, IN_COLS)

    hp, hs = _rms_call(xp, xs, g_mix)
    u_p, u_s, w_gates16 = _glu_call(hp, hs, w_in2, b_glu)
    tab_p = _rope_tables(np.arange(SEQ))
    tab_s = _rope_tables(PAST_LEN + np.arange(ROWS_S) % DEC_SEQ)
    q_p, q_s = _q_call(hp, hs, w_in2, tab_p, tab_s)
    mixer_w = (w_out.reshape(D_MODEL, D_MODEL), w_pw.reshape(D_CONV, D_MODEL), w_o_att.reshape(GROUP_COLS, D_MODEL))
    kv = [_kv_call(hp, hs, w_in2, tab_p, tab_s, g, mixer_w[g]) for g in range(N_GROUPS)]
    w_out16, w_pw16, w_o16 = kv[0][4], kv[1][4], kv[2][4]

    w_dw2 = w_dw.reshape(CONV_WIDTH, D_CONV)
    y_p, w_gate_up16, w_down16 = _conv_p_call(
        u_p, w_dw2, b_dw, ln_g, ln_b, w_gate.reshape(D_MODEL, FFN_HIDDEN), w_up.reshape(D_MODEL, FFN_HIDDEN),
        w_down.reshape(FFN_HIDDEN, D_MODEL))
    y_s, conv_s = _conv_s_call(state_conv, u_s.reshape(DEC_BATCH, DEC_SEQ, D_CONV), w_dw2, b_dw, ln_g, ln_b)
    y_s = y_s.reshape(ROWS_S, D_CONV).astype(BF16)
    conv_p = u_p.reshape(BATCH, SEQ, D_CONV)[:, SEQ - (CONV_WIDTH - 1):][None]

    caches = ((cache_k_w128, cache_v_w128), (cache_k_w512, cache_v_w512), (cache_k_w2048, cache_v_w2048))
    o_p, l_p = [], []
    for g in range(N_GROUPS):
        o, l = _attn_p_call(q_p, kv[g][0], kv[g][2], g)
        o_p.append(o)
        l_p.append(l)
    o_s = _attn_s_call(q_s, kv, caches)

    m_p, m_s = _mix_call(y_p, y_s, hp, hs, o_p, l_p, o_s, w_pw16, b_pw, w_o16, w_gates16)
    x1_p, x1_s, hf_p, hf_s = _outproj_call(m_p, m_s, xp, xs, w_out16, g_ffn)

    flat = lambda g, a: a.reshape(DEC_BATCH, WINDOWS[g] * HEADS_PER_GROUP, HEAD_DIM)
    cache_flat = [flat(g, caches[g][t]) for g in range(N_GROUPS) for t in range(2)]
    new_rows = [kv[g][t].reshape(DEC_BATCH, NEW_ROWS, HEAD_DIM) for g in range(N_GROUPS) for t in (1, 3)]
    y_p2, y_s2, *shifted = _ffn_call(x1_p, x1_s, hf_p, hf_s, w_gate_up16, w_down16,
                                     g_final.reshape(1, D_MODEL), cache_flat, new_rows)

    cache_out = []
    for g in range(N_GROUPS):
        keep = min(WINDOWS[g], SEQ)
        for t in range(2):
            prompt_rows = kv[g][2 * t] if keep == SEQ else kv[g][5 + t]
            cache_out.append(prompt_rows.reshape(1, BATCH, keep, HEADS_PER_GROUP, HEAD_DIM))
            cache_out.append(shifted[2 * g + t].reshape(1, DEC_BATCH, WINDOWS[g], HEADS_PER_GROUP, HEAD_DIM))
    return (y_p2.reshape(BATCH, SEQ, D_MODEL), y_s2.reshape(DEC_BATCH, DEC_SEQ, D_MODEL),
            conv_p, conv_s, *cache_out)
```

```python
import jax
import jax.numpy as jnp
import numpy as np
from jax import lax
from jax.experimental import pallas as pl
from jax.experimental.pallas import tpu as pltpu

F32 = jnp.float32
BF16 = jnp.bfloat16

D_MODEL = 2048
BATCH = 4
SEQ = 2048
DEC_BATCH = 32
DEC_SEQ = 4
PAST_LEN = 8192
HEAD_DIM = 128
HEADS_PER_GROUP = 4
DILATIONS = (1, 4, 16)
WINDOWS = (128, 512, 2048)
N_GROUPS = 3
GROUP_COLS = HEADS_PER_GROUP * HEAD_DIM
ATT_WIDTH = N_GROUPS * GROUP_COLS
SPAN = 128
ATT_SCALE = HEAD_DIM ** -0.5
ROT_DIM = HEAD_DIM // 4
ROPE_THETA = 500000.0
D_CONV = D_MODEL // 2
CONV_WIDTH = 31
FFN_HIDDEN = 5632
NORM_EPS = 1e-6
LN_EPS = 1e-5
IN_COLS = 2 * D_CONV + 3 * ATT_WIDTH + 2 * D_MODEL

ROWS_P = BATCH * SEQ
ROWS_S = DEC_BATCH * DEC_SEQ
NEG = -1e30

VMEM_LIMIT = 56 * 1024 * 1024
FFN_VMEM_LIMIT = 62 * 1024 * 1024

TM = 1024
NP = ROWS_P // TM
TN = 512
NC = 256
TM2 = 512
NP2 = ROWS_P // TM2
TH = 512
KH = FFN_HIDDEN // TH
TC = 512
HALO = 32
CS = 4


def _sigmoid(x):
    return 1.0 / (1.0 + jnp.exp(-x))


def _rms(x, g):
    return x * lax.rsqrt(jnp.mean(x * x, axis=-1, keepdims=True) + NORM_EPS) * g


def _params(sem, vmem_limit=VMEM_LIMIT):
    return pltpu.CompilerParams(dimension_semantics=sem, vmem_limit_bytes=vmem_limit)


def _rms_kernel(xp_ref, xs_ref, g_ref, hp_ref, hs_ref):
    hp_ref[...] = _rms(xp_ref[...], g_ref[...]).astype(BF16)

    @pl.when(pl.program_id(0) == 0)
    def _():
        hs_ref[...] = _rms(xs_ref[...], g_ref[...]).astype(BF16)


def _rms_call(xp, xs, g):
    return pl.pallas_call(
        _rms_kernel,
        grid=(NP,),
        in_specs=[
            pl.BlockSpec((TM, D_MODEL), lambda i: (i, 0)),
            pl.BlockSpec((ROWS_S, D_MODEL), lambda i: (0, 0)),
            pl.BlockSpec((1, D_MODEL), lambda i: (0, 0)),
        ],
        out_specs=[
            pl.BlockSpec((TM, D_MODEL), lambda i: (i, 0)),
            pl.BlockSpec((ROWS_S, D_MODEL), lambda i: (0, 0)),
        ],
        out_shape=[
            jax.ShapeDtypeStruct((ROWS_P, D_MODEL), BF16),
            jax.ShapeDtypeStruct((ROWS_S, D_MODEL), BF16),
        ],
        compiler_params=_params(("arbitrary",)),
        name="rms_in",
    )(xp, xs, g)


N_GATE_BLOCKS = 2 * D_MODEL // TN


def _glu_kernel(hp_ref, hs_ref, wa_ref, wb_ref, ba_ref, bb_ref, *refs):
    gate32_refs = refs[:N_GATE_BLOCKS]
    up_ref, us_ref, gate16_ref, wa_s, wb_s = refs[N_GATE_BLOCKS:]
    i = pl.program_id(1)

    @pl.when(i == 0)
    def _():
        wa_s[...] = wa_ref[...].astype(BF16)
        wb_s[...] = wb_ref[...].astype(BF16)

    for c, g_ref in enumerate(gate32_refs):
        gate16_ref[:, c * TN:(c + 1) * TN] = g_ref[...].astype(BF16)

    def glu(h, u_ref):
        for c in range(TN // NC):
            sl = slice(c * NC, (c + 1) * NC)
            za = jnp.dot(h, wa_s[:, sl], preferred_element_type=F32) + ba_ref[:, sl]
            zb = jnp.dot(h, wb_s[:, sl], preferred_element_type=F32) + bb_ref[:, sl]
            u_ref[:, sl] = za * _sigmoid(zb)

    glu(hp_ref[...], up_ref)

    @pl.when(i == 0)
    def _():
        glu(hs_ref[...], us_ref)


def _glu_call(hp, hs, w_in, b_glu):
    nj = D_CONV // TN
    slab = D_MODEL // (nj * NP)
    gate0 = (2 * D_CONV + 3 * ATT_WIDTH) // TN
    slab_row = lambda j, i: j * NP + i
    return pl.pallas_call(
        _glu_kernel,
        grid=(nj, NP),
        in_specs=[
            pl.BlockSpec((TM, D_MODEL), lambda j, i: (i, 0)),
            pl.BlockSpec((ROWS_S, D_MODEL), lambda j, i: (0, 0)),
            pl.BlockSpec((D_MODEL, TN), lambda j, i: (0, j)),
            pl.BlockSpec((D_MODEL, TN), lambda j, i: (0, j + nj)),
            pl.BlockSpec((1, TN), lambda j, i: (0, j)),
            pl.BlockSpec((1, TN), lambda j, i: (0, j + nj)),
        ] + [pl.BlockSpec((slab, TN), lambda j, i, c=c: (slab_row(j, i), gate0 + c)) for c in range(N_GATE_BLOCKS)],
        out_specs=[
            pl.BlockSpec((TM, TN), lambda j, i: (i, j)),
            pl.BlockSpec((ROWS_S, TN), lambda j, i: (0, j)),
            pl.BlockSpec((slab, 2 * D_MODEL), lambda j, i: (slab_row(j, i), 0)),
        ],
        out_shape=[
            jax.ShapeDtypeStruct((ROWS_P, D_CONV), F32),
            jax.ShapeDtypeStruct((ROWS_S, D_CONV), F32),
            jax.ShapeDtypeStruct((D_MODEL, 2 * D_MODEL), BF16),
        ],
        scratch_shapes=[pltpu.VMEM((D_MODEL, TN), BF16), pltpu.VMEM((D_MODEL, TN), BF16)],
        compiler_params=_params(("arbitrary", "arbitrary")),
        name="in_glu",
    )(hp, hs, w_in, w_in, b_glu, b_glu, *([w_in] * N_GATE_BLOCKS))


def _project_heads(h, w_s, t_ref, o_ref, rows):
    if t_ref is not None:
        cos, sin_lo, sin_hi = t_ref[0], t_ref[1], t_ref[2]
    for c in range(TN // NC):
        z = jnp.dot(h, w_s[:, c * NC:(c + 1) * NC], preferred_element_type=F32)
        for hc in range(NC // HEAD_DIM):
            t = z[:, hc * HEAD_DIM:(hc + 1) * HEAD_DIM]
            if t_ref is not None:
                t = (t * cos + pltpu.roll(t, HEAD_DIM - ROT_DIM // 2, 1) * sin_lo
                     + pltpu.roll(t, ROT_DIM // 2, 1) * sin_hi)
            head = c * (NC // HEAD_DIM) + hc
            o_ref[pl.ds(head, rows, stride=HEADS_PER_GROUP), :] = t


def _q_kernel(hp_ref, hs_ref, w0_ref, w1_ref, w2_ref, tp_ref, ts_ref, op_ref, os_ref, w_s):
    i = pl.program_id(0)

    @pl.when(i == 0)
    def _():
        for g, w_ref in enumerate((w0_ref, w1_ref, w2_ref)):
            w_s[g] = w_ref[...].astype(BF16)

    h = hp_ref[...]
    for g in range(N_GROUPS):
        _project_heads(h, w_s.at[g], tp_ref, op_ref.at[g], TM)

    @pl.when(i == 0)
    def _():
        h = hs_ref[...]
        for g in range(N_GROUPS):
            _project_heads(h, w_s.at[g], ts_ref, os_ref.at[g], ROWS_S)


def _make_kv_kernel(tail_rows):
    tiles_per_seq = SEQ // TM

    def kern(hp_ref, hs_ref, wk_ref, wv_ref, tp_ref, ts_ref, slab32_ref, *refs):
        kp_ref, ks_ref, vp_ref, vs_ref, slab16_ref = refs[0:5]
        kt_ref, vt_ref = refs[5:7] if tail_rows else (None, None)
        wk_s, wv_s = refs[-2:]
        i = pl.program_id(0)

        @pl.when(i == 0)
        def _():
            wk_s[...] = wk_ref[...].astype(BF16)
            wv_s[...] = wv_ref[...].astype(BF16)

        slab16_ref[...] = slab32_ref[...].astype(BF16)
        h = hp_ref[...]
        _project_heads(h, wk_s, tp_ref, kp_ref.at[0], TM)
        _project_heads(h, wv_s, None, vp_ref.at[0], TM)

        if tail_rows:
            @pl.when(i % tiles_per_seq == tiles_per_seq - 1)
            def _():
                kt_ref[0] = kp_ref[0, 4 * TM - tail_rows:4 * TM, :]
                vt_ref[0] = vp_ref[0, 4 * TM - tail_rows:4 * TM, :]

        @pl.when(i == 0)
        def _():
            h = hs_ref[...]
            _project_heads(h, wk_s, ts_ref, ks_ref.at[0], ROWS_S)
            _project_heads(h, wv_s, None, vs_ref.at[0], ROWS_S)

    return kern


Q_PLANE0 = 2 * D_CONV // TN


def _q_call(hp, hs, w_in, tab_p, tab_s):
    wspec = lambda g: pl.BlockSpec((D_MODEL, TN), lambda i: (0, Q_PLANE0 + g), pipeline_mode=pl.Buffered(1))
    return pl.pallas_call(
        _q_kernel,
        grid=(NP,),
        in_specs=[
            pl.BlockSpec((TM, D_MODEL), lambda i: (i, 0)),
            pl.BlockSpec((ROWS_S, D_MODEL), lambda i: (0, 0)),
            wspec(0), wspec(1), wspec(2),
            pl.BlockSpec((3, TM, HEAD_DIM), lambda i: (0, i % (SEQ // TM), 0)),
            pl.BlockSpec((3, ROWS_S, HEAD_DIM), lambda i: (0, 0, 0)),
        ],
        out_specs=[
            pl.BlockSpec((N_GROUPS, 4 * TM, HEAD_DIM), lambda i: (0, i, 0)),
            pl.BlockSpec((N_GROUPS, 4 * ROWS_S, HEAD_DIM), lambda i: (0, 0, 0)),
        ],
        out_shape=[
            jax.ShapeDtypeStruct((N_GROUPS, 4 * ROWS_P, HEAD_DIM), F32),
            jax.ShapeDtypeStruct((N_GROUPS, 4 * ROWS_S, HEAD_DIM), F32),
        ],
        scratch_shapes=[pltpu.VMEM((N_GROUPS, D_MODEL, TN), BF16)],
        compiler_params=_params(("arbitrary",)),
        name="in_q",
    )(hp, hs, w_in, w_in, w_in, tab_p, tab_s)


def _kv_call(hp, hs, w_in, tab_p, tab_s, g, w_extra):
    wspec = lambda plane: pl.BlockSpec((D_MODEL, TN), lambda i: (0, Q_PLANE0 + plane), pipeline_mode=pl.Buffered(1))
    outp = pl.BlockSpec((1, 4 * TM, HEAD_DIM), lambda i: (0, i, 0))
    outs = pl.BlockSpec((1, 4 * ROWS_S, HEAD_DIM), lambda i: (0, 0, 0))
    shp = jax.ShapeDtypeStruct((1, 4 * ROWS_P, HEAD_DIM), F32)
    shs = jax.ShapeDtypeStruct((1, 4 * ROWS_S, HEAD_DIM), F32)
    slab = pl.BlockSpec((w_extra.shape[0] // NP, w_extra.shape[1]), lambda i: (i, 0))
    out_specs = [outp, outs, outp, outs, slab]
    out_shape = [shp, shs, shp, shs, jax.ShapeDtypeStruct(w_extra.shape, BF16)]
    tail_rows = HEADS_PER_GROUP * WINDOWS[g] if WINDOWS[g] < SEQ else 0
    if tail_rows:
        assert tail_rows <= 4 * TM
        tail = pl.BlockSpec((1, tail_rows, HEAD_DIM), lambda i: (i // (SEQ // TM), 0, 0))
        out_specs += [tail, tail]
        out_shape += [jax.ShapeDtypeStruct((BATCH, tail_rows, HEAD_DIM), F32)] * 2
    return pl.pallas_call(
        _make_kv_kernel(tail_rows),
        grid=(NP,),
        in_specs=[
            pl.BlockSpec((TM, D_MODEL), lambda i: (i, 0)),
            pl.BlockSpec((ROWS_S, D_MODEL), lambda i: (0, 0)),
            wspec(N_GROUPS + g), wspec(2 * N_GROUPS + g),
            pl.BlockSpec((3, TM, HEAD_DIM), lambda i: (0, i % (SEQ // TM), 0)),
            pl.BlockSpec((3, ROWS_S, HEAD_DIM), lambda i: (0, 0, 0)),
            slab,
        ],
        out_specs=out_specs,
        out_shape=out_shape,
        scratch_shapes=[pltpu.VMEM((D_MODEL, TN), BF16), pltpu.VMEM((D_MODEL, TN), BF16)],
        compiler_params=_params(("arbitrary",)),
        name="in_kv%d" % g,
    )(hp, hs, w_in, w_in, tab_p, tab_s, w_extra)


def _ln_silu(y, g, b):
    mu = jnp.mean(y, axis=-1, keepdims=True)
    yc = y - mu
    var = jnp.mean(yc * yc, axis=-1, keepdims=True)
    z = yc * lax.rsqrt(var + LN_EPS) * g + b
    return z * _sigmoid(z)


def _conv_p_kernel(cur_ref, prev_ref, w_ref, bdw_ref, g_ref, b_ref, wg32_ref, wu32_ref, wd32_ref,
                   y_ref, wgu16_ref, wd16_ref, ext, ypre):
    c = pl.program_id(1)
    n_lc = D_CONV // 128
    for k in range(KH):
        src = slice(k * TH, (k + 1) * TH)
        wgu16_ref[:, 2 * k * TH:(2 * k + 1) * TH] = wg32_ref[:, src].astype(BF16)
        wgu16_ref[:, (2 * k + 1) * TH:(2 * k + 2) * TH] = wu32_ref[:, src].astype(BF16)
    wd16_ref[...] = wd32_ref[...].astype(BF16)
    for lc in range(n_lc):
        sl = slice(lc * 128, (lc + 1) * 128)
        ext[lc, pl.ds(HALO, TC), :] = cur_ref[:, sl]

    @pl.when(c > 0)
    def _():
        for lc in range(n_lc):
            ext[lc, 0:HALO, :] = prev_ref[TC - HALO:TC, lc * 128:(lc + 1) * 128]

    @pl.when(c == 0)
    def _():
        for lc in range(n_lc):
            ext[lc, 0:HALO, :] = jnp.zeros((HALO, 128), F32)

    base = HALO - (CONV_WIDTH - 1)
    for lc in range(n_lc):
        sl = slice(lc * 128, (lc + 1) * 128)
        for t0 in range(CS):
            acc = jnp.zeros((TC // CS, 128), F32)
            for j in range(CONV_WIDTH):
                acc = acc + ext[lc, pl.ds(base + t0 + j, TC // CS, stride=CS), :] * w_ref[j:j + 1, sl]
            ypre[lc, pl.ds(t0, TC // CS, stride=CS), :] = acc + bdw_ref[:, sl]

    total = jnp.zeros((TC, 1), F32)
    for lc in range(n_lc):
        total = total + jnp.sum(ypre[lc], axis=-1, keepdims=True)
    mu = total * (1.0 / D_CONV)
    sq = jnp.zeros((TC, 1), F32)
    for lc in range(n_lc):
        yc = ypre[lc] - mu
        sq = sq + jnp.sum(yc * yc, axis=-1, keepdims=True)
    inv = lax.rsqrt(sq * (1.0 / D_CONV) + LN_EPS)
    for lc in range(n_lc):
        sl = slice(lc * 128, (lc + 1) * 128)
        z = (ypre[lc] - mu) * inv * g_ref[:, sl] + b_ref[:, sl]
        y_ref[:, sl] = (z * _sigmoid(z)).astype(BF16)


def _conv_p_call(u_p, w_dw, b_dw, ln_g, ln_b, w_gate, w_up, w_down):
    nc = SEQ // TC
    steps = BATCH * nc
    vec = pl.BlockSpec((1, D_CONV), lambda b, c: (0, 0))
    slab = lambda rows, cols: pl.BlockSpec((rows // steps, cols), lambda b, c: (b * nc + c, 0))
    return pl.pallas_call(
        _conv_p_kernel,
        grid=(BATCH, nc),
        in_specs=[
            pl.BlockSpec((TC, D_CONV), lambda b, c: (b * nc + c, 0)),
            pl.BlockSpec((TC, D_CONV), lambda b, c: (b * nc + jnp.maximum(c - 1, 0), 0)),
            pl.BlockSpec((CONV_WIDTH, D_CONV), lambda b, c: (0, 0)),
            vec, vec, vec,
            slab(D_MODEL, FFN_HIDDEN), slab(D_MODEL, FFN_HIDDEN), slab(FFN_HIDDEN, D_MODEL),
        ],
        out_specs=[pl.BlockSpec((TC, D_CONV), lambda b, c: (b * nc + c, 0)),
                   slab(D_MODEL, 2 * FFN_HIDDEN), slab(FFN_HIDDEN, D_MODEL)],
        out_shape=[jax.ShapeDtypeStruct((ROWS_P, D_CONV), BF16),
                   jax.ShapeDtypeStruct((D_MODEL, 2 * FFN_HIDDEN), BF16),
                   jax.ShapeDtypeStruct((FFN_HIDDEN, D_MODEL), BF16)],
        scratch_shapes=[pltpu.VMEM((D_CONV // 128, HALO + TC, 128), F32),
                        pltpu.VMEM((D_CONV // 128, TC, 128), F32)],
        compiler_params=_params(("arbitrary", "arbitrary")),
        name="conv_prompt",
    )(u_p, u_p, w_dw, b_dw, ln_g, ln_b, w_gate, w_up, w_down)


CONV_SB = 8


def _conv_s_kernel(state_ref, u_ref, w_ref, bdw_ref, g_ref, b_ref, y_ref, new_state_ref, ext):
    past = CONV_WIDTH - 1
    ext[:, 0:past, :] = state_ref[0]
    ext[:, past:past + DEC_SEQ, :] = u_ref[...]
    new_state_ref[0] = ext[:, DEC_SEQ:past + DEC_SEQ, :]
    acc = jnp.zeros((CONV_SB, DEC_SEQ, D_CONV), F32)
    for j in range(CONV_WIDTH):
        acc = acc + ext[:, pl.ds(j, DEC_SEQ), :] * w_ref[j:j + 1, :]
    y_ref[...] = _ln_silu(acc + bdw_ref[...], g_ref[...], b_ref[...])


def _conv_s_call(state_conv, u_s, w_dw, b_dw, ln_g, ln_b):
    vec = pl.BlockSpec((1, D_CONV), lambda b: (0, 0))
    past = CONV_WIDTH - 1
    state = pl.BlockSpec((1, CONV_SB, past, D_CONV), lambda b: (0, b, 0, 0))
    rows = pl.BlockSpec((CONV_SB, DEC_SEQ, D_CONV), lambda b: (b, 0, 0))
    return pl.pallas_call(
        _conv_s_kernel,
        grid=(DEC_BATCH // CONV_SB,),
        in_specs=[state, rows, pl.BlockSpec((CONV_WIDTH, D_CONV), lambda b: (0, 0)), vec, vec, vec],
        out_specs=[rows, state],
        out_shape=[jax.ShapeDtypeStruct((DEC_BATCH, DEC_SEQ, D_CONV), F32),
                   jax.ShapeDtypeStruct(state_conv.shape, F32)],
        scratch_shapes=[pltpu.VMEM((CONV_SB, past + DEC_SEQ, D_CONV), F32)],
        compiler_params=_params(("arbitrary",)),
        name="conv_sample",
    )(state_conv, u_s, w_dw, b_dw, ln_g, ln_b)


def _softmax_block(q, k, v, mask):
    s = lax.dot_general(q.astype(BF16), k.astype(BF16), (((2,), (2,)), ((0,), (0,))),
                        preferred_element_type=F32) * ATT_SCALE
    s = jnp.where(mask[None], s, NEG)
    m = jnp.max(s, axis=-1, keepdims=True)
    p = jnp.exp(s - m)
    den = jnp.sum(p, axis=-1, keepdims=True)
    o = lax.dot_general(p.astype(BF16), v.astype(BF16), (((2,), (1,)), ((0,), (0,))),
                        preferred_element_type=F32) / den
    lse = m + jnp.log(den)
    return o, jnp.broadcast_to(lse, o.shape)


def _make_attn_p_kernel(dil):
    stride = HEADS_PER_GROUP * dil
    nblk = SEQ // dil // SPAN
    blk_rows = SPAN * stride
    pair = min(dil, 2)
    group_rows = pair * HEADS_PER_GROUP
    n_r2 = dil // pair

    def kern(q_ref, k_ref, v_ref, o_ref, l_ref):
        qi = lax.broadcasted_iota(jnp.int32, (SPAN, SPAN), 0)
        ki = lax.broadcasted_iota(jnp.int32, (SPAN, SPAN), 1)
        mask_first = ki <= qi
        qi2 = lax.broadcasted_iota(jnp.int32, (SPAN, 2 * SPAN), 0)
        ki2 = lax.broadcasted_iota(jnp.int32, (SPAN, 2 * SPAN), 1)
        mask_band = (ki2 >= qi2) & (ki2 <= qi2 + SPAN)

        def block(base, key_base, n_keys, mask):
            qs = [pl.ds(base + off, SPAN, stride=stride) for off in range(group_rows)]
            ks = [pl.ds(key_base + off, n_keys, stride=stride) for off in range(group_rows)]
            o, lse = _softmax_block(jnp.stack([q_ref[0, s, :] for s in qs]),
                                    jnp.stack([k_ref[0, s, :] for s in ks]),
                                    jnp.stack([v_ref[0, s, :] for s in ks]), mask)
            for u, s in enumerate(qs):
                o_ref[s, :] = o[u]
                l_ref[s, :] = lse[u]

        def first(r2, carry):
            base = pl.multiple_of(r2 * group_rows, max(group_rows, 8))
            block(base, base, SPAN, mask_first)
            return carry

        def band(t, carry):
            jb = 1 + t // n_r2
            base = pl.multiple_of(jb * blk_rows + (t % n_r2) * group_rows, max(group_rows, 8))
            block(base, base - blk_rows, 2 * SPAN, mask_band)
            return carry

        lax.fori_loop(0, n_r2, first, 0)
        if nblk > 1:
            lax.fori_loop(0, n_r2 * (nblk - 1), band, 0)

    return kern


def _attn_p_wide_kernel(q_ref, k_ref, v_ref, o_ref, l_ref):
    qi = lax.broadcasted_iota(jnp.int32, (SPAN, SPAN), 0)
    ki = lax.broadcasted_iota(jnp.int32, (SPAN, SPAN), 1)
    mask_first = ki <= qi
    qi2 = lax.broadcasted_iota(jnp.int32, (SPAN, 2 * SPAN), 0)
    ki2 = lax.broadcasted_iota(jnp.int32, (SPAN, 2 * SPAN), 1)
    mask_band = (ki2 >= qi2) & (ki2 <= qi2 + SPAN)
    nblk = q_ref.shape[2] // SPAN

    def tile(c, carry):
        q8 = jnp.swapaxes(q_ref[0, 0, :, c], 0, 1)
        k8 = jnp.swapaxes(k_ref[0, 0, :, c], 0, 1)
        v8 = jnp.swapaxes(v_ref[0, 0, :, c], 0, 1)
        outs = [_softmax_block(q8[:, 0:SPAN], k8[:, 0:SPAN], v8[:, 0:SPAN], mask_first)]
        for jb in range(1, nblk):
            own = slice(jb * SPAN, (jb + 1) * SPAN)
            keys = slice((jb - 1) * SPAN, (jb + 1) * SPAN)
            outs.append(_softmax_block(q8[:, own], k8[:, keys], v8[:, keys], mask_band))
        o = outs[0][0] if nblk == 1 else jnp.concatenate([o for o, _ in outs], axis=1)
        lse = outs[0][1] if nblk == 1 else jnp.concatenate([l for _, l in outs], axis=1)
        o_ref[0, :, c] = jnp.swapaxes(o, 0, 1)
        l_ref[0, :, c] = jnp.swapaxes(lse, 0, 1)
        return carry

    lax.fori_loop(0, q_ref.shape[3], tile, 0)


def _attn_p_call(q_p, k_p, v_p, g):
    dil = DILATIONS[g]
    rows = 4 * SEQ
    if HEADS_PER_GROUP * dil % 8 == 0 and dil > 1:
        tiles = HEADS_PER_GROUP * dil // 8
        n_el = SEQ // dil
        view = lambda a: a.reshape(a.shape[0], BATCH, n_el, tiles, 8, HEAD_DIM)
        blk = lambda plane: pl.BlockSpec((1, 1, n_el, tiles, 8, HEAD_DIM), lambda b: (plane, b, 0, 0, 0, 0))
        out = pl.BlockSpec((1, n_el, tiles, 8, HEAD_DIM), lambda b: (b, 0, 0, 0, 0))
        o, l = pl.pallas_call(
            _attn_p_wide_kernel,
            grid=(BATCH,),
            in_specs=[blk(g), blk(0), blk(0)],
            out_specs=[out, out],
            out_shape=[jax.ShapeDtypeStruct((BATCH, n_el, tiles, 8, HEAD_DIM), F32)] * 2,
            compiler_params=_params(("arbitrary",)),
            name="attn_prompt_g%d" % g,
        )(view(q_p), view(k_p), view(v_p))
        return o.reshape(4 * ROWS_P, HEAD_DIM), l.reshape(4 * ROWS_P, HEAD_DIM)
    blk = lambda plane: pl.BlockSpec((1, rows, HEAD_DIM), lambda b: (plane, b, 0))
    out = pl.BlockSpec((rows, HEAD_DIM), lambda b: (b, 0))
    return pl.pallas_call(
        _make_attn_p_kernel(dil),
        grid=(BATCH,),
        in_specs=[blk(g), blk(0), blk(0)],
        out_specs=[out, out],
        out_shape=[jax.ShapeDtypeStruct((4 * ROWS_P, HEAD_DIM), F32)] * 2,
        compiler_params=_params(("arbitrary",)),
        name="attn_prompt_g%d" % g,
    )(q_p, k_p, v_p)


NEW_ROWS = DEC_SEQ * HEADS_PER_GROUP


def _joint_softmax(parts):
    mx = None
    for s, _ in parts:
        m = jnp.max(s, axis=0) if s.ndim == 3 else s
        mx = m if mx is None else jnp.maximum(mx, m)
    den = jnp.zeros((8, 1), F32)
    acc = jnp.zeros((8, HEAD_DIM), F32)
    for s, v in parts:
        p = jnp.exp(s - mx)
        if s.ndim == 3:
            den = den + jnp.sum(p, axis=0)
            acc = acc + jnp.sum(p * v, axis=0)
        else:
            den = den + p
            acc = acc + p * v
    return acc / den


def _score(q, k):
    return jnp.sum(q * k, axis=-1, keepdims=True) * ATT_SCALE


ATT_SB = 4


def _attn_s_kernel(q_ref, kn0, vn0, kn1, vn1, kn2, vn2, k0, v0, k1, v1, k2, v2, o_ref, kc, vc):
    nb = WINDOWS[0] * HEADS_PER_GROUP
    half = HEADS_PER_GROUP
    n_even, n_odd = SPAN // 2 + 1, SPAN // 2
    for s in range(ATT_SB):
        new = slice(s * NEW_ROWS, (s + 1) * NEW_ROWS)
        kc[s, 0:nb, :] = k0[s]
        kc[s, nb:nb + NEW_ROWS, :] = kn0[0, new, :]
        vc[s, 0:nb, :] = v0[s]
        vc[s, nb:nb + NEW_ROWS, :] = vn0[0, new, :]
        for j in range(2):
            rows = slice(8 * j, 8 * j + 8)
            qrows = slice(s * NEW_ROWS + 8 * j, s * NEW_ROWS + 8 * j + 8)
            parts = []
            q = q_ref[0, qrows, :]
            for start, n in ((8 * j, n_even), (8 * j + half, n_odd)):
                k3 = kc[s, pl.ds(start, 8 * n), :].reshape(n, 8, HEAD_DIM)
                v3 = vc[s, pl.ds(start, 8 * n), :].reshape(n, 8, HEAD_DIM)
                parts.append((_score(q[None], k3), v3))
            for g, (k_ref, v_ref, kn_ref, vn_ref) in ((1, (k1, v1, kn1, vn1)), (2, (k2, v2, kn2, vn2))):
                q = q_ref[g, qrows, :]
                parts.append((_score(q[None], k_ref[s, :, rows, :]), v_ref[s, :, rows, :]))
                parts.append((_score(q, kn_ref[0, qrows, :]), vn_ref[0, qrows, :]))
            o_ref[qrows, :] = _joint_softmax(parts)


def _attn_s_call(q_s, kv, caches):
    step_rows = ATT_SB * NEW_ROWS
    new = pl.BlockSpec((1, step_rows, HEAD_DIM), lambda b: (0, b, 0))
    in_specs = [pl.BlockSpec((N_GROUPS, step_rows, HEAD_DIM), lambda b: (0, b, 0))] + [new] * (2 * N_GROUPS)
    args = [q_s] + [kv[g][t] for g in range(N_GROUPS) for t in (1, 3)]
    for g in range(N_GROUPS):
        width = HEADS_PER_GROUP * DILATIONS[g]
        if DILATIONS[g] == 1:
            shape = (DEC_BATCH, WINDOWS[g] * HEADS_PER_GROUP, HEAD_DIM)
            spec = pl.BlockSpec((ATT_SB,) + shape[1:], lambda b: (b, 0, 0))
        else:
            shape = (DEC_BATCH, SPAN, width, HEAD_DIM)
            spec = pl.BlockSpec((ATT_SB, SPAN, NEW_ROWS, HEAD_DIM), lambda b: (b, 0, 0, 0))
        in_specs += [spec, spec]
        args += [caches[g][0].reshape(shape), caches[g][1].reshape(shape)]
    rows0 = WINDOWS[0] * HEADS_PER_GROUP + NEW_ROWS
    return pl.pallas_call(
        _attn_s_kernel,
        grid=(DEC_BATCH // ATT_SB,),
        in_specs=in_specs,
        out_specs=pl.BlockSpec((step_rows, HEAD_DIM), lambda b: (b, 0)),
        out_shape=jax.ShapeDtypeStruct((4 * ROWS_S, HEAD_DIM), F32),
        scratch_shapes=[pltpu.VMEM((ATT_SB, rows0, HEAD_DIM), F32)] * 2,
        compiler_params=_params(("arbitrary",)),
        name="attn_sample",
    )(*args)


def _mix_kernel(*refs):
    (yp_ref, ys_ref, hp_ref, hs_ref) = refs[0:4]
    op_refs, lp_refs = refs[4:7], refs[7:10]
    os_ref, wpw_ref, bpw_ref, wo_ref, wg_ref, mp_ref, ms_ref = refs[10:17]
    i = pl.program_id(0)

    def combined_prompt_head(h):
        sl = pl.ds(h, TM2, stride=HEADS_PER_GROUP)
        ls = [l[sl, :] for l in lp_refs]
        mx = jnp.maximum(jnp.maximum(ls[0], ls[1]), ls[2])
        es = [jnp.exp(l - mx) for l in ls]
        num = es[0] * op_refs[0][sl, :] + es[1] * op_refs[1][sl, :] + es[2] * op_refs[2][sl, :]
        return num / (es[0] + es[1] + es[2])

    def sample_head(h):
        return os_ref[pl.ds(h, ROWS_S, stride=HEADS_PER_GROUP), :]

    def mix(y_ref, h_ref, head, m_ref):
        y, h = y_ref[...], h_ref[...]
        o_att = jnp.concatenate([head(hh).astype(BF16) for hh in range(HEADS_PER_GROUP)], axis=1)
        for c in range(D_MODEL // NC):
            sl = slice(c * NC, (c + 1) * NC)
            gl = slice(D_MODEL + c * NC, D_MODEL + (c + 1) * NC)
            conv = jnp.dot(y, wpw_ref[:, sl], preferred_element_type=F32) + bpw_ref[:, sl]
            att = jnp.dot(o_att, wo_ref[:, sl], preferred_element_type=F32)
            gate_conv = _sigmoid(jnp.dot(h, wg_ref[:, sl], preferred_element_type=F32))
            gate_att = _sigmoid(jnp.dot(h, wg_ref[:, gl], preferred_element_type=F32))
            m_ref[:, sl] = (gate_conv * conv + gate_att * att).astype(BF16)

    mix(yp_ref, hp_ref, combined_prompt_head, mp_ref)

    @pl.when(i == 0)
    def _():
        mix(ys_ref, hs_ref, sample_head, ms_ref)


def _mix_call(y_p, y_s, h_p, h_s, o_p, l_p, o_s, w_pw, b_pw, w_o, w_gates):
    pi = lambda i: (i, 0)
    zero = lambda i: (0, 0)
    hp = pl.BlockSpec((4 * TM2, HEAD_DIM), pi)
    hs = pl.BlockSpec((4 * ROWS_S, HEAD_DIM), zero)
    const = lambda shape: pl.BlockSpec(shape, zero, pipeline_mode=pl.Buffered(1))
    return pl.pallas_call(
        _mix_kernel,
        grid=(NP2,),
        in_specs=[
            pl.BlockSpec((TM2, D_CONV), pi), pl.BlockSpec((ROWS_S, D_CONV), zero),
            pl.BlockSpec((TM2, D_MODEL), pi), pl.BlockSpec((ROWS_S, D_MODEL), zero),
            hp, hp, hp, hp, hp, hp, hs,
            const((D_CONV, D_MODEL)), const((1, D_MODEL)), const((GROUP_COLS, D_MODEL)),
            const((D_MODEL, 2 * D_MODEL)),
        ],
        out_specs=[pl.BlockSpec((TM2, D_MODEL), pi), pl.BlockSpec((ROWS_S, D_MODEL), zero)],
        out_shape=[jax.ShapeDtypeStruct((ROWS_P, D_MODEL), BF16),
                   jax.ShapeDtypeStruct((ROWS_S, D_MODEL), BF16)],
        compiler_params=_params(("arbitrary",)),
        name="mix",
    )(y_p, y_s, h_p, h_s, *o_p, *l_p, o_s, w_pw, b_pw, w_o, w_gates)


def _outproj_kernel(mp_ref, ms_ref, xp_ref, xs_ref, w_ref, g_ref, op_ref, os_ref, hp_ref, hs_ref):
    i = pl.program_id(0)

    def project(m_ref, x_ref, o_ref, h_ref):
        x1 = x_ref[...] + jnp.dot(m_ref[...], w_ref[...], preferred_element_type=F32)
        o_ref[...] = x1
        h_ref[...] = _rms(x1, g_ref[...]).astype(BF16)

    project(mp_ref, xp_ref, op_ref, hp_ref)

    @pl.when(i == 0)
    def _():
        project(ms_ref, xs_ref, os_ref, hs_ref)


def _outproj_call(m_p, m_s, xp, xs, w_out, g_ffn):
    pi = lambda i: (i, 0)
    zero = lambda i: (0, 0)
    tile_p, tile_s = pl.BlockSpec((TM2, D_MODEL), pi), pl.BlockSpec((ROWS_S, D_MODEL), zero)
    return pl.pallas_call(
        _outproj_kernel,
        grid=(NP2,),
        in_specs=[tile_p, tile_s, tile_p, tile_s, pl.BlockSpec((D_MODEL, D_MODEL), zero),
                  pl.BlockSpec((1, D_MODEL), zero)],
        out_specs=[tile_p, tile_s, tile_p, tile_s],
        out_shape=[jax.ShapeDtypeStruct((ROWS_P, D_MODEL), F32), jax.ShapeDtypeStruct((ROWS_S, D_MODEL), F32),
                   jax.ShapeDtypeStruct((ROWS_P, D_MODEL), BF16), jax.ShapeDtypeStruct((ROWS_S, D_MODEL), BF16)],
        compiler_params=_params(("arbitrary",)),
        name="out_proj",
    )(m_p, m_s, xp, xs, w_out, g_ffn)


N_CACHE = 2 * N_GROUPS
COPY_STEPS = 2 * DEC_BATCH


def _shift_copies(c, w, cache_refs, new_refs, out_refs, bufs, sems):
    b = c // 2
    loads, stores = [], []
    for g in range(N_GROUPS):
        n = 2 * g + w
        keep = WINDOWS[g] * HEADS_PER_GROUP - NEW_ROWS
        loads.append(pltpu.make_async_copy(cache_refs[n].at[b, pl.ds(NEW_ROWS, keep)], bufs[g].at[w],
                                           sems.at[0, w, g]))
        stores.append(pltpu.make_async_copy(bufs[g].at[w], out_refs[n].at[b, pl.ds(0, keep)],
                                            sems.at[1, w, g]))
        stores.append(pltpu.make_async_copy(new_refs[n].at[b], out_refs[n].at[b, pl.ds(keep, NEW_ROWS)],
                                            sems.at[2, w, g]))
    return loads, stores


def _ffn_kernel(*refs):
    x_hbm, xs_ref, hp_ref, hs_ref, wgu_hbm, wd_hbm, gl_ref = refs[0:7]
    cache_refs = refs[7:7 + N_CACHE]
    new_refs = refs[7 + N_CACHE:7 + 2 * N_CACHE]
    yp_ref, ys_ref = refs[7 + 2 * N_CACHE:9 + 2 * N_CACHE]
    out_refs = refs[9 + 2 * N_CACHE:9 + 3 * N_CACHE]
    bufs = refs[9 + 3 * N_CACHE:9 + 3 * N_CACHE + N_GROUPS]
    wgu_buf, wd_buf, sems, x_sem, w_sems = refs[9 + 3 * N_CACHE + N_GROUPS:]
    i = pl.program_id(0)

    def weight_copies(chunk, slot):
        return (pltpu.make_async_copy(wgu_hbm.at[:, pl.ds(pl.multiple_of(chunk * 2 * TH, 2 * TH), 2 * TH)],
                                      wgu_buf.at[slot], w_sems.at[0, slot]),
                pltpu.make_async_copy(wd_hbm.at[pl.ds(pl.multiple_of(chunk * TH, TH), TH), :],
                                      wd_buf.at[slot], w_sems.at[1, slot]))

    @pl.when(i == 0)
    def _():
        for d in weight_copies(0, 0):
            d.start()
        ys_ref[...] = xs_ref[...]

    residual = pltpu.make_async_copy(x_hbm.at[pl.ds(pl.multiple_of(i * TM, TM), TM)], yp_ref, x_sem)
    residual.start()

    def hidden_chunk(k, carry):
        step = i * KH + k
        for w in range(2):
            @pl.when(step % 2 == w)
            def _():
                for d in weight_copies(k, w):
                    d.wait()

                @pl.when(step + 1 < NP * KH)
                def _():
                    for d in weight_copies(jnp.where(k + 1 == KH, 0, k + 1), 1 - w):
                        d.start()

                copies = lambda c, slot: _shift_copies(c, slot, cache_refs, new_refs, out_refs, bufs, sems)

                @pl.when((step == 0) & (step < COPY_STEPS))
                def _():
                    for d in copies(step, w)[0]:
                        d.start()

                @pl.when(step < COPY_STEPS)
                def _():
                    for d in copies(step, w)[0]:
                        d.wait()

                @pl.when((step >= 1) & (step <= COPY_STEPS))
                def _():
                    for d in copies(step - 1, 1 - w)[1]:
                        d.wait()

                @pl.when(step < COPY_STEPS)
                def _():
                    for d in copies(step, w)[1]:
                        d.start()

                @pl.when(step + 1 < COPY_STEPS)
                def _():
                    for d in copies(step + 1, 1 - w)[0]:
                        d.start()

        slot = step % 2

        def swiglu(h):
            a = jnp.dot(h, wgu_buf[slot, :, 0:TH], preferred_element_type=F32)
            u = jnp.dot(h, wgu_buf[slot, :, TH:2 * TH], preferred_element_type=F32)
            return (a * _sigmoid(a) * u).astype(BF16)

        act = swiglu(hp_ref[...])

        @pl.when(k == 0)
        def _():
            residual.wait()

        yp_ref[...] += jnp.dot(act, wd_buf[slot], preferred_element_type=F32)

        @pl.when(i == 0)
        def _():
            ys_ref[...] += jnp.dot(swiglu(hs_ref[...]), wd_buf[slot], preferred_element_type=F32)

        return carry

    lax.fori_loop(0, KH, hidden_chunk, 0)

    yp_ref[...] = _rms(yp_ref[...], gl_ref[...])

    @pl.when(i == 0)
    def _():
        ys_ref[...] = _rms(ys_ref[...], gl_ref[...])


def _ffn_call(x1_p, x1_s, h_p, h_s, w_gate_up, w_down, g_final, caches, new_rows):
    assert NP * KH >= COPY_STEPS
    pi = lambda i: (i, 0)
    zero = lambda i: (0, 0)
    hbm = pl.BlockSpec(memory_space=pl.ANY)
    return pl.pallas_call(
        _ffn_kernel,
        grid=(NP,),
        in_specs=[
            hbm,
            pl.BlockSpec((ROWS_S, D_MODEL), zero, pipeline_mode=pl.Buffered(1)),
            pl.BlockSpec((TM, D_MODEL), pi),
            pl.BlockSpec((ROWS_S, D_MODEL), zero, pipeline_mode=pl.Buffered(1)),
            hbm, hbm,
            pl.BlockSpec((1, D_MODEL), zero),
        ] + [hbm] * (2 * N_CACHE),
        out_specs=[pl.BlockSpec((TM, D_MODEL), pi), pl.BlockSpec((ROWS_S, D_MODEL), zero)] + [hbm] * N_CACHE,
        out_shape=[jax.ShapeDtypeStruct((ROWS_P, D_MODEL), F32),
                   jax.ShapeDtypeStruct((ROWS_S, D_MODEL), F32)]
        + [jax.ShapeDtypeStruct(c.shape, F32) for c in caches],
        scratch_shapes=[pltpu.VMEM((2, WINDOWS[g] * HEADS_PER_GROUP - NEW_ROWS, HEAD_DIM), F32)
                        for g in range(N_GROUPS)]
        + [pltpu.VMEM((2, D_MODEL, 2 * TH), BF16), pltpu.VMEM((2, TH, D_MODEL), BF16)]
        + [pltpu.SemaphoreType.DMA((3, 2, N_GROUPS)), pltpu.SemaphoreType.DMA(()),
           pltpu.SemaphoreType.DMA((2, 2))],
        compiler_params=_params(("arbitrary",), FFN_VMEM_LIMIT),
        name="ffn",
    )(x1_p, x1_s, h_p, h_s, w_gate_up, w_down, g_final, *caches, *new_rows)


def _rope_tables(pos):
    half = ROT_DIM // 2
    inv = np.power(ROPE_THETA, -np.arange(0, ROT_DIM, 2, dtype=np.float64) / ROT_DIM)
    ang = np.asarray(pos, np.float64)[:, None] * inv[None, :]
    cos, sin = np.cos(ang), np.sin(ang)
    n = ang.shape[0]
    ones = np.ones((n, HEAD_DIM - ROT_DIM))
    zeros = np.zeros((n, HEAD_DIM - ROT_DIM))
    zh = np.zeros((n, half))
    c = np.concatenate([cos, cos, ones], axis=1)
    s_lo = np.concatenate([-sin, zh, zeros], axis=1)
    s_hi = np.concatenate([zh, sin, zeros], axis=1)
    return jnp.asarray(np.stack([c, s_lo, s_hi]), F32)


def kernel(x_prompt, x_sample, state_conv, cache_k_w128, cache_v_w128, cache_k_w512, cache_v_w512,
           cache_k_w2048, cache_v_w2048, g_mix, w_in, b_glu, w_dw, b_dw, ln_g, ln_b, w_pw, b_pw,
           w_o_att, w_out, g_ffn, w_gate, w_up, w_down, g_final):
    xp = x_prompt.reshape(ROWS_P, D_MODEL)
    xs = x_sample.reshape(ROWS_S, D_MODEL)
    w_in2 = w_in.reshape(D_MODEL, IN_COLS)

    hp, hs = _rms_call(xp, xs, g_mix)
    u_p, u_s, w_gates16 = _glu_call(hp, hs, w_in2, b_glu)
    tab_p = _rope_tables(np.arange(SEQ))
    tab_s = _rope_tables(PAST_LEN + np.arange(ROWS_S) % DEC_SEQ)
    q_p, q_s = _q_call(hp, hs, w_in2, tab_p, tab_s)
    mixer_w = (w_out.reshape(D_MODEL, D_MODEL), w_pw.reshape(D_CONV, D_MODEL), w_o_att.reshape(GROUP_COLS, D_MODEL))
    kv = [_kv_call(hp, hs, w_in2, tab_p, tab_s, g, mixer_w[g]) for g in range(N_GROUPS)]
    w_out16, w_pw16, w_o16 = kv[0][4], kv[1][4], kv[2][4]

    w_dw2 = w_dw.reshape(CONV_WIDTH, D_CONV)
    y_p, w_gate_up16, w_down16 = _conv_p_call(
        u_p, w_dw2, b_dw, ln_g, ln_b, w_gate.reshape(D_MODEL, FFN_HIDDEN), w_up.reshape(D_MODEL, FFN_HIDDEN),
        w_down.reshape(FFN_HIDDEN, D_MODEL))
    y_s, conv_s = _conv_s_call(state_conv, u_s.reshape(DEC_BATCH, DEC_SEQ, D_CONV), w_dw2, b_dw, ln_g, ln_b)
    y_s = y_s.reshape(ROWS_S, D_CONV).astype(BF16)
    conv_p = u_p.reshape(BATCH, SEQ, D_CONV)[:, SEQ - (CONV_WIDTH - 1):][None]

    caches = ((cache_k_w128, cache_v_w128), (cache_k_w512, cache_v_w512), (cache_k_w2048, cache_v_w2048))
    o_p, l_p = [], []
    for g in range(N_GROUPS):
        o, l = _attn_p_call(q_p, kv[g][0], kv[g][2], g)
        o_p.append(o)
        l_p.append(l)
    o_s = _attn_s_call(q_s, kv, caches)

    m_p, m_s = _mix_call(y_p, y_s, hp, hs, o_p, l_p, o_s, w_pw16, b_pw, w_o16, w_gates16)
    x1_p, x1_s, hf_p, hf_s = _outproj_call(m_p, m_s, xp, xs, w_out16, g_ffn)

    flat = lambda g, a: a.reshape(DEC_BATCH, WINDOWS[g] * HEADS_PER_GROUP, HEAD_DIM)
    cache_flat = [flat(g, caches[g][t]) for g in range(N_GROUPS) for t in range(2)]
    new_rows = [kv[g][t].reshape(DEC_BATCH, NEW_ROWS, HEAD_DIM) for g in range(N_GROUPS) for t in (1, 3)]
    y_p2, y_s2, *shifted = _ffn_call(x1_p, x1_s, hf_p, hf_s, w_gate_up16, w_down16,
                                     g_final.reshape(1, D_MODEL), cache_flat, new_rows)

    cache_out = []
    for g in range(N_GROUPS):
        keep = min(WINDOWS[g], SEQ)
        for t in range(2):
            prompt_rows = kv[g][2 * t] if keep == SEQ else kv[g][5 + t]
            cache_out.append(prompt_rows.reshape(1, BATCH, keep, HEADS_PER_GROUP, HEAD_DIM))
            cache_out.append(shifted[2 * g + t].reshape(1, DEC_BATCH, WINDOWS[g], HEADS_PER_GROUP, HEAD_DIM))
    return (y_p2.reshape(BATCH, SEQ, D_MODEL), y_s2.reshape(DEC_BATCH, DEC_SEQ, D_MODEL),
            conv_p, conv_s, *cache_out)
```

```python
import jax
import jax.numpy as jnp
import numpy as np
from jax import lax
from jax.experimental import pallas as pl
from jax.experimental.pallas import tpu as pltpu

F32 = jnp.float32
BF16 = jnp.bfloat16

D_MODEL = 2048
BATCH = 4
SEQ = 2048
DEC_BATCH = 32
DEC_SEQ = 4
PAST_LEN = 8192
HEAD_DIM = 128
HEADS_PER_GROUP = 4
DILATIONS = (1, 4, 16)
WINDOWS = (128, 512, 2048)
N_GROUPS = 3
GROUP_COLS = HEADS_PER_GROUP * HEAD_DIM
ATT_WIDTH = N_GROUPS * GROUP_COLS
SPAN = 128
ATT_SCALE = HEAD_DIM ** -0.5
ROT_DIM = HEAD_DIM // 4
ROPE_THETA = 500000.0
D_CONV = D_MODEL // 2
CONV_WIDTH = 31
FFN_HIDDEN = 5632
NORM_EPS = 1e-6
LN_EPS = 1e-5
IN_COLS = 2 * D_CONV + 3 * ATT_WIDTH + 2 * D_MODEL

ROWS_P = BATCH * SEQ
ROWS_S = DEC_BATCH * DEC_SEQ
NEG = -1e30

VMEM_LIMIT = 56 * 1024 * 1024
FFN_VMEM_LIMIT = 62 * 1024 * 1024

TM = 1024
NP = ROWS_P // TM
TN = 512
NC = 256
TM2 = 512
NP2 = ROWS_P // TM2
TH = 512
KH = FFN_HIDDEN // TH
TC = 512
HALO = 32
CS = 4


def _sigmoid(x):
    return 1.0 / (1.0 + jnp.exp(-x))


def _rms(x, g):
    return x * lax.rsqrt(jnp.mean(x * x, axis=-1, keepdims=True) + NORM_EPS) * g


def _params(sem, vmem_limit=VMEM_LIMIT):
    return pltpu.CompilerParams(dimension_semantics=sem, vmem_limit_bytes=vmem_limit)


def _rms_kernel(xp_ref, xs_ref, g_ref, hp_ref, hs_ref):
    hp_ref[...] = _rms(xp_ref[...], g_ref[...]).astype(BF16)

    @pl.when(pl.program_id(0) == 0)
    def _():
        hs_ref[...] = _rms(xs_ref[...], g_ref[...]).astype(BF16)


def _rms_call(xp, xs, g):
    return pl.pallas_call(
        _rms_kernel,
        grid=(NP,),
        in_specs=[
            pl.BlockSpec((TM, D_MODEL), lambda i: (i, 0)),
            pl.BlockSpec((ROWS_S, D_MODEL), lambda i: (0, 0)),
            pl.BlockSpec((1, D_MODEL), lambda i: (0, 0)),
        ],
        out_specs=[
            pl.BlockSpec((TM, D_MODEL), lambda i: (i, 0)),
            pl.BlockSpec((ROWS_S, D_MODEL), lambda i: (0, 0)),
        ],
        out_shape=[
            jax.ShapeDtypeStruct((ROWS_P, D_MODEL), BF16),
            jax.ShapeDtypeStruct((ROWS_S, D_MODEL), BF16),
        ],
        compiler_params=_params(("arbitrary",)),
        name="rms_in",
    )(xp, xs, g)


N_GATE_BLOCKS = 2 * D_MODEL // TN


def _glu_kernel(hp_ref, hs_ref, wa_ref, wb_ref, ba_ref, bb_ref, *refs):
    gate32_refs = refs[:N_GATE_BLOCKS]
    up_ref, us_ref, gate16_ref, wa_s, wb_s = refs[N_GATE_BLOCKS:]
    i = pl.program_id(1)

    @pl.when(i == 0)
    def _():
        wa_s[...] = wa_ref[...].astype(BF16)
        wb_s[...] = wb_ref[...].astype(BF16)

    for c, g_ref in enumerate(gate32_refs):
        gate16_ref[:, c * TN:(c + 1) * TN] = g_ref[...].astype(BF16)

    def glu(h, u_ref):
        for c in range(TN // NC):
            sl = slice(c * NC, (c + 1) * NC)
            za = jnp.dot(h, wa_s[:, sl], preferred_element_type=F32) + ba_ref[:, sl]
            zb = jnp.dot(h, wb_s[:, sl], preferred_element_type=F32) + bb_ref[:, sl]
            u_ref[:, sl] = za * _sigmoid(zb)

    glu(hp_ref[...], up_ref)

    @pl.when(i == 0)
    def _():
        glu(hs_ref[...], us_ref)


def _glu_call(hp, hs, w_in, b_glu):
    nj = D_CONV // TN
    slab = D_MODEL // (nj * NP)
    gate0 = (2 * D_CONV + 3 * ATT_WIDTH) // TN
    slab_row = lambda j, i: j * NP + i
    return pl.pallas_call(
        _glu_kernel,
        grid=(nj, NP),
        in_specs=[
            pl.BlockSpec((TM, D_MODEL), lambda j, i: (i, 0)),
            pl.BlockSpec((ROWS_S, D_MODEL), lambda j, i: (0, 0)),
            pl.BlockSpec((D_MODEL, TN), lambda j, i: (0, j)),
            pl.BlockSpec((D_MODEL, TN), lambda j, i: (0, j + nj)),
            pl.BlockSpec((1, TN), lambda j, i: (0, j)),
            pl.BlockSpec((1, TN), lambda j, i: (0, j + nj)),
        ] + [pl.BlockSpec((slab, TN), lambda j, i, c=c: (slab_row(j, i), gate0 + c)) for c in range(N_GATE_BLOCKS)],
        out_specs=[
            pl.BlockSpec((TM, TN), lambda j, i: (i, j)),
            pl.BlockSpec((ROWS_S, TN), lambda j, i: (0, j)),
            pl.BlockSpec((slab, 2 * D_MODEL), lambda j, i: (slab_row(j, i), 0)),
        ],
        out_shape=[
            jax.ShapeDtypeStruct((ROWS_P, D_CONV), F32),
            jax.ShapeDtypeStruct((ROWS_S, D_CONV), F32),
            jax.ShapeDtypeStruct((D_MODEL, 2 * D_MODEL), BF16),
        ],
        scratch_shapes=[pltpu.VMEM((D_MODEL, TN), BF16), pltpu.VMEM((D_MODEL, TN), BF16)],
        compiler_params=_params(("arbitrary", "arbitrary")),
        name="in_glu",
    )(hp, hs, w_in, w_in, b_glu, b_glu, *([w_in] * N_GATE_BLOCKS))


def _project_heads(h, w_s, t_ref, o_ref, rows):
    if t_ref is not None:
        cos, sin_lo, sin_hi = t_ref[0], t_ref[1], t_ref[2]
    for c in range(TN // NC):
        z = jnp.dot(h, w_s[:, c * NC:(c + 1) * NC], preferred_element_type=F32)
        for hc in range(NC // HEAD_DIM):
            t = z[:, hc * HEAD_DIM:(hc + 1) * HEAD_DIM]
            if t_ref is not None:
                t = (t * cos + pltpu.roll(t, HEAD_DIM - ROT_DIM // 2, 1) * sin_lo
                     + pltpu.roll(t, ROT_DIM // 2, 1) * sin_hi)
            head = c * (NC // HEAD_DIM) + hc
            o_ref[pl.ds(head, rows, stride=HEADS_PER_GROUP), :] = t


def _q_kernel(hp_ref, hs_ref, w0_ref, w1_ref, w2_ref, tp_ref, ts_ref, op_ref, os_ref, w_s):
    i = pl.program_id(0)

    @pl.when(i == 0)
    def _():
        for g, w_ref in enumerate((w0_ref, w1_ref, w2_ref)):
            w_s[g] = w_ref[...].astype(BF16)

    h = hp_ref[...]
    for g in range(N_GROUPS):
        _project_heads(h, w_s.at[g], tp_ref, op_ref.at[g], TM)

    @pl.when(i == 0)
    def _():
        h = hs_ref[...]
        for g in range(N_GROUPS):
            _project_heads(h, w_s.at[g], ts_ref, os_ref.at[g], ROWS_S)


def _make_kv_kernel(tail_rows):
    tiles_per_seq = SEQ // TM

    def kern(hp_ref, hs_ref, wk_ref, wv_ref, tp_ref, ts_ref, slab32_ref, *refs):
        kp_ref, ks_ref, vp_ref, vs_ref, slab16_ref = refs[0:5]
        kt_ref, vt_ref = refs[5:7] if tail_rows else (None, None)
        wk_s, wv_s = refs[-2:]
        i = pl.program_id(0)

        @pl.when(i == 0)
        def _():
            wk_s[...] = wk_ref[...].astype(BF16)
            wv_s[...] = wv_ref[...].astype(BF16)

        slab16_ref[...] = slab32_ref[...].astype(BF16)
        h = hp_ref[...]
        _project_heads(h, wk_s, tp_ref, kp_ref.at[0], TM)
        _project_heads(h, wv_s, None, vp_ref.at[0], TM)

        if tail_rows:
            @pl.when(i % tiles_per_seq == tiles_per_seq - 1)
            def _():
                kt_ref[0] = kp_ref[0, 4 * TM - tail_rows:4 * TM, :]
                vt_ref[0] = vp_ref[0, 4 * TM - tail_rows:4 * TM, :]

        @pl.when(i == 0)
        def _():
            h = hs_ref[...]
            _project_heads(h, wk_s, ts_ref, ks_ref.at[0], ROWS_S)
            _project_heads(h, wv_s, None, vs_ref.at[0], ROWS_S)

    return kern


Q_PLANE0 = 2 * D_CONV // TN


def _q_call(hp, hs, w_in, tab_p, tab_s):
    wspec = lambda g: pl.BlockSpec((D_MODEL, TN), lambda i: (0, Q_PLANE0 + g), pipeline_mode=pl.Buffered(1))
    return pl.pallas_call(
        _q_kernel,
        grid=(NP,),
        in_specs=[
            pl.BlockSpec((TM, D_MODEL), lambda i: (i, 0)),
            pl.BlockSpec((ROWS_S, D_MODEL), lambda i: (0, 0)),
            wspec(0), wspec(1), wspec(2),
            pl.BlockSpec((3, TM, HEAD_DIM), lambda i: (0, i % (SEQ // TM), 0)),
            pl.BlockSpec((3, ROWS_S, HEAD_DIM), lambda i: (0, 0, 0)),
        ],
        out_specs=[
            pl.BlockSpec((N_GROUPS, 4 * TM, HEAD_DIM), lambda i: (0, i, 0)),
            pl.BlockSpec((N_GROUPS, 4 * ROWS_S, HEAD_DIM), lambda i: (0, 0, 0)),
        ],
        out_shape=[
            jax.ShapeDtypeStruct((N_GROUPS, 4 * ROWS_P, HEAD_DIM), F32),
            jax.ShapeDtypeStruct((N_GROUPS, 4 * ROWS_S, HEAD_DIM), F32),
        ],
        scratch_shapes=[pltpu.VMEM((N_GROUPS, D_MODEL, TN), BF16)],
        compiler_params=_params(("arbitrary",)),
        name="in_q",
    )(hp, hs, w_in, w_in, w_in, tab_p, tab_s)


def _kv_call(hp, hs, w_in, tab_p, tab_s, g, w_extra):
    wspec = lambda plane: pl.BlockSpec((D_MODEL, TN), lambda i: (0, Q_PLANE0 + plane), pipeline_mode=pl.Buffered(1))
    outp = pl.BlockSpec((1, 4 * TM, HEAD_DIM), lambda i: (0, i, 0))
    outs = pl.BlockSpec((1, 4 * ROWS_S, HEAD_DIM), lambda i: (0, 0, 0))
    shp = jax.ShapeDtypeStruct((1, 4 * ROWS_P, HEAD_DIM), F32)
    shs = jax.ShapeDtypeStruct((1, 4 * ROWS_S, HEAD_DIM), F32)
    slab = pl.BlockSpec((w_extra.shape[0] // NP, w_extra.shape[1]), lambda i: (i, 0))
    out_specs = [outp, outs, outp, outs, slab]
    out_shape = [shp, shs, shp, shs, jax.ShapeDtypeStruct(w_extra.shape, BF16)]
    tail_rows = HEADS_PER_GROUP * WINDOWS[g] if WINDOWS[g] < SEQ else 0
    if tail_rows:
        assert tail_rows <= 4 * TM
        tail = pl.BlockSpec((1, tail_rows, HEAD_DIM), lambda i: (i // (SEQ // TM), 0, 0))
        out_specs += [tail, tail]
        out_shape += [jax.ShapeDtypeStruct((BATCH, tail_rows, HEAD_DIM), F32)] * 2
    return pl.pallas_call(
        _make_kv_kernel(tail_rows),
        grid=(NP,),
        in_specs=[
            pl.BlockSpec((TM, D_MODEL), lambda i: (i, 0)),
            pl.BlockSpec((ROWS_S, D_MODEL), lambda i: (0, 0)),
            wspec(N_GROUPS + g), wspec(2 * N_GROUPS + g),
            pl.BlockSpec((3, TM, HEAD_DIM), lambda i: (0, i % (SEQ // TM), 0)),
            pl.BlockSpec((3, ROWS_S, HEAD_DIM), lambda i: (0, 0, 0)),
            slab,
        ],
        out_specs=out_specs,
        out_shape=out_shape,
        scratch_shapes=[pltpu.VMEM((D_MODEL, TN), BF16), pltpu.VMEM((D_MODEL, TN), BF16)],
        compiler_params=_params(("arbitrary",)),
        name="in_kv%d" % g,
    )(hp, hs, w_in, w_in, tab_p, tab_s, w_extra)


def _ln_silu(y, g, b):
    mu = jnp.mean(y, axis=-1, keepdims=True)
    yc = y - mu
    var = jnp.mean(yc * yc, axis=-1, keepdims=True)
    z = yc * lax.rsqrt(var + LN_EPS) * g + b
    return z * _sigmoid(z)


def _conv_p_kernel(cur_ref, prev_ref, w_ref, bdw_ref, g_ref, b_ref, wg32_ref, wu32_ref, wd32_ref,
                   y_ref, wgu16_ref, wd16_ref, ext, ypre):
    c = pl.program_id(1)
    n_lc = D_CONV // 128
    for k in range(KH):
        src = slice(k * TH, (k + 1) * TH)
        wgu16_ref[:, 2 * k * TH:(2 * k + 1) * TH] = wg32_ref[:, src].astype(BF16)
        wgu16_ref[:, (2 * k + 1) * TH:(2 * k + 2) * TH] = wu32_ref[:, src].astype(BF16)
    wd16_ref[...] = wd32_ref[...].astype(BF16)
    for lc in range(n_lc):
        sl = slice(lc * 128, (lc + 1) * 128)
        ext[lc, pl.ds(HALO, TC), :] = cur_ref[:, sl]

    @pl.when(c > 0)
    def _():
        for lc in range(n_lc):
            ext[lc, 0:HALO, :] = prev_ref[:, lc * 128:(lc + 1) * 128]

    @pl.when(c == 0)
    def _():
        for lc in range(n_lc):
            ext[lc, 0:HALO, :] = jnp.zeros((HALO, 128), F32)

    base = HALO - (CONV_WIDTH - 1)
    for lc in range(n_lc):
        sl = slice(lc * 128, (lc + 1) * 128)
        for t0 in range(CS):
            acc = jnp.zeros((TC // CS, 128), F32)
            for j in range(CONV_WIDTH):
                acc = acc + ext[lc, pl.ds(base + t0 + j, TC // CS, stride=CS), :] * w_ref[j:j + 1, sl]
            ypre[lc, pl.ds(t0, TC // CS, stride=CS), :] = acc + bdw_ref[:, sl]

    total = jnp.zeros((TC, 1), F32)
    for lc in range(n_lc):
        total = total + jnp.sum(ypre[lc], axis=-1, keepdims=True)
    mu = total * (1.0 / D_CONV)
    sq = jnp.zeros((TC, 1), F32)
    for lc in range(n_lc):
        yc = ypre[lc] - mu
        sq = sq + jnp.sum(yc * yc, axis=-1, keepdims=True)
    inv = lax.rsqrt(sq * (1.0 / D_CONV) + LN_EPS)
    for lc in range(n_lc):
        sl = slice(lc * 128, (lc + 1) * 128)
        z = (ypre[lc] - mu) * inv * g_ref[:, sl] + b_ref[:, sl]
        y_ref[:, sl] = (z * _sigmoid(z)).astype(BF16)


def _conv_p_call(u_p, w_dw, b_dw, ln_g, ln_b, w_gate, w_up, w_down):
    nc = SEQ // TC
    steps = BATCH * nc
    vec = pl.BlockSpec((1, D_CONV), lambda b, c: (0, 0))
    slab = lambda rows, cols: pl.BlockSpec((rows // steps, cols), lambda b, c: (b * nc + c, 0))
    return pl.pallas_call(
        _conv_p_kernel,
        grid=(BATCH, nc),
        in_specs=[
            pl.BlockSpec((TC, D_CONV), lambda b, c: (b * nc + c, 0)),
            pl.BlockSpec((HALO, D_CONV), lambda b, c: (jnp.maximum((b * nc + c) * (TC // HALO) - 1, 0), 0)),
            pl.BlockSpec((CONV_WIDTH, D_CONV), lambda b, c: (0, 0)),
            vec, vec, vec,
            slab(D_MODEL, FFN_HIDDEN), slab(D_MODEL, FFN_HIDDEN), slab(FFN_HIDDEN, D_MODEL),
        ],
        out_specs=[pl.BlockSpec((TC, D_CONV), lambda b, c: (b * nc + c, 0)),
                   slab(D_MODEL, 2 * FFN_HIDDEN), slab(FFN_HIDDEN, D_MODEL)],
        out_shape=[jax.ShapeDtypeStruct((ROWS_P, D_CONV), BF16),
                   jax.ShapeDtypeStruct((D_MODEL, 2 * FFN_HIDDEN), BF16),
                   jax.ShapeDtypeStruct((FFN_HIDDEN, D_MODEL), BF16)],
        scratch_shapes=[pltpu.VMEM((D_CONV // 128, HALO + TC, 128), F32),
                        pltpu.VMEM((D_CONV // 128, TC, 128), F32)],
        compiler_params=_params(("arbitrary", "arbitrary")),
        name="conv_prompt",
    )(u_p, u_p, w_dw, b_dw, ln_g, ln_b, w_gate, w_up, w_down)


CONV_SB = 8


def _conv_s_kernel(state_ref, u_ref, w_ref, bdw_ref, g_ref, b_ref, y_ref, new_state_ref, ext):
    past = CONV_WIDTH - 1
    ext[:, 0:past, :] = state_ref[0]
    ext[:, past:past + DEC_SEQ, :] = u_ref[...]
    new_state_ref[0] = ext[:, DEC_SEQ:past + DEC_SEQ, :]
    acc = jnp.zeros((CONV_SB, DEC_SEQ, D_CONV), F32)
    for j in range(CONV_WIDTH):
        acc = acc + ext[:, pl.ds(j, DEC_SEQ), :] * w_ref[j:j + 1, :]
    y_ref[...] = _ln_silu(acc + bdw_ref[...], g_ref[...], b_ref[...])


def _conv_s_call(state_conv, u_s, w_dw, b_dw, ln_g, ln_b):
    vec = pl.BlockSpec((1, D_CONV), lambda b: (0, 0))
    past = CONV_WIDTH - 1
    state = pl.BlockSpec((1, CONV_SB, past, D_CONV), lambda b: (0, b, 0, 0))
    rows = pl.BlockSpec((CONV_SB, DEC_SEQ, D_CONV), lambda b: (b, 0, 0))
    return pl.pallas_call(
        _conv_s_kernel,
        grid=(DEC_BATCH // CONV_SB,),
        in_specs=[state, rows, pl.BlockSpec((CONV_WIDTH, D_CONV), lambda b: (0, 0)), vec, vec, vec],
        out_specs=[rows, state],
        out_shape=[jax.ShapeDtypeStruct((DEC_BATCH, DEC_SEQ, D_CONV), F32),
                   jax.ShapeDtypeStruct(state_conv.shape, F32)],
        scratch_shapes=[pltpu.VMEM((CONV_SB, past + DEC_SEQ, D_CONV), F32)],
        compiler_params=_params(("arbitrary",)),
        name="conv_sample",
    )(state_conv, u_s, w_dw, b_dw, ln_g, ln_b)


def _softmax_block(q, k, v, mask):
    s = lax.dot_general(q.astype(BF16), k.astype(BF16), (((2,), (2,)), ((0,), (0,))),
                        preferred_element_type=F32) * ATT_SCALE
    s = jnp.where(mask[None], s, NEG)
    m = jnp.max(s, axis=-1, keepdims=True)
    p = jnp.exp(s - m)
    den = jnp.sum(p, axis=-1, keepdims=True)
    o = lax.dot_general(p.astype(BF16), v.astype(BF16), (((2,), (1,)), ((0,), (0,))),
                        preferred_element_type=F32) / den
    lse = m + jnp.log(den)
    return o, jnp.broadcast_to(lse, o.shape)


def _make_attn_p_kernel(dil):
    stride = HEADS_PER_GROUP * dil
    nblk = SEQ // dil // SPAN
    blk_rows = SPAN * stride
    pair = min(dil, 2)
    group_rows = pair * HEADS_PER_GROUP
    n_r2 = dil // pair

    def kern(q_ref, k_ref, v_ref, o_ref, l_ref):
        qi = lax.broadcasted_iota(jnp.int32, (SPAN, SPAN), 0)
        ki = lax.broadcasted_iota(jnp.int32, (SPAN, SPAN), 1)
        mask_first = ki <= qi
        qi2 = lax.broadcasted_iota(jnp.int32, (SPAN, 2 * SPAN), 0)
        ki2 = lax.broadcasted_iota(jnp.int32, (SPAN, 2 * SPAN), 1)
        mask_band = (ki2 >= qi2) & (ki2 <= qi2 + SPAN)

        def block(base, key_base, n_keys, mask):
            qs = [pl.ds(base + off, SPAN, stride=stride) for off in range(group_rows)]
            ks = [pl.ds(key_base + off, n_keys, stride=stride) for off in range(group_rows)]
            o, lse = _softmax_block(jnp.stack([q_ref[0, s, :] for s in qs]),
                                    jnp.stack([k_ref[0, s, :] for s in ks]),
                                    jnp.stack([v_ref[0, s, :] for s in ks]), mask)
            for u, s in enumerate(qs):
                o_ref[s, :] = o[u]
                l_ref[s, :] = lse[u]

        def first(r2, carry):
            base = pl.multiple_of(r2 * group_rows, max(group_rows, 8))
            block(base, base, SPAN, mask_first)
            return carry

        def band(t, carry):
            jb = 1 + t // n_r2
            base = pl.multiple_of(jb * blk_rows + (t % n_r2) * group_rows, max(group_rows, 8))
            block(base, base - blk_rows, 2 * SPAN, mask_band)
            return carry

        lax.fori_loop(0, n_r2, first, 0)
        if nblk > 1:
            lax.fori_loop(0, n_r2 * (nblk - 1), band, 0)

    return kern


def _attn_p_wide_kernel(q_ref, k_ref, v_ref, o_ref, l_ref):
    qi = lax.broadcasted_iota(jnp.int32, (SPAN, SPAN), 0)
    ki = lax.broadcasted_iota(jnp.int32, (SPAN, SPAN), 1)
    mask_first = ki <= qi
    qi2 = lax.broadcasted_iota(jnp.int32, (SPAN, 2 * SPAN), 0)
    ki2 = lax.broadcasted_iota(jnp.int32, (SPAN, 2 * SPAN), 1)
    mask_band = (ki2 >= qi2) & (ki2 <= qi2 + SPAN)
    nblk = q_ref.shape[2] // SPAN

    def tile(c, carry):
        q8 = jnp.swapaxes(q_ref[0, 0, :, c], 0, 1)
        k8 = jnp.swapaxes(k_ref[0, 0, :, c], 0, 1)
        v8 = jnp.swapaxes(v_ref[0, 0, :, c], 0, 1)
        outs = [_softmax_block(q8[:, 0:SPAN], k8[:, 0:SPAN], v8[:, 0:SPAN], mask_first)]
        for jb in range(1, nblk):
            own = slice(jb * SPAN, (jb + 1) * SPAN)
            keys = slice((jb - 1) * SPAN, (jb + 1) * SPAN)
            outs.append(_softmax_block(q8[:, own], k8[:, keys], v8[:, keys], mask_band))
        o = outs[0][0] if nblk == 1 else jnp.concatenate([o for o, _ in outs], axis=1)
        lse = outs[0][1] if nblk == 1 else jnp.concatenate([l for _, l in outs], axis=1)
        o_ref[0, :, c] = jnp.swapaxes(o, 0, 1)
        l_ref[0, :, c] = jnp.swapaxes(lse, 0, 1)
        return carry

    lax.fori_loop(0, q_ref.shape[3], tile, 0)


def _attn_p_call(q_p, k_p, v_p, g):
    dil = DILATIONS[g]
    rows = 4 * SEQ
    if HEADS_PER_GROUP * dil % 8 == 0 and dil > 1:
        tiles = HEADS_PER_GROUP * dil // 8
        n_el = SEQ // dil
        view = lambda a: a.reshape(a.shape[0], BATCH, n_el, tiles, 8, HEAD_DIM)
        blk = lambda plane: pl.BlockSpec((1, 1, n_el, tiles, 8, HEAD_DIM), lambda b: (plane, b, 0, 0, 0, 0))
        out = pl.BlockSpec((1, n_el, tiles, 8, HEAD_DIM), lambda b: (b, 0, 0, 0, 0))
        o, l = pl.pallas_call(
            _attn_p_wide_kernel,
            grid=(BATCH,),
            in_specs=[blk(g), blk(0), blk(0)],
            out_specs=[out, out],
            out_shape=[jax.ShapeDtypeStruct((BATCH, n_el, tiles, 8, HEAD_DIM), F32)] * 2,
            compiler_params=_params(("arbitrary",)),
            name="attn_prompt_g%d" % g,
        )(view(q_p), view(k_p), view(v_p))
        return o.reshape(4 * ROWS_P, HEAD_DIM), l.reshape(4 * ROWS_P, HEAD_DIM)
    blk = lambda plane: pl.BlockSpec((1, rows, HEAD_DIM), lambda b: (plane, b, 0))
    out = pl.BlockSpec((rows, HEAD_DIM), lambda b: (b, 0))
    return pl.pallas_call(
        _make_attn_p_kernel(dil),
        grid=(BATCH,),
        in_specs=[blk(g), blk(0), blk(0)],
        out_specs=[out, out],
        out_shape=[jax.ShapeDtypeStruct((4 * ROWS_P, HEAD_DIM), F32)] * 2,
        compiler_params=_params(("arbitrary",)),
        name="attn_prompt_g%d" % g,
    )(q_p, k_p, v_p)


NEW_ROWS = DEC_SEQ * HEADS_PER_GROUP


def _joint_softmax(parts):
    mx = None
    for s, _ in parts:
        m = jnp.max(s, axis=0) if s.ndim == 3 else s
        mx = m if mx is None else jnp.maximum(mx, m)
    den = jnp.zeros((8, 1), F32)
    acc = jnp.zeros((8, HEAD_DIM), F32)
    for s, v in parts:
        p = jnp.exp(s - mx)
        if s.ndim == 3:
            den = den + jnp.sum(p, axis=0)
            acc = acc + jnp.sum(p * v, axis=0)
        else:
            den = den + p
            acc = acc + p * v
    return acc / den


def _score(q, k):
    return jnp.sum(q * k, axis=-1, keepdims=True) * ATT_SCALE


ATT_SB = 4


def _attn_s_kernel(q_ref, kn0, vn0, kn1, vn1, kn2, vn2, k0, v0, k1, v1, k2, v2, o_ref, kc, vc):
    nb = WINDOWS[0] * HEADS_PER_GROUP
    half = HEADS_PER_GROUP
    n_even, n_odd = SPAN // 2 + 1, SPAN // 2
    for s in range(ATT_SB):
        new = slice(s * NEW_ROWS, (s + 1) * NEW_ROWS)
        kc[s, 0:nb, :] = k0[s]
        kc[s, nb:nb + NEW_ROWS, :] = kn0[0, new, :]
        vc[s, 0:nb, :] = v0[s]
        vc[s, nb:nb + NEW_ROWS, :] = vn0[0, new, :]
        for j in range(2):
            rows = slice(8 * j, 8 * j + 8)
            qrows = slice(s * NEW_ROWS + 8 * j, s * NEW_ROWS + 8 * j + 8)
            parts = []
            q = q_ref[0, qrows, :]
            for start, n in ((8 * j, n_even), (8 * j + half, n_odd)):
                k3 = kc[s, pl.ds(start, 8 * n), :].reshape(n, 8, HEAD_DIM)
                v3 = vc[s, pl.ds(start, 8 * n), :].reshape(n, 8, HEAD_DIM)
                parts.append((_score(q[None], k3), v3))
            for g, (k_ref, v_ref, kn_ref, vn_ref) in ((1, (k1, v1, kn1, vn1)), (2, (k2, v2, kn2, vn2))):
                q = q_ref[g, qrows, :]
                parts.append((_score(q[None], k_ref[s, :, rows, :]), v_ref[s, :, rows, :]))
                parts.append((_score(q, kn_ref[0, qrows, :]), vn_ref[0, qrows, :]))
            o_ref[qrows, :] = _joint_softmax(parts)


def _attn_s_call(q_s, kv, caches):
    step_rows = ATT_SB * NEW_ROWS
    new = pl.BlockSpec((1, step_rows, HEAD_DIM), lambda b: (0, b, 0))
    in_specs = [pl.BlockSpec((N_GROUPS, step_rows, HEAD_DIM), lambda b: (0, b, 0))] + [new] * (2 * N_GROUPS)
    args = [q_s] + [kv[g][t] for g in range(N_GROUPS) for t in (1, 3)]
    for g in range(N_GROUPS):
        width = HEADS_PER_GROUP * DILATIONS[g]
        if DILATIONS[g] == 1:
            shape = (DEC_BATCH, WINDOWS[g] * HEADS_PER_GROUP, HEAD_DIM)
            spec = pl.BlockSpec((ATT_SB,) + shape[1:], lambda b: (b, 0, 0))
        else:
            shape = (DEC_BATCH, SPAN, width, HEAD_DIM)
            spec = pl.BlockSpec((ATT_SB, SPAN, NEW_ROWS, HEAD_DIM), lambda b: (b, 0, 0, 0))
        in_specs += [spec, spec]
        args += [caches[g][0].reshape(shape), caches[g][1].reshape(shape)]
    rows0 = WINDOWS[0] * HEADS_PER_GROUP + NEW_ROWS
    return pl.pallas_call(
        _attn_s_kernel,
        grid=(DEC_BATCH // ATT_SB,),
        in_specs=in_specs,
        out_specs=pl.BlockSpec((step_rows, HEAD_DIM), lambda b: (b, 0)),
        out_shape=jax.ShapeDtypeStruct((4 * ROWS_S, HEAD_DIM), F32),
        scratch_shapes=[pltpu.VMEM((ATT_SB, rows0, HEAD_DIM), F32)] * 2,
        compiler_params=_params(("arbitrary",)),
        name="attn_sample",
    )(*args)


def _mix_kernel(*refs):
    (yp_ref, ys_ref, hp_ref, hs_ref) = refs[0:4]
    op_refs, lp_refs = refs[4:7], refs[7:10]
    os_ref, wpw_ref, bpw_ref, wo_ref, wg_ref, mp_ref, ms_ref = refs[10:17]
    i = pl.program_id(0)

    def combined_prompt_head(h):
        sl = pl.ds(h, TM2, stride=HEADS_PER_GROUP)
        ls = [l[sl, :] for l in lp_refs]
        mx = jnp.maximum(jnp.maximum(ls[0], ls[1]), ls[2])
        es = [jnp.exp(l - mx) for l in ls]
        num = es[0] * op_refs[0][sl, :] + es[1] * op_refs[1][sl, :] + es[2] * op_refs[2][sl, :]
        return num / (es[0] + es[1] + es[2])

    def sample_head(h):
        return os_ref[pl.ds(h, ROWS_S, stride=HEADS_PER_GROUP), :]

    def mix(y_ref, h_ref, head, m_ref):
        y, h = y_ref[...], h_ref[...]
        o_att = jnp.concatenate([head(hh).astype(BF16) for hh in range(HEADS_PER_GROUP)], axis=1)
        for c in range(D_MODEL // NC):
            sl = slice(c * NC, (c + 1) * NC)
            gl = slice(D_MODEL + c * NC, D_MODEL + (c + 1) * NC)
            conv = jnp.dot(y, wpw_ref[:, sl], preferred_element_type=F32) + bpw_ref[:, sl]
            att = jnp.dot(o_att, wo_ref[:, sl], preferred_element_type=F32)
            gate_conv = _sigmoid(jnp.dot(h, wg_ref[:, sl], preferred_element_type=F32))
            gate_att = _sigmoid(jnp.dot(h, wg_ref[:, gl], preferred_element_type=F32))
            m_ref[:, sl] = (gate_conv * conv + gate_att * att).astype(BF16)

    mix(yp_ref, hp_ref, combined_prompt_head, mp_ref)

    @pl.when(i == 0)
    def _():
        mix(ys_ref, hs_ref, sample_head, ms_ref)


def _mix_call(y_p, y_s, h_p, h_s, o_p, l_p, o_s, w_pw, b_pw, w_o, w_gates):
    pi = lambda i: (i, 0)
    zero = lambda i: (0, 0)
    hp = pl.BlockSpec((4 * TM2, HEAD_DIM), pi)
    hs = pl.BlockSpec((4 * ROWS_S, HEAD_DIM), zero)
    const = lambda shape: pl.BlockSpec(shape, zero, pipeline_mode=pl.Buffered(1))
    return pl.pallas_call(
        _mix_kernel,
        grid=(NP2,),
        in_specs=[
            pl.BlockSpec((TM2, D_CONV), pi), pl.BlockSpec((ROWS_S, D_CONV), zero),
            pl.BlockSpec((TM2, D_MODEL), pi), pl.BlockSpec((ROWS_S, D_MODEL), zero),
            hp, hp, hp, hp, hp, hp, hs,
            const((D_CONV, D_MODEL)), const((1, D_MODEL)), const((GROUP_COLS, D_MODEL)),
            const((D_MODEL, 2 * D_MODEL)),
        ],
        out_specs=[pl.BlockSpec((TM2, D_MODEL), pi), pl.BlockSpec((ROWS_S, D_MODEL), zero)],
        out_shape=[jax.ShapeDtypeStruct((ROWS_P, D_MODEL), BF16),
                   jax.ShapeDtypeStruct((ROWS_S, D_MODEL), BF16)],
        compiler_params=_params(("arbitrary",)),
        name="mix",
    )(y_p, y_s, h_p, h_s, *o_p, *l_p, o_s, w_pw, b_pw, w_o, w_gates)


def _outproj_kernel(mp_ref, ms_ref, xp_ref, xs_ref, w_ref, g_ref, op_ref, os_ref, hp_ref, hs_ref):
    i = pl.program_id(0)

    def project(m_ref, x_ref, o_ref, h_ref):
        x1 = x_ref[...] + jnp.dot(m_ref[...], w_ref[...], preferred_element_type=F32)
        o_ref[...] = x1
        h_ref[...] = _rms(x1, g_ref[...]).astype(BF16)

    project(mp_ref, xp_ref, op_ref, hp_ref)

    @pl.when(i == 0)
    def _():
        project(ms_ref, xs_ref, os_ref, hs_ref)


def _outproj_call(m_p, m_s, xp, xs, w_out, g_ffn):
    pi = lambda i: (i, 0)
    zero = lambda i: (0, 0)
    tile_p, tile_s = pl.BlockSpec((TM2, D_MODEL), pi), pl.BlockSpec((ROWS_S, D_MODEL), zero)
    return pl.pallas_call(
        _outproj_kernel,
        grid=(NP2,),
        in_specs=[tile_p, tile_s, tile_p, tile_s, pl.BlockSpec((D_MODEL, D_MODEL), zero),
                  pl.BlockSpec((1, D_MODEL), zero)],
        out_specs=[tile_p, tile_s, tile_p, tile_s],
        out_shape=[jax.ShapeDtypeStruct((ROWS_P, D_MODEL), F32), jax.ShapeDtypeStruct((ROWS_S, D_MODEL), F32),
                   jax.ShapeDtypeStruct((ROWS_P, D_MODEL), BF16), jax.ShapeDtypeStruct((ROWS_S, D_MODEL), BF16)],
        compiler_params=_params(("arbitrary",)),
        name="out_proj",
    )(m_p, m_s, xp, xs, w_out, g_ffn)


N_CACHE = 2 * N_GROUPS
COPY_STEPS = 2 * DEC_BATCH


def _shift_copies(c, w, cache_refs, new_refs, out_refs, bufs, sems):
    b = c // 2
    loads, stores = [], []
    for g in range(N_GROUPS):
        n = 2 * g + w
        keep = WINDOWS[g] * HEADS_PER_GROUP - NEW_ROWS
        loads.append(pltpu.make_async_copy(cache_refs[n].at[b, pl.ds(NEW_ROWS, keep)], bufs[g].at[w],
                                           sems.at[0, w, g]))
        stores.append(pltpu.make_async_copy(bufs[g].at[w], out_refs[n].at[b, pl.ds(0, keep)],
                                            sems.at[1, w, g]))
        stores.append(pltpu.make_async_copy(new_refs[n].at[b], out_refs[n].at[b, pl.ds(keep, NEW_ROWS)],
                                            sems.at[2, w, g]))
    return loads, stores


def _ffn_kernel(*refs):
    x_hbm, xs_ref, hp_ref, hs_ref, wgu_hbm, wd_hbm, gl_ref = refs[0:7]
    cache_refs = refs[7:7 + N_CACHE]
    new_refs = refs[7 + N_CACHE:7 + 2 * N_CACHE]
    yp_ref, ys_ref = refs[7 + 2 * N_CACHE:9 + 2 * N_CACHE]
    out_refs = refs[9 + 2 * N_CACHE:9 + 3 * N_CACHE]
    bufs = refs[9 + 3 * N_CACHE:9 + 3 * N_CACHE + N_GROUPS]
    wgu_buf, wd_buf, sems, x_sem, w_sems = refs[9 + 3 * N_CACHE + N_GROUPS:]
    i = pl.program_id(0)

    def weight_copies(chunk, slot):
        return (pltpu.make_async_copy(wgu_hbm.at[:, pl.ds(pl.multiple_of(chunk * 2 * TH, 2 * TH), 2 * TH)],
                                      wgu_buf.at[slot], w_sems.at[0, slot]),
                pltpu.make_async_copy(wd_hbm.at[pl.ds(pl.multiple_of(chunk * TH, TH), TH), :],
                                      wd_buf.at[slot], w_sems.at[1, slot]))

    @pl.when(i == 0)
    def _():
        for d in weight_copies(0, 0):
            d.start()
        ys_ref[...] = xs_ref[...]

    residual = pltpu.make_async_copy(x_hbm.at[pl.ds(pl.multiple_of(i * TM, TM), TM)], yp_ref, x_sem)
    residual.start()

    def hidden_chunk(k, carry):
        step = i * KH + k
        for w in range(2):
            @pl.when(step % 2 == w)
            def _():
                for d in weight_copies(k, w):
                    d.wait()

                @pl.when(step + 1 < NP * KH)
                def _():
                    for d in weight_copies(jnp.where(k + 1 == KH, 0, k + 1), 1 - w):
                        d.start()

                copies = lambda c, slot: _shift_copies(c, slot, cache_refs, new_refs, out_refs, bufs, sems)

                @pl.when((step == 0) & (step < COPY_STEPS))
                def _():
                    for d in copies(step, w)[0]:
                        d.start()

                @pl.when(step < COPY_STEPS)
                def _():
                    for d in copies(step, w)[0]:
                        d.wait()

                @pl.when((step >= 1) & (step <= COPY_STEPS))
                def _():
                    for d in copies(step - 1, 1 - w)[1]:
                        d.wait()

                @pl.when(step < COPY_STEPS)
                def _():
                    for d in copies(step, w)[1]:
                        d.start()

                @pl.when(step + 1 < COPY_STEPS)
                def _():
                    for d in copies(step + 1, 1 - w)[0]:
                        d.start()

        slot = step % 2

        def swiglu(h):
            a = jnp.dot(h, wgu_buf[slot, :, 0:TH], preferred_element_type=F32)
            u = jnp.dot(h, wgu_buf[slot, :, TH:2 * TH], preferred_element_type=F32)
            return (a * _sigmoid(a) * u).astype(BF16)

        act = swiglu(hp_ref[...])

        @pl.when(k == 0)
        def _():
            residual.wait()

        yp_ref[...] += jnp.dot(act, wd_buf[slot], preferred_element_type=F32)

        @pl.when(i == 0)
        def _():
            ys_ref[...] += jnp.dot(swiglu(hs_ref[...]), wd_buf[slot], preferred_element_type=F32)

        return carry

    lax.fori_loop(0, KH, hidden_chunk, 0)

    yp_ref[...] = _rms(yp_ref[...], gl_ref[...])

    @pl.when(i == 0)
    def _():
        ys_ref[...] = _rms(ys_ref[...], gl_ref[...])


def _ffn_call(x1_p, x1_s, h_p, h_s, w_gate_up, w_down, g_final, caches, new_rows):
    assert NP * KH >= COPY_STEPS
    pi = lambda i: (i, 0)
    zero = lambda i: (0, 0)
    hbm = pl.BlockSpec(memory_space=pl.ANY)
    return pl.pallas_call(
        _ffn_kernel,
        grid=(NP,),
        in_specs=[
            hbm,
            pl.BlockSpec((ROWS_S, D_MODEL), zero, pipeline_mode=pl.Buffered(1)),
            pl.BlockSpec((TM, D_MODEL), pi),
            pl.BlockSpec((ROWS_S, D_MODEL), zero, pipeline_mode=pl.Buffered(1)),
            hbm, hbm,
            pl.BlockSpec((1, D_MODEL), zero),
        ] + [hbm] * (2 * N_CACHE),
        out_specs=[pl.BlockSpec((TM, D_MODEL), pi), pl.BlockSpec((ROWS_S, D_MODEL), zero)] + [hbm] * N_CACHE,
        out_shape=[jax.ShapeDtypeStruct((ROWS_P, D_MODEL), F32),
                   jax.ShapeDtypeStruct((ROWS_S, D_MODEL), F32)]
        + [jax.ShapeDtypeStruct(c.shape, F32) for c in caches],
        scratch_shapes=[pltpu.VMEM((2, WINDOWS[g] * HEADS_PER_GROUP - NEW_ROWS, HEAD_DIM), F32)
                        for g in range(N_GROUPS)]
        + [pltpu.VMEM((2, D_MODEL, 2 * TH), BF16), pltpu.VMEM((2, TH, D_MODEL), BF16)]
        + [pltpu.SemaphoreType.DMA((3, 2, N_GROUPS)), pltpu.SemaphoreType.DMA(()),
           pltpu.SemaphoreType.DMA((2, 2))],
        compiler_params=_params(("arbitrary",), FFN_VMEM_LIMIT),
        name="ffn",
    )(x1_p, x1_s, h_p, h_s, w_gate_up, w_down, g_final, *caches, *new_rows)


def _rope_tables(pos):
    half = ROT_DIM // 2
    inv = np.power(ROPE_THETA, -np.arange(0, ROT_DIM, 2, dtype=np.float64) / ROT_DIM)
    ang = np.asarray(pos, np.float64)[:, None] * inv[None, :]
    cos, sin = np.cos(ang), np.sin(ang)
    n = ang.shape[0]
    ones = np.ones((n, HEAD_DIM - ROT_DIM))
    zeros = np.zeros((n, HEAD_DIM - ROT_DIM))
    zh = np.zeros((n, half))
    c = np.concatenate([cos, cos, ones], axis=1)
    s_lo = np.concatenate([-sin, zh, zeros], axis=1)
    s_hi = np.concatenate([zh, sin, zeros], axis=1)
    return jnp.asarray(np.stack([c, s_lo, s_hi]), F32)


def kernel(x_prompt, x_sample, state_conv, cache_k_w128, cache_v_w128, cache_k_w512, cache_v_w512,
           cache_k_w2048, cache_v_w2048, g_mix, w_in, b_glu, w_dw, b_dw, ln_g, ln_b, w_pw, b_pw,
           w_o_att, w_out, g_ffn, w_gate, w_up, w_down, g_final):
    xp = x_prompt.reshape(ROWS_P, D_MODEL)
    xs = x_sample.reshape(ROWS_S, D_MODEL)
    w_in2 = w_in.reshape(D_MODEL, IN_COLS)

    hp, hs = _rms_call(xp, xs, g_mix)
    u_p, u_s, w_gates16 = _glu_call(hp, hs, w_in2, b_glu)
    tab_p = _rope_tables(np.arange(SEQ))
    tab_s = _rope_tables(PAST_LEN + np.arange(ROWS_S) % DEC_SEQ)
    q_p, q_s = _q_call(hp, hs, w_in2, tab_p, tab_s)
    mixer_w = (w_out.reshape(D_MODEL, D_MODEL), w_pw.reshape(D_CONV, D_MODEL), w_o_att.reshape(GROUP_COLS, D_MODEL))
    kv = [_kv_call(hp, hs, w_in2, tab_p, tab_s, g, mixer_w[g]) for g in range(N_GROUPS)]
    w_out16, w_pw16, w_o16 = kv[0][4], kv[1][4], kv[2][4]

    w_dw2 = w_dw.reshape(CONV_WIDTH, D_CONV)
    y_p, w_gate_up16, w_down16 = _conv_p_call(
        u_p, w_dw2, b_dw, ln_g, ln_b, w_gate.reshape(D_MODEL, FFN_HIDDEN), w_up.reshape(D_MODEL, FFN_HIDDEN),
        w_down.reshape(FFN_HIDDEN, D_MODEL))
    y_s, conv_s = _conv_s_call(state_conv, u_s.reshape(DEC_BATCH, DEC_SEQ, D_CONV), w_dw2, b_dw, ln_g, ln_b)
    y_s = y_s.reshape(ROWS_S, D_CONV).astype(BF16)
    conv_p = u_p.reshape(BATCH, SEQ, D_CONV)[:, SEQ - (CONV_WIDTH - 1):][None]

    caches = ((cache_k_w128, cache_v_w128), (cache_k_w512, cache_v_w512), (cache_k_w2048, cache_v_w2048))
    o_p, l_p = [], []
    for g in range(N_GROUPS):
        o, l = _attn_p_call(q_p, kv[g][0], kv[g][2], g)
        o_p.append(o)
        l_p.append(l)
    o_s = _attn_s_call(q_s, kv, caches)

    m_p, m_s = _mix_call(y_p, y_s, hp, hs, o_p, l_p, o_s, w_pw16, b_pw, w_o16, w_gates16)
    x1_p, x1_s, hf_p, hf_s = _outproj_call(m_p, m_s, xp, xs, w_out16, g_ffn)

    flat = lambda g, a: a.reshape(DEC_BATCH, WINDOWS[g] * HEADS_PER_GROUP, HEAD_DIM)
    cache_flat = [flat(g, caches[g][t]) for g in range(N_GROUPS) for t in range(2)]
    new_rows = [kv[g][t].reshape(DEC_BATCH, NEW_ROWS, HEAD_DIM) for g in range(N_GROUPS) for t in (1, 3)]
    y_p2, y_s2, *shifted = _ffn_call(x1_p, x1_s, hf_p, hf_s, w_gate_up16, w_down16,
                                     g_final.reshape(1, D_MODEL), cache_flat, new_rows)

    cache_out = []
    for g in range(N_GROUPS):
        keep = min(WINDOWS[g], SEQ)
        for t in range(2):
            prompt_rows = kv[g][2 * t] if keep == SEQ else kv[g][5 + t]
            cache_out.append(prompt_rows.reshape(1, BATCH, keep, HEADS_PER_GROUP, HEAD_DIM))
            cache_out.append(shifted[2 * g + t].reshape(1, DEC_BATCH, WINDOWS[g], HEADS_PER_GROUP, HEAD_DIM))
    return (y_p2.reshape(BATCH, SEQ, D_MODEL), y_s2.reshape(DEC_BATCH, DEC_SEQ, D_MODEL),
            conv_p, conv_s, *cache_out)
```
